```python
import math
import jax
import jax.numpy as jnp
from jax import lax
import numpy as np

D_MODEL = 1024
BATCH = 16
SEQ = 2048
DEPTH = 1

CHUNK = 64
Q_BLOCK = 128

DIFF_HEADS = 4
DIFF_DH = 64
DIFF_QK_W = DIFF_HEADS * 2 * DIFF_DH
DIFF_V_W = DIFF_HEADS * 2 * DIFF_DH

GLA_HEADS = 4
GLA_DK = 64
GLA_DV = 128
GLA_QK_W = GLA_HEADS * GLA_DK
GLA_V_W = GLA_HEADS * GLA_DV
GLA_RANK = 16
GLA_GATE_NORM = 16.0

IN_WIDTHS = (DIFF_QK_W, DIFF_QK_W, DIFF_V_W,
             GLA_QK_W, GLA_QK_W, GLA_V_W, GLA_V_W, GLA_RANK,
             D_MODEL, D_MODEL)
IN_WIDTH = sum(IN_WIDTHS)

N_EXPERTS = 32
TOP_K = 4
D_FF = D_MODEL
SWIGLU_LIMIT = 7.0
SWIGLU_ALPHA = 1.702
MOE_BLOCK = 128

N_MOD = 6
EPS = 1e-6

kernel_name = 'hybrid_diffattn_gla_moe_block'


def rmsnorm(x, g):
    xf = x.astype(jnp.float32)
    y = xf * lax.rsqrt(jnp.mean(xf * xf, axis=-1, keepdims=True) + EPS)
    return (y * g.astype(jnp.float32)).astype(x.dtype)


def alibi_slopes(n_heads):
    return jnp.asarray(2.0 ** (-8.0 * np.arange(1, n_heads + 1) / n_heads), dtype=jnp.float32)


def diff_attention(q, k, v, lam):
    S = q.shape[1]
    slopes = alibi_slopes(DIFF_HEADS)
    pos = jnp.arange(S)
    scale = DIFF_DH ** -0.5
    outs = []
    for i in range(S // Q_BLOCK):
        q0 = i * Q_BLOCK
        kv_len = q0 + Q_BLOCK
        qb = q[:, q0:kv_len]
        kb = k[:, :kv_len]
        vb = v[:, :kv_len]
        s = jnp.einsum('bqhpd,bkhpd->bhpqk', qb, kb).astype(jnp.float32) * scale
        qpos = pos[q0:kv_len]
        kpos = pos[:kv_len]
        allowed = (qpos[:, None] // CHUNK) >= (kpos[None, :] // CHUNK)
        dist = jnp.abs(qpos[:, None] - kpos[None, :]).astype(jnp.float32)
        bias = -slopes[:, None, None] * dist
        s = jnp.where(allowed, s + bias[None, :, None], -jnp.inf)
        p = jax.nn.softmax(s, axis=-1)
        a = p[:, :, 0] - lam * p[:, :, 1]
        outs.append(jnp.einsum('bhqk,bkhe->bqhe', a.astype(vb.dtype), vb))
    return jnp.concatenate(outs, axis=1)


def chunked_gla(q, k, v, g):
    B, S, H, DK = q.shape
    DV = v.shape[-1]
    n = S // CHUNK
    f32 = jnp.float32
    q = q.astype(f32).reshape(B, n, CHUNK, H, DK) * (DK ** -0.5)
    k = k.astype(f32).reshape(B, n, CHUNK, H, DK)
    v = v.astype(f32).reshape(B, n, CHUNK, H, DV)
    G = jnp.cumsum(g.astype(f32).reshape(B, n, CHUNK, H, DK), axis=2)
    G_last = G[:, :, -1:]
    q_s = q * jnp.exp(G)
    k_s = k * jnp.exp(-G)
    k_dec = k * jnp.exp(G_last - G)
    causal = jnp.tril(jnp.ones((CHUNK, CHUNK), dtype=bool))
    A = jnp.einsum('bnchk,bnshk->bnhcs', q_s, k_s)
    A = jnp.where(causal, A, 0.0)
    o_intra = jnp.einsum('bnhcs,bnshv->bnchv', A, v)
    dS = jnp.einsum('bnshk,bnshv->bnhkv', k_dec, v)
    decay = jnp.exp(G_last[:, :, 0])

    def step(state, inp):
        d, ds = inp
        return d[..., None] * state + ds, state

    _, S_prev = lax.scan(step, jnp.zeros((B, H, DK, DV), f32),
                         (jnp.moveaxis(decay, 1, 0), jnp.moveaxis(dS, 1, 0)))
    S_prev = jnp.moveaxis(S_prev, 0, 1)
    o_inter = jnp.einsum('bnchk,bnhkv->bnchv', q_s, S_prev)
    return (o_intra + o_inter).reshape(B, S, H, DV)


def moe_ffn(h, w_router, b_router, w_gate_up, b_gate_up, w_down, b_down):
    B, S, D = h.shape
    N = B * S
    hf = h.reshape(N, D)
    logits = (hf @ w_router + b_router).astype(jnp.float32)
    top_vals, top_idx = lax.top_k(logits, TOP_K)
    weights = jax.nn.softmax(top_vals, axis=-1)
    A = N * TOP_K
    e_flat = top_idx.reshape(A)
    w_flat = weights.reshape(A)
    tok_flat = jnp.arange(A, dtype=jnp.int32) // TOP_K
    order = jnp.argsort(e_flat)
    e_sorted = e_flat[order]
    counts = jnp.bincount(e_flat, length=N_EXPERTS)
    starts = jnp.cumsum(counts) - counts
    padded = ((counts + MOE_BLOCK - 1) // MOE_BLOCK) * MOE_BLOCK
    pad_ends = jnp.cumsum(padded)
    pad_starts = pad_ends - padded
    dest = pad_starts[e_sorted] + (jnp.arange(A) - starts[e_sorted])
    NB = -(-A // MOE_BLOCK) + N_EXPERTS
    P = NB * MOE_BLOCK
    tok_buf = jnp.full((P,), N, dtype=jnp.int32).at[dest].set(tok_flat[order])
    w_buf = jnp.zeros((P,), dtype=jnp.float32).at[dest].set(w_flat[order])
    blk_expert = jnp.minimum(
        jnp.searchsorted(pad_ends, jnp.arange(NB) * MOE_BLOCK, side='right'), N_EXPERTS - 1)
    h_pad = jnp.concatenate([hf, jnp.zeros((1, D), hf.dtype)], axis=0)
    xb = h_pad[tok_buf].reshape(NB, MOE_BLOCK, D)

    def expert_block(args):
        x_i, e = args
        gu = x_i @ w_gate_up[e] + b_gate_up[e]
        gate, up = jnp.split(gu, 2, axis=-1)
        gate = jnp.minimum(gate, SWIGLU_LIMIT)
        up = jnp.clip(up, -SWIGLU_LIMIT, SWIGLU_LIMIT)
        act = (up + 1.0) * (gate * jax.nn.sigmoid(SWIGLU_ALPHA * gate))
        return act @ w_down[e] + b_down[e]

    yb = lax.map(expert_block, (xb, blk_expert)).reshape(P, D)
    y = jax.ops.segment_sum(yb * w_buf[:, None].astype(yb.dtype), tok_buf, num_segments=N + 1)[:N]
    return y.reshape(B, S, D)


def hybrid_layer(x, c, layer_idx, w_ada, b_ada, norm1_g, w_in,
                 lambda_q1, lambda_k1, lambda_q2, lambda_k2, diff_norm_g,
                 w_alpha_up, b_alpha, gla_norm_g, w_branch_diff, w_branch_gla, w_out,
                 norm2_g, w_router, b_router, w_gate_up, b_gate_up, w_down, b_down):
    B, S, D = x.shape
    mod = jax.nn.silu(c) @ w_ada + b_ada
    sh1, sc1, gt1, sh2, sc2, gt2 = jnp.split(mod[:, None, :], N_MOD, axis=-1)

    h = rmsnorm(x, norm1_g) * (1.0 + sc1) + sh1
    proj = h @ w_in
    split_at = list(np.cumsum(IN_WIDTHS)[:-1])
    dq, dk, dv, gq, gk, gv, gr, g_lr, gate_a, gate_b = jnp.split(proj, split_at, axis=-1)

    lambda_init = 0.8 - 0.6 * math.exp(-0.3 * layer_idx)
    lam = (jnp.exp(jnp.sum(lambda_q1 * lambda_k1)) - jnp.exp(jnp.sum(lambda_q2 * lambda_k2))
           + lambda_init).astype(jnp.float32)
    o_a = diff_attention(dq.reshape(B, S, DIFF_HEADS, 2, DIFF_DH),
                         dk.reshape(B, S, DIFF_HEADS, 2, DIFF_DH),
                         dv.reshape(B, S, DIFF_HEADS, 2 * DIFF_DH), lam)
    o_a = (rmsnorm(o_a, diff_norm_g) * (1.0 - lambda_init)).reshape(B, S, DIFF_V_W)

    log_alpha = jax.nn.log_sigmoid((g_lr @ w_alpha_up + b_alpha).astype(jnp.float32)) / GLA_GATE_NORM
    o_b = chunked_gla(gq.reshape(B, S, GLA_HEADS, GLA_DK),
                      gk.reshape(B, S, GLA_HEADS, GLA_DK),
                      gv.reshape(B, S, GLA_HEADS, GLA_DV),
                      log_alpha.reshape(B, S, GLA_HEADS, GLA_DK)).astype(x.dtype)
    o_b = rmsnorm(o_b, gla_norm_g).reshape(B, S, GLA_V_W) * jax.nn.silu(gr)

    merged = (jax.nn.sigmoid(gate_a) * (o_a @ w_branch_diff)
              + jax.nn.sigmoid(gate_b) * (o_b @ w_branch_gla))
    x = x + gt1 * (merged @ w_out)

    h2 = rmsnorm(x, norm2_g) * (1.0 + sc2) + sh2
    x = x + gt2 * moe_ffn(h2, w_router, b_router, w_gate_up, b_gate_up, w_down, b_down)
    return x


def setup_inputs(seed: int = 0) -> dict:
    key = jax.random.key(seed)
    ks = jax.random.split(key, 26)
    f32 = jnp.float32
    L, D, E, F = DEPTH, D_MODEL, N_EXPERTS, D_FF

    def nrm(k, shape, scale):
        return jax.random.normal(k, shape, f32) * scale

    return {
        'x': nrm(ks[0], (BATCH, SEQ, D), 1.0),
        'c': nrm(ks[1], (BATCH, D), 1.0),
        'w_ada': nrm(ks[2], (L, D, N_MOD * D), D ** -0.5),
        'b_ada': nrm(ks[3], (L, N_MOD * D), 0.02),
        'norm1_g': 1.0 + nrm(ks[4], (L, D), 0.1),
        'w_in': nrm(ks[5], (L, D, IN_WIDTH), D ** -0.5),
        'lambda_q1': nrm(ks[6], (L, DIFF_DH), 0.1),
        'lambda_k1': nrm(ks[7], (L, DIFF_DH), 0.1),
        'lambda_q2': nrm(ks[8], (L, DIFF_DH), 0.1),
        'lambda_k2': nrm(ks[9], (L, DIFF_DH), 0.1),
        'diff_norm_g': 1.0 + nrm(ks[10], (L, 2 * DIFF_DH), 0.1),
        'w_alpha_up': nrm(ks[11], (L, GLA_RANK, GLA_QK_W), GLA_RANK ** -0.5),
        'b_alpha': nrm(ks[12], (L, GLA_QK_W), 0.1),
        'gla_norm_g': 1.0 + nrm(ks[13], (L, GLA_DV), 0.1),
        'w_branch_diff': nrm(ks[14], (L, DIFF_V_W, D), DIFF_V_W ** -0.5),
        'w_branch_gla': nrm(ks[15], (L, GLA_V_W, D), GLA_V_W ** -0.5),
        'w_out': nrm(ks[16], (L, D, D), D ** -0.5),
        'norm2_g': 1.0 + nrm(ks[17], (L, D), 0.1),
        'w_router': nrm(ks[18], (L, D, E), D ** -0.5),
        'b_router': nrm(ks[19], (L, E), 0.01),
        'w_gate_up': nrm(ks[20], (L, E, D, 2 * F), D ** -0.5),
        'b_gate_up': nrm(ks[21], (L, E, 2 * F), 0.01),
        'w_down': nrm(ks[22], (L, E, F, D), F ** -0.5),
        'b_down': nrm(ks[23], (L, E, D), 0.01),
        'final_norm_g': 1.0 + nrm(ks[24], (D,), 0.1),
    }


def reference(x, c, w_ada, b_ada, norm1_g, w_in, lambda_q1, lambda_k1, lambda_q2, lambda_k2,
              diff_norm_g, w_alpha_up, b_alpha, gla_norm_g, w_branch_diff, w_branch_gla, w_out,
              norm2_g, w_router, b_router, w_gate_up, b_gate_up, w_down, b_down, final_norm_g):
    for l in range(DEPTH):
        x = hybrid_layer(x, c, l, w_ada[l], b_ada[l], norm1_g[l], w_in[l],
                         lambda_q1[l], lambda_k1[l], lambda_q2[l], lambda_k2[l], diff_norm_g[l],
                         w_alpha_up[l], b_alpha[l], gla_norm_g[l], w_branch_diff[l], w_branch_gla[l],
                         w_out[l], norm2_g[l], w_router[l], b_router[l], w_gate_up[l],
                         b_gate_up[l], w_down[l], b_down[l])
    return rmsnorm(x, final_norm_g)
```

```python
import functools
import math

import jax
import jax.numpy as jnp
import numpy as np
from jax import lax
from jax.experimental import pallas as pl
from jax.experimental.pallas import tpu as pltpu

F32 = jnp.float32
BF16 = jnp.bfloat16
U32 = jnp.uint32
I32 = jnp.int32

D_MODEL = 1024
BATCH = 16
SEQ = 2048
N_TOK = BATCH * SEQ
CHUNK = 64
DIFF_HEADS = 4
DIFF_DH = 64
GLA_HEADS = 4
GLA_DK = 64
GLA_DV = 128
GLA_RANK = 16
GLA_GATE_NORM = 16.0
N_EXPERTS = 32
TOP_K = 4
D_FF = D_MODEL
SWIGLU_LIMIT = 7.0
SWIGLU_ALPHA = 1.702
N_MOD = 6
EPS = 1e-6
LAMBDA_INIT = 0.8 - 0.6 * math.exp(-0.3 * 0)

LANES = 128
HALF = D_MODEL // 2

TM_IN = 512
TQ = 256
TM_MERGE = 512
BLK = 256
N_BLK = (N_TOK * TOP_K) // BLK + N_EXPERTS
P_ROWS = N_BLK * BLK
TM_FIN = 512
VMEM_LIMIT = 56 * 1024 * 1024


def _nt_dot(a, b):
    return lax.dot_general(a, b, (((1,), (1,)), ((), ())), preferred_element_type=F32)


def _tn_dot(a, b):
    return lax.dot_general(a, b, (((0,), (0,)), ((), ())), preferred_element_type=F32)


def _split3(x):
    hi = x.astype(BF16)
    r1 = x - hi.astype(F32)
    mid = r1.astype(BF16)
    lo = (r1 - mid.astype(F32)).astype(BF16)
    return hi, mid, lo


def _pack_halves(y):
    a = pltpu.bitcast(y[:, :HALF].astype(BF16).astype(F32), U32)
    b = pltpu.bitcast(y[:, HALF:].astype(BF16).astype(F32), U32)
    return (a >> 16) | b


def _unpack_halves(u):
    lo = pltpu.bitcast(u << 16, F32)
    hi = pltpu.bitcast(u & jnp.uint32(0xFFFF0000), F32)
    return lo, hi


def _mod_kernel(c_ref, w_ref, b_ref, o_ref):
    c = c_ref[...]
    s = c * jax.nn.sigmoid(c)
    o_ref[0] = jnp.dot(s.astype(BF16), w_ref[...].astype(BF16),
                       preferred_element_type=F32) + b_ref[...]


def _modulation(c, w_ada, b_ada):
    return pl.pallas_call(
        _mod_kernel,
        grid=(N_MOD,),
        in_specs=[
            pl.BlockSpec((BATCH, D_MODEL), lambda j: (0, 0)),
            pl.BlockSpec((D_MODEL, D_MODEL), lambda j: (0, j)),
            pl.BlockSpec((1, D_MODEL), lambda j: (0, j)),
        ],
        out_specs=pl.BlockSpec((1, BATCH, D_MODEL), lambda j: (j, 0, 0)),
        out_shape=jax.ShapeDtypeStruct((N_MOD, BATCH, D_MODEL), F32),
        compiler_params=pltpu.CompilerParams(dimension_semantics=("arbitrary",)),
        name="adaln_mod",
    )(c, w_ada, b_ada.reshape(1, N_MOD * D_MODEL))


W_A = 3 * DIFF_HEADS * 2 * DIFF_DH
W_G = 2 * GLA_HEADS * GLA_DK + 2 * GLA_HEADS * GLA_DV
W_GATE = 2 * D_MODEL


def _in_kernel(x_ref, sh_ref, sc_ref, g_ref, wa_ref, wg_ref, wgate_ref, wlr_ref,
               oa_ref, og_ref, ogate_ref, olr_ref):
    x = x_ref[...]
    inv = lax.rsqrt(jnp.mean(x * x, axis=-1, keepdims=True) + EPS)
    h = (x * inv * g_ref[...]) * (1.0 + sc_ref[...]) + sh_ref[...]
    hb = h.astype(BF16)
    oa_ref[...] = jnp.dot(hb, wa_ref[...], preferred_element_type=F32).astype(BF16)
    og_ref[...] = jnp.dot(hb, wg_ref[...], preferred_element_type=F32).astype(BF16)
    ogate_ref[...] = jnp.dot(hb, wgate_ref[...], preferred_element_type=F32).astype(BF16)
    olr_ref[...] = jnp.dot(hb, wlr_ref[...], preferred_element_type=F32)


def _in_proj(x, mod4, norm1_g, w_a, w_g, w_gate, w_lr):
    nrow = SEQ // TM_IN
    full = lambda shape: pl.BlockSpec(shape, lambda b, i: (0,) * len(shape))
    return pl.pallas_call(
        _in_kernel,
        grid=(BATCH, nrow),
        in_specs=[
            pl.BlockSpec((None, TM_IN, D_MODEL), lambda b, i: (b, i, 0)),
            pl.BlockSpec((None, None, 1, D_MODEL), lambda b, i: (0, b, 0, 0)),
            pl.BlockSpec((None, None, 1, D_MODEL), lambda b, i: (1, b, 0, 0)),
            full((1, D_MODEL)),
            full((D_MODEL, W_A)),
            full((D_MODEL, W_G)),
            full((D_MODEL, W_GATE)),
            full((D_MODEL, LANES)),
        ],
        out_specs=[
            pl.BlockSpec((None, TM_IN, W_A), lambda b, i: (b, i, 0)),
            pl.BlockSpec((None, TM_IN, W_G), lambda b, i: (b, i, 0)),
            pl.BlockSpec((None, TM_IN, W_GATE), lambda b, i: (b, i, 0)),
            pl.BlockSpec((None, TM_IN, LANES), lambda b, i: (b, i, 0)),
        ],
        out_shape=[
            jax.ShapeDtypeStruct((BATCH, SEQ, W_A), BF16),
            jax.ShapeDtypeStruct((BATCH, SEQ, W_G), BF16),
            jax.ShapeDtypeStruct((BATCH, SEQ, W_GATE), BF16),
            jax.ShapeDtypeStruct((BATCH, SEQ, LANES), F32),
        ],
        compiler_params=pltpu.CompilerParams(
            dimension_semantics=("arbitrary", "arbitrary"), vmem_limit_bytes=VMEM_LIMIT),
        name="in_proj",
    )(x, mod4, mod4, norm1_g, w_a, w_g, w_gate, w_lr)


def _attn_kernel(slope_ref, lam_ref, q_ref, k_ref, v_ref, g_ref, o_ref):
    h = pl.program_id(1)
    qi = pl.program_id(2)
    slope = slope_ref[h]
    lam = lam_ref[0]
    q = q_ref[...] * jnp.asarray(DIFF_DH ** -0.5, BF16)
    lane = lax.broadcasted_iota(I32, (1, 2 * DIFF_DH), 1)
    qa = jnp.where(lane < DIFF_DH, q, jnp.zeros_like(q))
    qb = jnp.where(lane >= DIFF_DH, q, jnp.zeros_like(q))
    qpos = qi * TQ + lax.broadcasted_iota(I32, (TQ, 1), 0)

    def block(j, carry):
        m1, l1, a1, m2, l2, a2 = carry
        off = pl.multiple_of(j * TQ, TQ)
        k = k_ref[pl.ds(off, TQ), :]
        v = v_ref[pl.ds(off, TQ), :]
        kpos = j * TQ + lax.broadcasted_iota(I32, (1, TQ), 1)
        allowed = (qpos >> 6) >= (kpos >> 6)
        bias = -slope * jnp.abs(qpos - kpos).astype(F32)

        def upd(qq, m, l, a):
            s = jnp.where(allowed, _nt_dot(qq, k) + bias, -jnp.inf)
            m_new = jnp.maximum(m, jnp.max(s, axis=-1, keepdims=True))
            alpha = jnp.exp(m - m_new)
            p = jnp.exp(s - m_new)
            l_new = alpha * l + jnp.sum(p, axis=-1, keepdims=True)
            a_new = alpha * a + jnp.dot(p.astype(BF16), v, preferred_element_type=F32)
            return m_new, l_new, a_new

        m1, l1, a1 = upd(qa, m1, l1, a1)
        m2, l2, a2 = upd(qb, m2, l2, a2)
        return m1, l1, a1, m2, l2, a2

    neg = jnp.full((TQ, 1), -jnp.inf, F32)
    zl = jnp.zeros((TQ, 1), F32)
    za = jnp.zeros((TQ, 2 * DIFF_DH), F32)
    m1, l1, a1, m2, l2, a2 = lax.fori_loop(0, qi + 1, block, (neg, zl, za, neg, zl, za))
    o = a1 / l1 - lam * (a2 / l2)
    inv = lax.rsqrt(jnp.mean(o * o, axis=-1, keepdims=True) + EPS)
    o_ref[...] = (o * inv * g_ref[...] * (1.0 - LAMBDA_INIT)).astype(BF16)


def _diff_attention(qkv_a, slopes, lam, diff_norm_g):
    nq = SEQ // TQ
    return pl.pallas_call(
        _attn_kernel,
        grid=(BATCH, DIFF_HEADS, nq),
        in_specs=[
            pl.BlockSpec(memory_space=pltpu.SMEM),
            pl.BlockSpec(memory_space=pltpu.SMEM),
            pl.BlockSpec((None, TQ, LANES), lambda b, h, i: (b, i, h)),
            pl.BlockSpec((None, SEQ, LANES), lambda b, h, i: (b, 0, DIFF_HEADS + h)),
            pl.BlockSpec((None, SEQ, LANES), lambda b, h, i: (b, 0, 2 * DIFF_HEADS + h)),
            pl.BlockSpec((1, LANES), lambda b, h, i: (0, 0)),
        ],
        out_specs=pl.BlockSpec((None, TQ, LANES), lambda b, h, i: (b, i, h)),
        out_shape=jax.ShapeDtypeStruct((BATCH, SEQ, DIFF_HEADS * 2 * DIFF_DH), BF16),
        compiler_params=pltpu.CompilerParams(
            dimension_semantics=("arbitrary", "arbitrary", "arbitrary")),
        name="diff_attn",
    )(slopes, lam, qkv_a, qkv_a, qkv_a, diff_norm_g)


N_CHUNK = SEQ // CHUNK
PAIR = 2 * GLA_DK
PAIR_V = 2 * GLA_DV
CS_ROWS = 256


def _gla_kernel(q_ref, k_ref, v_ref, r_ref, lr_ref, wup_ref, bup_ref, g_ref, o_ref,
                gcum_ref, state_ref):
    w_hi, w_mid, w_lo = _split3(wup_ref[...])
    rr = lax.broadcasted_iota(I32, (CS_ROWS, CS_ROWS), 0)
    cc = lax.broadcasted_iota(I32, (CS_ROWS, CS_ROWS), 1)
    tri = jnp.where(((rr >> 6) == (cc >> 6)) & (cc <= rr), 1.0, 0.0).astype(BF16)
    for blk in range(SEQ // CS_ROWS):
        rows = pl.ds(blk * CS_ROWS, CS_ROWS)
        a_hi, a_mid, a_lo = _split3(lr_ref[rows, :])
        z = (jnp.dot(a_hi, w_hi, preferred_element_type=F32)
             + jnp.dot(a_hi, w_mid, preferred_element_type=F32)
             + jnp.dot(a_mid, w_hi, preferred_element_type=F32)
             + jnp.dot(a_hi, w_lo, preferred_element_type=F32)
             + jnp.dot(a_lo, w_hi, preferred_element_type=F32)
             + jnp.dot(a_mid, w_mid, preferred_element_type=F32)) + bup_ref[...]
        la = (jnp.minimum(z, 0.0) - jnp.log(1.0 + jnp.exp(-jnp.abs(z)))) * (1.0 / GLA_GATE_NORM)
        l_hi, l_mid, l_lo = _split3(la)
        gcum_ref[rows, :] = (jnp.dot(tri, l_hi, preferred_element_type=F32)
                             + jnp.dot(tri, l_mid, preferred_element_type=F32)
                             + jnp.dot(tri, l_lo, preferred_element_type=F32))

    state_ref[...] = jnp.zeros_like(state_ref)
    lane_k = lax.broadcasted_iota(I32, (1, PAIR), 1)
    row_v = lax.broadcasted_iota(I32, (PAIR_V, PAIR), 0)
    col_k = lax.broadcasted_iota(I32, (PAIR_V, PAIR), 1)
    same_head = (row_v >= GLA_DV) == (col_k >= GLA_DK)
    cr = lax.broadcasted_iota(I32, (CHUNK, CHUNK), 0)
    cs = lax.broadcasted_iota(I32, (CHUNK, CHUNK), 1)
    causal = cs <= cr
    scale = GLA_DK ** -0.5

    def chunk(n, _):
        rows = pl.ds(pl.multiple_of(n * CHUNK, CHUNK), CHUNK)
        gc = gcum_ref[rows, :]
        g_last = gcum_ref[pl.ds(n * CHUNK + CHUNK - 1, 1), :]
        qf = q_ref[rows, :].astype(F32) * scale
        kf = k_ref[rows, :].astype(F32)
        q_s = (qf * jnp.exp(gc)).astype(BF16)
        k_s = (kf * jnp.exp(-gc)).astype(BF16)
        k_d = (kf * jnp.exp(g_last - gc)).astype(BF16)
        decay = jnp.exp(g_last)
        for pr in range(GLA_HEADS // 2):
            kl = slice(pr * PAIR, (pr + 1) * PAIR)
            vl = slice(pr * PAIR_V, (pr + 1) * PAIR_V)
            qs_p, ks_p, kd_p = q_s[:, kl], k_s[:, kl], k_d[:, kl]
            v_p = v_ref[rows, vl]
            st = state_ref[pr]
            o_inter = _nt_dot(qs_p, st.astype(BF16))
            d_st = _tn_dot(v_p, kd_p)
            state_ref[pr] = st * decay[:, kl] + jnp.where(same_head, d_st, 0.0)
            for sub in range(2):
                hd = 2 * pr + sub
                in_head = (lane_k >= sub * GLA_DK) & (lane_k < (sub + 1) * GLA_DK)
                a = _nt_dot(jnp.where(in_head, qs_p, jnp.zeros_like(qs_p)), ks_p)
                a = jnp.where(causal, a, 0.0).astype(BF16)
                vs = slice(hd * GLA_DV, (hd + 1) * GLA_DV)
                o = (jnp.dot(a, v_ref[rows, vs], preferred_element_type=F32)
                     + o_inter[:, sub * GLA_DV:(sub + 1) * GLA_DV])
                inv = lax.rsqrt(jnp.mean(o * o, axis=-1, keepdims=True) + EPS)
                r = r_ref[rows, vs].astype(F32)
                o_ref[rows, vs] = (o * inv * g_ref[...] * (r * jax.nn.sigmoid(r))).astype(BF16)
        return 0

    lax.fori_loop(0, N_CHUNK, chunk, 0)


def _gla(qkv_g, glr, w_up, b_up, gla_norm_g):
    qk_w = GLA_HEADS * GLA_DK
    v_w = GLA_HEADS * GLA_DV
    return pl.pallas_call(
        _gla_kernel,
        grid=(BATCH,),
        in_specs=[
            pl.BlockSpec((None, SEQ, qk_w), lambda b: (b, 0, 0)),
            pl.BlockSpec((None, SEQ, qk_w), lambda b: (b, 0, 1)),
            pl.BlockSpec((None, SEQ, v_w), lambda b: (b, 0, 1)),
            pl.BlockSpec((None, SEQ, v_w), lambda b: (b, 0, 2)),
            pl.BlockSpec((None, SEQ, LANES), lambda b: (b, 0, 0)),
            pl.BlockSpec((LANES, qk_w), lambda b: (0, 0)),
            pl.BlockSpec((1, qk_w), lambda b: (0, 0)),
            pl.BlockSpec((1, GLA_DV), lambda b: (0, 0)),
        ],
        out_specs=pl.BlockSpec((None, SEQ, v_w), lambda b: (b, 0, 0)),
        out_shape=jax.ShapeDtypeStruct((BATCH, SEQ, v_w), BF16),
        scratch_shapes=[
            pltpu.VMEM((SEQ, qk_w), F32),
            pltpu.VMEM((GLA_HEADS // 2, PAIR_V, PAIR), F32),
        ],
        compiler_params=pltpu.CompilerParams(
            dimension_semantics=("arbitrary",), vmem_limit_bytes=VMEM_LIMIT),
        name="gla",
    )(qkv_g, qkv_g, qkv_g, qkv_g, glr, w_up, b_up, gla_norm_g)


def _merge_kernel(oa_ref, ob_ref, gate_ref, x_ref, gt1_ref, sh2_ref, sc2_ref, g2_ref,
                  wpa_ref, wpb_ref, wo_ref, wr_ref, br_ref,
                  x1_ref, h2p_ref, eidx_ref, rank_ref, wts_ref, cnt_ref,
                  upper_ref, carry_ref):
    i = pl.program_id(0)

    @pl.when(i == 0)
    def _():
        rr = lax.broadcasted_iota(I32, (TM_MERGE, TM_MERGE), 0)
        cc = lax.broadcasted_iota(I32, (TM_MERGE, TM_MERGE), 1)
        upper_ref[...] = jnp.where(rr < cc, 1.0, 0.0).astype(BF16)
        carry_ref[...] = jnp.zeros_like(carry_ref)

    ga = gate_ref[:, :D_MODEL].astype(F32)
    gb = gate_ref[:, D_MODEL:].astype(F32)
    merged = (jax.nn.sigmoid(ga) * jnp.dot(oa_ref[...], wpa_ref[...], preferred_element_type=F32)
              + jax.nn.sigmoid(gb) * jnp.dot(ob_ref[...], wpb_ref[...], preferred_element_type=F32))
    y = jnp.dot(merged.astype(BF16), wo_ref[...], preferred_element_type=F32)
    x1 = x_ref[...] + gt1_ref[...] * y
    x1_ref[...] = x1
    inv = lax.rsqrt(jnp.mean(x1 * x1, axis=-1, keepdims=True) + EPS)
    h2 = (x1 * inv * g2_ref[...]) * (1.0 + sc2_ref[...]) + sh2_ref[...]
    h2p_ref[...] = _pack_halves(h2)

    h_hi, h_mid, _ = _split3(h2)
    w_hi, w_mid, _ = _split3(wr_ref[...])
    logits = (_nt_dot(w_hi, h_hi) + _nt_dot(w_hi, h_mid) + _nt_dot(w_mid, h_hi)) + br_ref[...]

    eio = lax.broadcasted_iota(I32, (N_EXPERTS, TM_MERGE), 0)
    vals, idxs, sels = [], [], []
    cur = logits
    for _k in range(TOP_K):
        m = jnp.max(cur, axis=0, keepdims=True)
        idx = jnp.min(jnp.where(cur == m, eio, N_EXPERTS), axis=0, keepdims=True)
        sel = eio == idx
        vals.append(m)
        idxs.append(idx)
        sels.append(sel)
        cur = jnp.where(sel, -jnp.inf, cur)
    es = [jnp.exp(v - vals[0]) for v in vals]
    tot = es[0] + es[1] + es[2] + es[3]
    onehot = jnp.zeros((N_EXPERTS, TM_MERGE), F32)
    for sel in sels:
        onehot = onehot + jnp.where(sel, 1.0, 0.0)
    before = jnp.dot(onehot.astype(BF16), upper_ref[...], preferred_element_type=F32) + carry_ref[:, 0:1]
    ranks = [jnp.sum(jnp.where(sel, before, 0.0), axis=0, keepdims=True) for sel in sels]
    carry_ref[...] = carry_ref[...] + jnp.sum(onehot, axis=1, keepdims=True)
    cnt_ref[...] = carry_ref[...]

    zi = jnp.zeros((8 - TOP_K, TM_MERGE), I32)
    zf = jnp.zeros((8 - TOP_K, TM_MERGE), F32)
    eidx_ref[...] = jnp.concatenate(idxs + [zi], axis=0)
    rank_ref[...] = jnp.concatenate([r.astype(I32) for r in ranks] + [zi], axis=0)
    wts_ref[...] = jnp.concatenate([e / tot for e in es] + [zf], axis=0)


def _merge_route(o_a, o_b, gates, x2d, mod4, norm2_g, w_pa, w_pb, w_o, w_rt, b_r):
    ntile = N_TOK // TM_MERGE
    per_b = SEQ // TM_MERGE
    full = lambda shape: pl.BlockSpec(shape, lambda i: (0,) * len(shape))
    row = lambda w: pl.BlockSpec((TM_MERGE, w), lambda i: (i, 0))
    modspec = lambda j: pl.BlockSpec((None, None, 1, D_MODEL), lambda i: (j, i // per_b, 0, 0))
    col = pl.BlockSpec((8, TM_MERGE), lambda i: (0, i))
    return pl.pallas_call(
        _merge_kernel,
        grid=(ntile,),
        in_specs=[
            row(DIFF_HEADS * 2 * DIFF_DH), row(GLA_HEADS * GLA_DV), row(W_GATE), row(D_MODEL),
            modspec(2), modspec(3), modspec(4),
            full((1, D_MODEL)),
            full((DIFF_HEADS * 2 * DIFF_DH, D_MODEL)), full((GLA_HEADS * GLA_DV, D_MODEL)),
            full((D_MODEL, D_MODEL)),
            full((N_EXPERTS, D_MODEL)), full((N_EXPERTS, 1)),
        ],
        out_specs=[
            row(D_MODEL), row(HALF), col, col, col,
            pl.BlockSpec((N_EXPERTS, LANES), lambda i: (0, 0)),
        ],
        out_shape=[
            jax.ShapeDtypeStruct((N_TOK, D_MODEL), F32),
            jax.ShapeDtypeStruct((N_TOK, HALF), U32),
            jax.ShapeDtypeStruct((8, N_TOK), I32),
            jax.ShapeDtypeStruct((8, N_TOK), I32),
            jax.ShapeDtypeStruct((8, N_TOK), F32),
            jax.ShapeDtypeStruct((N_EXPERTS, LANES), F32),
        ],
        scratch_shapes=[
            pltpu.VMEM((TM_MERGE, TM_MERGE), BF16),
            pltpu.VMEM((N_EXPERTS, LANES), F32),
        ],
        compiler_params=pltpu.CompilerParams(
            dimension_semantics=("arbitrary",), vmem_limit_bytes=VMEM_LIMIT),
        name="merge_route",
    )(o_a, o_b, gates, x2d, mod4, mod4, mod4, norm2_g, w_pa, w_pb, w_o, w_rt, b_r)


def _ffn_kernel(be_ref, nv_ref, x_ref, wgu_ref, bgu_ref, wd_ref, bd_ref, y_ref, wgu_bf, wd_bf):
    i = pl.program_id(0)
    prev = be_ref[jnp.maximum(i - 1, 0)]
    valid = i < nv_ref[0]

    @pl.when(valid & ((i == 0) | (be_ref[i] != prev)))
    def _():
        wgu_bf[...] = wgu_ref[...].astype(BF16)
        wd_bf[...] = wd_ref[...].astype(BF16)

    @pl.when(valid)
    def _():
        lo, hi = _unpack_halves(x_ref[...])
        x = jnp.concatenate([lo, hi], axis=1).astype(BF16)
        gu = jnp.dot(x, wgu_bf[...], preferred_element_type=F32) + bgu_ref[...]
        gate = jnp.minimum(gu[:, :D_FF], SWIGLU_LIMIT)
        up = jnp.clip(gu[:, D_FF:], -SWIGLU_LIMIT, SWIGLU_LIMIT)
        act = (up + 1.0) * (gate * jax.nn.sigmoid(SWIGLU_ALPHA * gate))
        y = jnp.dot(act.astype(BF16), wd_bf[...], preferred_element_type=F32) + bd_ref[...]
        y_ref[...] = _pack_halves(y)

    @pl.when(jnp.logical_not(valid))
    def _():
        y_ref[...] = jnp.zeros_like(y_ref)


def _expert_ffn(blk_expert, n_valid, xb, w_gate_up, b_gate_up, w_down, b_down):
    grid_spec = pltpu.PrefetchScalarGridSpec(
        num_scalar_prefetch=2,
        grid=(N_BLK,),
        in_specs=[
            pl.BlockSpec((BLK, HALF), lambda i, be, nv: (i, 0)),
            pl.BlockSpec((None, D_MODEL, 2 * D_FF), lambda i, be, nv: (be[i], 0, 0)),
            pl.BlockSpec((None, 1, 2 * D_FF), lambda i, be, nv: (be[i], 0, 0)),
            pl.BlockSpec((None, D_FF, D_MODEL), lambda i, be, nv: (be[i], 0, 0)),
            pl.BlockSpec((None, 1, D_MODEL), lambda i, be, nv: (be[i], 0, 0)),
        ],
        out_specs=pl.BlockSpec((BLK, HALF), lambda i, be, nv: (i, 0)),
        scratch_shapes=[
            pltpu.VMEM((D_MODEL, 2 * D_FF), BF16),
            pltpu.VMEM((D_FF, D_MODEL), BF16),
        ],
    )
    return pl.pallas_call(
        _ffn_kernel,
        grid_spec=grid_spec,
        out_shape=jax.ShapeDtypeStruct((P_ROWS, HALF), U32),
        compiler_params=pltpu.CompilerParams(
            dimension_semantics=("arbitrary",), vmem_limit_bytes=VMEM_LIMIT),
        name="expert_ffn",
    )(blk_expert, n_valid, xb, w_gate_up, b_gate_up.reshape(N_EXPERTS, 1, 2 * D_FF),
      w_down, b_down.reshape(N_EXPERTS, 1, D_MODEL))


def _final_kernel(x1_ref, yg_ref, w_ref, gt2_ref, g_ref, o_ref):
    w = w_ref[...]
    ylo = jnp.zeros((TM_FIN, HALF), F32)
    yhi = jnp.zeros((TM_FIN, HALF), F32)
    for k in range(TOP_K):
        lo, hi = _unpack_halves(yg_ref[:, k * HALF:(k + 1) * HALF])
        wk = w[:, k:k + 1]
        ylo = ylo + wk * lo
        yhi = yhi + wk * hi
    y = jnp.concatenate([ylo, yhi], axis=1)
    x2 = x1_ref[...] + gt2_ref[...] * y
    inv = lax.rsqrt(jnp.mean(x2 * x2, axis=-1, keepdims=True) + EPS)
    o_ref[...] = x2 * inv * g_ref[...]


def _final(x1, yg, w4, mod4, final_norm_g):
    per_b = SEQ // TM_FIN
    return pl.pallas_call(
        _final_kernel,
        grid=(N_TOK // TM_FIN,),
        in_specs=[
            pl.BlockSpec((TM_FIN, D_MODEL), lambda i: (i, 0)),
            pl.BlockSpec((TM_FIN, TOP_K * HALF), lambda i: (i, 0)),
            pl.BlockSpec((TM_FIN, TOP_K), lambda i: (i, 0)),
            pl.BlockSpec((None, None, 1, D_MODEL), lambda i: (5, i // per_b, 0, 0)),
            pl.BlockSpec((1, D_MODEL), lambda i: (0, 0)),
        ],
        out_specs=pl.BlockSpec((TM_FIN, D_MODEL), lambda i: (i, 0)),
        out_shape=jax.ShapeDtypeStruct((N_TOK, D_MODEL), F32),
        compiler_params=pltpu.CompilerParams(
            dimension_semantics=("arbitrary",), vmem_limit_bytes=VMEM_LIMIT),
        name="combine_final",
    )(x1, yg, w4, mod4, final_norm_g)


def _lambda_kernel(p_ref, o_ref):
    p = p_ref[...]
    s1 = jnp.sum(p[0:1] * p[1:2], axis=-1, keepdims=True)
    s2 = jnp.sum(p[2:3] * p[3:4], axis=-1, keepdims=True)
    o_ref[...] = jnp.broadcast_to(jnp.exp(s1) - jnp.exp(s2) + LAMBDA_INIT, (1, LANES))


def kernel(x, c, w_ada, b_ada, norm1_g, w_in, lambda_q1, lambda_k1, lambda_q2, lambda_k2, diff_norm_g, w_alpha_up, b_alpha, gla_norm_g, w_branch_diff, w_branch_gla, w_out, norm2_g, w_router, b_router, w_gate_up, b_gate_up, w_down, b_down, final_norm_g):
    w_in0 = w_in[0]
    c_a, c_g = W_A, W_A + W_G
    w_a = w_in0[:, :c_a].astype(BF16)
    w_g = w_in0[:, c_a:c_g].astype(BF16)
    w_lr = jnp.pad(w_in0[:, c_g:c_g + GLA_RANK], ((0, 0), (0, LANES - GLA_RANK))).astype(BF16)
    w_gate = w_in0[:, c_g + GLA_RANK:].astype(BF16)
    w_up = jnp.pad(w_alpha_up[0], ((0, LANES - GLA_RANK), (0, 0)))
    lam_in = jnp.concatenate([lambda_q1, lambda_k1, lambda_q2, lambda_k2], axis=0)
    slopes = jnp.asarray(2.0 ** (-8.0 * np.arange(1, DIFF_HEADS + 1) / DIFF_HEADS), dtype=F32)

    mod = _modulation(c, w_ada[0], b_ada[0])
    mod4 = mod.reshape(N_MOD, BATCH, 1, D_MODEL)
    lam = pl.pallas_call(
        _lambda_kernel, out_shape=jax.ShapeDtypeStruct((1, LANES), F32), name="lambda")(lam_in)[0, :1]

    qkv_a, qkv_g, gates, glr = _in_proj(x, mod4, norm1_g, w_a, w_g, w_gate, w_lr)
    o_a = _diff_attention(qkv_a, slopes, lam, diff_norm_g)
    o_b = _gla(qkv_g, glr, w_up, b_alpha, gla_norm_g)

    x1, h2p, eidx, rank, wts, cnt = _merge_route(
        o_a.reshape(N_TOK, -1), o_b.reshape(N_TOK, -1), gates.reshape(N_TOK, W_GATE),
        x.reshape(N_TOK, D_MODEL), mod4, norm2_g,
        w_branch_diff[0].astype(BF16), w_branch_gla[0].astype(BF16), w_out[0].astype(BF16),
        w_router[0].T, b_router[0].reshape(N_EXPERTS, 1))

    counts = cnt[:, 0].astype(I32)
    padded = ((counts + BLK - 1) // BLK) * BLK
    pad_ends = jnp.cumsum(padded)
    pad_starts = pad_ends - padded
    blk_expert = jnp.minimum(
        jnp.searchsorted(pad_ends, jnp.arange(N_BLK, dtype=I32) * BLK, side='right'),
        N_EXPERTS - 1).astype(I32)
    n_valid = (pad_ends[-1:] // BLK).astype(I32)
    dest = pad_starts[eidx[:TOP_K]] + rank[:TOP_K]
    tok = jnp.broadcast_to(jnp.arange(N_TOK, dtype=I32)[None, :], (TOP_K, N_TOK))
    tok_buf = jnp.zeros((P_ROWS,), I32).at[dest.reshape(-1)].set(tok.reshape(-1))

    xb = jnp.take(h2p, tok_buf, axis=0)
    yb = _expert_ffn(blk_expert, n_valid, xb, w_gate_up[0], b_gate_up[0], w_down[0], b_down[0])
    yg = jnp.take(yb, dest.T.reshape(-1), axis=0).reshape(N_TOK, TOP_K * HALF)

    out = _final(x1, yg, wts[:TOP_K].T, mod4, final_norm_g.reshape(1, D_MODEL))
    return out.reshape(BATCH, SEQ, D_MODEL)
```

```python
import functools
import math

import jax
import jax.numpy as jnp
import numpy as np
from jax import lax
from jax.experimental import pallas as pl
from jax.experimental.pallas import tpu as pltpu
from jax.experimental.pallas import tpu_sc as plsc

F32 = jnp.float32
BF16 = jnp.bfloat16
U32 = jnp.uint32
I32 = jnp.int32

D_MODEL = 1024
BATCH = 16
SEQ = 2048
N_TOK = BATCH * SEQ
CHUNK = 64
DIFF_HEADS = 4
DIFF_DH = 64
GLA_HEADS = 4
GLA_DK = 64
GLA_DV = 128
GLA_RANK = 16
GLA_GATE_NORM = 16.0
N_EXPERTS = 32
TOP_K = 4
D_FF = D_MODEL
SWIGLU_LIMIT = 7.0
SWIGLU_ALPHA = 1.702
N_MOD = 6
EPS = 1e-6
LAMBDA_INIT = 0.8 - 0.6 * math.exp(-0.3 * 0)

LANES = 128
HALF = D_MODEL // 2

TM_IN = 512
TQ = 256
TM_MERGE = 512
BLK = 256
N_BLK = (N_TOK * TOP_K) // BLK + N_EXPERTS
P_ROWS = N_BLK * BLK
TM_FIN = 512
VMEM_LIMIT = 56 * 1024 * 1024


def _nt_dot(a, b):
    return lax.dot_general(a, b, (((1,), (1,)), ((), ())), preferred_element_type=F32)


def _tn_dot(a, b):
    return lax.dot_general(a, b, (((0,), (0,)), ((), ())), preferred_element_type=F32)


def _split3(x):
    hi = x.astype(BF16)
    r1 = x - hi.astype(F32)
    mid = r1.astype(BF16)
    lo = (r1 - mid.astype(F32)).astype(BF16)
    return hi, mid, lo


def _pack_halves(y):
    a = pltpu.bitcast(y[:, :HALF].astype(BF16).astype(F32), U32)
    b = pltpu.bitcast(y[:, HALF:].astype(BF16).astype(F32), U32)
    return (a >> 16) | b


def _unpack_halves(u):
    lo = pltpu.bitcast(u << 16, F32)
    hi = pltpu.bitcast(u & jnp.uint32(0xFFFF0000), F32)
    return lo, hi


def _mod_kernel(c_ref, w_ref, b_ref, o_ref):
    c = c_ref[...]
    s = c * jax.nn.sigmoid(c)
    o_ref[0] = jnp.dot(s.astype(BF16), w_ref[...].astype(BF16),
                       preferred_element_type=F32) + b_ref[...]


def _modulation(c, w_ada, b_ada):
    return pl.pallas_call(
        _mod_kernel,
        grid=(N_MOD,),
        in_specs=[
            pl.BlockSpec((BATCH, D_MODEL), lambda j: (0, 0)),
            pl.BlockSpec((D_MODEL, D_MODEL), lambda j: (0, j)),
            pl.BlockSpec((1, D_MODEL), lambda j: (0, j)),
        ],
        out_specs=pl.BlockSpec((1, BATCH, D_MODEL), lambda j: (j, 0, 0)),
        out_shape=jax.ShapeDtypeStruct((N_MOD, BATCH, D_MODEL), F32),
        compiler_params=pltpu.CompilerParams(dimension_semantics=("arbitrary",)),
        name="adaln_mod",
    )(c, w_ada, b_ada.reshape(1, N_MOD * D_MODEL))


W_A = 3 * DIFF_HEADS * 2 * DIFF_DH
W_G = 2 * GLA_HEADS * GLA_DK + 2 * GLA_HEADS * GLA_DV
W_GATE = 2 * D_MODEL


def _in_kernel(x_ref, sh_ref, sc_ref, g_ref, wa_ref, wg_ref, wgate_ref, wlr_ref,
               oa_ref, og_ref, ogate_ref, olr_ref):
    x = x_ref[...]
    inv = lax.rsqrt(jnp.mean(x * x, axis=-1, keepdims=True) + EPS)
    h = (x * inv * g_ref[...]) * (1.0 + sc_ref[...]) + sh_ref[...]
    hb = h.astype(BF16)
    oa_ref[...] = jnp.dot(hb, wa_ref[...], preferred_element_type=F32).astype(BF16)
    og_ref[...] = jnp.dot(hb, wg_ref[...], preferred_element_type=F32).astype(BF16)
    ogate_ref[...] = jnp.dot(hb, wgate_ref[...], preferred_element_type=F32).astype(BF16)
    olr_ref[...] = jnp.dot(hb, wlr_ref[...], preferred_element_type=F32)


def _in_proj(x, mod4, norm1_g, w_a, w_g, w_gate, w_lr):
    nrow = SEQ // TM_IN
    full = lambda shape: pl.BlockSpec(shape, lambda b, i: (0,) * len(shape))
    return pl.pallas_call(
        _in_kernel,
        grid=(BATCH, nrow),
        in_specs=[
            pl.BlockSpec((None, TM_IN, D_MODEL), lambda b, i: (b, i, 0)),
            pl.BlockSpec((None, None, 1, D_MODEL), lambda b, i: (0, b, 0, 0)),
            pl.BlockSpec((None, None, 1, D_MODEL), lambda b, i: (1, b, 0, 0)),
            full((1, D_MODEL)),
            full((D_MODEL, W_A)),
            full((D_MODEL, W_G)),
            full((D_MODEL, W_GATE)),
            full((D_MODEL, LANES)),
        ],
        out_specs=[
            pl.BlockSpec((None, TM_IN, W_A), lambda b, i: (b, i, 0)),
            pl.BlockSpec((None, TM_IN, W_G), lambda b, i: (b, i, 0)),
            pl.BlockSpec((None, TM_IN, W_GATE), lambda b, i: (b, i, 0)),
            pl.BlockSpec((None, TM_IN, LANES), lambda b, i: (b, i, 0)),
        ],
        out_shape=[
            jax.ShapeDtypeStruct((BATCH, SEQ, W_A), BF16),
            jax.ShapeDtypeStruct((BATCH, SEQ, W_G), BF16),
            jax.ShapeDtypeStruct((BATCH, SEQ, W_GATE), BF16),
            jax.ShapeDtypeStruct((BATCH, SEQ, LANES), F32),
        ],
        compiler_params=pltpu.CompilerParams(
            dimension_semantics=("arbitrary", "arbitrary"), vmem_limit_bytes=VMEM_LIMIT),
        name="in_proj",
    )(x, mod4, mod4, norm1_g, w_a, w_g, w_gate, w_lr)


def _attn_kernel(slope_ref, lam_ref, q_ref, k_ref, v_ref, g_ref, o_ref):
    h = pl.program_id(1)
    qi = pl.program_id(2)
    slope = slope_ref[h]
    lam = lam_ref[0]
    q = q_ref[...] * jnp.asarray(DIFF_DH ** -0.5, BF16)
    lane = lax.broadcasted_iota(I32, (1, 2 * DIFF_DH), 1)
    qa = jnp.where(lane < DIFF_DH, q, jnp.zeros_like(q))
    qb = jnp.where(lane >= DIFF_DH, q, jnp.zeros_like(q))
    qpos = qi * TQ + lax.broadcasted_iota(I32, (TQ, 1), 0)

    def block(j, carry):
        m1, l1, a1, m2, l2, a2 = carry
        off = pl.multiple_of(j * TQ, TQ)
        k = k_ref[pl.ds(off, TQ), :]
        v = v_ref[pl.ds(off, TQ), :]
        kpos = j * TQ + lax.broadcasted_iota(I32, (1, TQ), 1)
        allowed = (qpos >> 6) >= (kpos >> 6)
        bias = -slope * jnp.abs(qpos - kpos).astype(F32)

        def upd(qq, m, l, a):
            s = jnp.where(allowed, _nt_dot(qq, k) + bias, -jnp.inf)
            m_new = jnp.maximum(m, jnp.max(s, axis=-1, keepdims=True))
            alpha = jnp.exp(m - m_new)
            p = jnp.exp(s - m_new)
            l_new = alpha * l + jnp.sum(p, axis=-1, keepdims=True)
            a_new = alpha * a + jnp.dot(p.astype(BF16), v, preferred_element_type=F32)
            return m_new, l_new, a_new

        m1, l1, a1 = upd(qa, m1, l1, a1)
        m2, l2, a2 = upd(qb, m2, l2, a2)
        return m1, l1, a1, m2, l2, a2

    neg = jnp.full((TQ, 1), -jnp.inf, F32)
    zl = jnp.zeros((TQ, 1), F32)
    za = jnp.zeros((TQ, 2 * DIFF_DH), F32)
    m1, l1, a1, m2, l2, a2 = lax.fori_loop(0, qi + 1, block, (neg, zl, za, neg, zl, za))
    o = a1 / l1 - lam * (a2 / l2)
    inv = lax.rsqrt(jnp.mean(o * o, axis=-1, keepdims=True) + EPS)
    o_ref[...] = (o * inv * g_ref[...] * (1.0 - LAMBDA_INIT)).astype(BF16)


def _diff_attention(qkv_a, slopes, lam, diff_norm_g):
    nq = SEQ // TQ
    return pl.pallas_call(
        _attn_kernel,
        grid=(BATCH, DIFF_HEADS, nq),
        in_specs=[
            pl.BlockSpec(memory_space=pltpu.SMEM),
            pl.BlockSpec(memory_space=pltpu.SMEM),
            pl.BlockSpec((None, TQ, LANES), lambda b, h, i: (b, i, h)),
            pl.BlockSpec((None, SEQ, LANES), lambda b, h, i: (b, 0, DIFF_HEADS + h)),
            pl.BlockSpec((None, SEQ, LANES), lambda b, h, i: (b, 0, 2 * DIFF_HEADS + h)),
            pl.BlockSpec((1, LANES), lambda b, h, i: (0, 0)),
        ],
        out_specs=pl.BlockSpec((None, TQ, LANES), lambda b, h, i: (b, i, h)),
        out_shape=jax.ShapeDtypeStruct((BATCH, SEQ, DIFF_HEADS * 2 * DIFF_DH), BF16),
        compiler_params=pltpu.CompilerParams(
            dimension_semantics=("arbitrary", "arbitrary", "arbitrary")),
        name="diff_attn",
    )(slopes, lam, qkv_a, qkv_a, qkv_a, diff_norm_g)


N_CHUNK = SEQ // CHUNK
PAIR = 2 * GLA_DK
PAIR_V = 2 * GLA_DV
CS_ROWS = 256


def _gla_kernel(q_ref, k_ref, v_ref, r_ref, lr_ref, wup_ref, bup_ref, g_ref, o_ref,
                gcum_ref, state_ref):
    w_hi, w_mid, w_lo = _split3(wup_ref[...])
    rr = lax.broadcasted_iota(I32, (CS_ROWS, CS_ROWS), 0)
    cc = lax.broadcasted_iota(I32, (CS_ROWS, CS_ROWS), 1)
    tri = jnp.where(((rr >> 6) == (cc >> 6)) & (cc <= rr), 1.0, 0.0).astype(BF16)
    for blk in range(SEQ // CS_ROWS):
        rows = pl.ds(blk * CS_ROWS, CS_ROWS)
        a_hi, a_mid, a_lo = _split3(lr_ref[rows, :])
        z = (jnp.dot(a_hi, w_hi, preferred_element_type=F32)
             + jnp.dot(a_hi, w_mid, preferred_element_type=F32)
             + jnp.dot(a_mid, w_hi, preferred_element_type=F32)
             + jnp.dot(a_hi, w_lo, preferred_element_type=F32)
             + jnp.dot(a_lo, w_hi, preferred_element_type=F32)
             + jnp.dot(a_mid, w_mid, preferred_element_type=F32)) + bup_ref[...]
        la = (jnp.minimum(z, 0.0) - jnp.log(1.0 + jnp.exp(-jnp.abs(z)))) * (1.0 / GLA_GATE_NORM)
        l_hi, l_mid, l_lo = _split3(la)
        gcum_ref[rows, :] = (jnp.dot(tri, l_hi, preferred_element_type=F32)
                             + jnp.dot(tri, l_mid, preferred_element_type=F32)
                             + jnp.dot(tri, l_lo, preferred_element_type=F32))

    state_ref[...] = jnp.zeros_like(state_ref)
    lane_k = lax.broadcasted_iota(I32, (1, PAIR), 1)
    row_v = lax.broadcasted_iota(I32, (PAIR_V, PAIR), 0)
    col_k = lax.broadcasted_iota(I32, (PAIR_V, PAIR), 1)
    same_head = (row_v >= GLA_DV) == (col_k >= GLA_DK)
    cr = lax.broadcasted_iota(I32, (CHUNK, CHUNK), 0)
    cs = lax.broadcasted_iota(I32, (CHUNK, CHUNK), 1)
    causal = cs <= cr
    scale = GLA_DK ** -0.5

    def chunk(n, _):
        rows = pl.ds(pl.multiple_of(n * CHUNK, CHUNK), CHUNK)
        gc = gcum_ref[rows, :]
        g_last = gcum_ref[pl.ds(n * CHUNK + CHUNK - 1, 1), :]
        qf = q_ref[rows, :].astype(F32) * scale
        kf = k_ref[rows, :].astype(F32)
        q_s = (qf * jnp.exp(gc)).astype(BF16)
        k_s = (kf * jnp.exp(-gc)).astype(BF16)
        k_d = (kf * jnp.exp(g_last - gc)).astype(BF16)
        decay = jnp.exp(g_last)
        for pr in range(GLA_HEADS // 2):
            kl = slice(pr * PAIR, (pr + 1) * PAIR)
            vl = slice(pr * PAIR_V, (pr + 1) * PAIR_V)
            qs_p, ks_p, kd_p = q_s[:, kl], k_s[:, kl], k_d[:, kl]
            v_p = v_ref[rows, vl]
            st = state_ref[pr]
            o_inter = _nt_dot(qs_p, st.astype(BF16))
            d_st = _tn_dot(v_p, kd_p)
            state_ref[pr] = st * decay[:, kl] + jnp.where(same_head, d_st, 0.0)
            for sub in range(2):
                hd = 2 * pr + sub
                in_head = (lane_k >= sub * GLA_DK) & (lane_k < (sub + 1) * GLA_DK)
                a = _nt_dot(jnp.where(in_head, qs_p, jnp.zeros_like(qs_p)), ks_p)
                a = jnp.where(causal, a, 0.0).astype(BF16)
                vs = slice(hd * GLA_DV, (hd + 1) * GLA_DV)
                o = (jnp.dot(a, v_ref[rows, vs], preferred_element_type=F32)
                     + o_inter[:, sub * GLA_DV:(sub + 1) * GLA_DV])
                inv = lax.rsqrt(jnp.mean(o * o, axis=-1, keepdims=True) + EPS)
                r = r_ref[rows, vs].astype(F32)
                o_ref[rows, vs] = (o * inv * g_ref[...] * (r * jax.nn.sigmoid(r))).astype(BF16)
        return 0

    lax.fori_loop(0, N_CHUNK, chunk, 0)


def _gla(qkv_g, glr, w_up, b_up, gla_norm_g):
    qk_w = GLA_HEADS * GLA_DK
    v_w = GLA_HEADS * GLA_DV
    return pl.pallas_call(
        _gla_kernel,
        grid=(BATCH,),
        in_specs=[
            pl.BlockSpec((None, SEQ, qk_w), lambda b: (b, 0, 0)),
            pl.BlockSpec((None, SEQ, qk_w), lambda b: (b, 0, 1)),
            pl.BlockSpec((None, SEQ, v_w), lambda b: (b, 0, 1)),
            pl.BlockSpec((None, SEQ, v_w), lambda b: (b, 0, 2)),
            pl.BlockSpec((None, SEQ, LANES), lambda b: (b, 0, 0)),
            pl.BlockSpec((LANES, qk_w), lambda b: (0, 0)),
            pl.BlockSpec((1, qk_w), lambda b: (0, 0)),
            pl.BlockSpec((1, GLA_DV), lambda b: (0, 0)),
        ],
        out_specs=pl.BlockSpec((None, SEQ, v_w), lambda b: (b, 0, 0)),
        out_shape=jax.ShapeDtypeStruct((BATCH, SEQ, v_w), BF16),
        scratch_shapes=[
            pltpu.VMEM((SEQ, qk_w), F32),
            pltpu.VMEM((GLA_HEADS // 2, PAIR_V, PAIR), F32),
        ],
        compiler_params=pltpu.CompilerParams(
            dimension_semantics=("arbitrary",), vmem_limit_bytes=VMEM_LIMIT),
        name="gla",
    )(qkv_g, qkv_g, qkv_g, qkv_g, glr, w_up, b_up, gla_norm_g)


def _merge_kernel(oa_ref, ob_ref, gate_ref, x_ref, gt1_ref, sh2_ref, sc2_ref, g2_ref,
                  wpa_ref, wpb_ref, wo_ref, wr_ref, br_ref,
                  x1_ref, h2p_ref, eidx_ref, rank_ref, wts_ref, cnt_ref,
                  upper_ref, carry_ref):
    i = pl.program_id(0)

    @pl.when(i == 0)
    def _():
        rr = lax.broadcasted_iota(I32, (TM_MERGE, TM_MERGE), 0)
        cc = lax.broadcasted_iota(I32, (TM_MERGE, TM_MERGE), 1)
        upper_ref[...] = jnp.where(rr < cc, 1.0, 0.0).astype(BF16)
        carry_ref[...] = jnp.zeros_like(carry_ref)

    ga = gate_ref[:, :D_MODEL].astype(F32)
    gb = gate_ref[:, D_MODEL:].astype(F32)
    merged = (jax.nn.sigmoid(ga) * jnp.dot(oa_ref[...], wpa_ref[...], preferred_element_type=F32)
              + jax.nn.sigmoid(gb) * jnp.dot(ob_ref[...], wpb_ref[...], preferred_element_type=F32))
    y = jnp.dot(merged.astype(BF16), wo_ref[...], preferred_element_type=F32)
    x1 = x_ref[...] + gt1_ref[...] * y
    x1_ref[...] = x1
    inv = lax.rsqrt(jnp.mean(x1 * x1, axis=-1, keepdims=True) + EPS)
    h2 = (x1 * inv * g2_ref[...]) * (1.0 + sc2_ref[...]) + sh2_ref[...]
    h2p_ref[...] = _pack_halves(h2)

    h_hi, h_mid, _ = _split3(h2)
    w_hi, w_mid, _ = _split3(wr_ref[...])
    logits = (_nt_dot(w_hi, h_hi) + _nt_dot(w_hi, h_mid) + _nt_dot(w_mid, h_hi)) + br_ref[...]

    eio = lax.broadcasted_iota(I32, (N_EXPERTS, TM_MERGE), 0)
    vals, idxs, sels = [], [], []
    cur = logits
    for _k in range(TOP_K):
        m = jnp.max(cur, axis=0, keepdims=True)
        idx = jnp.min(jnp.where(cur == m, eio, N_EXPERTS), axis=0, keepdims=True)
        sel = eio == idx
        vals.append(m)
        idxs.append(idx)
        sels.append(sel)
        cur = jnp.where(sel, -jnp.inf, cur)
    es = [jnp.exp(v - vals[0]) for v in vals]
    tot = es[0] + es[1] + es[2] + es[3]
    onehot = jnp.zeros((N_EXPERTS, TM_MERGE), F32)
    for sel in sels:
        onehot = onehot + jnp.where(sel, 1.0, 0.0)
    before = jnp.dot(onehot.astype(BF16), upper_ref[...], preferred_element_type=F32) + carry_ref[:, 0:1]
    ranks = [jnp.sum(jnp.where(sel, before, 0.0), axis=0, keepdims=True) for sel in sels]
    carry_ref[...] = carry_ref[...] + jnp.sum(onehot, axis=1, keepdims=True)
    cnt_ref[...] = carry_ref[...]

    zi = jnp.zeros((8 - TOP_K, TM_MERGE), I32)
    zf = jnp.zeros((8 - TOP_K, TM_MERGE), F32)
    eidx_ref[...] = jnp.concatenate(idxs + [zi], axis=0)
    rank_ref[...] = jnp.concatenate([r.astype(I32) for r in ranks] + [zi], axis=0)
    wts_ref[...] = jnp.concatenate([e / tot for e in es] + [zf], axis=0)


def _merge_route(o_a, o_b, gates, x2d, mod4, norm2_g, w_pa, w_pb, w_o, w_rt, b_r):
    ntile = N_TOK // TM_MERGE
    per_b = SEQ // TM_MERGE
    full = lambda shape: pl.BlockSpec(shape, lambda i: (0,) * len(shape))
    row = lambda w: pl.BlockSpec((TM_MERGE, w), lambda i: (i, 0))
    modspec = lambda j: pl.BlockSpec((None, None, 1, D_MODEL), lambda i: (j, i // per_b, 0, 0))
    col = pl.BlockSpec((8, TM_MERGE), lambda i: (0, i))
    return pl.pallas_call(
        _merge_kernel,
        grid=(ntile,),
        in_specs=[
            row(DIFF_HEADS * 2 * DIFF_DH), row(GLA_HEADS * GLA_DV), row(W_GATE), row(D_MODEL),
            modspec(2), modspec(3), modspec(4),
            full((1, D_MODEL)),
            full((DIFF_HEADS * 2 * DIFF_DH, D_MODEL)), full((GLA_HEADS * GLA_DV, D_MODEL)),
            full((D_MODEL, D_MODEL)),
            full((N_EXPERTS, D_MODEL)), full((N_EXPERTS, 1)),
        ],
        out_specs=[
            row(D_MODEL), row(HALF), col, col, col,
            pl.BlockSpec((N_EXPERTS, LANES), lambda i: (0, 0)),
        ],
        out_shape=[
            jax.ShapeDtypeStruct((N_TOK, D_MODEL), F32),
            jax.ShapeDtypeStruct((N_TOK, HALF), U32),
            jax.ShapeDtypeStruct((8, N_TOK), I32),
            jax.ShapeDtypeStruct((8, N_TOK), I32),
            jax.ShapeDtypeStruct((8, N_TOK), F32),
            jax.ShapeDtypeStruct((N_EXPERTS, LANES), F32),
        ],
        scratch_shapes=[
            pltpu.VMEM((TM_MERGE, TM_MERGE), BF16),
            pltpu.VMEM((N_EXPERTS, LANES), F32),
        ],
        compiler_params=pltpu.CompilerParams(
            dimension_semantics=("arbitrary",), vmem_limit_bytes=VMEM_LIMIT),
        name="merge_route",
    )(o_a, o_b, gates, x2d, mod4, mod4, mod4, norm2_g, w_pa, w_pb, w_o, w_rt, b_r)


def _ffn_kernel(be_ref, nv_ref, x_ref, wgu_ref, bgu_ref, wd_ref, bd_ref, y_ref, wgu_bf, wd_bf):
    i = pl.program_id(0)
    prev = be_ref[jnp.maximum(i - 1, 0)]
    valid = i < nv_ref[0]

    @pl.when(valid & ((i == 0) | (be_ref[i] != prev)))
    def _():
        wgu_bf[...] = wgu_ref[...].astype(BF16)
        wd_bf[...] = wd_ref[...].astype(BF16)

    @pl.when(valid)
    def _():
        lo, hi = _unpack_halves(x_ref[...])
        x = jnp.concatenate([lo, hi], axis=1).astype(BF16)
        gu = jnp.dot(x, wgu_bf[...], preferred_element_type=F32) + bgu_ref[...]
        gate = jnp.minimum(gu[:, :D_FF], SWIGLU_LIMIT)
        up = jnp.clip(gu[:, D_FF:], -SWIGLU_LIMIT, SWIGLU_LIMIT)
        act = (up + 1.0) * (gate * jax.nn.sigmoid(SWIGLU_ALPHA * gate))
        y = jnp.dot(act.astype(BF16), wd_bf[...], preferred_element_type=F32) + bd_ref[...]
        y_ref[...] = _pack_halves(y)

    @pl.when(jnp.logical_not(valid))
    def _():
        y_ref[...] = jnp.zeros_like(y_ref)


def _expert_ffn(blk_expert, n_valid, xb, w_gate_up, b_gate_up, w_down, b_down):
    grid_spec = pltpu.PrefetchScalarGridSpec(
        num_scalar_prefetch=2,
        grid=(N_BLK,),
        in_specs=[
            pl.BlockSpec((BLK, HALF), lambda i, be, nv: (i, 0)),
            pl.BlockSpec((None, D_MODEL, 2 * D_FF), lambda i, be, nv: (be[i], 0, 0)),
            pl.BlockSpec((None, 1, 2 * D_FF), lambda i, be, nv: (be[i], 0, 0)),
            pl.BlockSpec((None, D_FF, D_MODEL), lambda i, be, nv: (be[i], 0, 0)),
            pl.BlockSpec((None, 1, D_MODEL), lambda i, be, nv: (be[i], 0, 0)),
        ],
        out_specs=pl.BlockSpec((BLK, HALF), lambda i, be, nv: (i, 0)),
        scratch_shapes=[
            pltpu.VMEM((D_MODEL, 2 * D_FF), BF16),
            pltpu.VMEM((D_FF, D_MODEL), BF16),
        ],
    )
    return pl.pallas_call(
        _ffn_kernel,
        grid_spec=grid_spec,
        out_shape=jax.ShapeDtypeStruct((P_ROWS, HALF), U32),
        compiler_params=pltpu.CompilerParams(
            dimension_semantics=("arbitrary",), vmem_limit_bytes=VMEM_LIMIT),
        name="expert_ffn",
    )(blk_expert, n_valid, xb, w_gate_up, b_gate_up.reshape(N_EXPERTS, 1, 2 * D_FF),
      w_down, b_down.reshape(N_EXPERTS, 1, D_MODEL))


def _final_kernel(x1_ref, y0_ref, y1_ref, y2_ref, y3_ref, w_ref, gt2_ref, g_ref, o_ref):
    w = w_ref[...]
    ylo = jnp.zeros((TM_FIN, HALF), F32)
    yhi = jnp.zeros((TM_FIN, HALF), F32)
    for k, y_ref in enumerate((y0_ref, y1_ref, y2_ref, y3_ref)):
        lo, hi = _unpack_halves(y_ref[...])
        wk = w[:, k:k + 1]
        ylo = ylo + wk * lo
        yhi = yhi + wk * hi
    y = jnp.concatenate([ylo, yhi], axis=1)
    x2 = x1_ref[...] + gt2_ref[...] * y
    inv = lax.rsqrt(jnp.mean(x2 * x2, axis=-1, keepdims=True) + EPS)
    o_ref[...] = x2 * inv * g_ref[...]


def _final(x1, yg, w4, mod4, final_norm_g):
    per_b = SEQ // TM_FIN
    ntile = N_TOK // TM_FIN
    slot = lambda k: pl.BlockSpec((TM_FIN, HALF), lambda i: (k * ntile + i, 0))
    return pl.pallas_call(
        _final_kernel,
        grid=(ntile,),
        in_specs=[
            pl.BlockSpec((TM_FIN, D_MODEL), lambda i: (i, 0)),
            slot(0), slot(1), slot(2), slot(3),
            pl.BlockSpec((TM_FIN, TOP_K), lambda i: (i, 0)),
            pl.BlockSpec((None, None, 1, D_MODEL), lambda i: (5, i // per_b, 0, 0)),
            pl.BlockSpec((1, D_MODEL), lambda i: (0, 0)),
        ],
        out_specs=pl.BlockSpec((TM_FIN, D_MODEL), lambda i: (i, 0)),
        out_shape=jax.ShapeDtypeStruct((N_TOK, D_MODEL), F32),
        compiler_params=pltpu.CompilerParams(
            dimension_semantics=("arbitrary",), vmem_limit_bytes=VMEM_LIMIT),
        name="combine_final",
    )(x1, yg, yg, yg, yg, w4, mod4, final_norm_g)


SC_CORES = 2
SC_SUBCORES = 16
SC_WORKERS = SC_CORES * SC_SUBCORES
SC_CHUNK = 64


def _sc_mesh():
    return plsc.VectorSubcoreMesh(core_axis_name="c", subcore_axis_name="s")


def _dispatch_rows(h2p, dest_flat):
    per_w = N_TOK // SC_WORKERS
    nchunk = per_w // SC_CHUNK

    @functools.partial(
        pl.kernel, mesh=_sc_mesh(), out_type=jax.ShapeDtypeStruct((P_ROWS, HALF), U32),
        scratch_types=[pltpu.VMEM((SC_CHUNK,), I32)] * TOP_K
        + [pltpu.VMEM((SC_CHUNK, HALF), U32), pltpu.SemaphoreType.DMA],
        name="moe_dispatch")
    def k(src_hbm, dest_hbm, out_hbm, i0, i1, i2, i3, rows_v, sem):
        idx = (i0, i1, i2, i3)
        wid = lax.axis_index("s") * SC_CORES + lax.axis_index("c")

        @pl.loop(0, nchunk)
        def _(j):
            t0 = wid * per_w + j * SC_CHUNK
            pltpu.sync_copy(src_hbm.at[pl.ds(t0, SC_CHUNK)], rows_v)
            for kk in range(TOP_K):
                pltpu.sync_copy(dest_hbm.at[pl.ds(kk * N_TOK + t0, SC_CHUNK)], idx[kk])
            for kk in range(TOP_K):
                pltpu.sync_copy(rows_v, out_hbm.at[idx[kk]])

    return k(h2p, dest_flat)


def _combine_rows(yb, dest_flat):
    n_out = TOP_K * N_TOK
    per_w = n_out // SC_WORKERS
    nchunk = per_w // SC_CHUNK

    @functools.partial(
        pl.kernel, mesh=_sc_mesh(), out_type=jax.ShapeDtypeStruct((n_out, HALF), U32),
        scratch_types=[pltpu.VMEM((SC_CHUNK,), I32), pltpu.VMEM((SC_CHUNK, HALF), U32),
                       pltpu.SemaphoreType.DMA],
        name="moe_combine")
    def k(tab_hbm, idx_hbm, out_hbm, idx_v, rows_v, sem):
        wid = lax.axis_index("s") * SC_CORES + lax.axis_index("c")

        @pl.loop(0, nchunk)
        def _(j):
            a0 = wid * per_w + j * SC_CHUNK
            pltpu.sync_copy(idx_hbm.at[pl.ds(a0, SC_CHUNK)], idx_v)
            pltpu.sync_copy(tab_hbm.at[idx_v], rows_v)
            pltpu.sync_copy(rows_v, out_hbm.at[pl.ds(a0, SC_CHUNK)])

    return k(yb, dest_flat)


def _lambda_kernel(p_ref, o_ref):
    p = p_ref[...]
    s1 = jnp.sum(p[0:1] * p[1:2], axis=-1, keepdims=True)
    s2 = jnp.sum(p[2:3] * p[3:4], axis=-1, keepdims=True)
    o_ref[...] = jnp.broadcast_to(jnp.exp(s1) - jnp.exp(s2) + LAMBDA_INIT, (1, LANES))


def kernel(x, c, w_ada, b_ada, norm1_g, w_in, lambda_q1, lambda_k1, lambda_q2, lambda_k2, diff_norm_g, w_alpha_up, b_alpha, gla_norm_g, w_branch_diff, w_branch_gla, w_out, norm2_g, w_router, b_router, w_gate_up, b_gate_up, w_down, b_down, final_norm_g):
    w_in0 = w_in[0]
    c_a, c_g = W_A, W_A + W_G
    w_a = w_in0[:, :c_a].astype(BF16)
    w_g = w_in0[:, c_a:c_g].astype(BF16)
    w_lr = jnp.pad(w_in0[:, c_g:c_g + GLA_RANK], ((0, 0), (0, LANES - GLA_RANK))).astype(BF16)
    w_gate = w_in0[:, c_g + GLA_RANK:].astype(BF16)
    w_up = jnp.pad(w_alpha_up[0], ((0, LANES - GLA_RANK), (0, 0)))
    lam_in = jnp.concatenate([lambda_q1, lambda_k1, lambda_q2, lambda_k2], axis=0)
    slopes = jnp.asarray(2.0 ** (-8.0 * np.arange(1, DIFF_HEADS + 1) / DIFF_HEADS), dtype=F32)

    mod = _modulation(c, w_ada[0], b_ada[0])
    mod4 = mod.reshape(N_MOD, BATCH, 1, D_MODEL)
    lam = pl.pallas_call(
        _lambda_kernel, out_shape=jax.ShapeDtypeStruct((1, LANES), F32), name="lambda")(lam_in)[0, :1]

    qkv_a, qkv_g, gates, glr = _in_proj(x, mod4, norm1_g, w_a, w_g, w_gate, w_lr)
    o_a = _diff_attention(qkv_a, slopes, lam, diff_norm_g)
    o_b = _gla(qkv_g, glr, w_up, b_alpha, gla_norm_g)

    x1, h2p, eidx, rank, wts, cnt = _merge_route(
        o_a.reshape(N_TOK, -1), o_b.reshape(N_TOK, -1), gates.reshape(N_TOK, W_GATE),
        x.reshape(N_TOK, D_MODEL), mod4, norm2_g,
        w_branch_diff[0].astype(BF16), w_branch_gla[0].astype(BF16), w_out[0].astype(BF16),
        w_router[0].T, b_router[0].reshape(N_EXPERTS, 1))

    counts = cnt[:, 0].astype(I32)
    padded = ((counts + BLK - 1) // BLK) * BLK
    pad_ends = jnp.cumsum(padded)
    pad_starts = pad_ends - padded
    blk_start = jnp.arange(N_BLK, dtype=I32) * BLK
    blk_expert = jnp.minimum(
        jnp.sum((pad_ends[None, :] <= blk_start[:, None]).astype(I32), axis=1), N_EXPERTS - 1)
    n_valid = (pad_ends[-1:] // BLK).astype(I32)
    dest = (pad_starts[eidx[:TOP_K]] + rank[:TOP_K]).reshape(-1)

    xb = _dispatch_rows(h2p, dest)
    yb = _expert_ffn(blk_expert, n_valid, xb, w_gate_up[0], b_gate_up[0], w_down[0], b_down[0])
    yg = _combine_rows(yb, dest)

    out = _final(x1, yg, wts[:TOP_K].T, mod4, final_norm_g.reshape(1, D_MODEL))
    return out.reshape(BATCH, SEQ, D_MODEL)
```

```python
import functools
import math

import jax
import jax.numpy as jnp
import numpy as np
from jax import lax
from jax.experimental import pallas as pl
from jax.experimental.pallas import tpu as pltpu
from jax.experimental.pallas import tpu_sc as plsc

F32 = jnp.float32
BF16 = jnp.bfloat16
U32 = jnp.uint32
I32 = jnp.int32

D_MODEL = 1024
BATCH = 16
SEQ = 2048
N_TOK = BATCH * SEQ
CHUNK = 64
DIFF_HEADS = 4
DIFF_DH = 64
GLA_HEADS = 4
GLA_DK = 64
GLA_DV = 128
GLA_RANK = 16
GLA_GATE_NORM = 16.0
N_EXPERTS = 32
TOP_K = 4
D_FF = D_MODEL
SWIGLU_LIMIT = 7.0
SWIGLU_ALPHA = 1.702
N_MOD = 6
EPS = 1e-6
LAMBDA_INIT = 0.8 - 0.6 * math.exp(-0.3 * 0)

LANES = 128
HALF = D_MODEL // 2
ROW_W = HALF
ROW_DT = U32

TM_IN = 512
TQ = 512
TM_MERGE = 512
BLK = 256
N_BLK = (N_TOK * TOP_K) // BLK + N_EXPERTS
P_ROWS = N_BLK * BLK
TM_FIN = 512
VMEM_LIMIT = 56 * 1024 * 1024


def _nt_dot(a, b):
    return lax.dot_general(a, b, (((1,), (1,)), ((), ())), preferred_element_type=F32)


def _tn_dot(a, b):
    return lax.dot_general(a, b, (((0,), (0,)), ((), ())), preferred_element_type=F32)


def _split3(x):
    hi = x.astype(BF16)
    r1 = x - hi.astype(F32)
    mid = r1.astype(BF16)
    lo = (r1 - mid.astype(F32)).astype(BF16)
    return hi, mid, lo


def _pack_halves(y):
    return pltpu.pack_elementwise([y[:, :HALF], y[:, HALF:]], packed_dtype=BF16)


def _unpack_halves(u):
    lo = pltpu.unpack_elementwise(u, index=0, packed_dtype=BF16, unpacked_dtype=F32)
    hi = pltpu.unpack_elementwise(u, index=1, packed_dtype=BF16, unpacked_dtype=F32)
    return lo, hi


def _mod_kernel(c_ref, w_ref, b_ref, o_ref):
    c = c_ref[...]
    s = c * jax.nn.sigmoid(c)
    o_ref[0] = jnp.dot(s.astype(BF16), w_ref[...].astype(BF16),
                       preferred_element_type=F32) + b_ref[...]


def _modulation(c, w_ada, b_ada):
    return pl.pallas_call(
        _mod_kernel,
        grid=(N_MOD,),
        in_specs=[
            pl.BlockSpec((BATCH, D_MODEL), lambda j: (0, 0)),
            pl.BlockSpec((D_MODEL, D_MODEL), lambda j: (0, j)),
            pl.BlockSpec((1, D_MODEL), lambda j: (0, j)),
        ],
        out_specs=pl.BlockSpec((1, BATCH, D_MODEL), lambda j: (j, 0, 0)),
        out_shape=jax.ShapeDtypeStruct((N_MOD, BATCH, D_MODEL), F32),
        compiler_params=pltpu.CompilerParams(dimension_semantics=("arbitrary",)),
        name="adaln_mod",
    )(c, w_ada, b_ada.reshape(1, N_MOD * D_MODEL))


W_A = 3 * DIFF_HEADS * 2 * DIFF_DH
W_G = 2 * GLA_HEADS * GLA_DK + 2 * GLA_HEADS * GLA_DV
W_GATE = 2 * D_MODEL


def _in_kernel(x_ref, sh_ref, sc_ref, g_ref, wa_ref, wg_ref, wgate_ref, wlr_ref,
               oa_ref, og_ref, ogate_ref, olr_ref):
    x = x_ref[...]
    inv = lax.rsqrt(jnp.mean(x * x, axis=-1, keepdims=True) + EPS)
    h = (x * inv * g_ref[...]) * (1.0 + sc_ref[...]) + sh_ref[...]
    hb = h.astype(BF16)
    oa_ref[...] = jnp.dot(hb, wa_ref[...], preferred_element_type=F32).astype(BF16)
    og_ref[...] = jnp.dot(hb, wg_ref[...], preferred_element_type=F32).astype(BF16)
    ogate_ref[...] = jnp.dot(hb, wgate_ref[...], preferred_element_type=F32).astype(BF16)
    olr_ref[...] = jnp.dot(hb, wlr_ref[...], preferred_element_type=F32)


def _in_proj(x, mod4, norm1_g, w_a, w_g, w_gate, w_lr):
    nrow = SEQ // TM_IN
    full = lambda shape: pl.BlockSpec(shape, lambda b, i: (0,) * len(shape))
    return pl.pallas_call(
        _in_kernel,
        grid=(BATCH, nrow),
        in_specs=[
            pl.BlockSpec((None, TM_IN, D_MODEL), lambda b, i: (b, i, 0)),
            pl.BlockSpec((None, None, 1, D_MODEL), lambda b, i: (0, b, 0, 0)),
            pl.BlockSpec((None, None, 1, D_MODEL), lambda b, i: (1, b, 0, 0)),
            full((1, D_MODEL)),
            full((D_MODEL, W_A)),
            full((D_MODEL, W_G)),
            full((D_MODEL, W_GATE)),
            full((D_MODEL, LANES)),
        ],
        out_specs=[
            pl.BlockSpec((None, TM_IN, W_A), lambda b, i: (b, i, 0)),
            pl.BlockSpec((None, TM_IN, W_G), lambda b, i: (b, i, 0)),
            pl.BlockSpec((None, TM_IN, W_GATE), lambda b, i: (b, i, 0)),
            pl.BlockSpec((None, TM_IN, LANES), lambda b, i: (b, i, 0)),
        ],
        out_shape=[
            jax.ShapeDtypeStruct((BATCH, SEQ, W_A), BF16),
            jax.ShapeDtypeStruct((BATCH, SEQ, W_G), BF16),
            jax.ShapeDtypeStruct((BATCH, SEQ, W_GATE), BF16),
            jax.ShapeDtypeStruct((BATCH, SEQ, LANES), F32),
        ],
        compiler_params=pltpu.CompilerParams(
            dimension_semantics=("arbitrary", "arbitrary"), vmem_limit_bytes=VMEM_LIMIT),
        name="in_proj",
    )(x, mod4, mod4, norm1_g, w_a, w_g, w_gate, w_lr)


def _attn_kernel(slope_ref, lam_ref, q_ref, k_ref, v_ref, kaug_ref, g_ref, o_ref,
                 sa_ref, sb_ref, corr_ref, m_ref, l_ref, acc_ref):
    h = pl.program_id(1)
    qi = pl.program_id(2)
    slope = slope_ref[h]
    lam = lam_ref[0]
    q = q_ref[...] * jnp.asarray(DIFF_DH ** -0.5, BF16)
    lane = lax.broadcasted_iota(I32, (TQ, 2 * DIFF_DH), 1)
    zero = jnp.zeros_like(q)
    aug = jnp.where(lane == 0, 8.0 * slope, jnp.where(lane == 1, slope, 0.0)).astype(BF16)
    lhs = jnp.concatenate([
        jnp.concatenate([jnp.where(lane < DIFF_DH, q, zero), aug], axis=1),
        jnp.concatenate([jnp.where(lane >= DIFF_DH, q, zero), aug], axis=1)], axis=0)

    def key_rows(j):
        return pl.ds(pl.multiple_of(j * TQ, TQ), TQ)

    def scores(j):
        rows = key_rows(j)
        kk = jnp.concatenate([k_ref[rows, :], kaug_ref[rows, :]], axis=1)
        return _nt_dot(lhs, kk)

    def update(s, j):
        m = m_ref[...]
        m_new = jnp.maximum(m, jnp.max(s, axis=-1, keepdims=True))
        alpha = jnp.exp(m - m_new)
        p = jnp.exp(s - m_new)
        m_ref[...] = m_new
        l_ref[...] = alpha * l_ref[...] + jnp.sum(p, axis=-1, keepdims=True)
        acc_ref[...] = alpha * acc_ref[...] + jnp.dot(p.astype(BF16), v_ref[key_rows(j), :],
                                                      preferred_element_type=F32)

    @pl.when(qi == 0)
    def _():
        r = lax.broadcasted_iota(I32, (TQ, TQ), 0)
        c = lax.broadcasted_iota(I32, (TQ, TQ), 1)
        ahead = jnp.maximum(c - r, 0).astype(F32)
        corr_ref[...] = jnp.where((r >> 6) >= (c >> 6), (-2.0 * slope) * ahead, -jnp.inf)

    m_ref[...] = jnp.full_like(m_ref, -jnp.inf)
    l_ref[...] = jnp.zeros_like(l_ref)
    acc_ref[...] = jnp.zeros_like(acc_ref)

    sa_ref[...] = scores(0)

    def off_diag(j, _):
        @pl.when((j & 1) == 0)
        def _():
            sb_ref[...] = scores(j + 1)
            update(sa_ref[...], j)

        @pl.when((j & 1) == 1)
        def _():
            sa_ref[...] = scores(j + 1)
            update(sb_ref[...], j)

        return 0

    lax.fori_loop(0, qi, off_diag, 0)

    def diagonal(s_ref):
        corr = corr_ref[...]
        update(s_ref[...] + jnp.concatenate([corr, corr], axis=0), qi)

    @pl.when((qi & 1) == 0)
    def _():
        diagonal(sa_ref)

    @pl.when((qi & 1) == 1)
    def _():
        diagonal(sb_ref)

    o = acc_ref[...] / l_ref[...]
    o = o[:TQ] - lam * o[TQ:]
    inv = lax.rsqrt(jnp.mean(o * o, axis=-1, keepdims=True) + EPS)
    o_ref[...] = (o * inv * g_ref[...] * (1.0 - LAMBDA_INIT)).astype(BF16)


def _diff_attention(qkv_a, slopes, lam, kaug, diff_norm_g):
    nq = SEQ // TQ
    return pl.pallas_call(
        _attn_kernel,
        grid=(BATCH, DIFF_HEADS, nq),
        in_specs=[
            pl.BlockSpec(memory_space=pltpu.SMEM),
            pl.BlockSpec(memory_space=pltpu.SMEM),
            pl.BlockSpec((None, TQ, LANES), lambda b, h, i: (b, i, h)),
            pl.BlockSpec((None, SEQ, LANES), lambda b, h, i: (b, 0, DIFF_HEADS + h)),
            pl.BlockSpec((None, SEQ, LANES), lambda b, h, i: (b, 0, 2 * DIFF_HEADS + h)),
            pl.BlockSpec((SEQ, LANES), lambda b, h, i: (0, 0)),
            pl.BlockSpec((1, LANES), lambda b, h, i: (0, 0)),
        ],
        out_specs=pl.BlockSpec((None, TQ, LANES), lambda b, h, i: (b, i, h)),
        out_shape=jax.ShapeDtypeStruct((BATCH, SEQ, DIFF_HEADS * 2 * DIFF_DH), BF16),
        scratch_shapes=[
            pltpu.VMEM((2 * TQ, TQ), F32), pltpu.VMEM((2 * TQ, TQ), F32),
            pltpu.VMEM((TQ, TQ), F32),
            pltpu.VMEM((2 * TQ, 1), F32), pltpu.VMEM((2 * TQ, 1), F32),
            pltpu.VMEM((2 * TQ, 2 * DIFF_DH), F32),
        ],
        compiler_params=pltpu.CompilerParams(
            dimension_semantics=("arbitrary", "arbitrary", "arbitrary")),
        name="diff_attn",
    )(slopes, lam, qkv_a, qkv_a, qkv_a, kaug, diff_norm_g)


N_CHUNK = SEQ // CHUNK
PAIR = 2 * GLA_DK
PAIR_V = 2 * GLA_DV
CS_ROWS = 256


def _gla_kernel(q_ref, k_ref, v_ref, r_ref, lr_ref, wup_ref, bup_ref, g_ref, o_ref,
                gcum_ref, state_ref):
    w_hi, w_mid, w_lo = _split3(wup_ref[...])
    rr = lax.broadcasted_iota(I32, (CS_ROWS, CS_ROWS), 0)
    cc = lax.broadcasted_iota(I32, (CS_ROWS, CS_ROWS), 1)
    tri = jnp.where(((rr >> 6) == (cc >> 6)) & (cc <= rr), 1.0, 0.0).astype(BF16)
    for blk in range(SEQ // CS_ROWS):
        rows = pl.ds(blk * CS_ROWS, CS_ROWS)
        a_hi, a_mid, a_lo = _split3(lr_ref[rows, :])
        z = (jnp.dot(a_hi, w_hi, preferred_element_type=F32)
             + jnp.dot(a_hi, w_mid, preferred_element_type=F32)
             + jnp.dot(a_mid, w_hi, preferred_element_type=F32)
             + jnp.dot(a_hi, w_lo, preferred_element_type=F32)
             + jnp.dot(a_lo, w_hi, preferred_element_type=F32)
             + jnp.dot(a_mid, w_mid, preferred_element_type=F32)) + bup_ref[...]
        la = (jnp.minimum(z, 0.0) - jnp.log(1.0 + jnp.exp(-jnp.abs(z)))) * (1.0 / GLA_GATE_NORM)
        l_hi, l_mid, l_lo = _split3(la)
        gcum_ref[rows, :] = (jnp.dot(tri, l_hi, preferred_element_type=F32)
                             + jnp.dot(tri, l_mid, preferred_element_type=F32)
                             + jnp.dot(tri, l_lo, preferred_element_type=F32))

    state_ref[...] = jnp.zeros_like(state_ref)
    lane_k = lax.broadcasted_iota(I32, (1, PAIR), 1)
    row_v = lax.broadcasted_iota(I32, (PAIR_V, PAIR), 0)
    col_k = lax.broadcasted_iota(I32, (PAIR_V, PAIR), 1)
    same_head = (row_v >= GLA_DV) == (col_k >= GLA_DK)
    cr = lax.broadcasted_iota(I32, (CHUNK, CHUNK), 0)
    cs = lax.broadcasted_iota(I32, (CHUNK, CHUNK), 1)
    causal = cs <= cr
    scale = GLA_DK ** -0.5

    def chunk(n, _):
        rows = pl.ds(pl.multiple_of(n * CHUNK, CHUNK), CHUNK)
        gc = gcum_ref[rows, :]
        g_last = gcum_ref[pl.ds(n * CHUNK + CHUNK - 1, 1), :]
        qf = q_ref[rows, :].astype(F32) * scale
        kf = k_ref[rows, :].astype(F32)
        q_s = (qf * jnp.exp(gc)).astype(BF16)
        k_s = (kf * jnp.exp(-gc)).astype(BF16)
        k_d = (kf * jnp.exp(g_last - gc)).astype(BF16)
        decay = jnp.exp(g_last)
        for pr in range(GLA_HEADS // 2):
            kl = slice(pr * PAIR, (pr + 1) * PAIR)
            vl = slice(pr * PAIR_V, (pr + 1) * PAIR_V)
            qs_p, ks_p, kd_p = q_s[:, kl], k_s[:, kl], k_d[:, kl]
            v_p = v_ref[rows, vl]
            st = state_ref[pr]
            o_inter = _nt_dot(qs_p, st.astype(BF16))
            d_st = _tn_dot(v_p, kd_p)
            state_ref[pr] = st * decay[:, kl] + jnp.where(same_head, d_st, 0.0)
            for sub in range(2):
                hd = 2 * pr + sub
                in_head = (lane_k >= sub * GLA_DK) & (lane_k < (sub + 1) * GLA_DK)
                a = _nt_dot(jnp.where(in_head, qs_p, jnp.zeros_like(qs_p)), ks_p)
                a = jnp.where(causal, a, 0.0).astype(BF16)
                vs = slice(hd * GLA_DV, (hd + 1) * GLA_DV)
                o = (jnp.dot(a, v_ref[rows, vs], preferred_element_type=F32)
                     + o_inter[:, sub * GLA_DV:(sub + 1) * GLA_DV])
                inv = lax.rsqrt(jnp.mean(o * o, axis=-1, keepdims=True) + EPS)
                r = r_ref[rows, vs].astype(F32)
                o_ref[rows, vs] = (o * inv * g_ref[...] * (r * jax.nn.sigmoid(r))).astype(BF16)
        return 0

    lax.fori_loop(0, N_CHUNK, chunk, 0)


def _gla(qkv_g, glr, w_up, b_up, gla_norm_g):
    qk_w = GLA_HEADS * GLA_DK
    v_w = GLA_HEADS * GLA_DV
    return pl.pallas_call(
        _gla_kernel,
        grid=(BATCH,),
        in_specs=[
            pl.BlockSpec((None, SEQ, qk_w), lambda b: (b, 0, 0)),
            pl.BlockSpec((None, SEQ, qk_w), lambda b: (b, 0, 1)),
            pl.BlockSpec((None, SEQ, v_w), lambda b: (b, 0, 1)),
            pl.BlockSpec((None, SEQ, v_w), lambda b: (b, 0, 2)),
            pl.BlockSpec((None, SEQ, LANES), lambda b: (b, 0, 0)),
            pl.BlockSpec((LANES, qk_w), lambda b: (0, 0)),
            pl.BlockSpec((1, qk_w), lambda b: (0, 0)),
            pl.BlockSpec((1, GLA_DV), lambda b: (0, 0)),
        ],
        out_specs=pl.BlockSpec((None, SEQ, v_w), lambda b: (b, 0, 0)),
        out_shape=jax.ShapeDtypeStruct((BATCH, SEQ, v_w), BF16),
        scratch_shapes=[
            pltpu.VMEM((SEQ, qk_w), F32),
            pltpu.VMEM((GLA_HEADS // 2, PAIR_V, PAIR), F32),
        ],
        compiler_params=pltpu.CompilerParams(
            dimension_semantics=("arbitrary",), vmem_limit_bytes=VMEM_LIMIT),
        name="gla",
    )(qkv_g, qkv_g, qkv_g, qkv_g, glr, w_up, b_up, gla_norm_g)


def _merge_kernel(oa_ref, ob_ref, gate_ref, x_ref, gt1_ref, sh2_ref, sc2_ref, g2_ref,
                  wpa_ref, wpb_ref, wo_ref, wr_ref, br_ref,
                  x1_ref, h2p_ref, eidx_ref, rank_ref, wts_ref, cnt_ref,
                  upper_ref, carry_ref):
    i = pl.program_id(0)

    @pl.when(i == 0)
    def _():
        rr = lax.broadcasted_iota(I32, (TM_MERGE, TM_MERGE), 0)
        cc = lax.broadcasted_iota(I32, (TM_MERGE, TM_MERGE), 1)
        upper_ref[...] = jnp.where(rr < cc, 1.0, 0.0).astype(BF16)
        carry_ref[...] = jnp.zeros_like(carry_ref)

    ga = gate_ref[:, :D_MODEL].astype(F32)
    gb = gate_ref[:, D_MODEL:].astype(F32)
    merged = (jax.nn.sigmoid(ga) * jnp.dot(oa_ref[...], wpa_ref[...], preferred_element_type=F32)
              + jax.nn.sigmoid(gb) * jnp.dot(ob_ref[...], wpb_ref[...], preferred_element_type=F32))
    y = jnp.dot(merged.astype(BF16), wo_ref[...], preferred_element_type=F32)
    x1 = x_ref[...] + gt1_ref[...] * y
    x1_ref[...] = x1
    inv = lax.rsqrt(jnp.mean(x1 * x1, axis=-1, keepdims=True) + EPS)
    h2 = (x1 * inv * g2_ref[...]) * (1.0 + sc2_ref[...]) + sh2_ref[...]
    h2p_ref[...] = _pack_halves(h2)

    h_hi, h_mid, _ = _split3(h2)
    w_hi, w_mid, _ = _split3(wr_ref[...])
    logits = (_nt_dot(w_hi, h_hi) + _nt_dot(w_hi, h_mid) + _nt_dot(w_mid, h_hi)) + br_ref[...]

    eio = lax.broadcasted_iota(I32, (N_EXPERTS, TM_MERGE), 0)
    vals, idxs, sels = [], [], []
    cur = logits
    for _k in range(TOP_K):
        m = jnp.max(cur, axis=0, keepdims=True)
        idx = jnp.min(jnp.where(cur == m, eio, N_EXPERTS), axis=0, keepdims=True)
        sel = eio == idx
        vals.append(m)
        idxs.append(idx)
        sels.append(sel)
        cur = jnp.where(sel, -jnp.inf, cur)
    es = [jnp.exp(v - vals[0]) for v in vals]
    tot = es[0] + es[1] + es[2] + es[3]
    onehot = jnp.zeros((N_EXPERTS, TM_MERGE), F32)
    for sel in sels:
        onehot = onehot + jnp.where(sel, 1.0, 0.0)
    before = jnp.dot(onehot.astype(BF16), upper_ref[...], preferred_element_type=F32) + carry_ref[:, 0:1]
    ranks = [jnp.sum(jnp.where(sel, before, 0.0), axis=0, keepdims=True) for sel in sels]
    carry_ref[...] = carry_ref[...] + jnp.sum(onehot, axis=1, keepdims=True)
    cnt_ref[...] = carry_ref[...]

    zi = jnp.zeros((8 - TOP_K, TM_MERGE), I32)
    zf = jnp.zeros((8 - TOP_K, TM_MERGE), F32)
    eidx_ref[...] = jnp.concatenate(idxs + [zi], axis=0)
    rank_ref[...] = jnp.concatenate([r.astype(I32) for r in ranks] + [zi], axis=0)
    wts_ref[...] = jnp.concatenate([e / tot for e in es] + [zf], axis=0)


def _merge_route(o_a, o_b, gates, x2d, mod4, norm2_g, w_pa, w_pb, w_o, w_rt, b_r):
    ntile = N_TOK // TM_MERGE
    per_b = SEQ // TM_MERGE
    full = lambda shape: pl.BlockSpec(shape, lambda i: (0,) * len(shape))
    row = lambda w: pl.BlockSpec((TM_MERGE, w), lambda i: (i, 0))
    modspec = lambda j: pl.BlockSpec((None, None, 1, D_MODEL), lambda i: (j, i // per_b, 0, 0))
    col = pl.BlockSpec((8, TM_MERGE), lambda i: (0, i))
    return pl.pallas_call(
        _merge_kernel,
        grid=(ntile,),
        in_specs=[
            row(DIFF_HEADS * 2 * DIFF_DH), row(GLA_HEADS * GLA_DV), row(W_GATE), row(D_MODEL),
            modspec(2), modspec(3), modspec(4),
            full((1, D_MODEL)),
            full((DIFF_HEADS * 2 * DIFF_DH, D_MODEL)), full((GLA_HEADS * GLA_DV, D_MODEL)),
            full((D_MODEL, D_MODEL)),
            full((N_EXPERTS, D_MODEL)), full((N_EXPERTS, 1)),
        ],
        out_specs=[
            row(D_MODEL), row(ROW_W), col, col, col,
            pl.BlockSpec((N_EXPERTS, LANES), lambda i: (0, 0)),
        ],
        out_shape=[
            jax.ShapeDtypeStruct((N_TOK, D_MODEL), F32),
            jax.ShapeDtypeStruct((N_TOK, ROW_W), ROW_DT),
            jax.ShapeDtypeStruct((8, N_TOK), I32),
            jax.ShapeDtypeStruct((8, N_TOK), I32),
            jax.ShapeDtypeStruct((8, N_TOK), F32),
            jax.ShapeDtypeStruct((N_EXPERTS, LANES), F32),
        ],
        scratch_shapes=[
            pltpu.VMEM((TM_MERGE, TM_MERGE), BF16),
            pltpu.VMEM((N_EXPERTS, LANES), F32),
        ],
        compiler_params=pltpu.CompilerParams(
            dimension_semantics=("arbitrary",), vmem_limit_bytes=VMEM_LIMIT),
        name="merge_route",
    )(o_a, o_b, gates, x2d, mod4, mod4, mod4, norm2_g, w_pa, w_pb, w_o, w_rt, b_r)


TP = 4096
NB_PAD = ((N_BLK + LANES - 1) // LANES) * LANES


def _plan_kernel(cnt_ref, eidx_ref, rank_ref, dest_ref, be_ref, nv_ref):
    cnt = cnt_ref[...]
    padded = jnp.floor((cnt + (BLK - 1.0)) * (1.0 / BLK)) * BLK
    er = lax.broadcasted_iota(I32, (N_EXPERTS, N_EXPERTS), 0)
    ec = lax.broadcasted_iota(I32, (N_EXPERTS, N_EXPERTS), 1)
    lower = jnp.where(ec < er, 1.0, 0.0).astype(BF16)
    p_hi, p_mid, p_lo = _split3(padded)
    starts = (jnp.dot(lower, p_hi, preferred_element_type=F32)
              + jnp.dot(lower, p_mid, preferred_element_type=F32)
              + jnp.dot(lower, p_lo, preferred_element_type=F32))
    ends = starts + padded
    blk_start = (lax.broadcasted_iota(I32, (1, NB_PAD), 1) * BLK).astype(F32)
    n_before = jnp.sum(jnp.where(ends[:, 0:1] <= blk_start, 1.0, 0.0), axis=0, keepdims=True)
    be_ref[...] = jnp.minimum(n_before, N_EXPERTS - 1.0).astype(I32)
    nv_ref[...] = (jnp.max(ends, axis=0, keepdims=True) * (1.0 / BLK)).astype(I32)

    eio = lax.broadcasted_iota(I32, (N_EXPERTS, TP), 0)
    rows = []
    for k in range(TOP_K):
        onehot = eio == eidx_ref[k:k + 1, :]
        base = jnp.sum(jnp.where(onehot, starts[:, 0:1], 0.0), axis=0, keepdims=True)
        rows.append(base.astype(I32) + rank_ref[k:k + 1, :])
    dest_ref[...] = jnp.concatenate(rows + [jnp.zeros((8 - TOP_K, TP), I32)], axis=0)


def _route_plan(cnt, eidx, rank):
    return pl.pallas_call(
        _plan_kernel,
        grid=(N_TOK // TP,),
        in_specs=[
            pl.BlockSpec((N_EXPERTS, LANES), lambda i: (0, 0)),
            pl.BlockSpec((8, TP), lambda i: (0, i)),
            pl.BlockSpec((8, TP), lambda i: (0, i)),
        ],
        out_specs=[
            pl.BlockSpec((8, TP), lambda i: (0, i)),
            pl.BlockSpec((1, NB_PAD), lambda i: (0, 0)),
            pl.BlockSpec((1, LANES), lambda i: (0, 0)),
        ],
        out_shape=[
            jax.ShapeDtypeStruct((8, N_TOK), I32),
            jax.ShapeDtypeStruct((1, NB_PAD), I32),
            jax.ShapeDtypeStruct((1, LANES), I32),
        ],
        compiler_params=pltpu.CompilerParams(dimension_semantics=("arbitrary",)),
        name="route_plan",
    )(cnt, eidx, rank)


def _ffn_kernel(be_ref, nv_ref, x_ref, wgu_ref, bgu_ref, wd_ref, bd_ref, y_ref, wgu_bf, wd_bf):
    i = pl.program_id(0)
    prev = be_ref[jnp.maximum(i - 1, 0)]
    valid = i < nv_ref[0]

    @pl.when(valid & ((i == 0) | (be_ref[i] != prev)))
    def _():
        wgu_bf[...] = wgu_ref[...].astype(BF16)
        wd_bf[...] = wd_ref[...].astype(BF16)

    @pl.when(valid)
    def _():
        lo, hi = _unpack_halves(x_ref[...])
        x = jnp.concatenate([lo, hi], axis=1).astype(BF16)
        gu = jnp.dot(x, wgu_bf[...], preferred_element_type=F32) + bgu_ref[...]
        gate = jnp.minimum(gu[:, :D_FF], SWIGLU_LIMIT)
        up = jnp.clip(gu[:, D_FF:], -SWIGLU_LIMIT, SWIGLU_LIMIT)
        act = (up + 1.0) * (gate * jax.nn.sigmoid(SWIGLU_ALPHA * gate))
        y = jnp.dot(act.astype(BF16), wd_bf[...], preferred_element_type=F32) + bd_ref[...]
        y_ref[...] = _pack_halves(y)

    @pl.when(jnp.logical_not(valid))
    def _():
        y_ref[...] = jnp.zeros_like(y_ref)


def _expert_ffn(blk_expert, n_valid, xb, w_gate_up, b_gate_up, w_down, b_down):
    grid_spec = pltpu.PrefetchScalarGridSpec(
        num_scalar_prefetch=2,
        grid=(N_BLK,),
        in_specs=[
            pl.BlockSpec((BLK, ROW_W), lambda i, be, nv: (i, 0)),
            pl.BlockSpec((None, D_MODEL, 2 * D_FF), lambda i, be, nv: (be[i], 0, 0)),
            pl.BlockSpec((None, 1, 2 * D_FF), lambda i, be, nv: (be[i], 0, 0)),
            pl.BlockSpec((None, D_FF, D_MODEL), lambda i, be, nv: (be[i], 0, 0)),
            pl.BlockSpec((None, 1, D_MODEL), lambda i, be, nv: (be[i], 0, 0)),
        ],
        out_specs=pl.BlockSpec((BLK, ROW_W), lambda i, be, nv: (i, 0)),
        scratch_shapes=[
            pltpu.VMEM((D_MODEL, 2 * D_FF), BF16),
            pltpu.VMEM((D_FF, D_MODEL), BF16),
        ],
    )
    return pl.pallas_call(
        _ffn_kernel,
        grid_spec=grid_spec,
        out_shape=jax.ShapeDtypeStruct((P_ROWS, ROW_W), ROW_DT),
        compiler_params=pltpu.CompilerParams(
            dimension_semantics=("arbitrary",), vmem_limit_bytes=VMEM_LIMIT),
        name="expert_ffn",
    )(blk_expert, n_valid, xb, w_gate_up, b_gate_up.reshape(N_EXPERTS, 1, 2 * D_FF),
      w_down, b_down.reshape(N_EXPERTS, 1, D_MODEL))


def _final_kernel(x1_ref, y0_ref, y1_ref, y2_ref, y3_ref, w_ref, gt2_ref, g_ref, o_ref):
    w = w_ref[...]
    ylo = jnp.zeros((TM_FIN, HALF), F32)
    yhi = jnp.zeros((TM_FIN, HALF), F32)
    for k, y_ref in enumerate((y0_ref, y1_ref, y2_ref, y3_ref)):
        lo, hi = _unpack_halves(y_ref[...])
        wk = w[:, k:k + 1]
        ylo = ylo + wk * lo
        yhi = yhi + wk * hi
    y = jnp.concatenate([ylo, yhi], axis=1)
    x2 = x1_ref[...] + gt2_ref[...] * y
    inv = lax.rsqrt(jnp.mean(x2 * x2, axis=-1, keepdims=True) + EPS)
    o_ref[...] = x2 * inv * g_ref[...]


def _final(x1, yg, w4, mod4, final_norm_g):
    per_b = SEQ // TM_FIN
    ntile = N_TOK // TM_FIN
    slot = lambda k: pl.BlockSpec((TM_FIN, ROW_W), lambda i: (k * ntile + i, 0))
    return pl.pallas_call(
        _final_kernel,
        grid=(ntile,),
        in_specs=[
            pl.BlockSpec((TM_FIN, D_MODEL), lambda i: (i, 0)),
            slot(0), slot(1), slot(2), slot(3),
            pl.BlockSpec((TM_FIN, TOP_K), lambda i: (i, 0)),
            pl.BlockSpec((None, None, 1, D_MODEL), lambda i: (5, i // per_b, 0, 0)),
            pl.BlockSpec((1, D_MODEL), lambda i: (0, 0)),
        ],
        out_specs=pl.BlockSpec((TM_FIN, D_MODEL), lambda i: (i, 0)),
        out_shape=jax.ShapeDtypeStruct((N_TOK, D_MODEL), F32),
        compiler_params=pltpu.CompilerParams(
            dimension_semantics=("arbitrary",), vmem_limit_bytes=VMEM_LIMIT),
        name="combine_final",
    )(x1, yg, yg, yg, yg, w4, mod4, final_norm_g)


SC_CORES = 2
SC_SUBCORES = 16
SC_WORKERS = SC_CORES * SC_SUBCORES
SC_CHUNK = 64


def _sc_mesh():
    return plsc.VectorSubcoreMesh(core_axis_name="c", subcore_axis_name="s")


def _dispatch_rows(h2p, dest_flat):
    per_w = N_TOK // SC_WORKERS
    nchunk = per_w // SC_CHUNK

    @functools.partial(
        pl.kernel, mesh=_sc_mesh(), out_type=jax.ShapeDtypeStruct((P_ROWS, HALF), U32),
        scratch_types=[pltpu.VMEM((SC_CHUNK,), I32)] * TOP_K
        + [pltpu.VMEM((SC_CHUNK, HALF), U32), pltpu.SemaphoreType.DMA],
        name="moe_dispatch")
    def k(src_hbm, dest_hbm, out_hbm, i0, i1, i2, i3, rows_v, sem):
        idx = (i0, i1, i2, i3)
        wid = lax.axis_index("s") * SC_CORES + lax.axis_index("c")

        @pl.loop(0, nchunk)
        def _(j):
            t0 = wid * per_w + j * SC_CHUNK
            pltpu.sync_copy(src_hbm.at[pl.ds(t0, SC_CHUNK)], rows_v)
            for kk in range(TOP_K):
                pltpu.sync_copy(dest_hbm.at[pl.ds(kk * N_TOK + t0, SC_CHUNK)], idx[kk])
            for kk in range(TOP_K):
                pltpu.sync_copy(rows_v, out_hbm.at[idx[kk]])

    return k(h2p, dest_flat)


def _combine_rows(yb, dest_flat):
    n_out = TOP_K * N_TOK
    per_w = n_out // SC_WORKERS
    nchunk = per_w // SC_CHUNK

    @functools.partial(
        pl.kernel, mesh=_sc_mesh(), out_type=jax.ShapeDtypeStruct((n_out, HALF), U32),
        scratch_types=[pltpu.VMEM((SC_CHUNK,), I32), pltpu.VMEM((SC_CHUNK, HALF), U32),
                       pltpu.SemaphoreType.DMA],
        name="moe_combine")
    def k(tab_hbm, idx_hbm, out_hbm, idx_v, rows_v, sem):
        wid = lax.axis_index("s") * SC_CORES + lax.axis_index("c")

        @pl.loop(0, nchunk)
        def _(j):
            a0 = wid * per_w + j * SC_CHUNK
            pltpu.sync_copy(idx_hbm.at[pl.ds(a0, SC_CHUNK)], idx_v)
            pltpu.sync_copy(tab_hbm.at[idx_v], rows_v)
            pltpu.sync_copy(rows_v, out_hbm.at[pl.ds(a0, SC_CHUNK)])

    return k(yb, dest_flat)


def _lambda_kernel(p_ref, o_ref):
    p = p_ref[...]
    s1 = jnp.sum(p[0:1] * p[1:2], axis=-1, keepdims=True)
    s2 = jnp.sum(p[2:3] * p[3:4], axis=-1, keepdims=True)
    o_ref[...] = jnp.broadcast_to(jnp.exp(s1) - jnp.exp(s2) + LAMBDA_INIT, (1, LANES))


def kernel(x, c, w_ada, b_ada, norm1_g, w_in, lambda_q1, lambda_k1, lambda_q2, lambda_k2, diff_norm_g, w_alpha_up, b_alpha, gla_norm_g, w_branch_diff, w_branch_gla, w_out, norm2_g, w_router, b_router, w_gate_up, b_gate_up, w_down, b_down, final_norm_g):
    w_in0 = w_in[0]
    c_a, c_g = W_A, W_A + W_G
    w_a = w_in0[:, :c_a].astype(BF16)
    w_g = w_in0[:, c_a:c_g].astype(BF16)
    w_lr = jnp.pad(w_in0[:, c_g:c_g + GLA_RANK], ((0, 0), (0, LANES - GLA_RANK))).astype(BF16)
    w_gate = w_in0[:, c_g + GLA_RANK:].astype(BF16)
    w_up = jnp.pad(w_alpha_up[0], ((0, LANES - GLA_RANK), (0, 0)))
    lam_in = jnp.concatenate([lambda_q1, lambda_k1, lambda_q2, lambda_k2], axis=0)
    slopes = jnp.asarray(2.0 ** (-8.0 * np.arange(1, DIFF_HEADS + 1) / DIFF_HEADS), dtype=F32)

    mod = _modulation(c, w_ada[0], b_ada[0])
    mod4 = mod.reshape(N_MOD, BATCH, 1, D_MODEL)
    lam = pl.pallas_call(
        _lambda_kernel, out_shape=jax.ShapeDtypeStruct((1, LANES), F32), name="lambda")(lam_in)[0, :1]

    qkv_a, qkv_g, gates, glr = _in_proj(x, mod4, norm1_g, w_a, w_g, w_gate, w_lr)
    pos = jnp.arange(SEQ, dtype=I32)
    kaug = jnp.zeros((SEQ, LANES), F32).at[:, 0].set((pos >> 3).astype(F32)).at[:, 1].set(
        (pos & 7).astype(F32)).astype(BF16)
    o_a = _diff_attention(qkv_a, slopes, lam, kaug, diff_norm_g)
    o_b = _gla(qkv_g, glr, w_up, b_alpha, gla_norm_g)

    x1, h2p, eidx, rank, wts, cnt = _merge_route(
        o_a.reshape(N_TOK, -1), o_b.reshape(N_TOK, -1), gates.reshape(N_TOK, W_GATE),
        x.reshape(N_TOK, D_MODEL), mod4, norm2_g,
        w_branch_diff[0].astype(BF16), w_branch_gla[0].astype(BF16), w_out[0].astype(BF16),
        w_router[0].T, b_router[0].reshape(N_EXPERTS, 1))

    dest8, be, nv = _route_plan(cnt, eidx, rank)
    dest = dest8[:TOP_K].reshape(-1)

    xb = _dispatch_rows(h2p, dest)
    yb = _expert_ffn(be[0, :N_BLK], nv[0, :1], xb, w_gate_up[0], b_gate_up[0], w_down[0], b_down[0])
    yg = _combine_rows(yb, dest)

    out = _final(x1, yg, wts[:TOP_K].T, mod4, final_norm_g.reshape(1, D_MODEL))
    return out.reshape(BATCH, SEQ, D_MODEL)
```

```python
import functools
import math

import jax
import jax.numpy as jnp
import numpy as np
from jax import lax
from jax.experimental import pallas as pl
from jax.experimental.pallas import tpu as pltpu
from jax.experimental.pallas import tpu_sc as plsc

F32 = jnp.float32
BF16 = jnp.bfloat16
U32 = jnp.uint32
I32 = jnp.int32

D_MODEL = 1024
BATCH = 16
SEQ = 2048
N_TOK = BATCH * SEQ
CHUNK = 64
DIFF_HEADS = 4
DIFF_DH = 64
GLA_HEADS = 4
GLA_DK = 64
GLA_DV = 128
GLA_RANK = 16
GLA_GATE_NORM = 16.0
N_EXPERTS = 32
TOP_K = 4
D_FF = D_MODEL
SWIGLU_LIMIT = 7.0
SWIGLU_ALPHA = 1.702
N_MOD = 6
EPS = 1e-6
LAMBDA_INIT = 0.8 - 0.6 * math.exp(-0.3 * 0)

LANES = 128
HALF = D_MODEL // 2
ROW_W = HALF
ROW_DT = U32

TM_IN = 512
TQ = 512
TM_MERGE = 512
BLK = 256
N_BLK = (N_TOK * TOP_K) // BLK + N_EXPERTS
P_ROWS = N_BLK * BLK
TM_FIN = 512
VMEM_LIMIT = 56 * 1024 * 1024


def _nt_dot(a, b):
    return lax.dot_general(a, b, (((1,), (1,)), ((), ())), preferred_element_type=F32)


def _tn_dot(a, b):
    return lax.dot_general(a, b, (((0,), (0,)), ((), ())), preferred_element_type=F32)


def _split3(x):
    hi = x.astype(BF16)
    r1 = x - hi.astype(F32)
    mid = r1.astype(BF16)
    lo = (r1 - mid.astype(F32)).astype(BF16)
    return hi, mid, lo


def _pack_halves(y):
    return pltpu.pack_elementwise([y[:, :HALF], y[:, HALF:]], packed_dtype=BF16)


def _unpack_halves(u):
    lo = pltpu.unpack_elementwise(u, index=0, packed_dtype=BF16, unpacked_dtype=F32)
    hi = pltpu.unpack_elementwise(u, index=1, packed_dtype=BF16, unpacked_dtype=F32)
    return lo, hi


def _mod_kernel(c_ref, w_ref, b_ref, o_ref):
    c = c_ref[...]
    s = c * jax.nn.sigmoid(c)
    o_ref[0] = jnp.dot(s.astype(BF16), w_ref[...].astype(BF16),
                       preferred_element_type=F32) + b_ref[...]


def _modulation(c, w_ada, b_ada):
    return pl.pallas_call(
        _mod_kernel,
        grid=(N_MOD,),
        in_specs=[
            pl.BlockSpec((BATCH, D_MODEL), lambda j: (0, 0)),
            pl.BlockSpec((D_MODEL, D_MODEL), lambda j: (0, j)),
            pl.BlockSpec((1, D_MODEL), lambda j: (0, j)),
        ],
        out_specs=pl.BlockSpec((1, BATCH, D_MODEL), lambda j: (j, 0, 0)),
        out_shape=jax.ShapeDtypeStruct((N_MOD, BATCH, D_MODEL), F32),
        compiler_params=pltpu.CompilerParams(dimension_semantics=("arbitrary",)),
        name="adaln_mod",
    )(c, w_ada, b_ada.reshape(1, N_MOD * D_MODEL))


W_A = 3 * DIFF_HEADS * 2 * DIFF_DH
W_G = 2 * GLA_HEADS * GLA_DK + 2 * GLA_HEADS * GLA_DV
W_GATE = 2 * D_MODEL


def _in_kernel(x_ref, sh_ref, sc_ref, g_ref, wa_ref, wg_ref, wgate_ref, wlr_ref,
               oa_ref, og_ref, ogate_ref, olr_ref):
    x = x_ref[...]
    inv = lax.rsqrt(jnp.mean(x * x, axis=-1, keepdims=True) + EPS)
    h = (x * inv * g_ref[...]) * (1.0 + sc_ref[...]) + sh_ref[...]
    hb = h.astype(BF16)
    oa_ref[...] = jnp.dot(hb, wa_ref[...], preferred_element_type=F32).astype(BF16)
    og_ref[...] = jnp.dot(hb, wg_ref[...], preferred_element_type=F32).astype(BF16)
    ogate_ref[...] = jnp.dot(hb, wgate_ref[...], preferred_element_type=F32).astype(BF16)
    olr_ref[...] = jnp.dot(hb, wlr_ref[...], preferred_element_type=F32)


def _in_proj(x, mod4, norm1_g, w_a, w_g, w_gate, w_lr):
    nrow = SEQ // TM_IN
    full = lambda shape: pl.BlockSpec(shape, lambda b, i: (0,) * len(shape))
    return pl.pallas_call(
        _in_kernel,
        grid=(BATCH, nrow),
        in_specs=[
            pl.BlockSpec((None, TM_IN, D_MODEL), lambda b, i: (b, i, 0)),
            pl.BlockSpec((None, None, 1, D_MODEL), lambda b, i: (0, b, 0, 0)),
            pl.BlockSpec((None, None, 1, D_MODEL), lambda b, i: (1, b, 0, 0)),
            full((1, D_MODEL)),
            full((D_MODEL, W_A)),
            full((D_MODEL, W_G)),
            full((D_MODEL, W_GATE)),
            full((D_MODEL, LANES)),
        ],
        out_specs=[
            pl.BlockSpec((None, TM_IN, W_A), lambda b, i: (b, i, 0)),
            pl.BlockSpec((None, TM_IN, W_G), lambda b, i: (b, i, 0)),
            pl.BlockSpec((None, TM_IN, W_GATE), lambda b, i: (b, i, 0)),
            pl.BlockSpec((None, TM_IN, LANES), lambda b, i: (b, i, 0)),
        ],
        out_shape=[
            jax.ShapeDtypeStruct((BATCH, SEQ, W_A), BF16),
            jax.ShapeDtypeStruct((BATCH, SEQ, W_G), BF16),
            jax.ShapeDtypeStruct((BATCH, SEQ, W_GATE), BF16),
            jax.ShapeDtypeStruct((BATCH, SEQ, LANES), F32),
        ],
        compiler_params=pltpu.CompilerParams(
            dimension_semantics=("arbitrary", "arbitrary"), vmem_limit_bytes=VMEM_LIMIT),
        name="in_proj",
    )(x, mod4, mod4, norm1_g, w_a, w_g, w_gate, w_lr)


def _attn_kernel(slope_ref, lam_ref, q_ref, k_ref, v_ref, kaug_ref, g_ref, o_ref,
                 sa_ref, sb_ref, corr_ref, m_ref, l_ref, acc_ref):
    h = pl.program_id(1)
    qi = pl.program_id(2)
    slope = slope_ref[h]
    lam = lam_ref[0]
    q = q_ref[...] * jnp.asarray(DIFF_DH ** -0.5, BF16)
    lane = lax.broadcasted_iota(I32, (TQ, 2 * DIFF_DH), 1)
    zero = jnp.zeros_like(q)
    aug = jnp.where(lane == 0, 8.0 * slope, jnp.where(lane == 1, slope, 0.0)).astype(BF16)
    lhs = jnp.concatenate([
        jnp.concatenate([jnp.where(lane < DIFF_DH, q, zero), aug], axis=1),
        jnp.concatenate([jnp.where(lane >= DIFF_DH, q, zero), aug], axis=1)], axis=0)

    def key_rows(j):
        return pl.ds(pl.multiple_of(j * TQ, TQ), TQ)

    def scores(j):
        rows = key_rows(j)
        kk = jnp.concatenate([k_ref[rows, :], kaug_ref[rows, :]], axis=1)
        return _nt_dot(lhs, kk)

    def update(s, j):
        m = m_ref[...]
        m_new = jnp.maximum(m, jnp.max(s, axis=-1, keepdims=True))
        alpha = jnp.exp(m - m_new)
        p = jnp.exp(s - m_new)
        m_ref[...] = m_new
        l_ref[...] = alpha * l_ref[...] + jnp.sum(p, axis=-1, keepdims=True)
        acc_ref[...] = alpha * acc_ref[...] + jnp.dot(p.astype(BF16), v_ref[key_rows(j), :],
                                                      preferred_element_type=F32)

    @pl.when(qi == 0)
    def _():
        r = lax.broadcasted_iota(I32, (TQ, TQ), 0)
        c = lax.broadcasted_iota(I32, (TQ, TQ), 1)
        ahead = jnp.maximum(c - r, 0).astype(F32)
        corr_ref[...] = jnp.where((r >> 6) >= (c >> 6), (-2.0 * slope) * ahead, -jnp.inf)

    m_ref[...] = jnp.full_like(m_ref, -jnp.inf)
    l_ref[...] = jnp.zeros_like(l_ref)
    acc_ref[...] = jnp.zeros_like(acc_ref)

    sa_ref[...] = scores(0)

    def off_diag(j, _):
        @pl.when((j & 1) == 0)
        def _():
            sb_ref[...] = scores(j + 1)
            update(sa_ref[...], j)

        @pl.when((j & 1) == 1)
        def _():
            sa_ref[...] = scores(j + 1)
            update(sb_ref[...], j)

        return 0

    lax.fori_loop(0, qi, off_diag, 0)

    def diagonal(s_ref):
        corr = corr_ref[...]
        update(s_ref[...] + jnp.concatenate([corr, corr], axis=0), qi)

    @pl.when((qi & 1) == 0)
    def _():
        diagonal(sa_ref)

    @pl.when((qi & 1) == 1)
    def _():
        diagonal(sb_ref)

    o = acc_ref[...] / l_ref[...]
    o = o[:TQ] - lam * o[TQ:]
    inv = lax.rsqrt(jnp.mean(o * o, axis=-1, keepdims=True) + EPS)
    o_ref[...] = (o * inv * g_ref[...] * (1.0 - LAMBDA_INIT)).astype(BF16)


def _diff_attention(qkv_a, slopes, lam, kaug, diff_norm_g):
    nq = SEQ // TQ
    return pl.pallas_call(
        _attn_kernel,
        grid=(BATCH, DIFF_HEADS, nq),
        in_specs=[
            pl.BlockSpec(memory_space=pltpu.SMEM),
            pl.BlockSpec(memory_space=pltpu.SMEM),
            pl.BlockSpec((None, TQ, LANES), lambda b, h, i: (b, i, h)),
            pl.BlockSpec((None, SEQ, LANES), lambda b, h, i: (b, 0, DIFF_HEADS + h)),
            pl.BlockSpec((None, SEQ, LANES), lambda b, h, i: (b, 0, 2 * DIFF_HEADS + h)),
            pl.BlockSpec((SEQ, LANES), lambda b, h, i: (0, 0)),
            pl.BlockSpec((1, LANES), lambda b, h, i: (0, 0)),
        ],
        out_specs=pl.BlockSpec((None, TQ, LANES), lambda b, h, i: (b, i, h)),
        out_shape=jax.ShapeDtypeStruct((BATCH, SEQ, DIFF_HEADS * 2 * DIFF_DH), BF16),
        scratch_shapes=[
            pltpu.VMEM((2 * TQ, TQ), F32), pltpu.VMEM((2 * TQ, TQ), F32),
            pltpu.VMEM((TQ, TQ), F32),
            pltpu.VMEM((2 * TQ, 1), F32), pltpu.VMEM((2 * TQ, 1), F32),
            pltpu.VMEM((2 * TQ, 2 * DIFF_DH), F32),
        ],
        compiler_params=pltpu.CompilerParams(
            dimension_semantics=("arbitrary", "arbitrary", "arbitrary")),
        name="diff_attn",
    )(slopes, lam, qkv_a, qkv_a, qkv_a, kaug, diff_norm_g)


N_CHUNK = SEQ // CHUNK
PAIR = 2 * GLA_DK
PAIR_V = 2 * GLA_DV
CS_ROWS = 256
GLA_UNROLL = 4


def _gla_kernel(q_ref, k_ref, v_ref, r_ref, lr_ref, wup_ref, bup_ref, g_ref, o_ref,
                gcum_ref, state_ref):
    w_hi, w_mid, _ = _split3(wup_ref[...])
    rr = lax.broadcasted_iota(I32, (CS_ROWS, CS_ROWS), 0)
    cc = lax.broadcasted_iota(I32, (CS_ROWS, CS_ROWS), 1)
    tri = jnp.where(((rr >> 6) == (cc >> 6)) & (cc <= rr), 1.0, 0.0).astype(BF16)
    for blk in range(SEQ // CS_ROWS):
        rows = pl.ds(blk * CS_ROWS, CS_ROWS)
        a_hi, a_mid, _ = _split3(lr_ref[rows, :])
        z = (jnp.dot(a_hi, w_hi, preferred_element_type=F32)
             + jnp.dot(a_hi, w_mid, preferred_element_type=F32)
             + jnp.dot(a_mid, w_hi, preferred_element_type=F32)) + bup_ref[...]
        la = (jnp.minimum(z, 0.0) - jnp.log(1.0 + jnp.exp(-jnp.abs(z)))) * (1.0 / GLA_GATE_NORM)
        l_hi, l_mid, l_lo = _split3(la)
        gcum_ref[rows, :] = (jnp.dot(tri, l_hi, preferred_element_type=F32)
                             + jnp.dot(tri, l_mid, preferred_element_type=F32)
                             + jnp.dot(tri, l_lo, preferred_element_type=F32))

    state_ref[...] = jnp.zeros_like(state_ref)
    lane_k = lax.broadcasted_iota(I32, (1, PAIR), 1)
    row_v = lax.broadcasted_iota(I32, (PAIR_V, PAIR), 0)
    col_k = lax.broadcasted_iota(I32, (PAIR_V, PAIR), 1)
    same_head = (row_v >= GLA_DV) == (col_k >= GLA_DK)
    cr = lax.broadcasted_iota(I32, (CHUNK, CHUNK), 0)
    cs = lax.broadcasted_iota(I32, (CHUNK, CHUNK), 1)
    causal = cs <= cr
    scale = GLA_DK ** -0.5

    def chunk(n):
        rows = pl.ds(pl.multiple_of(n * CHUNK, CHUNK), CHUNK)
        gc = gcum_ref[rows, :]
        g_last = gcum_ref[pl.ds(n * CHUNK + CHUNK - 1, 1), :]
        qf = q_ref[rows, :].astype(F32) * scale
        kf = k_ref[rows, :].astype(F32)
        q_s = (qf * jnp.exp(gc)).astype(BF16)
        k_s = (kf * jnp.exp(-gc)).astype(BF16)
        k_d = (kf * jnp.exp(g_last - gc)).astype(BF16)
        decay = jnp.exp(g_last)
        for pr in range(GLA_HEADS // 2):
            kl = slice(pr * PAIR, (pr + 1) * PAIR)
            vl = slice(pr * PAIR_V, (pr + 1) * PAIR_V)
            qs_p, ks_p, kd_p = q_s[:, kl], k_s[:, kl], k_d[:, kl]
            v_p = v_ref[rows, vl]
            st = state_ref[pr]
            o_inter = _nt_dot(qs_p, st.astype(BF16))
            d_st = _tn_dot(v_p, kd_p)
            state_ref[pr] = st * decay[:, kl] + jnp.where(same_head, d_st, 0.0)
            for sub in range(2):
                hd = 2 * pr + sub
                in_head = (lane_k >= sub * GLA_DK) & (lane_k < (sub + 1) * GLA_DK)
                a = _nt_dot(jnp.where(in_head, qs_p, jnp.zeros_like(qs_p)), ks_p)
                a = jnp.where(causal, a, 0.0).astype(BF16)
                vs = slice(hd * GLA_DV, (hd + 1) * GLA_DV)
                o = (jnp.dot(a, v_ref[rows, vs], preferred_element_type=F32)
                     + o_inter[:, sub * GLA_DV:(sub + 1) * GLA_DV])
                inv = lax.rsqrt(jnp.mean(o * o, axis=-1, keepdims=True) + EPS)
                r = r_ref[rows, vs].astype(F32)
                o_ref[rows, vs] = (o * inv * g_ref[...] * (r * jax.nn.sigmoid(r))).astype(BF16)

    def chunk_group(t, _):
        for u in range(GLA_UNROLL):
            chunk(t * GLA_UNROLL + u)
        return 0

    lax.fori_loop(0, N_CHUNK // GLA_UNROLL, chunk_group, 0)


def _gla(qkv_g, glr, w_up, b_up, gla_norm_g):
    qk_w = GLA_HEADS * GLA_DK
    v_w = GLA_HEADS * GLA_DV
    return pl.pallas_call(
        _gla_kernel,
        grid=(BATCH,),
        in_specs=[
            pl.BlockSpec((None, SEQ, qk_w), lambda b: (b, 0, 0)),
            pl.BlockSpec((None, SEQ, qk_w), lambda b: (b, 0, 1)),
            pl.BlockSpec((None, SEQ, v_w), lambda b: (b, 0, 1)),
            pl.BlockSpec((None, SEQ, v_w), lambda b: (b, 0, 2)),
            pl.BlockSpec((None, SEQ, LANES), lambda b: (b, 0, 0)),
            pl.BlockSpec((LANES, qk_w), lambda b: (0, 0)),
            pl.BlockSpec((1, qk_w), lambda b: (0, 0)),
            pl.BlockSpec((1, GLA_DV), lambda b: (0, 0)),
        ],
        out_specs=pl.BlockSpec((None, SEQ, v_w), lambda b: (b, 0, 0)),
        out_shape=jax.ShapeDtypeStruct((BATCH, SEQ, v_w), BF16),
        scratch_shapes=[
            pltpu.VMEM((SEQ, qk_w), F32),
            pltpu.VMEM((GLA_HEADS // 2, PAIR_V, PAIR), F32),
        ],
        compiler_params=pltpu.CompilerParams(
            dimension_semantics=("arbitrary",), vmem_limit_bytes=VMEM_LIMIT),
        name="gla",
    )(qkv_g, qkv_g, qkv_g, qkv_g, glr, w_up, b_up, gla_norm_g)


def _merge_kernel(oa_ref, ob_ref, gate_ref, x_ref, gt1_ref, sh2_ref, sc2_ref, g2_ref,
                  wpa_ref, wpb_ref, wo_ref, wr_ref, br_ref,
                  x1_ref, h2p_ref, eidx_ref, rank_ref, wts_ref, cnt_ref,
                  upper_ref, carry_ref):
    i = pl.program_id(0)

    @pl.when(i == 0)
    def _():
        rr = lax.broadcasted_iota(I32, (TM_MERGE, TM_MERGE), 0)
        cc = lax.broadcasted_iota(I32, (TM_MERGE, TM_MERGE), 1)
        upper_ref[...] = jnp.where(rr < cc, 1.0, 0.0).astype(BF16)
        carry_ref[...] = jnp.zeros_like(carry_ref)

    ga = gate_ref[:, :D_MODEL].astype(F32)
    gb = gate_ref[:, D_MODEL:].astype(F32)
    merged = (jax.nn.sigmoid(ga) * jnp.dot(oa_ref[...], wpa_ref[...], preferred_element_type=F32)
              + jax.nn.sigmoid(gb) * jnp.dot(ob_ref[...], wpb_ref[...], preferred_element_type=F32))
    y = jnp.dot(merged.astype(BF16), wo_ref[...], preferred_element_type=F32)
    x1 = x_ref[...] + gt1_ref[...] * y
    x1_ref[...] = x1
    inv = lax.rsqrt(jnp.mean(x1 * x1, axis=-1, keepdims=True) + EPS)
    h2 = (x1 * inv * g2_ref[...]) * (1.0 + sc2_ref[...]) + sh2_ref[...]
    h2p_ref[...] = _pack_halves(h2)

    h_hi, h_mid, _ = _split3(h2)
    w_hi, w_mid, _ = _split3(wr_ref[...])
    logits = (_nt_dot(w_hi, h_hi) + _nt_dot(w_hi, h_mid) + _nt_dot(w_mid, h_hi)) + br_ref[...]

    eio = lax.broadcasted_iota(I32, (N_EXPERTS, TM_MERGE), 0)
    vals, idxs, sels = [], [], []
    cur = logits
    for _k in range(TOP_K):
        m = jnp.max(cur, axis=0, keepdims=True)
        idx = jnp.min(jnp.where(cur == m, eio, N_EXPERTS), axis=0, keepdims=True)
        sel = eio == idx
        vals.append(m)
        idxs.append(idx)
        sels.append(sel)
        cur = jnp.where(sel, -jnp.inf, cur)
    es = [jnp.exp(v - vals[0]) for v in vals]
    tot = es[0] + es[1] + es[2] + es[3]
    onehot = jnp.zeros((N_EXPERTS, TM_MERGE), F32)
    for sel in sels:
        onehot = onehot + jnp.where(sel, 1.0, 0.0)
    before = jnp.dot(onehot.astype(BF16), upper_ref[...], preferred_element_type=F32) + carry_ref[:, 0:1]
    ranks = [jnp.sum(jnp.where(sel, before, 0.0), axis=0, keepdims=True) for sel in sels]
    carry_ref[...] = carry_ref[...] + jnp.sum(onehot, axis=1, keepdims=True)
    cnt_ref[...] = carry_ref[...]

    zi = jnp.zeros((8 - TOP_K, TM_MERGE), I32)
    zf = jnp.zeros((8 - TOP_K, TM_MERGE), F32)
    eidx_ref[...] = jnp.concatenate(idxs + [zi], axis=0)
    rank_ref[...] = jnp.concatenate([r.astype(I32) for r in ranks] + [zi], axis=0)
    wts_ref[...] = jnp.concatenate([e / tot for e in es] + [zf], axis=0)


def _merge_route(o_a, o_b, gates, x2d, mod4, norm2_g, w_pa, w_pb, w_o, w_rt, b_r):
    ntile = N_TOK // TM_MERGE
    per_b = SEQ // TM_MERGE
    full = lambda shape: pl.BlockSpec(shape, lambda i: (0,) * len(shape))
    row = lambda w: pl.BlockSpec((TM_MERGE, w), lambda i: (i, 0))
    modspec = lambda j: pl.BlockSpec((None, None, 1, D_MODEL), lambda i: (j, i // per_b, 0, 0))
    col = pl.BlockSpec((8, TM_MERGE), lambda i: (0, i))
    return pl.pallas_call(
        _merge_kernel,
        grid=(ntile,),
        in_specs=[
            row(DIFF_HEADS * 2 * DIFF_DH), row(GLA_HEADS * GLA_DV), row(W_GATE), row(D_MODEL),
            modspec(2), modspec(3), modspec(4),
            full((1, D_MODEL)),
            full((DIFF_HEADS * 2 * DIFF_DH, D_MODEL)), full((GLA_HEADS * GLA_DV, D_MODEL)),
            full((D_MODEL, D_MODEL)),
            full((N_EXPERTS, D_MODEL)), full((N_EXPERTS, 1)),
        ],
        out_specs=[
            row(D_MODEL), row(ROW_W), col, col, col,
            pl.BlockSpec((N_EXPERTS, LANES), lambda i: (0, 0)),
        ],
        out_shape=[
            jax.ShapeDtypeStruct((N_TOK, D_MODEL), F32),
            jax.ShapeDtypeStruct((N_TOK, ROW_W), ROW_DT),
            jax.ShapeDtypeStruct((8, N_TOK), I32),
            jax.ShapeDtypeStruct((8, N_TOK), I32),
            jax.ShapeDtypeStruct((8, N_TOK), F32),
            jax.ShapeDtypeStruct((N_EXPERTS, LANES), F32),
        ],
        scratch_shapes=[
            pltpu.VMEM((TM_MERGE, TM_MERGE), BF16),
            pltpu.VMEM((N_EXPERTS, LANES), F32),
        ],
        compiler_params=pltpu.CompilerParams(
            dimension_semantics=("arbitrary",), vmem_limit_bytes=VMEM_LIMIT),
        name="merge_route",
    )(o_a, o_b, gates, x2d, mod4, mod4, mod4, norm2_g, w_pa, w_pb, w_o, w_rt, b_r)


TP = 4096
NB_PAD = ((N_BLK + LANES - 1) // LANES) * LANES


def _plan_kernel(cnt_ref, eidx_ref, rank_ref, dest_ref, be_ref, nv_ref):
    cnt = cnt_ref[...]
    padded = jnp.floor((cnt + (BLK - 1.0)) * (1.0 / BLK)) * BLK
    er = lax.broadcasted_iota(I32, (N_EXPERTS, N_EXPERTS), 0)
    ec = lax.broadcasted_iota(I32, (N_EXPERTS, N_EXPERTS), 1)
    lower = jnp.where(ec < er, 1.0, 0.0).astype(BF16)
    p_hi, p_mid, p_lo = _split3(padded)
    starts = (jnp.dot(lower, p_hi, preferred_element_type=F32)
              + jnp.dot(lower, p_mid, preferred_element_type=F32)
              + jnp.dot(lower, p_lo, preferred_element_type=F32))
    ends = starts + padded
    blk_start = (lax.broadcasted_iota(I32, (1, NB_PAD), 1) * BLK).astype(F32)
    n_before = jnp.sum(jnp.where(ends[:, 0:1] <= blk_start, 1.0, 0.0), axis=0, keepdims=True)
    be_ref[...] = jnp.minimum(n_before, N_EXPERTS - 1.0).astype(I32)
    nv_ref[...] = (jnp.max(ends, axis=0, keepdims=True) * (1.0 / BLK)).astype(I32)

    eio = lax.broadcasted_iota(I32, (N_EXPERTS, TP), 0)
    rows = []
    for k in range(TOP_K):
        onehot = eio == eidx_ref[k:k + 1, :]
        base = jnp.sum(jnp.where(onehot, starts[:, 0:1], 0.0), axis=0, keepdims=True)
        rows.append(base.astype(I32) + rank_ref[k:k + 1, :])
    dest_ref[...] = jnp.concatenate(rows + [jnp.zeros((8 - TOP_K, TP), I32)], axis=0)


def _route_plan(cnt, eidx, rank):
    return pl.pallas_call(
        _plan_kernel,
        grid=(N_TOK // TP,),
        in_specs=[
            pl.BlockSpec((N_EXPERTS, LANES), lambda i: (0, 0)),
            pl.BlockSpec((8, TP), lambda i: (0, i)),
            pl.BlockSpec((8, TP), lambda i: (0, i)),
        ],
        out_specs=[
            pl.BlockSpec((8, TP), lambda i: (0, i)),
            pl.BlockSpec((1, NB_PAD), lambda i: (0, 0)),
            pl.BlockSpec((1, LANES), lambda i: (0, 0)),
        ],
        out_shape=[
            jax.ShapeDtypeStruct((8, N_TOK), I32),
            jax.ShapeDtypeStruct((1, NB_PAD), I32),
            jax.ShapeDtypeStruct((1, LANES), I32),
        ],
        compiler_params=pltpu.CompilerParams(dimension_semantics=("arbitrary",)),
        name="route_plan",
    )(cnt, eidx, rank)


def _ffn_kernel(be_ref, nv_ref, x_ref, wgu_ref, bgu_ref, wd_ref, bd_ref, y_ref):
    i = pl.program_id(0)
    valid = i < nv_ref[0]

    @pl.when(valid)
    def _():
        lo, hi = _unpack_halves(x_ref[...])
        x = jnp.concatenate([lo, hi], axis=1).astype(BF16)
        mm = lambda a, w: lax.dot_general(a, w, (((1,), (0,)), ((), ())), preferred_element_type=F32)
        gu = mm(x, wgu_ref[...]) + bgu_ref[...]
        gate = jnp.minimum(gu[:, :D_FF], SWIGLU_LIMIT)
        up = jnp.clip(gu[:, D_FF:], -SWIGLU_LIMIT, SWIGLU_LIMIT)
        act = (up + 1.0) * (gate * jax.nn.sigmoid(SWIGLU_ALPHA * gate))
        y = mm(act.astype(BF16), wd_ref[...]) + bd_ref[...]
        y_ref[...] = _pack_halves(y)

    @pl.when(jnp.logical_not(valid))
    def _():
        y_ref[...] = jnp.zeros_like(y_ref)


def _expert_ffn(blk_expert, n_valid, xb, w_gate_up, b_gate_up, w_down, b_down):
    grid_spec = pltpu.PrefetchScalarGridSpec(
        num_scalar_prefetch=2,
        grid=(N_BLK,),
        in_specs=[
            pl.BlockSpec((BLK, ROW_W), lambda i, be, nv: (i, 0)),
            pl.BlockSpec((None, D_MODEL, 2 * D_FF), lambda i, be, nv: (be[i], 0, 0)),
            pl.BlockSpec((None, 1, 2 * D_FF), lambda i, be, nv: (be[i], 0, 0)),
            pl.BlockSpec((None, D_FF, D_MODEL), lambda i, be, nv: (be[i], 0, 0)),
            pl.BlockSpec((None, 1, D_MODEL), lambda i, be, nv: (be[i], 0, 0)),
        ],
        out_specs=pl.BlockSpec((BLK, ROW_W), lambda i, be, nv: (i, 0)),
    )
    return pl.pallas_call(
        _ffn_kernel,
        grid_spec=grid_spec,
        out_shape=jax.ShapeDtypeStruct((P_ROWS, ROW_W), ROW_DT),
        compiler_params=pltpu.CompilerParams(
            dimension_semantics=("arbitrary",), vmem_limit_bytes=VMEM_LIMIT),
        name="expert_ffn",
    )(blk_expert, n_valid, xb, w_gate_up, b_gate_up.reshape(N_EXPERTS, 1, 2 * D_FF),
      w_down, b_down.reshape(N_EXPERTS, 1, D_MODEL))


def _final_kernel(x1_ref, y0_ref, y1_ref, y2_ref, y3_ref, w_ref, gt2_ref, g_ref, o_ref):
    w = w_ref[...]
    ylo = jnp.zeros((TM_FIN, HALF), F32)
    yhi = jnp.zeros((TM_FIN, HALF), F32)
    for k, y_ref in enumerate((y0_ref, y1_ref, y2_ref, y3_ref)):
        lo, hi = _unpack_halves(y_ref[...])
        wk = w[:, k:k + 1]
        ylo = ylo + wk * lo
        yhi = yhi + wk * hi
    y = jnp.concatenate([ylo, yhi], axis=1)
    x2 = x1_ref[...] + gt2_ref[...] * y
    inv = lax.rsqrt(jnp.mean(x2 * x2, axis=-1, keepdims=True) + EPS)
    o_ref[...] = x2 * inv * g_ref[...]


def _final(x1, yg, w4, mod4, final_norm_g):
    per_b = SEQ // TM_FIN
    ntile = N_TOK // TM_FIN
    slot = lambda k: pl.BlockSpec((TM_FIN, ROW_W), lambda i: (k * ntile + i, 0))
    return pl.pallas_call(
        _final_kernel,
        grid=(ntile,),
        in_specs=[
            pl.BlockSpec((TM_FIN, D_MODEL), lambda i: (i, 0)),
            slot(0), slot(1), slot(2), slot(3),
            pl.BlockSpec((TM_FIN, TOP_K), lambda i: (i, 0)),
            pl.BlockSpec((None, None, 1, D_MODEL), lambda i: (5, i // per_b, 0, 0)),
            pl.BlockSpec((1, D_MODEL), lambda i: (0, 0)),
        ],
        out_specs=pl.BlockSpec((TM_FIN, D_MODEL), lambda i: (i, 0)),
        out_shape=jax.ShapeDtypeStruct((N_TOK, D_MODEL), F32),
        compiler_params=pltpu.CompilerParams(
            dimension_semantics=("arbitrary",), vmem_limit_bytes=VMEM_LIMIT),
        name="combine_final",
    )(x1, yg, yg, yg, yg, w4, mod4, final_norm_g)


SC_CORES = 2
SC_SUBCORES = 16
SC_WORKERS = SC_CORES * SC_SUBCORES
SC_CHUNK = 64


def _sc_mesh():
    return plsc.VectorSubcoreMesh(core_axis_name="c", subcore_axis_name="s")


def _dispatch_rows(h2p, dest_flat):
    per_w = N_TOK // SC_WORKERS
    nchunk = per_w // SC_CHUNK

    @functools.partial(
        pl.kernel, mesh=_sc_mesh(), out_type=jax.ShapeDtypeStruct((P_ROWS, HALF), U32),
        scratch_types=[pltpu.VMEM((SC_CHUNK,), I32)] * TOP_K
        + [pltpu.VMEM((SC_CHUNK, HALF), U32), pltpu.SemaphoreType.DMA],
        name="moe_dispatch")
    def k(src_hbm, dest_hbm, out_hbm, i0, i1, i2, i3, rows_v, sem):
        idx = (i0, i1, i2, i3)
        wid = lax.axis_index("s") * SC_CORES + lax.axis_index("c")

        @pl.loop(0, nchunk)
        def _(j):
            t0 = wid * per_w + j * SC_CHUNK
            pltpu.sync_copy(src_hbm.at[pl.ds(t0, SC_CHUNK)], rows_v)
            for kk in range(TOP_K):
                pltpu.sync_copy(dest_hbm.at[pl.ds(kk * N_TOK + t0, SC_CHUNK)], idx[kk])
            for kk in range(TOP_K):
                pltpu.sync_copy(rows_v, out_hbm.at[idx[kk]])

    return k(h2p, dest_flat)


def _combine_rows(yb, dest_flat):
    n_out = TOP_K * N_TOK
    per_w = n_out // SC_WORKERS
    nchunk = per_w // SC_CHUNK

    @functools.partial(
        pl.kernel, mesh=_sc_mesh(), out_type=jax.ShapeDtypeStruct((n_out, HALF), U32),
        scratch_types=[pltpu.VMEM((SC_CHUNK,), I32), pltpu.VMEM((SC_CHUNK, HALF), U32),
                       pltpu.SemaphoreType.DMA],
        name="moe_combine")
    def k(tab_hbm, idx_hbm, out_hbm, idx_v, rows_v, sem):
        wid = lax.axis_index("s") * SC_CORES + lax.axis_index("c")

        @pl.loop(0, nchunk)
        def _(j):
            a0 = wid * per_w + j * SC_CHUNK
            pltpu.sync_copy(idx_hbm.at[pl.ds(a0, SC_CHUNK)], idx_v)
            pltpu.sync_copy(tab_hbm.at[idx_v], rows_v)
            pltpu.sync_copy(rows_v, out_hbm.at[pl.ds(a0, SC_CHUNK)])

    return k(yb, dest_flat)


def _lambda_kernel(p_ref, o_ref):
    p = p_ref[...]
    s1 = jnp.sum(p[0:1] * p[1:2], axis=-1, keepdims=True)
    s2 = jnp.sum(p[2:3] * p[3:4], axis=-1, keepdims=True)
    o_ref[...] = jnp.broadcast_to(jnp.exp(s1) - jnp.exp(s2) + LAMBDA_INIT, (1, LANES))


def kernel(x, c, w_ada, b_ada, norm1_g, w_in, lambda_q1, lambda_k1, lambda_q2, lambda_k2, diff_norm_g, w_alpha_up, b_alpha, gla_norm_g, w_branch_diff, w_branch_gla, w_out, norm2_g, w_router, b_router, w_gate_up, b_gate_up, w_down, b_down, final_norm_g):
    w_in0 = w_in[0]
    c_a, c_g = W_A, W_A + W_G
    w_a = w_in0[:, :c_a].astype(BF16)
    w_g = w_in0[:, c_a:c_g].astype(BF16)
    w_lr = jnp.pad(w_in0[:, c_g:c_g + GLA_RANK], ((0, 0), (0, LANES - GLA_RANK))).astype(BF16)
    w_gate = w_in0[:, c_g + GLA_RANK:].astype(BF16)
    w_up = jnp.pad(w_alpha_up[0], ((0, LANES - GLA_RANK), (0, 0)))
    lam_in = jnp.concatenate([lambda_q1, lambda_k1, lambda_q2, lambda_k2], axis=0)
    slopes = jnp.asarray(2.0 ** (-8.0 * np.arange(1, DIFF_HEADS + 1) / DIFF_HEADS), dtype=F32)

    mod = _modulation(c, w_ada[0], b_ada[0])
    mod4 = mod.reshape(N_MOD, BATCH, 1, D_MODEL)
    lam = pl.pallas_call(
        _lambda_kernel, out_shape=jax.ShapeDtypeStruct((1, LANES), F32), name="lambda")(lam_in)[0, :1]

    qkv_a, qkv_g, gates, glr = _in_proj(x, mod4, norm1_g, w_a, w_g, w_gate, w_lr)
    pos = jnp.arange(SEQ, dtype=I32)
    kaug = jnp.zeros((SEQ, LANES), F32).at[:, 0].set((pos >> 3).astype(F32)).at[:, 1].set(
        (pos & 7).astype(F32)).astype(BF16)
    o_a = _diff_attention(qkv_a, slopes, lam, kaug, diff_norm_g)
    o_b = _gla(qkv_g, glr, w_up, b_alpha, gla_norm_g)

    x1, h2p, eidx, rank, wts, cnt = _merge_route(
        o_a.reshape(N_TOK, -1), o_b.reshape(N_TOK, -1), gates.reshape(N_TOK, W_GATE),
        x.reshape(N_TOK, D_MODEL), mod4, norm2_g,
        w_branch_diff[0].astype(BF16), w_branch_gla[0].astype(BF16), w_out[0].astype(BF16),
        w_router[0].T, b_router[0].reshape(N_EXPERTS, 1))

    dest8, be, nv = _route_plan(cnt, eidx, rank)
    dest = dest8[:TOP_K].reshape(-1)

    xb = _dispatch_rows(h2p, dest)
    yb = _expert_ffn(be[0, :N_BLK], nv[0, :1], xb, w_gate_up[0], b_gate_up[0], w_down[0], b_down[0])
    yg = _combine_rows(yb, dest)

    out = _final(x1, yg, wts[:TOP_K].T, mod4, final_norm_g.reshape(1, D_MODEL))
    return out.reshape(BATCH, SEQ, D_MODEL)
```

```python
import functools
import math

import jax
import jax.numpy as jnp
import numpy as np
from jax import lax
from jax.experimental import pallas as pl
from jax.experimental.pallas import tpu as pltpu
from jax.experimental.pallas import tpu_sc as plsc

F32 = jnp.float32
BF16 = jnp.bfloat16
U32 = jnp.uint32
I32 = jnp.int32

D_MODEL = 1024
BATCH = 16
SEQ = 2048
N_TOK = BATCH * SEQ
CHUNK = 64
DIFF_HEADS = 4
DIFF_DH = 64
GLA_HEADS = 4
GLA_DK = 64
GLA_DV = 128
GLA_RANK = 16
GLA_GATE_NORM = 16.0
N_EXPERTS = 32
TOP_K = 4
D_FF = D_MODEL
SWIGLU_LIMIT = 7.0
SWIGLU_ALPHA = 1.702
N_MOD = 6
EPS = 1e-6
LAMBDA_INIT = 0.8 - 0.6 * math.exp(-0.3 * 0)

LANES = 128
HALF = D_MODEL // 2
ROW_W = HALF
ROW_DT = U32

TM_IN = 512
TQ = 512
TM_MERGE = 512
BLK = 512
N_BLK = (N_TOK * TOP_K) // BLK + N_EXPERTS
P_ROWS = N_BLK * BLK
TM_FIN = 512
VMEM_LIMIT = 56 * 1024 * 1024


def _nt_dot(a, b):
    return lax.dot_general(a, b, (((1,), (1,)), ((), ())), preferred_element_type=F32)


def _tn_dot(a, b):
    return lax.dot_general(a, b, (((0,), (0,)), ((), ())), preferred_element_type=F32)


def _split3(x):
    hi = x.astype(BF16)
    r1 = x - hi.astype(F32)
    mid = r1.astype(BF16)
    lo = (r1 - mid.astype(F32)).astype(BF16)
    return hi, mid, lo


def _pack_halves(y):
    return pltpu.pack_elementwise([y[:, :HALF], y[:, HALF:]], packed_dtype=BF16)


def _unpack_halves(u):
    lo = pltpu.unpack_elementwise(u, index=0, packed_dtype=BF16, unpacked_dtype=F32)
    hi = pltpu.unpack_elementwise(u, index=1, packed_dtype=BF16, unpacked_dtype=F32)
    return lo, hi


def _mod_kernel(c_ref, w_ref, b_ref, o_ref):
    c = c_ref[...]
    s = c * jax.nn.sigmoid(c)
    o_ref[0] = jnp.dot(s.astype(BF16), w_ref[...].astype(BF16),
                       preferred_element_type=F32) + b_ref[...]


def _modulation(c, w_ada, b_ada):
    return pl.pallas_call(
        _mod_kernel,
        grid=(N_MOD,),
        in_specs=[
            pl.BlockSpec((BATCH, D_MODEL), lambda j: (0, 0)),
            pl.BlockSpec((D_MODEL, D_MODEL), lambda j: (0, j)),
            pl.BlockSpec((1, D_MODEL), lambda j: (0, j)),
        ],
        out_specs=pl.BlockSpec((1, BATCH, D_MODEL), lambda j: (j, 0, 0)),
        out_shape=jax.ShapeDtypeStruct((N_MOD, BATCH, D_MODEL), F32),
        compiler_params=pltpu.CompilerParams(dimension_semantics=("arbitrary",)),
        name="adaln_mod",
    )(c, w_ada, b_ada.reshape(1, N_MOD * D_MODEL))


W_A = 3 * DIFF_HEADS * 2 * DIFF_DH
W_G = 2 * GLA_HEADS * GLA_DK + 2 * GLA_HEADS * GLA_DV
W_GATE = 2 * D_MODEL


def _in_kernel(x_ref, sh_ref, sc_ref, g_ref, wa_ref, wg_ref, wgate_ref, wlr_ref,
               oa_ref, og_ref, ogate_ref, olr_ref):
    x = x_ref[...]
    inv = lax.rsqrt(jnp.mean(x * x, axis=-1, keepdims=True) + EPS)
    h = (x * inv * g_ref[...]) * (1.0 + sc_ref[...]) + sh_ref[...]
    hb = h.astype(BF16)
    oa_ref[...] = jnp.dot(hb, wa_ref[...], preferred_element_type=F32).astype(BF16)
    og_ref[...] = jnp.dot(hb, wg_ref[...], preferred_element_type=F32).astype(BF16)
    ogate_ref[...] = jnp.dot(hb, wgate_ref[...], preferred_element_type=F32).astype(BF16)
    olr_ref[...] = jnp.dot(hb, wlr_ref[...], preferred_element_type=F32)


def _in_proj(x, mod4, norm1_g, w_a, w_g, w_gate, w_lr):
    nrow = SEQ // TM_IN
    full = lambda shape: pl.BlockSpec(shape, lambda b, i: (0,) * len(shape))
    return pl.pallas_call(
        _in_kernel,
        grid=(BATCH, nrow),
        in_specs=[
            pl.BlockSpec((None, TM_IN, D_MODEL), lambda b, i: (b, i, 0)),
            pl.BlockSpec((None, None, 1, D_MODEL), lambda b, i: (0, b, 0, 0)),
            pl.BlockSpec((None, None, 1, D_MODEL), lambda b, i: (1, b, 0, 0)),
            full((1, D_MODEL)),
            full((D_MODEL, W_A)),
            full((D_MODEL, W_G)),
            full((D_MODEL, W_GATE)),
            full((D_MODEL, LANES)),
        ],
        out_specs=[
            pl.BlockSpec((None, TM_IN, W_A), lambda b, i: (b, i, 0)),
            pl.BlockSpec((None, TM_IN, W_G), lambda b, i: (b, i, 0)),
            pl.BlockSpec((None, TM_IN, W_GATE), lambda b, i: (b, i, 0)),
            pl.BlockSpec((None, TM_IN, LANES), lambda b, i: (b, i, 0)),
        ],
        out_shape=[
            jax.ShapeDtypeStruct((BATCH, SEQ, W_A), BF16),
            jax.ShapeDtypeStruct((BATCH, SEQ, W_G), BF16),
            jax.ShapeDtypeStruct((BATCH, SEQ, W_GATE), BF16),
            jax.ShapeDtypeStruct((BATCH, SEQ, LANES), F32),
        ],
        compiler_params=pltpu.CompilerParams(
            dimension_semantics=("arbitrary", "arbitrary"), vmem_limit_bytes=VMEM_LIMIT),
        name="in_proj",
    )(x, mod4, mod4, norm1_g, w_a, w_g, w_gate, w_lr)


def _attn_kernel(slope_ref, lam_ref, q_ref, k_ref, v_ref, kaug_ref, g_ref, o_ref,
                 sa_ref, sb_ref, corr_ref, m_ref, l_ref, acc_ref):
    slope = slope_ref[pl.program_id(1)]
    lam = lam_ref[0]
    lane = lax.broadcasted_iota(I32, (TQ, 2 * DIFF_DH), 1)
    aug = jnp.where(lane == 0, 8.0 * slope, jnp.where(lane == 1, slope, 0.0)).astype(BF16)

    def blk(j):
        return slice(j * TQ, (j + 1) * TQ)

    def stacked_queries(qi):
        q = q_ref[blk(qi), :] * jnp.asarray(DIFF_DH ** -0.5, BF16)
        zero = jnp.zeros_like(q)
        return jnp.concatenate([
            jnp.concatenate([jnp.where(lane < DIFF_DH, q, zero), aug], axis=1),
            jnp.concatenate([jnp.where(lane >= DIFF_DH, q, zero), aug], axis=1)], axis=0)

    def scores(lhs, j):
        kk = jnp.concatenate([k_ref[blk(j), :], kaug_ref[blk(j), :]], axis=1)
        return _nt_dot(kk, lhs)

    def update(s, j, first):
        if first:
            m_new = jnp.max(s, axis=0, keepdims=True)
            p = jnp.exp(s - m_new)
            l_ref[...] = jnp.sum(p, axis=0, keepdims=True)
            acc_ref[...] = _tn_dot(v_ref[blk(j), :], p.astype(BF16))
        else:
            m = m_ref[...]
            m_new = jnp.maximum(m, jnp.max(s, axis=0, keepdims=True))
            alpha = jnp.exp(m - m_new)
            p = jnp.exp(s - m_new)
            l_ref[...] = alpha * l_ref[...] + jnp.sum(p, axis=0, keepdims=True)
            acc_ref[...] = alpha * acc_ref[...] + _tn_dot(v_ref[blk(j), :], p.astype(BF16))
        m_ref[...] = m_new

    kr = lax.broadcasted_iota(I32, (TQ, TQ), 0)
    qc = lax.broadcasted_iota(I32, (TQ, TQ), 1)
    ahead = jnp.maximum(kr - qc, 0).astype(F32)
    corr_ref[...] = jnp.where((qc >> 6) >= (kr >> 6), (-2.0 * slope) * ahead, -jnp.inf)

    pairs = [(qi, j) for qi in range(SEQ // TQ) for j in range(qi + 1)]
    bufs = (sa_ref, sb_ref)
    lhs = stacked_queries(0)
    bufs[0][...] = scores(lhs, 0)
    for t, (qi, j) in enumerate(pairs):
        if t + 1 < len(pairs):
            nqi, nj = pairs[t + 1]
            if nqi != qi:
                lhs = stacked_queries(nqi)
            bufs[(t + 1) % 2][...] = scores(lhs, nj)
        s = bufs[t % 2][...]
        if j == qi:
            corr = corr_ref[...]
            s = s + jnp.concatenate([corr, corr], axis=1)
        update(s, j, first=(j == 0))
        if j == qi:
            ot = acc_ref[...] / l_ref[...]
            o = (ot[:, :TQ] - lam * ot[:, TQ:]).T
            inv = lax.rsqrt(jnp.mean(o * o, axis=-1, keepdims=True) + EPS)
            o_ref[blk(qi), :] = (o * inv * g_ref[...] * (1.0 - LAMBDA_INIT)).astype(BF16)


def _diff_attention(qkv_a, slopes, lam, kaug, diff_norm_g):
    return pl.pallas_call(
        _attn_kernel,
        grid=(BATCH, DIFF_HEADS),
        in_specs=[
            pl.BlockSpec(memory_space=pltpu.SMEM),
            pl.BlockSpec(memory_space=pltpu.SMEM),
            pl.BlockSpec((None, SEQ, LANES), lambda b, h: (b, 0, h)),
            pl.BlockSpec((None, SEQ, LANES), lambda b, h: (b, 0, DIFF_HEADS + h)),
            pl.BlockSpec((None, SEQ, LANES), lambda b, h: (b, 0, 2 * DIFF_HEADS + h)),
            pl.BlockSpec((SEQ, LANES), lambda b, h: (0, 0)),
            pl.BlockSpec((1, LANES), lambda b, h: (0, 0)),
        ],
        out_specs=pl.BlockSpec((None, SEQ, LANES), lambda b, h: (b, 0, h)),
        out_shape=jax.ShapeDtypeStruct((BATCH, SEQ, DIFF_HEADS * 2 * DIFF_DH), BF16),
        scratch_shapes=[
            pltpu.VMEM((TQ, 2 * TQ), F32), pltpu.VMEM((TQ, 2 * TQ), F32),
            pltpu.VMEM((TQ, TQ), F32),
            pltpu.VMEM((1, 2 * TQ), F32), pltpu.VMEM((1, 2 * TQ), F32),
            pltpu.VMEM((2 * DIFF_DH, 2 * TQ), F32),
        ],
        compiler_params=pltpu.CompilerParams(
            dimension_semantics=("arbitrary", "arbitrary")),
        name="diff_attn",
    )(slopes, lam, qkv_a, qkv_a, qkv_a, kaug, diff_norm_g)


N_CHUNK = SEQ // CHUNK
PAIR = 2 * GLA_DK
PAIR_V = 2 * GLA_DV
CS_ROWS = 256
GLA_UNROLL = 4


def _gla_kernel(q_ref, k_ref, v_ref, r_ref, lr_ref, wup_ref, bup_ref, g_ref, o_ref,
                gcum_ref, state_ref):
    w_hi, w_mid, _ = _split3(wup_ref[...])
    rr = lax.broadcasted_iota(I32, (CS_ROWS, CS_ROWS), 0)
    cc = lax.broadcasted_iota(I32, (CS_ROWS, CS_ROWS), 1)
    tri = jnp.where(((rr >> 6) == (cc >> 6)) & (cc <= rr), 1.0, 0.0).astype(BF16)
    for blk in range(SEQ // CS_ROWS):
        rows = pl.ds(blk * CS_ROWS, CS_ROWS)
        a_hi, a_mid, _ = _split3(lr_ref[rows, :])
        z = (jnp.dot(a_hi, w_hi, preferred_element_type=F32)
             + jnp.dot(a_hi, w_mid, preferred_element_type=F32)
             + jnp.dot(a_mid, w_hi, preferred_element_type=F32)) + bup_ref[...]
        la = (jnp.minimum(z, 0.0) - jnp.log(1.0 + jnp.exp(-jnp.abs(z)))) * (1.0 / GLA_GATE_NORM)
        l_hi, l_mid, l_lo = _split3(la)
        gcum_ref[rows, :] = (jnp.dot(tri, l_hi, preferred_element_type=F32)
                             + jnp.dot(tri, l_mid, preferred_element_type=F32)
                             + jnp.dot(tri, l_lo, preferred_element_type=F32))

    state_ref[...] = jnp.zeros_like(state_ref)
    lane_k = lax.broadcasted_iota(I32, (1, PAIR), 1)
    row_v = lax.broadcasted_iota(I32, (PAIR_V, PAIR), 0)
    col_k = lax.broadcasted_iota(I32, (PAIR_V, PAIR), 1)
    same_head = (row_v >= GLA_DV) == (col_k >= GLA_DK)
    cr = lax.broadcasted_iota(I32, (CHUNK, CHUNK), 0)
    cs = lax.broadcasted_iota(I32, (CHUNK, CHUNK), 1)
    causal = cs <= cr
    scale = GLA_DK ** -0.5

    def chunk(n):
        rows = pl.ds(pl.multiple_of(n * CHUNK, CHUNK), CHUNK)
        gc = gcum_ref[rows, :]
        g_last = gcum_ref[pl.ds(n * CHUNK + CHUNK - 1, 1), :]
        qf = q_ref[rows, :].astype(F32) * scale
        kf = k_ref[rows, :].astype(F32)
        q_s = (qf * jnp.exp(gc)).astype(BF16)
        k_s = (kf * jnp.exp(-gc)).astype(BF16)
        k_d = (kf * jnp.exp(g_last - gc)).astype(BF16)
        decay = jnp.exp(g_last)
        for pr in range(GLA_HEADS // 2):
            kl = slice(pr * PAIR, (pr + 1) * PAIR)
            vl = slice(pr * PAIR_V, (pr + 1) * PAIR_V)
            qs_p, ks_p, kd_p = q_s[:, kl], k_s[:, kl], k_d[:, kl]
            v_p = v_ref[rows, vl]
            st = state_ref[pr]
            o_inter = _nt_dot(qs_p, st.astype(BF16))
            d_st = _tn_dot(v_p, kd_p)
            state_ref[pr] = st * decay[:, kl] + jnp.where(same_head, d_st, 0.0)
            for sub in range(2):
                hd = 2 * pr + sub
                in_head = (lane_k >= sub * GLA_DK) & (lane_k < (sub + 1) * GLA_DK)
                a = _nt_dot(jnp.where(in_head, qs_p, jnp.zeros_like(qs_p)), ks_p)
                a = jnp.where(causal, a, 0.0).astype(BF16)
                vs = slice(hd * GLA_DV, (hd + 1) * GLA_DV)
                o = (jnp.dot(a, v_ref[rows, vs], preferred_element_type=F32)
                     + o_inter[:, sub * GLA_DV:(sub + 1) * GLA_DV])
                inv = lax.rsqrt(jnp.mean(o * o, axis=-1, keepdims=True) + EPS)
                r = r_ref[rows, vs].astype(F32)
                o_ref[rows, vs] = (o * inv * g_ref[...] * (r * jax.nn.sigmoid(r))).astype(BF16)

    def chunk_group(t, _):
        for u in range(GLA_UNROLL):
            chunk(t * GLA_UNROLL + u)
        return 0

    lax.fori_loop(0, N_CHUNK // GLA_UNROLL, chunk_group, 0)


def _gla(qkv_g, glr, w_up, b_up, gla_norm_g):
    qk_w = GLA_HEADS * GLA_DK
    v_w = GLA_HEADS * GLA_DV
    return pl.pallas_call(
        _gla_kernel,
        grid=(BATCH,),
        in_specs=[
            pl.BlockSpec((None, SEQ, qk_w), lambda b: (b, 0, 0)),
            pl.BlockSpec((None, SEQ, qk_w), lambda b: (b, 0, 1)),
            pl.BlockSpec((None, SEQ, v_w), lambda b: (b, 0, 1)),
            pl.BlockSpec((None, SEQ, v_w), lambda b: (b, 0, 2)),
            pl.BlockSpec((None, SEQ, LANES), lambda b: (b, 0, 0)),
            pl.BlockSpec((LANES, qk_w), lambda b: (0, 0)),
            pl.BlockSpec((1, qk_w), lambda b: (0, 0)),
            pl.BlockSpec((1, GLA_DV), lambda b: (0, 0)),
        ],
        out_specs=pl.BlockSpec((None, SEQ, v_w), lambda b: (b, 0, 0)),
        out_shape=jax.ShapeDtypeStruct((BATCH, SEQ, v_w), BF16),
        scratch_shapes=[
            pltpu.VMEM((SEQ, qk_w), F32),
            pltpu.VMEM((GLA_HEADS // 2, PAIR_V, PAIR), F32),
        ],
        compiler_params=pltpu.CompilerParams(
            dimension_semantics=("arbitrary",), vmem_limit_bytes=VMEM_LIMIT),
        name="gla",
    )(qkv_g, qkv_g, qkv_g, qkv_g, glr, w_up, b_up, gla_norm_g)


def _merge_kernel(oa_ref, ob_ref, gate_ref, x_ref, gt1_ref, sh2_ref, sc2_ref, g2_ref,
                  wpa_ref, wpb_ref, wo_ref, wr_ref, br_ref,
                  x1_ref, h2p_ref, eidx_ref, rank_ref, wts_ref, cnt_ref,
                  upper_ref, carry_ref):
    i = pl.program_id(0)

    @pl.when(i == 0)
    def _():
        rr = lax.broadcasted_iota(I32, (TM_MERGE, TM_MERGE), 0)
        cc = lax.broadcasted_iota(I32, (TM_MERGE, TM_MERGE), 1)
        upper_ref[...] = jnp.where(rr < cc, 1.0, 0.0).astype(BF16)
        carry_ref[...] = jnp.zeros_like(carry_ref)

    ga = gate_ref[:, :D_MODEL].astype(F32)
    gb = gate_ref[:, D_MODEL:].astype(F32)
    merged = (jax.nn.sigmoid(ga) * jnp.dot(oa_ref[...], wpa_ref[...], preferred_element_type=F32)
              + jax.nn.sigmoid(gb) * jnp.dot(ob_ref[...], wpb_ref[...], preferred_element_type=F32))
    y = jnp.dot(merged.astype(BF16), wo_ref[...], preferred_element_type=F32)
    x1 = x_ref[...] + gt1_ref[...] * y
    x1_ref[...] = x1
    inv = lax.rsqrt(jnp.mean(x1 * x1, axis=-1, keepdims=True) + EPS)
    h2 = (x1 * inv * g2_ref[...]) * (1.0 + sc2_ref[...]) + sh2_ref[...]
    h2p_ref[...] = _pack_halves(h2)

    h_hi, h_mid, _ = _split3(h2)
    w_hi, w_mid, _ = _split3(wr_ref[...])
    logits = (_nt_dot(w_hi, h_hi) + _nt_dot(w_hi, h_mid) + _nt_dot(w_mid, h_hi)) + br_ref[...]

    eio = lax.broadcasted_iota(I32, (N_EXPERTS, TM_MERGE), 0)
    vals, idxs, sels = [], [], []
    cur = logits
    for _k in range(TOP_K):
        m = jnp.max(cur, axis=0, keepdims=True)
        idx = jnp.min(jnp.where(cur == m, eio, N_EXPERTS), axis=0, keepdims=True)
        sel = eio == idx
        vals.append(m)
        idxs.append(idx)
        sels.append(sel)
        cur = jnp.where(sel, -jnp.inf, cur)
    es = [jnp.exp(v - vals[0]) for v in vals]
    tot = es[0] + es[1] + es[2] + es[3]
    onehot = jnp.zeros((N_EXPERTS, TM_MERGE), F32)
    for sel in sels:
        onehot = onehot + jnp.where(sel, 1.0, 0.0)
    before = jnp.dot(onehot.astype(BF16), upper_ref[...], preferred_element_type=F32) + carry_ref[:, 0:1]
    ranks = [jnp.sum(jnp.where(sel, before, 0.0), axis=0, keepdims=True) for sel in sels]
    carry_ref[...] = carry_ref[...] + jnp.sum(onehot, axis=1, keepdims=True)
    cnt_ref[...] = carry_ref[...]

    zi = jnp.zeros((8 - TOP_K, TM_MERGE), I32)
    zf = jnp.zeros((8 - TOP_K, TM_MERGE), F32)
    eidx_ref[...] = jnp.concatenate(idxs + [zi], axis=0)
    rank_ref[...] = jnp.concatenate([r.astype(I32) for r in ranks] + [zi], axis=0)
    wts_ref[...] = jnp.concatenate([e / tot for e in es] + [zf], axis=0)


def _merge_route(o_a, o_b, gates, x2d, mod4, norm2_g, w_pa, w_pb, w_o, w_rt, b_r):
    ntile = N_TOK // TM_MERGE
    per_b = SEQ // TM_MERGE
    full = lambda shape: pl.BlockSpec(shape, lambda i: (0,) * len(shape))
    row = lambda w: pl.BlockSpec((TM_MERGE, w), lambda i: (i, 0))
    modspec = lambda j: pl.BlockSpec((None, None, 1, D_MODEL), lambda i: (j, i // per_b, 0, 0))
    col = pl.BlockSpec((8, TM_MERGE), lambda i: (0, i))
    return pl.pallas_call(
        _merge_kernel,
        grid=(ntile,),
        in_specs=[
            row(DIFF_HEADS * 2 * DIFF_DH), row(GLA_HEADS * GLA_DV), row(W_GATE), row(D_MODEL),
            modspec(2), modspec(3), modspec(4),
            full((1, D_MODEL)),
            full((DIFF_HEADS * 2 * DIFF_DH, D_MODEL)), full((GLA_HEADS * GLA_DV, D_MODEL)),
            full((D_MODEL, D_MODEL)),
            full((N_EXPERTS, D_MODEL)), full((N_EXPERTS, 1)),
        ],
        out_specs=[
            row(D_MODEL), row(ROW_W), col, col, col,
            pl.BlockSpec((N_EXPERTS, LANES), lambda i: (0, 0)),
        ],
        out_shape=[
            jax.ShapeDtypeStruct((N_TOK, D_MODEL), F32),
            jax.ShapeDtypeStruct((N_TOK, ROW_W), ROW_DT),
            jax.ShapeDtypeStruct((8, N_TOK), I32),
            jax.ShapeDtypeStruct((8, N_TOK), I32),
            jax.ShapeDtypeStruct((8, N_TOK), F32),
            jax.ShapeDtypeStruct((N_EXPERTS, LANES), F32),
        ],
        scratch_shapes=[
            pltpu.VMEM((TM_MERGE, TM_MERGE), BF16),
            pltpu.VMEM((N_EXPERTS, LANES), F32),
        ],
        compiler_params=pltpu.CompilerParams(
            dimension_semantics=("arbitrary",), vmem_limit_bytes=VMEM_LIMIT),
        name="merge_route",
    )(o_a, o_b, gates, x2d, mod4, mod4, mod4, norm2_g, w_pa, w_pb, w_o, w_rt, b_r)


TP = 4096
NB_PAD = ((N_BLK + LANES - 1) // LANES) * LANES


def _plan_kernel(cnt_ref, eidx_ref, rank_ref, dest_ref, be_ref, nv_ref):
    cnt = cnt_ref[...]
    padded = jnp.floor((cnt + (BLK - 1.0)) * (1.0 / BLK)) * BLK
    er = lax.broadcasted_iota(I32, (N_EXPERTS, N_EXPERTS), 0)
    ec = lax.broadcasted_iota(I32, (N_EXPERTS, N_EXPERTS), 1)
    lower = jnp.where(ec < er, 1.0, 0.0).astype(BF16)
    p_hi, p_mid, p_lo = _split3(padded)
    starts = (jnp.dot(lower, p_hi, preferred_element_type=F32)
              + jnp.dot(lower, p_mid, preferred_element_type=F32)
              + jnp.dot(lower, p_lo, preferred_element_type=F32))
    ends = starts + padded
    blk_start = (lax.broadcasted_iota(I32, (1, NB_PAD), 1) * BLK).astype(F32)
    n_before = jnp.sum(jnp.where(ends[:, 0:1] <= blk_start, 1.0, 0.0), axis=0, keepdims=True)
    be_ref[...] = jnp.minimum(n_before, N_EXPERTS - 1.0).astype(I32)
    nv_ref[...] = (jnp.max(ends, axis=0, keepdims=True) * (1.0 / BLK)).astype(I32)

    eio = lax.broadcasted_iota(I32, (N_EXPERTS, TP), 0)
    rows = []
    for k in range(TOP_K):
        onehot = eio == eidx_ref[k:k + 1, :]
        base = jnp.sum(jnp.where(onehot, starts[:, 0:1], 0.0), axis=0, keepdims=True)
        rows.append(base.astype(I32) + rank_ref[k:k + 1, :])
    dest_ref[...] = jnp.concatenate(rows + [jnp.zeros((8 - TOP_K, TP), I32)], axis=0)


def _route_plan(cnt, eidx, rank):
    return pl.pallas_call(
        _plan_kernel,
        grid=(N_TOK // TP,),
        in_specs=[
            pl.BlockSpec((N_EXPERTS, LANES), lambda i: (0, 0)),
            pl.BlockSpec((8, TP), lambda i: (0, i)),
            pl.BlockSpec((8, TP), lambda i: (0, i)),
        ],
        out_specs=[
            pl.BlockSpec((8, TP), lambda i: (0, i)),
            pl.BlockSpec((1, NB_PAD), lambda i: (0, 0)),
            pl.BlockSpec((1, LANES), lambda i: (0, 0)),
        ],
        out_shape=[
            jax.ShapeDtypeStruct((8, N_TOK), I32),
            jax.ShapeDtypeStruct((1, NB_PAD), I32),
            jax.ShapeDtypeStruct((1, LANES), I32),
        ],
        compiler_params=pltpu.CompilerParams(dimension_semantics=("arbitrary",)),
        name="route_plan",
    )(cnt, eidx, rank)


def _ffn_kernel(be_ref, nv_ref, x_ref, wgu_ref, bgu_ref, wd_ref, bd_ref, y_ref):
    i = pl.program_id(0)
    valid = i < nv_ref[0]

    @pl.when(valid)
    def _():
        lo, hi = _unpack_halves(x_ref[...])
        x = jnp.concatenate([lo, hi], axis=1).astype(BF16)
        mm = lambda a, w: lax.dot_general(a, w, (((1,), (0,)), ((), ())), preferred_element_type=F32)
        gu = mm(x, wgu_ref[...]) + bgu_ref[...]
        gate = jnp.minimum(gu[:, :D_FF], SWIGLU_LIMIT)
        up = jnp.clip(gu[:, D_FF:], -SWIGLU_LIMIT, SWIGLU_LIMIT)
        act = (up + 1.0) * (gate * jax.nn.sigmoid(SWIGLU_ALPHA * gate))
        y = mm(act.astype(BF16), wd_ref[...]) + bd_ref[...]
        y_ref[...] = _pack_halves(y)

    @pl.when(jnp.logical_not(valid))
    def _():
        y_ref[...] = jnp.zeros_like(y_ref)


def _expert_ffn(blk_expert, n_valid, xb, w_gate_up, b_gate_up, w_down, b_down):
    grid_spec = pltpu.PrefetchScalarGridSpec(
        num_scalar_prefetch=2,
        grid=(N_BLK,),
        in_specs=[
            pl.BlockSpec((BLK, ROW_W), lambda i, be, nv: (i, 0)),
            pl.BlockSpec((None, D_MODEL, 2 * D_FF), lambda i, be, nv: (be[i], 0, 0)),
            pl.BlockSpec((None, 1, 2 * D_FF), lambda i, be, nv: (be[i], 0, 0)),
            pl.BlockSpec((None, D_FF, D_MODEL), lambda i, be, nv: (be[i], 0, 0)),
            pl.BlockSpec((None, 1, D_MODEL), lambda i, be, nv: (be[i], 0, 0)),
        ],
        out_specs=pl.BlockSpec((BLK, ROW_W), lambda i, be, nv: (i, 0)),
    )
    return pl.pallas_call(
        _ffn_kernel,
        grid_spec=grid_spec,
        out_shape=jax.ShapeDtypeStruct((P_ROWS, ROW_W), ROW_DT),
        compiler_params=pltpu.CompilerParams(
            dimension_semantics=("arbitrary",), vmem_limit_bytes=VMEM_LIMIT),
        name="expert_ffn",
    )(blk_expert, n_valid, xb, w_gate_up, b_gate_up.reshape(N_EXPERTS, 1, 2 * D_FF),
      w_down, b_down.reshape(N_EXPERTS, 1, D_MODEL))


def _final_kernel(x1_ref, y0_ref, y1_ref, y2_ref, y3_ref, w_ref, gt2_ref, g_ref, o_ref):
    w = w_ref[...]
    ylo = jnp.zeros((TM_FIN, HALF), F32)
    yhi = jnp.zeros((TM_FIN, HALF), F32)
    for k, y_ref in enumerate((y0_ref, y1_ref, y2_ref, y3_ref)):
        lo, hi = _unpack_halves(y_ref[...])
        wk = w[:, k:k + 1]
        ylo = ylo + wk * lo
        yhi = yhi + wk * hi
    y = jnp.concatenate([ylo, yhi], axis=1)
    x2 = x1_ref[...] + gt2_ref[...] * y
    inv = lax.rsqrt(jnp.mean(x2 * x2, axis=-1, keepdims=True) + EPS)
    o_ref[...] = x2 * inv * g_ref[...]


def _final(x1, yg, w4, mod4, final_norm_g):
    per_b = SEQ // TM_FIN
    ntile = N_TOK // TM_FIN
    slot = lambda k: pl.BlockSpec((TM_FIN, ROW_W), lambda i: (k * ntile + i, 0))
    return pl.pallas_call(
        _final_kernel,
        grid=(ntile,),
        in_specs=[
            pl.BlockSpec((TM_FIN, D_MODEL), lambda i: (i, 0)),
            slot(0), slot(1), slot(2), slot(3),
            pl.BlockSpec((TM_FIN, TOP_K), lambda i: (i, 0)),
            pl.BlockSpec((None, None, 1, D_MODEL), lambda i: (5, i // per_b, 0, 0)),
            pl.BlockSpec((1, D_MODEL), lambda i: (0, 0)),
        ],
        out_specs=pl.BlockSpec((TM_FIN, D_MODEL), lambda i: (i, 0)),
        out_shape=jax.ShapeDtypeStruct((N_TOK, D_MODEL), F32),
        compiler_params=pltpu.CompilerParams(
            dimension_semantics=("arbitrary",), vmem_limit_bytes=VMEM_LIMIT),
        name="combine_final",
    )(x1, yg, yg, yg, yg, w4, mod4, final_norm_g)


SC_CORES = 2
SC_SUBCORES = 16
SC_WORKERS = SC_CORES * SC_SUBCORES
SC_CHUNK = 64


def _sc_mesh():
    return plsc.VectorSubcoreMesh(core_axis_name="c", subcore_axis_name="s")


def _dispatch_rows(h2p, dest_flat):
    per_w = N_TOK // SC_WORKERS
    nchunk = per_w // SC_CHUNK

    @functools.partial(
        pl.kernel, mesh=_sc_mesh(), out_type=jax.ShapeDtypeStruct((P_ROWS, HALF), U32),
        scratch_types=[pltpu.VMEM((SC_CHUNK,), I32)] * TOP_K
        + [pltpu.VMEM((SC_CHUNK, HALF), U32), pltpu.SemaphoreType.DMA],
        name="moe_dispatch")
    def k(src_hbm, dest_hbm, out_hbm, i0, i1, i2, i3, rows_v, sem):
        idx = (i0, i1, i2, i3)
        wid = lax.axis_index("s") * SC_CORES + lax.axis_index("c")

        @pl.loop(0, nchunk)
        def _(j):
            t0 = wid * per_w + j * SC_CHUNK
            pltpu.sync_copy(src_hbm.at[pl.ds(t0, SC_CHUNK)], rows_v)
            for kk in range(TOP_K):
                pltpu.sync_copy(dest_hbm.at[pl.ds(kk * N_TOK + t0, SC_CHUNK)], idx[kk])
            for kk in range(TOP_K):
                pltpu.sync_copy(rows_v, out_hbm.at[idx[kk]])

    return k(h2p, dest_flat)


def _combine_rows(yb, dest_flat):
    n_out = TOP_K * N_TOK
    per_w = n_out // SC_WORKERS
    nchunk = per_w // SC_CHUNK

    @functools.partial(
        pl.kernel, mesh=_sc_mesh(), out_type=jax.ShapeDtypeStruct((n_out, HALF), U32),
        scratch_types=[pltpu.VMEM((SC_CHUNK,), I32), pltpu.VMEM((SC_CHUNK, HALF), U32),
                       pltpu.SemaphoreType.DMA],
        name="moe_combine")
    def k(tab_hbm, idx_hbm, out_hbm, idx_v, rows_v, sem):
        wid = lax.axis_index("s") * SC_CORES + lax.axis_index("c")

        @pl.loop(0, nchunk)
        def _(j):
            a0 = wid * per_w + j * SC_CHUNK
            pltpu.sync_copy(idx_hbm.at[pl.ds(a0, SC_CHUNK)], idx_v)
            pltpu.sync_copy(tab_hbm.at[idx_v], rows_v)
            pltpu.sync_copy(rows_v, out_hbm.at[pl.ds(a0, SC_CHUNK)])

    return k(yb, dest_flat)


def _lambda_kernel(p_ref, o_ref):
    p = p_ref[...]
    s1 = jnp.sum(p[0:1] * p[1:2], axis=-1, keepdims=True)
    s2 = jnp.sum(p[2:3] * p[3:4], axis=-1, keepdims=True)
    o_ref[...] = jnp.broadcast_to(jnp.exp(s1) - jnp.exp(s2) + LAMBDA_INIT, (1, LANES))


def kernel(x, c, w_ada, b_ada, norm1_g, w_in, lambda_q1, lambda_k1, lambda_q2, lambda_k2, diff_norm_g, w_alpha_up, b_alpha, gla_norm_g, w_branch_diff, w_branch_gla, w_out, norm2_g, w_router, b_router, w_gate_up, b_gate_up, w_down, b_down, final_norm_g):
    w_in0 = w_in[0]
    c_a, c_g = W_A, W_A + W_G
    w_a = w_in0[:, :c_a].astype(BF16)
    w_g = w_in0[:, c_a:c_g].astype(BF16)
    w_lr = jnp.pad(w_in0[:, c_g:c_g + GLA_RANK], ((0, 0), (0, LANES - GLA_RANK))).astype(BF16)
    w_gate = w_in0[:, c_g + GLA_RANK:].astype(BF16)
    w_up = jnp.pad(w_alpha_up[0], ((0, LANES - GLA_RANK), (0, 0)))
    lam_in = jnp.concatenate([lambda_q1, lambda_k1, lambda_q2, lambda_k2], axis=0)
    slopes = jnp.asarray(2.0 ** (-8.0 * np.arange(1, DIFF_HEADS + 1) / DIFF_HEADS), dtype=F32)

    mod = _modulation(c, w_ada[0], b_ada[0])
    mod4 = mod.reshape(N_MOD, BATCH, 1, D_MODEL)
    lam = pl.pallas_call(
        _lambda_kernel, out_shape=jax.ShapeDtypeStruct((1, LANES), F32), name="lambda")(lam_in)[0, :1]

    qkv_a, qkv_g, gates, glr = _in_proj(x, mod4, norm1_g, w_a, w_g, w_gate, w_lr)
    pos = jnp.arange(SEQ, dtype=I32)
    kaug = jnp.zeros((SEQ, LANES), F32).at[:, 0].set((pos >> 3).astype(F32)).at[:, 1].set(
        (pos & 7).astype(F32)).astype(BF16)
    o_a = _diff_attention(qkv_a, slopes, lam, kaug, diff_norm_g)
    o_b = _gla(qkv_g, glr, w_up, b_alpha, gla_norm_g)

    x1, h2p, eidx, rank, wts, cnt = _merge_route(
        o_a.reshape(N_TOK, -1), o_b.reshape(N_TOK, -1), gates.reshape(N_TOK, W_GATE),
        x.reshape(N_TOK, D_MODEL), mod4, norm2_g,
        w_branch_diff[0].astype(BF16), w_branch_gla[0].astype(BF16), w_out[0].astype(BF16),
        w_router[0].T, b_router[0].reshape(N_EXPERTS, 1))

    dest8, be, nv = _route_plan(cnt, eidx, rank)
    dest = dest8[:TOP_K].reshape(-1)

    xb = _dispatch_rows(h2p, dest)
    yb = _expert_ffn(be[0, :N_BLK], nv[0, :1], xb, w_gate_up[0], b_gate_up[0], w_down[0], b_down[0])
    yg = _combine_rows(yb, dest)

    out = _final(x1, yg, wts[:TOP_K].T, mod4, final_norm_g.reshape(1, D_MODEL))
    return out.reshape(BATCH, SEQ, D_MODEL)
```

```python
import functools
import math

import jax
import jax.numpy as jnp
import numpy as np
from jax import lax
from jax.experimental import pallas as pl
from jax.experimental.pallas import tpu as pltpu
from jax.experimental.pallas import tpu_sc as plsc

F32 = jnp.float32
BF16 = jnp.bfloat16
U32 = jnp.uint32
I32 = jnp.int32

D_MODEL = 1024
BATCH = 16
SEQ = 2048
N_TOK = BATCH * SEQ
CHUNK = 64
DIFF_HEADS = 4
DIFF_DH = 64
GLA_HEADS = 4
GLA_DK = 64
GLA_DV = 128
GLA_RANK = 16
GLA_GATE_NORM = 16.0
N_EXPERTS = 32
TOP_K = 4
D_FF = D_MODEL
SWIGLU_LIMIT = 7.0
SWIGLU_ALPHA = 1.702
N_MOD = 6
EPS = 1e-6
LAMBDA_INIT = 0.8 - 0.6 * math.exp(-0.3 * 0)

LANES = 128
HALF = D_MODEL // 2
ROW_W = HALF
ROW_DT = U32

TM_IN = 512
TQ = 512
TM_MERGE = 512
BLK = 512
N_BLK = (N_TOK * TOP_K) // BLK + N_EXPERTS
P_ROWS = N_BLK * BLK
TM_FIN = 512
VMEM_LIMIT = 56 * 1024 * 1024


def _nt_dot(a, b):
    return lax.dot_general(a, b, (((1,), (1,)), ((), ())), preferred_element_type=F32)


def _tn_dot(a, b):
    return lax.dot_general(a, b, (((0,), (0,)), ((), ())), preferred_element_type=F32)


def _split3(x):
    hi = x.astype(BF16)
    r1 = x - hi.astype(F32)
    mid = r1.astype(BF16)
    lo = (r1 - mid.astype(F32)).astype(BF16)
    return hi, mid, lo


def _pack_halves(y):
    return pltpu.pack_elementwise([y[:, :HALF], y[:, HALF:]], packed_dtype=BF16)


def _unpack_halves(u):
    lo = pltpu.unpack_elementwise(u, index=0, packed_dtype=BF16, unpacked_dtype=F32)
    hi = pltpu.unpack_elementwise(u, index=1, packed_dtype=BF16, unpacked_dtype=F32)
    return lo, hi


def _mod_kernel(c_ref, w_ref, b_ref, o_ref):
    c = c_ref[...]
    s = c * jax.nn.sigmoid(c)
    o_ref[0] = jnp.dot(s.astype(BF16), w_ref[...].astype(BF16),
                       preferred_element_type=F32) + b_ref[...]


def _modulation(c, w_ada, b_ada):
    return pl.pallas_call(
        _mod_kernel,
        grid=(N_MOD,),
        in_specs=[
            pl.BlockSpec((BATCH, D_MODEL), lambda j: (0, 0)),
            pl.BlockSpec((D_MODEL, D_MODEL), lambda j: (0, j)),
            pl.BlockSpec((1, D_MODEL), lambda j: (0, j)),
        ],
        out_specs=pl.BlockSpec((1, BATCH, D_MODEL), lambda j: (j, 0, 0)),
        out_shape=jax.ShapeDtypeStruct((N_MOD, BATCH, D_MODEL), F32),
        compiler_params=pltpu.CompilerParams(dimension_semantics=("arbitrary",)),
        name="adaln_mod",
    )(c, w_ada, b_ada.reshape(1, N_MOD * D_MODEL))


W_A = 3 * DIFF_HEADS * 2 * DIFF_DH
W_G = 2 * GLA_HEADS * GLA_DK + 2 * GLA_HEADS * GLA_DV
W_GATE = 2 * D_MODEL


def _in_kernel(x_ref, sh_ref, sc_ref, g_ref, wa_ref, wg_ref, wgate_ref, wlr_ref,
               oa_ref, og_ref, ogate_ref, olr_ref):
    x = x_ref[...]
    inv = lax.rsqrt(jnp.mean(x * x, axis=-1, keepdims=True) + EPS)
    h = (x * inv * g_ref[...]) * (1.0 + sc_ref[...]) + sh_ref[...]
    hb = h.astype(BF16)
    oa_ref[...] = jnp.dot(hb, wa_ref[...], preferred_element_type=F32).astype(BF16)
    og_ref[...] = jnp.dot(hb, wg_ref[...], preferred_element_type=F32).astype(BF16)
    ogate_ref[...] = jnp.dot(hb, wgate_ref[...], preferred_element_type=F32).astype(BF16)
    olr_ref[...] = jnp.dot(hb, wlr_ref[...], preferred_element_type=F32)


def _in_proj(x, mod4, norm1_g, w_a, w_g, w_gate, w_lr):
    nrow = SEQ // TM_IN
    full = lambda shape: pl.BlockSpec(shape, lambda b, i: (0,) * len(shape))
    return pl.pallas_call(
        _in_kernel,
        grid=(BATCH, nrow),
        in_specs=[
            pl.BlockSpec((None, TM_IN, D_MODEL), lambda b, i: (b, i, 0)),
            pl.BlockSpec((None, None, 1, D_MODEL), lambda b, i: (0, b, 0, 0)),
            pl.BlockSpec((None, None, 1, D_MODEL), lambda b, i: (1, b, 0, 0)),
            full((1, D_MODEL)),
            full((D_MODEL, W_A)),
            full((D_MODEL, W_G)),
            full((D_MODEL, W_GATE)),
            full((D_MODEL, LANES)),
        ],
        out_specs=[
            pl.BlockSpec((None, TM_IN, W_A), lambda b, i: (b, i, 0)),
            pl.BlockSpec((None, TM_IN, W_G), lambda b, i: (b, i, 0)),
            pl.BlockSpec((None, TM_IN, W_GATE), lambda b, i: (b, i, 0)),
            pl.BlockSpec((None, TM_IN, LANES), lambda b, i: (b, i, 0)),
        ],
        out_shape=[
            jax.ShapeDtypeStruct((BATCH, SEQ, W_A), BF16),
            jax.ShapeDtypeStruct((BATCH, SEQ, W_G), BF16),
            jax.ShapeDtypeStruct((BATCH, SEQ, W_GATE), BF16),
            jax.ShapeDtypeStruct((BATCH, SEQ, LANES), F32),
        ],
        compiler_params=pltpu.CompilerParams(
            dimension_semantics=("arbitrary", "arbitrary"), vmem_limit_bytes=VMEM_LIMIT),
        name="in_proj",
    )(x, mod4, mod4, norm1_g, w_a, w_g, w_gate, w_lr)


def _attn_kernel(slope_ref, lam_ref, q_ref, k_ref, v_ref, kaug_ref, g_ref, o_ref,
                 sa_ref, sb_ref, corr_ref, m_ref, l_ref, acc_ref):
    slope = slope_ref[pl.program_id(1)]
    lam = lam_ref[0]
    lane = lax.broadcasted_iota(I32, (TQ, 2 * DIFF_DH), 1)
    aug = jnp.where(lane == 0, 8.0 * slope, jnp.where(lane == 1, slope, 0.0)).astype(BF16)

    def blk(j):
        return slice(j * TQ, (j + 1) * TQ)

    def stacked_queries(qi):
        q = q_ref[blk(qi), :] * jnp.asarray(DIFF_DH ** -0.5, BF16)
        zero = jnp.zeros_like(q)
        return jnp.concatenate([
            jnp.concatenate([jnp.where(lane < DIFF_DH, q, zero), aug], axis=1),
            jnp.concatenate([jnp.where(lane >= DIFF_DH, q, zero), aug], axis=1)], axis=0)

    def scores(lhs, j):
        kk = jnp.concatenate([k_ref[blk(j), :], kaug_ref[blk(j), :]], axis=1)
        return _nt_dot(kk, lhs)

    def update(s, j, first):
        if first:
            m_new = jnp.max(s, axis=0, keepdims=True)
            p = jnp.exp(s - m_new)
            l_ref[...] = jnp.sum(p, axis=0, keepdims=True)
            acc_ref[...] = _tn_dot(v_ref[blk(j), :], p.astype(BF16))
        else:
            m = m_ref[...]
            m_new = jnp.maximum(m, jnp.max(s, axis=0, keepdims=True))
            alpha = jnp.exp(m - m_new)
            p = jnp.exp(s - m_new)
            l_ref[...] = alpha * l_ref[...] + jnp.sum(p, axis=0, keepdims=True)
            acc_ref[...] = alpha * acc_ref[...] + _tn_dot(v_ref[blk(j), :], p.astype(BF16))
        m_ref[...] = m_new

    kr = lax.broadcasted_iota(I32, (TQ, TQ), 0)
    qc = lax.broadcasted_iota(I32, (TQ, TQ), 1)
    ahead = jnp.maximum(kr - qc, 0).astype(F32)
    corr_ref[...] = jnp.where((qc >> 6) >= (kr >> 6), (-2.0 * slope) * ahead, -jnp.inf)

    pairs = [(qi, j) for qi in range(SEQ // TQ) for j in range(qi + 1)]
    bufs = (sa_ref, sb_ref)
    lhs = stacked_queries(0)
    bufs[0][...] = scores(lhs, 0)
    for t, (qi, j) in enumerate(pairs):
        if t + 1 < len(pairs):
            nqi, nj = pairs[t + 1]
            if nqi != qi:
                lhs = stacked_queries(nqi)
            bufs[(t + 1) % 2][...] = scores(lhs, nj)
        s = bufs[t % 2][...]
        if j == qi:
            corr = corr_ref[...]
            s = s + jnp.concatenate([corr, corr], axis=1)
        update(s, j, first=(j == 0))
        if j == qi:
            ot = acc_ref[...] / l_ref[...]
            o = (ot[:, :TQ] - lam * ot[:, TQ:]).T
            inv = lax.rsqrt(jnp.mean(o * o, axis=-1, keepdims=True) + EPS)
            o_ref[blk(qi), :] = (o * inv * g_ref[...] * (1.0 - LAMBDA_INIT)).astype(BF16)


def _diff_attention(qkv_a, slopes, lam, kaug, diff_norm_g):
    return pl.pallas_call(
        _attn_kernel,
        grid=(BATCH, DIFF_HEADS),
        in_specs=[
            pl.BlockSpec(memory_space=pltpu.SMEM),
            pl.BlockSpec(memory_space=pltpu.SMEM),
            pl.BlockSpec((None, SEQ, LANES), lambda b, h: (b, 0, h)),
            pl.BlockSpec((None, SEQ, LANES), lambda b, h: (b, 0, DIFF_HEADS + h)),
            pl.BlockSpec((None, SEQ, LANES), lambda b, h: (b, 0, 2 * DIFF_HEADS + h)),
            pl.BlockSpec((SEQ, LANES), lambda b, h: (0, 0)),
            pl.BlockSpec((1, LANES), lambda b, h: (0, 0)),
        ],
        out_specs=pl.BlockSpec((None, SEQ, LANES), lambda b, h: (b, 0, h)),
        out_shape=jax.ShapeDtypeStruct((BATCH, SEQ, DIFF_HEADS * 2 * DIFF_DH), BF16),
        scratch_shapes=[
            pltpu.VMEM((TQ, 2 * TQ), F32), pltpu.VMEM((TQ, 2 * TQ), F32),
            pltpu.VMEM((TQ, TQ), F32),
            pltpu.VMEM((1, 2 * TQ), F32), pltpu.VMEM((1, 2 * TQ), F32),
            pltpu.VMEM((2 * DIFF_DH, 2 * TQ), F32),
        ],
        compiler_params=pltpu.CompilerParams(
            dimension_semantics=("arbitrary", "arbitrary")),
        name="diff_attn",
    )(slopes, lam, qkv_a, qkv_a, qkv_a, kaug, diff_norm_g)


N_CHUNK = SEQ // CHUNK
PAIR = 2 * GLA_DK
PAIR_V = 2 * GLA_DV
CS_ROWS = 256
GLA_UNROLL = 4


def _gla_kernel(q_ref, k_ref, v_ref, r_ref, lr_ref, wup_ref, bup_ref, g_ref, o_ref,
                gcum_ref, state_ref):
    w_hi, w_mid, _ = _split3(wup_ref[...])
    rr = lax.broadcasted_iota(I32, (CS_ROWS, CS_ROWS), 0)
    cc = lax.broadcasted_iota(I32, (CS_ROWS, CS_ROWS), 1)
    tri = jnp.where(((rr >> 6) == (cc >> 6)) & (cc <= rr), 1.0, 0.0).astype(BF16)
    for blk in range(SEQ // CS_ROWS):
        rows = pl.ds(blk * CS_ROWS, CS_ROWS)
        a_hi, a_mid, _ = _split3(lr_ref[rows, :])
        z = (jnp.dot(a_hi, w_hi, preferred_element_type=F32)
             + jnp.dot(a_hi, w_mid, preferred_element_type=F32)
             + jnp.dot(a_mid, w_hi, preferred_element_type=F32)) + bup_ref[...]
        la = (jnp.minimum(z, 0.0) - jnp.log(1.0 + jnp.exp(-jnp.abs(z)))) * (1.0 / GLA_GATE_NORM)
        l_hi, l_mid, l_lo = _split3(la)
        gcum_ref[rows, :] = (jnp.dot(tri, l_hi, preferred_element_type=F32)
                             + jnp.dot(tri, l_mid, preferred_element_type=F32)
                             + jnp.dot(tri, l_lo, preferred_element_type=F32))

    state_ref[...] = jnp.zeros_like(state_ref)
    lane_k = lax.broadcasted_iota(I32, (1, PAIR), 1)
    row_v = lax.broadcasted_iota(I32, (PAIR_V, PAIR), 0)
    col_k = lax.broadcasted_iota(I32, (PAIR_V, PAIR), 1)
    same_head = (row_v >= GLA_DV) == (col_k >= GLA_DK)
    cr = lax.broadcasted_iota(I32, (CHUNK, CHUNK), 0)
    cs = lax.broadcasted_iota(I32, (CHUNK, CHUNK), 1)
    causal = cs <= cr
    scale = GLA_DK ** -0.5

    def chunk(n):
        rows = pl.ds(pl.multiple_of(n * CHUNK, CHUNK), CHUNK)
        gc = gcum_ref[rows, :]
        g_last = gcum_ref[pl.ds(n * CHUNK + CHUNK - 1, 1), :]
        qf = q_ref[rows, :].astype(F32) * scale
        kf = k_ref[rows, :].astype(F32)
        q_s = (qf * jnp.exp(gc)).astype(BF16)
        k_s = (kf * jnp.exp(-gc)).astype(BF16)
        k_d = (kf * jnp.exp(g_last - gc)).astype(BF16)
        decay = jnp.exp(g_last)
        for pr in range(GLA_HEADS // 2):
            kl = slice(pr * PAIR, (pr + 1) * PAIR)
            vl = slice(pr * PAIR_V, (pr + 1) * PAIR_V)
            qs_p, ks_p, kd_p = q_s[:, kl], k_s[:, kl], k_d[:, kl]
            v_p = v_ref[rows, vl]
            st = state_ref[pr]
            o_inter = _nt_dot(qs_p, st.astype(BF16))
            d_st = _tn_dot(v_p, kd_p)
            state_ref[pr] = st * decay[:, kl] + jnp.where(same_head, d_st, 0.0)
            for sub in range(2):
                hd = 2 * pr + sub
                in_head = (lane_k >= sub * GLA_DK) & (lane_k < (sub + 1) * GLA_DK)
                a = _nt_dot(jnp.where(in_head, qs_p, jnp.zeros_like(qs_p)), ks_p)
                a = jnp.where(causal, a, 0.0).astype(BF16)
                vs = slice(hd * GLA_DV, (hd + 1) * GLA_DV)
                o = (jnp.dot(a, v_ref[rows, vs], preferred_element_type=F32)
                     + o_inter[:, sub * GLA_DV:(sub + 1) * GLA_DV])
                inv = lax.rsqrt(jnp.mean(o * o, axis=-1, keepdims=True) + EPS)
                r = r_ref[rows, vs].astype(F32)
                o_ref[rows, vs] = (o * inv * g_ref[...] * (r * jax.nn.sigmoid(r))).astype(BF16)

    def chunk_group(t, _):
        for u in range(GLA_UNROLL):
            chunk(t * GLA_UNROLL + u)
        return 0

    lax.fori_loop(0, N_CHUNK // GLA_UNROLL, chunk_group, 0)


def _gla(qkv_g, glr, w_up, b_up, gla_norm_g):
    qk_w = GLA_HEADS * GLA_DK
    v_w = GLA_HEADS * GLA_DV
    return pl.pallas_call(
        _gla_kernel,
        grid=(BATCH,),
        in_specs=[
            pl.BlockSpec((None, SEQ, qk_w), lambda b: (b, 0, 0)),
            pl.BlockSpec((None, SEQ, qk_w), lambda b: (b, 0, 1)),
            pl.BlockSpec((None, SEQ, v_w), lambda b: (b, 0, 1)),
            pl.BlockSpec((None, SEQ, v_w), lambda b: (b, 0, 2)),
            pl.BlockSpec((None, SEQ, LANES), lambda b: (b, 0, 0)),
            pl.BlockSpec((LANES, qk_w), lambda b: (0, 0)),
            pl.BlockSpec((1, qk_w), lambda b: (0, 0)),
            pl.BlockSpec((1, GLA_DV), lambda b: (0, 0)),
        ],
        out_specs=pl.BlockSpec((None, SEQ, v_w), lambda b: (b, 0, 0)),
        out_shape=jax.ShapeDtypeStruct((BATCH, SEQ, v_w), BF16),
        scratch_shapes=[
            pltpu.VMEM((SEQ, qk_w), F32),
            pltpu.VMEM((GLA_HEADS // 2, PAIR_V, PAIR), F32),
        ],
        compiler_params=pltpu.CompilerParams(
            dimension_semantics=("arbitrary",), vmem_limit_bytes=VMEM_LIMIT),
        name="gla",
    )(qkv_g, qkv_g, qkv_g, qkv_g, glr, w_up, b_up, gla_norm_g)


def _merge_kernel(oa_ref, ob_ref, gate_ref, x_ref, gt1_ref, sh2_ref, sc2_ref, g2_ref,
                  wpa_ref, wpb_ref, wo_ref, wr_ref, br_ref,
                  x1_ref, h2p_ref, eidx_ref, rank_ref, wts_ref, cnt_ref,
                  upper_ref, carry_ref):
    i = pl.program_id(0)

    @pl.when(i == 0)
    def _():
        rr = lax.broadcasted_iota(I32, (TM_MERGE, TM_MERGE), 0)
        cc = lax.broadcasted_iota(I32, (TM_MERGE, TM_MERGE), 1)
        upper_ref[...] = jnp.where(rr < cc, 1.0, 0.0).astype(BF16)
        carry_ref[...] = jnp.zeros_like(carry_ref)

    ga = gate_ref[:, :D_MODEL].astype(F32)
    gb = gate_ref[:, D_MODEL:].astype(F32)
    merged = (jax.nn.sigmoid(ga) * jnp.dot(oa_ref[...], wpa_ref[...], preferred_element_type=F32)
              + jax.nn.sigmoid(gb) * jnp.dot(ob_ref[...], wpb_ref[...], preferred_element_type=F32))
    y = jnp.dot(merged.astype(BF16), wo_ref[...], preferred_element_type=F32)
    x1 = x_ref[...] + gt1_ref[...] * y
    x1_ref[...] = x1
    inv = lax.rsqrt(jnp.mean(x1 * x1, axis=-1, keepdims=True) + EPS)
    h2 = (x1 * inv * g2_ref[...]) * (1.0 + sc2_ref[...]) + sh2_ref[...]
    h2p_ref[...] = _pack_halves(h2)

    h_hi, h_mid, _ = _split3(h2)
    w_hi, w_mid, _ = _split3(wr_ref[...])
    logits = (_nt_dot(w_hi, h_hi) + _nt_dot(w_hi, h_mid) + _nt_dot(w_mid, h_hi)) + br_ref[...]

    eio = lax.broadcasted_iota(I32, (N_EXPERTS, TM_MERGE), 0)
    vals, idxs, sels = [], [], []
    cur = logits
    for _k in range(TOP_K):
        m = jnp.max(cur, axis=0, keepdims=True)
        idx = jnp.min(jnp.where(cur == m, eio, N_EXPERTS), axis=0, keepdims=True)
        sel = eio == idx
        vals.append(m)
        idxs.append(idx)
        sels.append(sel)
        cur = jnp.where(sel, -jnp.inf, cur)
    es = [jnp.exp(v - vals[0]) for v in vals]
    tot = es[0] + es[1] + es[2] + es[3]
    onehot = jnp.zeros((N_EXPERTS, TM_MERGE), F32)
    for sel in sels:
        onehot = onehot + jnp.where(sel, 1.0, 0.0)
    before = jnp.dot(onehot.astype(BF16), upper_ref[...], preferred_element_type=F32) + carry_ref[:, 0:1]
    ranks = [jnp.sum(jnp.where(sel, before, 0.0), axis=0, keepdims=True) for sel in sels]
    carry_ref[...] = carry_ref[...] + jnp.sum(onehot, axis=1, keepdims=True)
    cnt_ref[...] = carry_ref[...]

    zi = jnp.zeros((8 - TOP_K, TM_MERGE), I32)
    zf = jnp.zeros((8 - TOP_K, TM_MERGE), F32)
    eidx_ref[...] = jnp.concatenate(idxs + [zi], axis=0)
    rank_ref[...] = jnp.concatenate([r.astype(I32) for r in ranks] + [zi], axis=0)
    wts_ref[...] = jnp.concatenate([e / tot for e in es] + [zf], axis=0)


def _merge_route(o_a, o_b, gates, x2d, mod4, norm2_g, w_pa, w_pb, w_o, w_rt, b_r):
    ntile = N_TOK // TM_MERGE
    per_b = SEQ // TM_MERGE
    full = lambda shape: pl.BlockSpec(shape, lambda i: (0,) * len(shape))
    row = lambda w: pl.BlockSpec((TM_MERGE, w), lambda i: (i, 0))
    modspec = lambda j: pl.BlockSpec((None, None, 1, D_MODEL), lambda i: (j, i // per_b, 0, 0))
    col = pl.BlockSpec((8, TM_MERGE), lambda i: (0, i))
    return pl.pallas_call(
        _merge_kernel,
        grid=(ntile,),
        in_specs=[
            row(DIFF_HEADS * 2 * DIFF_DH), row(GLA_HEADS * GLA_DV), row(W_GATE), row(D_MODEL),
            modspec(2), modspec(3), modspec(4),
            full((1, D_MODEL)),
            full((DIFF_HEADS * 2 * DIFF_DH, D_MODEL)), full((GLA_HEADS * GLA_DV, D_MODEL)),
            full((D_MODEL, D_MODEL)),
            full((N_EXPERTS, D_MODEL)), full((N_EXPERTS, 1)),
        ],
        out_specs=[
            row(D_MODEL), row(ROW_W), col, col, col,
            pl.BlockSpec((N_EXPERTS, LANES), lambda i: (0, 0)),
        ],
        out_shape=[
            jax.ShapeDtypeStruct((N_TOK, D_MODEL), F32),
            jax.ShapeDtypeStruct((N_TOK, ROW_W), ROW_DT),
            jax.ShapeDtypeStruct((8, N_TOK), I32),
            jax.ShapeDtypeStruct((8, N_TOK), I32),
            jax.ShapeDtypeStruct((8, N_TOK), F32),
            jax.ShapeDtypeStruct((N_EXPERTS, LANES), F32),
        ],
        scratch_shapes=[
            pltpu.VMEM((TM_MERGE, TM_MERGE), BF16),
            pltpu.VMEM((N_EXPERTS, LANES), F32),
        ],
        compiler_params=pltpu.CompilerParams(
            dimension_semantics=("arbitrary",), vmem_limit_bytes=VMEM_LIMIT),
        name="merge_route",
    )(o_a, o_b, gates, x2d, mod4, mod4, mod4, norm2_g, w_pa, w_pb, w_o, w_rt, b_r)


TP = 4096
NB_PAD = ((N_BLK + LANES - 1) // LANES) * LANES


def _plan_kernel(cnt_ref, eidx_ref, rank_ref, dest_ref, be_ref, nv_ref):
    cnt = cnt_ref[...]
    padded = jnp.floor((cnt + (BLK - 1.0)) * (1.0 / BLK)) * BLK
    er = lax.broadcasted_iota(I32, (N_EXPERTS, N_EXPERTS), 0)
    ec = lax.broadcasted_iota(I32, (N_EXPERTS, N_EXPERTS), 1)
    lower = jnp.where(ec < er, 1.0, 0.0).astype(BF16)
    p_hi, p_mid, p_lo = _split3(padded)
    starts = (jnp.dot(lower, p_hi, preferred_element_type=F32)
              + jnp.dot(lower, p_mid, preferred_element_type=F32)
              + jnp.dot(lower, p_lo, preferred_element_type=F32))
    ends = starts + padded
    blk_start = (lax.broadcasted_iota(I32, (1, NB_PAD), 1) * BLK).astype(F32)
    n_before = jnp.sum(jnp.where(ends[:, 0:1] <= blk_start, 1.0, 0.0), axis=0, keepdims=True)
    be_ref[...] = jnp.minimum(n_before, N_EXPERTS - 1.0).astype(I32)
    nv_ref[...] = (jnp.max(ends, axis=0, keepdims=True) * (1.0 / BLK)).astype(I32)

    eio = lax.broadcasted_iota(I32, (N_EXPERTS, TP), 0)
    rows = []
    for k in range(TOP_K):
        onehot = eio == eidx_ref[k:k + 1, :]
        base = jnp.sum(jnp.where(onehot, starts[:, 0:1], 0.0), axis=0, keepdims=True)
        rows.append(base.astype(I32) + rank_ref[k:k + 1, :])
    dest_ref[...] = jnp.concatenate(rows + [jnp.zeros((8 - TOP_K, TP), I32)], axis=0)


def _route_plan(cnt, eidx, rank):
    return pl.pallas_call(
        _plan_kernel,
        grid=(N_TOK // TP,),
        in_specs=[
            pl.BlockSpec((N_EXPERTS, LANES), lambda i: (0, 0)),
            pl.BlockSpec((8, TP), lambda i: (0, i)),
            pl.BlockSpec((8, TP), lambda i: (0, i)),
        ],
        out_specs=[
            pl.BlockSpec((8, TP), lambda i: (0, i)),
            pl.BlockSpec((1, NB_PAD), lambda i: (0, 0)),
            pl.BlockSpec((1, LANES), lambda i: (0, 0)),
        ],
        out_shape=[
            jax.ShapeDtypeStruct((8, N_TOK), I32),
            jax.ShapeDtypeStruct((1, NB_PAD), I32),
            jax.ShapeDtypeStruct((1, LANES), I32),
        ],
        compiler_params=pltpu.CompilerParams(dimension_semantics=("arbitrary",)),
        name="route_plan",
    )(cnt, eidx, rank)


def _ffn_kernel(be_ref, nv_ref, x_ref, wgu_ref, bgu_ref, wd_ref, bd_ref, y_ref):
    i = pl.program_id(0)
    valid = i < nv_ref[0]

    @pl.when(valid)
    def _():
        lo, hi = _unpack_halves(x_ref[...])
        x = jnp.concatenate([lo, hi], axis=1).astype(BF16)
        mm = lambda a, w: lax.dot_general(a, w, (((1,), (0,)), ((), ())), preferred_element_type=F32)
        gu = mm(x, wgu_ref[...]) + bgu_ref[...]
        gate = jnp.minimum(gu[:, :D_FF], SWIGLU_LIMIT)
        up = jnp.clip(gu[:, D_FF:], -SWIGLU_LIMIT, SWIGLU_LIMIT)
        act = (up + 1.0) * (gate * jax.nn.sigmoid(SWIGLU_ALPHA * gate))
        y = mm(act.astype(BF16), wd_ref[...]) + bd_ref[...]
        y_ref[...] = _pack_halves(y)

    @pl.when(jnp.logical_not(valid))
    def _():
        y_ref[...] = jnp.zeros_like(y_ref)


def _expert_ffn(blk_expert, n_valid, xb, w_gate_up, b_gate_up, w_down, b_down):
    grid_spec = pltpu.PrefetchScalarGridSpec(
        num_scalar_prefetch=2,
        grid=(N_BLK,),
        in_specs=[
            pl.BlockSpec((BLK, ROW_W), lambda i, be, nv: (i, 0)),
            pl.BlockSpec((None, D_MODEL, 2 * D_FF), lambda i, be, nv: (be[i], 0, 0)),
            pl.BlockSpec((None, 1, 2 * D_FF), lambda i, be, nv: (be[i], 0, 0)),
            pl.BlockSpec((None, D_FF, D_MODEL), lambda i, be, nv: (be[i], 0, 0)),
            pl.BlockSpec((None, 1, D_MODEL), lambda i, be, nv: (be[i], 0, 0)),
        ],
        out_specs=pl.BlockSpec((BLK, ROW_W), lambda i, be, nv: (i, 0)),
    )
    return pl.pallas_call(
        _ffn_kernel,
        grid_spec=grid_spec,
        out_shape=jax.ShapeDtypeStruct((P_ROWS, ROW_W), ROW_DT),
        compiler_params=pltpu.CompilerParams(
            dimension_semantics=("arbitrary",), vmem_limit_bytes=VMEM_LIMIT),
        name="expert_ffn",
    )(blk_expert, n_valid, xb, w_gate_up, b_gate_up.reshape(N_EXPERTS, 1, 2 * D_FF),
      w_down, b_down.reshape(N_EXPERTS, 1, D_MODEL))


def _final_kernel(x1_ref, y0_ref, y1_ref, y2_ref, y3_ref, w_ref, gt2_ref, g_ref, o_ref):
    w = w_ref[...]
    ylo = jnp.zeros((TM_FIN, HALF), F32)
    yhi = jnp.zeros((TM_FIN, HALF), F32)
    for k, y_ref in enumerate((y0_ref, y1_ref, y2_ref, y3_ref)):
        lo, hi = _unpack_halves(y_ref[...])
        wk = w[:, k:k + 1]
        ylo = ylo + wk * lo
        yhi = yhi + wk * hi
    y = jnp.concatenate([ylo, yhi], axis=1)
    x2 = x1_ref[...] + gt2_ref[...] * y
    inv = lax.rsqrt(jnp.mean(x2 * x2, axis=-1, keepdims=True) + EPS)
    o_ref[...] = x2 * inv * g_ref[...]


def _final(x1, yg, w4, mod4, final_norm_g):
    per_b = SEQ // TM_FIN
    ntile = N_TOK // TM_FIN
    slot = lambda k: pl.BlockSpec((TM_FIN, ROW_W), lambda i: (k * ntile + i, 0))
    return pl.pallas_call(
        _final_kernel,
        grid=(ntile,),
        in_specs=[
            pl.BlockSpec((TM_FIN, D_MODEL), lambda i: (i, 0)),
            slot(0), slot(1), slot(2), slot(3),
            pl.BlockSpec((TM_FIN, TOP_K), lambda i: (i, 0)),
            pl.BlockSpec((None, None, 1, D_MODEL), lambda i: (5, i // per_b, 0, 0)),
            pl.BlockSpec((1, D_MODEL), lambda i: (0, 0)),
        ],
        out_specs=pl.BlockSpec((TM_FIN, D_MODEL), lambda i: (i, 0)),
        out_shape=jax.ShapeDtypeStruct((N_TOK, D_MODEL), F32),
        compiler_params=pltpu.CompilerParams(
            dimension_semantics=("arbitrary",), vmem_limit_bytes=VMEM_LIMIT),
        name="combine_final",
    )(x1, yg, yg, yg, yg, w4, mod4, final_norm_g)


SC_CORES = 2
SC_SUBCORES = 16
SC_WORKERS = SC_CORES * SC_SUBCORES
SC_CHUNK = 64


def _sc_mesh():
    return plsc.VectorSubcoreMesh(core_axis_name="c", subcore_axis_name="s")


def _row_buffers():
    return ([pltpu.VMEM((SC_CHUNK, HALF), U32)] * 2 + [pltpu.SemaphoreType.DMA] * 5)


def _dispatch_rows(h2p, dest2d):
    per_w = N_TOK // SC_WORKERS
    nchunk = per_w // SC_CHUNK
    rows_per_k = N_TOK // SC_CHUNK
    assert nchunk % 2 == 0

    @functools.partial(
        pl.kernel, mesh=_sc_mesh(), out_type=jax.ShapeDtypeStruct((P_ROWS, HALF), U32),
        scratch_types=[pltpu.VMEM((TOP_K * nchunk, SC_CHUNK), I32)] + _row_buffers(),
        name="moe_dispatch")
    def k(src_hbm, dest_hbm, out_hbm, idx_v, buf0, buf1, isem, l0, l1, s0, s1):
        wid = lax.axis_index("s") * SC_CORES + lax.axis_index("c")
        bufs, lsem, ssem = (buf0, buf1), (l0, l1), (s0, s1)
        idx_loads = [
            pltpu.make_async_copy(dest_hbm.at[pl.ds(kk * rows_per_k + wid * nchunk, nchunk)],
                                  idx_v.at[pl.ds(kk * nchunk, nchunk)], isem) for kk in range(TOP_K)]

        def load(j, b):
            return pltpu.make_async_copy(src_hbm.at[pl.ds(wid * per_w + j * SC_CHUNK, SC_CHUNK)], bufs[b], lsem[b])

        def scatters(j, b):
            return [pltpu.make_async_copy(bufs[b], out_hbm.at[idx_v.at[kk * nchunk + j]], ssem[b])
                    for kk in range(TOP_K)]

        for cp in idx_loads:
            cp.start()
        load(0, 0).start()
        for cp in idx_loads:
            cp.wait()

        @pl.loop(0, nchunk // 2)
        def _(i):
            for b in range(2):
                j = 2 * i + b

                @pl.when(j >= 1)
                def _():
                    for cp in scatters(j - 1, 1 - b):
                        cp.wait()

                @pl.when(j + 1 < nchunk)
                def _():
                    load(j + 1, 1 - b).start()

                load(j, b).wait()
                for cp in scatters(j, b):
                    cp.start()

        for cp in scatters(nchunk - 1, 1):
            cp.wait()

    return k(h2p, dest2d)


def _combine_rows(yb, dest2d):
    n_out = TOP_K * N_TOK
    per_w = n_out // SC_WORKERS
    nchunk = per_w // SC_CHUNK
    assert nchunk % 2 == 0

    @functools.partial(
        pl.kernel, mesh=_sc_mesh(), out_type=jax.ShapeDtypeStruct((n_out, HALF), U32),
        scratch_types=[pltpu.VMEM((nchunk, SC_CHUNK), I32)] + _row_buffers(),
        name="moe_combine")
    def k(tab_hbm, idx_hbm, out_hbm, idx_v, buf0, buf1, isem, g0, g1, s0, s1):
        wid = lax.axis_index("s") * SC_CORES + lax.axis_index("c")
        bufs, gsem, ssem = (buf0, buf1), (g0, g1), (s0, s1)
        idx_load = pltpu.make_async_copy(idx_hbm.at[pl.ds(wid * nchunk, nchunk)], idx_v, isem)

        def gather(j, b):
            return pltpu.make_async_copy(tab_hbm.at[idx_v.at[j]], bufs[b], gsem[b])

        def store(j, b):
            return pltpu.make_async_copy(bufs[b], out_hbm.at[pl.ds(wid * per_w + j * SC_CHUNK, SC_CHUNK)], ssem[b])

        idx_load.start()
        idx_load.wait()
        gather(0, 0).start()

        @pl.loop(0, nchunk // 2)
        def _(i):
            for b in range(2):
                j = 2 * i + b

                @pl.when(j >= 1)
                def _():
                    store(j - 1, 1 - b).wait()

                @pl.when(j + 1 < nchunk)
                def _():
                    gather(j + 1, 1 - b).start()

                gather(j, b).wait()
                store(j, b).start()

        store(nchunk - 1, 1).wait()

    return k(yb, dest2d)


def _lambda_kernel(p_ref, o_ref):
    p = p_ref[...]
    s1 = jnp.sum(p[0:1] * p[1:2], axis=-1, keepdims=True)
    s2 = jnp.sum(p[2:3] * p[3:4], axis=-1, keepdims=True)
    o_ref[...] = jnp.broadcast_to(jnp.exp(s1) - jnp.exp(s2) + LAMBDA_INIT, (1, LANES))


def kernel(x, c, w_ada, b_ada, norm1_g, w_in, lambda_q1, lambda_k1, lambda_q2, lambda_k2, diff_norm_g, w_alpha_up, b_alpha, gla_norm_g, w_branch_diff, w_branch_gla, w_out, norm2_g, w_router, b_router, w_gate_up, b_gate_up, w_down, b_down, final_norm_g):
    w_in0 = w_in[0]
    c_a, c_g = W_A, W_A + W_G
    w_a = w_in0[:, :c_a].astype(BF16)
    w_g = w_in0[:, c_a:c_g].astype(BF16)
    w_lr = jnp.pad(w_in0[:, c_g:c_g + GLA_RANK], ((0, 0), (0, LANES - GLA_RANK))).astype(BF16)
    w_gate = w_in0[:, c_g + GLA_RANK:].astype(BF16)
    w_up = jnp.pad(w_alpha_up[0], ((0, LANES - GLA_RANK), (0, 0)))
    lam_in = jnp.concatenate([lambda_q1, lambda_k1, lambda_q2, lambda_k2], axis=0)
    slopes = jnp.asarray(2.0 ** (-8.0 * np.arange(1, DIFF_HEADS + 1) / DIFF_HEADS), dtype=F32)

    mod = _modulation(c, w_ada[0], b_ada[0])
    mod4 = mod.reshape(N_MOD, BATCH, 1, D_MODEL)
    lam = pl.pallas_call(
        _lambda_kernel, out_shape=jax.ShapeDtypeStruct((1, LANES), F32), name="lambda")(lam_in)[0, :1]

    qkv_a, qkv_g, gates, glr = _in_proj(x, mod4, norm1_g, w_a, w_g, w_gate, w_lr)
    pos = jnp.arange(SEQ, dtype=I32)
    kaug = jnp.zeros((SEQ, LANES), F32).at[:, 0].set((pos >> 3).astype(F32)).at[:, 1].set(
        (pos & 7).astype(F32)).astype(BF16)
    o_a = _diff_attention(qkv_a, slopes, lam, kaug, diff_norm_g)
    o_b = _gla(qkv_g, glr, w_up, b_alpha, gla_norm_g)

    x1, h2p, eidx, rank, wts, cnt = _merge_route(
        o_a.reshape(N_TOK, -1), o_b.reshape(N_TOK, -1), gates.reshape(N_TOK, W_GATE),
        x.reshape(N_TOK, D_MODEL), mod4, norm2_g,
        w_branch_diff[0].astype(BF16), w_branch_gla[0].astype(BF16), w_out[0].astype(BF16),
        w_router[0].T, b_router[0].reshape(N_EXPERTS, 1))

    dest8, be, nv = _route_plan(cnt, eidx, rank)
    dest = dest8[:TOP_K].reshape(-1, SC_CHUNK)

    xb = _dispatch_rows(h2p, dest)
    yb = _expert_ffn(be[0, :N_BLK], nv[0, :1], xb, w_gate_up[0], b_gate_up[0], w_down[0], b_down[0])
    yg = _combine_rows(yb, dest)

    out = _final(x1, yg, wts[:TOP_K].T, mod4, final_norm_g.reshape(1, D_MODEL))
    return out.reshape(BATCH, SEQ, D_MODEL)
```

```python
import functools
import math

import jax
import jax.numpy as jnp
import numpy as np
from jax import lax
from jax.experimental import pallas as pl
from jax.experimental.pallas import tpu as pltpu
from jax.experimental.pallas import tpu_sc as plsc

F32 = jnp.float32
BF16 = jnp.bfloat16
U32 = jnp.uint32
I32 = jnp.int32

D_MODEL = 1024
BATCH = 16
SEQ = 2048
N_TOK = BATCH * SEQ
CHUNK = 64
DIFF_HEADS = 4
DIFF_DH = 64
GLA_HEADS = 4
GLA_DK = 64
GLA_DV = 128
GLA_RANK = 16
GLA_GATE_NORM = 16.0
N_EXPERTS = 32
TOP_K = 4
D_FF = D_MODEL
SWIGLU_LIMIT = 7.0
SWIGLU_ALPHA = 1.702
N_MOD = 6
EPS = 1e-6
LAMBDA_INIT = 0.8 - 0.6 * math.exp(-0.3 * 0)

LANES = 128
HALF = D_MODEL // 2
ROW_W = HALF
ROW_DT = U32

TM_IN = 512
TQ = 512
VT_ROWS = 2 * DIFF_DH + 16
LOG2E = math.log2(math.e)
TM_MERGE = 512
MERGE_PARTS = 1
BLK = 512
N_BLK = (N_TOK * TOP_K) // BLK + N_EXPERTS
P_ROWS = N_BLK * BLK
TM_FIN = 512
VMEM_LIMIT = 56 * 1024 * 1024


def _nt_dot(a, b):
    return lax.dot_general(a, b, (((1,), (1,)), ((), ())), preferred_element_type=F32)


def _tn_dot(a, b):
    return lax.dot_general(a, b, (((0,), (0,)), ((), ())), preferred_element_type=F32)


def _split3(x):
    hi = x.astype(BF16)
    r1 = x - hi.astype(F32)
    mid = r1.astype(BF16)
    lo = (r1 - mid.astype(F32)).astype(BF16)
    return hi, mid, lo


def _pack_halves(y):
    return pltpu.pack_elementwise([y[:, :HALF], y[:, HALF:]], packed_dtype=BF16)


def _unpack_halves(u):
    lo = pltpu.unpack_elementwise(u, index=0, packed_dtype=BF16, unpacked_dtype=F32)
    hi = pltpu.unpack_elementwise(u, index=1, packed_dtype=BF16, unpacked_dtype=F32)
    return lo, hi


def _mod_kernel(c_ref, w_ref, b_ref, o_ref):
    c = c_ref[...]
    s = c * jax.nn.sigmoid(c)
    o_ref[0] = jnp.dot(s.astype(BF16), w_ref[...].astype(BF16),
                       preferred_element_type=F32) + b_ref[...]


def _modulation(c, w_ada, b_ada):
    return pl.pallas_call(
        _mod_kernel,
        grid=(N_MOD,),
        in_specs=[
            pl.BlockSpec((BATCH, D_MODEL), lambda j: (0, 0)),
            pl.BlockSpec((D_MODEL, D_MODEL), lambda j: (0, j)),
            pl.BlockSpec((1, D_MODEL), lambda j: (0, j)),
        ],
        out_specs=pl.BlockSpec((1, BATCH, D_MODEL), lambda j: (j, 0, 0)),
        out_shape=jax.ShapeDtypeStruct((N_MOD, BATCH, D_MODEL), F32),
        compiler_params=pltpu.CompilerParams(dimension_semantics=("arbitrary",)),
        name="adaln_mod",
    )(c, w_ada, b_ada.reshape(1, N_MOD * D_MODEL))


W_A = 3 * DIFF_HEADS * 2 * DIFF_DH
W_G = 2 * GLA_HEADS * GLA_DK + 2 * GLA_HEADS * GLA_DV
W_GATE = 2 * D_MODEL


def _in_kernel(x_ref, sh_ref, sc_ref, g_ref, wa_ref, wg_ref, wgate_ref, wlr_ref,
               oa_ref, og_ref, ogate_ref, olr_ref):
    x = x_ref[...]
    inv = lax.rsqrt(jnp.mean(x * x, axis=-1, keepdims=True) + EPS)
    h = (x * inv * g_ref[...]) * (1.0 + sc_ref[...]) + sh_ref[...]
    hb = h.astype(BF16)
    oa_ref[...] = jnp.dot(hb, wa_ref[...], preferred_element_type=F32).astype(BF16)
    og_ref[...] = jnp.dot(hb, wg_ref[...], preferred_element_type=F32).astype(BF16)
    ogate_ref[...] = jnp.dot(hb, wgate_ref[...], preferred_element_type=F32).astype(BF16)
    olr_ref[...] = jnp.dot(hb, wlr_ref[...], preferred_element_type=F32)


def _in_proj(x, mod4, norm1_g, w_a, w_g, w_gate, w_lr):
    nrow = SEQ // TM_IN
    full = lambda shape: pl.BlockSpec(shape, lambda b, i: (0,) * len(shape))
    return pl.pallas_call(
        _in_kernel,
        grid=(BATCH, nrow),
        in_specs=[
            pl.BlockSpec((None, TM_IN, D_MODEL), lambda b, i: (b, i, 0)),
            pl.BlockSpec((None, None, 1, D_MODEL), lambda b, i: (0, b, 0, 0)),
            pl.BlockSpec((None, None, 1, D_MODEL), lambda b, i: (1, b, 0, 0)),
            full((1, D_MODEL)),
            full((D_MODEL, W_A)),
            full((D_MODEL, W_G)),
            full((D_MODEL, W_GATE)),
            full((D_MODEL, LANES)),
        ],
        out_specs=[
            pl.BlockSpec((None, TM_IN, W_A), lambda b, i: (b, i, 0)),
            pl.BlockSpec((None, TM_IN, W_G), lambda b, i: (b, i, 0)),
            pl.BlockSpec((None, TM_IN, W_GATE), lambda b, i: (b, i, 0)),
            pl.BlockSpec((None, TM_IN, LANES), lambda b, i: (b, i, 0)),
        ],
        out_shape=[
            jax.ShapeDtypeStruct((BATCH, SEQ, W_A), BF16),
            jax.ShapeDtypeStruct((BATCH, SEQ, W_G), BF16),
            jax.ShapeDtypeStruct((BATCH, SEQ, W_GATE), BF16),
            jax.ShapeDtypeStruct((BATCH, SEQ, LANES), F32),
        ],
        compiler_params=pltpu.CompilerParams(
            dimension_semantics=("arbitrary", "arbitrary"), vmem_limit_bytes=VMEM_LIMIT),
        name="in_proj",
    )(x, mod4, mod4, norm1_g, w_a, w_g, w_gate, w_lr)


def _attn_kernel(slope_ref, lam_ref, q_ref, k_ref, v_ref, kaug_ref, g_ref, o_ref,
                 sa_ref, sb_ref, corr_ref, vt_ref, m_ref, acc_ref):
    c_alibi = slope_ref[pl.program_id(1)] * LOG2E
    lam = lam_ref[0]
    lane = lax.broadcasted_iota(I32, (TQ, 2 * DIFF_DH), 1)
    c_vec = jnp.full((TQ, 2 * DIFF_DH), c_alibi, F32)
    c_hi = c_vec.astype(BF16).astype(F32)
    c_lo = (c_vec - c_hi).astype(BF16).astype(F32)
    zero = jnp.zeros((TQ, 2 * DIFF_DH), BF16)
    aug = jnp.where(lane == 0, 8.0 * c_hi, jnp.where(lane == 1, 8.0 * c_lo,
                    jnp.where(lane == 2, c_hi, jnp.where(lane == 3, c_lo, 0.0)))).astype(BF16)

    def blk(j):
        return slice(j * TQ, (j + 1) * TQ)

    def stacked_queries(qi):
        q = (q_ref[blk(qi), :].astype(F32) * (DIFF_DH ** -0.5 * LOG2E)).astype(BF16)
        return jnp.concatenate([
            jnp.concatenate([jnp.where(lane < DIFF_DH, q, zero), aug], axis=1),
            jnp.concatenate([jnp.where(lane >= DIFF_DH, q, zero), aug], axis=1)], axis=0)

    def scores(lhs, j):
        kk = jnp.concatenate([k_ref[blk(j), :], kaug_ref[blk(j), :]], axis=1)
        return _nt_dot(kk, lhs)

    ones_rows = jnp.where(lax.broadcasted_iota(I32, (VT_ROWS - 2 * DIFF_DH, TQ), 0) == 0, 1.0, 0.0).astype(BF16)
    for j in range(SEQ // TQ):
        vt_ref[j, :2 * DIFF_DH, :] = v_ref[blk(j), :].astype(F32).T.astype(BF16)
        vt_ref[j, 2 * DIFF_DH:, :] = ones_rows

    def update(s, j, first):
        if first:
            m_new = jnp.max(s, axis=0, keepdims=True)
            p = jnp.exp2((s - m_new).astype(BF16))
            acc_ref[...] = jnp.dot(vt_ref[j], p, preferred_element_type=F32)
        else:
            m = m_ref[...]
            m_new = jnp.maximum(m, jnp.max(s, axis=0, keepdims=True))
            alpha = jnp.exp2(m - m_new)
            p = jnp.exp2((s - m_new).astype(BF16))
            acc_ref[...] = alpha * acc_ref[...] + jnp.dot(vt_ref[j], p, preferred_element_type=F32)
        m_ref[...] = m_new

    kr = lax.broadcasted_iota(I32, (TQ, TQ), 0)
    qc = lax.broadcasted_iota(I32, (TQ, TQ), 1)
    ahead = jnp.maximum(kr - qc, 0).astype(F32)
    corr_ref[...] = jnp.where((qc >> 6) >= (kr >> 6), (-2.0 * c_alibi) * ahead, -jnp.inf)

    pairs = [(qi, j) for qi in range(SEQ // TQ) for j in range(qi + 1)]
    bufs = (sa_ref, sb_ref)
    lhs = stacked_queries(0)
    bufs[0][...] = scores(lhs, 0)
    for t, (qi, j) in enumerate(pairs):
        if t + 1 < len(pairs):
            nqi, nj = pairs[t + 1]
            if nqi != qi:
                lhs = stacked_queries(nqi)
            bufs[(t + 1) % 2][...] = scores(lhs, nj)
        s = bufs[t % 2][...]
        if j == qi:
            corr = corr_ref[...]
            s = s + jnp.concatenate([corr, corr], axis=1)
        update(s, j, first=(j == 0))
        if j == qi:
            ot = acc_ref[:2 * DIFF_DH, :] / acc_ref[2 * DIFF_DH:2 * DIFF_DH + 1, :]
            o = (ot[:, :TQ] - lam * ot[:, TQ:]).T
            inv = lax.rsqrt(jnp.mean(o * o, axis=-1, keepdims=True) + EPS)
            o_ref[blk(qi), :] = (o * inv * g_ref[...] * (1.0 - LAMBDA_INIT)).astype(BF16)


def _diff_attention(qkv_a, slopes, lam, kaug, diff_norm_g):
    return pl.pallas_call(
        _attn_kernel,
        grid=(BATCH, DIFF_HEADS),
        in_specs=[
            pl.BlockSpec(memory_space=pltpu.SMEM),
            pl.BlockSpec(memory_space=pltpu.SMEM),
            pl.BlockSpec((None, SEQ, LANES), lambda b, h: (b, 0, h)),
            pl.BlockSpec((None, SEQ, LANES), lambda b, h: (b, 0, DIFF_HEADS + h)),
            pl.BlockSpec((None, SEQ, LANES), lambda b, h: (b, 0, 2 * DIFF_HEADS + h)),
            pl.BlockSpec((SEQ, LANES), lambda b, h: (0, 0)),
            pl.BlockSpec((1, LANES), lambda b, h: (0, 0)),
        ],
        out_specs=pl.BlockSpec((None, SEQ, LANES), lambda b, h: (b, 0, h)),
        out_shape=jax.ShapeDtypeStruct((BATCH, SEQ, DIFF_HEADS * 2 * DIFF_DH), BF16),
        scratch_shapes=[
            pltpu.VMEM((TQ, 2 * TQ), F32), pltpu.VMEM((TQ, 2 * TQ), F32),
            pltpu.VMEM((TQ, TQ), F32),
            pltpu.VMEM((SEQ // TQ, VT_ROWS, TQ), BF16),
            pltpu.VMEM((1, 2 * TQ), F32),
            pltpu.VMEM((VT_ROWS, 2 * TQ), F32),
        ],
        compiler_params=pltpu.CompilerParams(
            dimension_semantics=("arbitrary", "arbitrary")),
        name="diff_attn",
    )(slopes, lam, qkv_a, qkv_a, qkv_a, kaug, diff_norm_g)


N_CHUNK = SEQ // CHUNK
PAIR = 2 * GLA_DK
PAIR_V = 2 * GLA_DV
CS_ROWS = 256
GLA_UNROLL = 8


def _gla_kernel(q_ref, k_ref, v_ref, r_ref, lr_ref, wup_ref, bup_ref, g_ref, o_ref,
                gcum_ref, state_ref):
    w_hi, w_mid, _ = _split3(wup_ref[...])
    rr = lax.broadcasted_iota(I32, (CS_ROWS, CS_ROWS), 0)
    cc = lax.broadcasted_iota(I32, (CS_ROWS, CS_ROWS), 1)
    tri = jnp.where(((rr >> 6) == (cc >> 6)) & (cc <= rr), 1.0, 0.0).astype(BF16)
    for blk in range(SEQ // CS_ROWS):
        rows = pl.ds(blk * CS_ROWS, CS_ROWS)
        a_hi, a_mid, _ = _split3(lr_ref[rows, :])
        z = (jnp.dot(a_hi, w_hi, preferred_element_type=F32)
             + jnp.dot(a_hi, w_mid, preferred_element_type=F32)
             + jnp.dot(a_mid, w_hi, preferred_element_type=F32)) + bup_ref[...]
        la = (jnp.minimum(z, 0.0) - jnp.log(1.0 + jnp.exp(-jnp.abs(z)))) * (1.0 / GLA_GATE_NORM)
        l_hi, l_mid, l_lo = _split3(la)
        gcum_ref[rows, :] = (jnp.dot(tri, l_hi, preferred_element_type=F32)
                             + jnp.dot(tri, l_mid, preferred_element_type=F32)
                             + jnp.dot(tri, l_lo, preferred_element_type=F32))

    state_ref[...] = jnp.zeros_like(state_ref)
    lane_k = lax.broadcasted_iota(I32, (1, PAIR), 1)
    row_v = lax.broadcasted_iota(I32, (PAIR_V, PAIR), 0)
    col_k = lax.broadcasted_iota(I32, (PAIR_V, PAIR), 1)
    same_head = (row_v >= GLA_DV) == (col_k >= GLA_DK)
    cr = lax.broadcasted_iota(I32, (CHUNK, CHUNK), 0)
    cs = lax.broadcasted_iota(I32, (CHUNK, CHUNK), 1)
    causal = cs <= cr
    scale = GLA_DK ** -0.5

    def chunk(n):
        rows = pl.ds(pl.multiple_of(n * CHUNK, CHUNK), CHUNK)
        gc = gcum_ref[rows, :]
        g_last = gcum_ref[pl.ds(n * CHUNK + CHUNK - 1, 1), :]
        qf = q_ref[rows, :].astype(F32) * scale
        kf = k_ref[rows, :].astype(F32)
        q_s = (qf * jnp.exp(gc)).astype(BF16)
        k_s = (kf * jnp.exp(-gc)).astype(BF16)
        k_d = (kf * jnp.exp(g_last - gc)).astype(BF16)
        decay = jnp.exp(g_last)
        for pr in range(GLA_HEADS // 2):
            kl = slice(pr * PAIR, (pr + 1) * PAIR)
            vl = slice(pr * PAIR_V, (pr + 1) * PAIR_V)
            qs_p, ks_p, kd_p = q_s[:, kl], k_s[:, kl], k_d[:, kl]
            v_p = v_ref[rows, vl]
            st = state_ref[pr]
            o_inter = _nt_dot(qs_p, st.astype(BF16))
            d_st = _tn_dot(v_p, kd_p)
            state_ref[pr] = st * decay[:, kl] + jnp.where(same_head, d_st, 0.0)
            for sub in range(2):
                hd = 2 * pr + sub
                in_head = (lane_k >= sub * GLA_DK) & (lane_k < (sub + 1) * GLA_DK)
                a = _nt_dot(jnp.where(in_head, qs_p, jnp.zeros_like(qs_p)), ks_p)
                a = jnp.where(causal, a, 0.0).astype(BF16)
                vs = slice(hd * GLA_DV, (hd + 1) * GLA_DV)
                o = (jnp.dot(a, v_ref[rows, vs], preferred_element_type=F32)
                     + o_inter[:, sub * GLA_DV:(sub + 1) * GLA_DV])
                inv = lax.rsqrt(jnp.mean(o * o, axis=-1, keepdims=True) + EPS)
                r = r_ref[rows, vs].astype(F32)
                o_ref[rows, vs] = (o * inv * g_ref[...] * (r * jax.nn.sigmoid(r))).astype(BF16)

    def chunk_group(t, _):
        for u in range(GLA_UNROLL):
            chunk(t * GLA_UNROLL + u)
        return 0

    lax.fori_loop(0, N_CHUNK // GLA_UNROLL, chunk_group, 0)


def _gla(qkv_g, glr, w_up, b_up, gla_norm_g):
    qk_w = GLA_HEADS * GLA_DK
    v_w = GLA_HEADS * GLA_DV
    return pl.pallas_call(
        _gla_kernel,
        grid=(BATCH,),
        in_specs=[
            pl.BlockSpec((None, SEQ, qk_w), lambda b: (b, 0, 0)),
            pl.BlockSpec((None, SEQ, qk_w), lambda b: (b, 0, 1)),
            pl.BlockSpec((None, SEQ, v_w), lambda b: (b, 0, 1)),
            pl.BlockSpec((None, SEQ, v_w), lambda b: (b, 0, 2)),
            pl.BlockSpec((None, SEQ, LANES), lambda b: (b, 0, 0)),
            pl.BlockSpec((LANES, qk_w), lambda b: (0, 0)),
            pl.BlockSpec((1, qk_w), lambda b: (0, 0)),
            pl.BlockSpec((1, GLA_DV), lambda b: (0, 0)),
        ],
        out_specs=pl.BlockSpec((None, SEQ, v_w), lambda b: (b, 0, 0)),
        out_shape=jax.ShapeDtypeStruct((BATCH, SEQ, v_w), BF16),
        scratch_shapes=[
            pltpu.VMEM((SEQ, qk_w), F32),
            pltpu.VMEM((GLA_HEADS // 2, PAIR_V, PAIR), F32),
        ],
        compiler_params=pltpu.CompilerParams(
            dimension_semantics=("arbitrary",), vmem_limit_bytes=VMEM_LIMIT),
        name="gla",
    )(qkv_g, qkv_g, qkv_g, qkv_g, glr, w_up, b_up, gla_norm_g)


def _merge_kernel(oa_ref, ob_ref, gate_ref, x_ref, gt1_ref, sh2_ref, sc2_ref, g2_ref,
                  wpa_ref, wpb_ref, wo_ref, wr_ref, br_ref,
                  x1_ref, h2p_ref, eidx_ref, rank_ref, wts_ref, cnt_ref,
                  upper_ref, carry_ref):
    i = pl.program_id(0)

    @pl.when(i == 0)
    def _():
        rr = lax.broadcasted_iota(I32, (TM_MERGE, TM_MERGE), 0)
        cc = lax.broadcasted_iota(I32, (TM_MERGE, TM_MERGE), 1)
        upper_ref[...] = jnp.where(rr < cc, 1.0, 0.0).astype(BF16)
        carry_ref[...] = jnp.zeros_like(carry_ref)

    w_hi, w_mid, _ = _split3(wr_ref[...])

    def token_rows(rows):
        ga = gate_ref[rows, :D_MODEL].astype(F32)
        gb = gate_ref[rows, D_MODEL:].astype(F32)
        merged = (jax.nn.sigmoid(ga) * jnp.dot(oa_ref[rows, :], wpa_ref[...], preferred_element_type=F32)
                  + jax.nn.sigmoid(gb) * jnp.dot(ob_ref[rows, :], wpb_ref[...], preferred_element_type=F32))
        y = jnp.dot(merged.astype(BF16), wo_ref[...], preferred_element_type=F32)
        x1 = x_ref[rows, :] + gt1_ref[...] * y
        x1_ref[rows, :] = x1
        inv = lax.rsqrt(jnp.mean(x1 * x1, axis=-1, keepdims=True) + EPS)
        h2 = (x1 * inv * g2_ref[...]) * (1.0 + sc2_ref[...]) + sh2_ref[...]
        h2p_ref[rows, :] = _pack_halves(h2)
        h_hi, h_mid, _ = _split3(h2)
        return _nt_dot(w_hi, h_hi) + _nt_dot(w_hi, h_mid) + _nt_dot(w_mid, h_hi)

    part = TM_MERGE // MERGE_PARTS
    logits = jnp.concatenate(
        [token_rows(slice(r * part, (r + 1) * part)) for r in range(MERGE_PARTS)], axis=1) + br_ref[...]

    eio = lax.broadcasted_iota(I32, (N_EXPERTS, TM_MERGE), 0)
    vals, idxs, sels = [], [], []
    cur = logits
    for _k in range(TOP_K):
        m = jnp.max(cur, axis=0, keepdims=True)
        idx = jnp.min(jnp.where(cur == m, eio, N_EXPERTS), axis=0, keepdims=True)
        sel = eio == idx
        vals.append(m)
        idxs.append(idx)
        sels.append(sel)
        cur = jnp.where(sel, -jnp.inf, cur)
    es = [jnp.exp(v - vals[0]) for v in vals]
    tot = es[0] + es[1] + es[2] + es[3]
    onehot = jnp.zeros((N_EXPERTS, TM_MERGE), F32)
    for sel in sels:
        onehot = onehot + jnp.where(sel, 1.0, 0.0)
    before = jnp.dot(onehot.astype(BF16), upper_ref[...], preferred_element_type=F32) + carry_ref[:, 0:1]
    ranks = [jnp.sum(jnp.where(sel, before, 0.0), axis=0, keepdims=True) for sel in sels]
    carry_ref[...] = carry_ref[...] + jnp.sum(onehot, axis=1, keepdims=True)
    cnt_ref[...] = carry_ref[...]

    zi = jnp.zeros((8 - TOP_K, TM_MERGE), I32)
    zf = jnp.zeros((8 - TOP_K, TM_MERGE), F32)
    eidx_ref[...] = jnp.concatenate(idxs + [zi], axis=0)
    rank_ref[...] = jnp.concatenate([r.astype(I32) for r in ranks] + [zi], axis=0)
    wts_ref[...] = jnp.concatenate([e / tot for e in es] + [zf], axis=0)


def _merge_route(o_a, o_b, gates, x2d, mod4, norm2_g, w_pa, w_pb, w_o, w_rt, b_r):
    ntile = N_TOK // TM_MERGE
    per_b = SEQ // TM_MERGE
    full = lambda shape: pl.BlockSpec(shape, lambda i: (0,) * len(shape))
    row = lambda w: pl.BlockSpec((TM_MERGE, w), lambda i: (i, 0))
    modspec = lambda j: pl.BlockSpec((None, None, 1, D_MODEL), lambda i: (j, i // per_b, 0, 0))
    col = pl.BlockSpec((8, TM_MERGE), lambda i: (0, i))
    return pl.pallas_call(
        _merge_kernel,
        grid=(ntile,),
        in_specs=[
            row(DIFF_HEADS * 2 * DIFF_DH), row(GLA_HEADS * GLA_DV), row(W_GATE), row(D_MODEL),
            modspec(2), modspec(3), modspec(4),
            full((1, D_MODEL)),
            full((DIFF_HEADS * 2 * DIFF_DH, D_MODEL)), full((GLA_HEADS * GLA_DV, D_MODEL)),
            full((D_MODEL, D_MODEL)),
            full((N_EXPERTS, D_MODEL)), full((N_EXPERTS, 1)),
        ],
        out_specs=[
            row(D_MODEL), row(ROW_W), col, col, col,
            pl.BlockSpec((N_EXPERTS, LANES), lambda i: (0, 0)),
        ],
        out_shape=[
            jax.ShapeDtypeStruct((N_TOK, D_MODEL), F32),
            jax.ShapeDtypeStruct((N_TOK, ROW_W), ROW_DT),
            jax.ShapeDtypeStruct((8, N_TOK), I32),
            jax.ShapeDtypeStruct((8, N_TOK), I32),
            jax.ShapeDtypeStruct((8, N_TOK), F32),
            jax.ShapeDtypeStruct((N_EXPERTS, LANES), F32),
        ],
        scratch_shapes=[
            pltpu.VMEM((TM_MERGE, TM_MERGE), BF16),
            pltpu.VMEM((N_EXPERTS, LANES), F32),
        ],
        compiler_params=pltpu.CompilerParams(
            dimension_semantics=("arbitrary",), vmem_limit_bytes=VMEM_LIMIT),
        name="merge_route",
    )(o_a, o_b, gates, x2d, mod4, mod4, mod4, norm2_g, w_pa, w_pb, w_o, w_rt, b_r)


TP = 4096
NB_PAD = ((N_BLK + LANES - 1) // LANES) * LANES


def _plan_kernel(cnt_ref, eidx_ref, rank_ref, dest_ref, be_ref, nv_ref):
    cnt = cnt_ref[...]
    padded = jnp.floor((cnt + (BLK - 1.0)) * (1.0 / BLK)) * BLK
    er = lax.broadcasted_iota(I32, (N_EXPERTS, N_EXPERTS), 0)
    ec = lax.broadcasted_iota(I32, (N_EXPERTS, N_EXPERTS), 1)
    lower = jnp.where(ec < er, 1.0, 0.0).astype(BF16)
    p_hi, p_mid, p_lo = _split3(padded)
    starts = (jnp.dot(lower, p_hi, preferred_element_type=F32)
              + jnp.dot(lower, p_mid, preferred_element_type=F32)
              + jnp.dot(lower, p_lo, preferred_element_type=F32))
    ends = starts + padded
    blk_start = (lax.broadcasted_iota(I32, (1, NB_PAD), 1) * BLK).astype(F32)
    n_before = jnp.sum(jnp.where(ends[:, 0:1] <= blk_start, 1.0, 0.0), axis=0, keepdims=True)
    be_ref[...] = jnp.minimum(n_before, N_EXPERTS - 1.0).astype(I32)
    nv_ref[...] = (jnp.max(ends, axis=0, keepdims=True) * (1.0 / BLK)).astype(I32)

    eio = lax.broadcasted_iota(I32, (N_EXPERTS, TP), 0)
    rows = []
    for k in range(TOP_K):
        onehot = eio == eidx_ref[k:k + 1, :]
        base = jnp.sum(jnp.where(onehot, starts[:, 0:1], 0.0), axis=0, keepdims=True)
        rows.append(base.astype(I32) + rank_ref[k:k + 1, :])
    dest_ref[...] = jnp.concatenate(rows + [jnp.zeros((8 - TOP_K, TP), I32)], axis=0)


def _route_plan(cnt, eidx, rank):
    return pl.pallas_call(
        _plan_kernel,
        grid=(N_TOK // TP,),
        in_specs=[
            pl.BlockSpec((N_EXPERTS, LANES), lambda i: (0, 0)),
            pl.BlockSpec((8, TP), lambda i: (0, i)),
            pl.BlockSpec((8, TP), lambda i: (0, i)),
        ],
        out_specs=[
            pl.BlockSpec((8, TP), lambda i: (0, i)),
            pl.BlockSpec((1, NB_PAD), lambda i: (0, 0)),
            pl.BlockSpec((1, LANES), lambda i: (0, 0)),
        ],
        out_shape=[
            jax.ShapeDtypeStruct((8, N_TOK), I32),
            jax.ShapeDtypeStruct((1, NB_PAD), I32),
            jax.ShapeDtypeStruct((1, LANES), I32),
        ],
        compiler_params=pltpu.CompilerParams(dimension_semantics=("arbitrary",)),
        name="route_plan",
    )(cnt, eidx, rank)


def _ffn_kernel(be_ref, nv_ref, x_ref, wgu_ref, bgu_ref, wd_ref, bd_ref, y_ref):
    i = pl.program_id(0)
    valid = i < nv_ref[0]

    @pl.when(valid)
    def _():
        lo, hi = _unpack_halves(x_ref[...])
        x = jnp.concatenate([lo, hi], axis=1).astype(BF16)
        mm = lambda a, w: lax.dot_general(a, w, (((1,), (0,)), ((), ())), preferred_element_type=F32)
        gu = mm(x, wgu_ref[...]) + bgu_ref[...]
        gate = jnp.minimum(gu[:, :D_FF], SWIGLU_LIMIT)
        up = jnp.clip(gu[:, D_FF:], -SWIGLU_LIMIT, SWIGLU_LIMIT)
        act = (up + 1.0) * (gate * jax.nn.sigmoid(SWIGLU_ALPHA * gate))
        y = mm(act.astype(BF16), wd_ref[...]) + bd_ref[...]
        y_ref[...] = _pack_halves(y)

    @pl.when(jnp.logical_not(valid))
    def _():
        y_ref[...] = jnp.zeros_like(y_ref)


def _expert_ffn(blk_expert, n_valid, xb, w_gate_up, b_gate_up, w_down, b_down):
    grid_spec = pltpu.PrefetchScalarGridSpec(
        num_scalar_prefetch=2,
        grid=(N_BLK,),
        in_specs=[
            pl.BlockSpec((BLK, ROW_W), lambda i, be, nv: (i, 0)),
            pl.BlockSpec((None, D_MODEL, 2 * D_FF), lambda i, be, nv: (be[i], 0, 0)),
            pl.BlockSpec((None, 1, 2 * D_FF), lambda i, be, nv: (be[i], 0, 0)),
            pl.BlockSpec((None, D_FF, D_MODEL), lambda i, be, nv: (be[i], 0, 0)),
            pl.BlockSpec((None, 1, D_MODEL), lambda i, be, nv: (be[i], 0, 0)),
        ],
        out_specs=pl.BlockSpec((BLK, ROW_W), lambda i, be, nv: (i, 0)),
    )
    return pl.pallas_call(
        _ffn_kernel,
        grid_spec=grid_spec,
        out_shape=jax.ShapeDtypeStruct((P_ROWS, ROW_W), ROW_DT),
        compiler_params=pltpu.CompilerParams(
            dimension_semantics=("arbitrary",), vmem_limit_bytes=VMEM_LIMIT),
        name="expert_ffn",
    )(blk_expert, n_valid, xb, w_gate_up, b_gate_up.reshape(N_EXPERTS, 1, 2 * D_FF),
      w_down, b_down.reshape(N_EXPERTS, 1, D_MODEL))


def _final_kernel(x1_ref, y0_ref, y1_ref, y2_ref, y3_ref, w_ref, gt2_ref, g_ref, o_ref):
    w = w_ref[...]
    ylo = jnp.zeros((TM_FIN, HALF), F32)
    yhi = jnp.zeros((TM_FIN, HALF), F32)
    for k, y_ref in enumerate((y0_ref, y1_ref, y2_ref, y3_ref)):
        lo, hi = _unpack_halves(y_ref[...])
        wk = w[:, k:k + 1]
        ylo = ylo + wk * lo
        yhi = yhi + wk * hi
    y = jnp.concatenate([ylo, yhi], axis=1)
    x2 = x1_ref[...] + gt2_ref[...] * y
    inv = lax.rsqrt(jnp.mean(x2 * x2, axis=-1, keepdims=True) + EPS)
    o_ref[...] = x2 * inv * g_ref[...]


def _final(x1, yg, w4, mod4, final_norm_g):
    per_b = SEQ // TM_FIN
    ntile = N_TOK // TM_FIN
    slot = lambda k: pl.BlockSpec((TM_FIN, ROW_W), lambda i: (k * ntile + i, 0))
    return pl.pallas_call(
        _final_kernel,
        grid=(ntile,),
        in_specs=[
            pl.BlockSpec((TM_FIN, D_MODEL), lambda i: (i, 0)),
            slot(0), slot(1), slot(2), slot(3),
            pl.BlockSpec((TM_FIN, TOP_K), lambda i: (i, 0)),
            pl.BlockSpec((None, None, 1, D_MODEL), lambda i: (5, i // per_b, 0, 0)),
            pl.BlockSpec((1, D_MODEL), lambda i: (0, 0)),
        ],
        out_specs=pl.BlockSpec((TM_FIN, D_MODEL), lambda i: (i, 0)),
        out_shape=jax.ShapeDtypeStruct((N_TOK, D_MODEL), F32),
        compiler_params=pltpu.CompilerParams(
            dimension_semantics=("arbitrary",), vmem_limit_bytes=VMEM_LIMIT),
        name="combine_final",
    )(x1, yg, yg, yg, yg, w4, mod4, final_norm_g)


SC_CORES = 2
SC_SUBCORES = 16
SC_WORKERS = SC_CORES * SC_SUBCORES
SC_CHUNK = 64


def _sc_mesh():
    return plsc.VectorSubcoreMesh(core_axis_name="c", subcore_axis_name="s")


def _row_buffers():
    return ([pltpu.VMEM((SC_CHUNK, HALF), U32)] * 2 + [pltpu.SemaphoreType.DMA] * 5)


def _dispatch_rows(h2p, dest2d):
    per_w = N_TOK // SC_WORKERS
    nchunk = per_w // SC_CHUNK
    rows_per_k = N_TOK // SC_CHUNK
    assert nchunk % 2 == 0

    @functools.partial(
        pl.kernel, mesh=_sc_mesh(), out_type=jax.ShapeDtypeStruct((P_ROWS, HALF), U32),
        scratch_types=[pltpu.VMEM((TOP_K * nchunk, SC_CHUNK), I32)] + _row_buffers(),
        name="moe_dispatch")
    def k(src_hbm, dest_hbm, out_hbm, idx_v, buf0, buf1, isem, l0, l1, s0, s1):
        wid = lax.axis_index("s") * SC_CORES + lax.axis_index("c")
        bufs, lsem, ssem = (buf0, buf1), (l0, l1), (s0, s1)
        idx_loads = [
            pltpu.make_async_copy(dest_hbm.at[pl.ds(kk * rows_per_k + wid * nchunk, nchunk)],
                                  idx_v.at[pl.ds(kk * nchunk, nchunk)], isem) for kk in range(TOP_K)]

        def load(j, b):
            return pltpu.make_async_copy(src_hbm.at[pl.ds(wid * per_w + j * SC_CHUNK, SC_CHUNK)], bufs[b], lsem[b])

        def scatters(j, b):
            return [pltpu.make_async_copy(bufs[b], out_hbm.at[idx_v.at[kk * nchunk + j]], ssem[b])
                    for kk in range(TOP_K)]

        for cp in idx_loads:
            cp.start()
        load(0, 0).start()
        for cp in idx_loads:
            cp.wait()

        @pl.loop(0, nchunk // 2)
        def _(i):
            for b in range(2):
                j = 2 * i + b

                @pl.when(j >= 1)
                def _():
                    for cp in scatters(j - 1, 1 - b):
                        cp.wait()

                @pl.when(j + 1 < nchunk)
                def _():
                    load(j + 1, 1 - b).start()

                load(j, b).wait()
                for cp in scatters(j, b):
                    cp.start()

        for cp in scatters(nchunk - 1, 1):
            cp.wait()

    return k(h2p, dest2d)


def _combine_rows(yb, dest2d):
    n_out = TOP_K * N_TOK
    per_w = n_out // SC_WORKERS
    nchunk = per_w // SC_CHUNK
    assert nchunk % 2 == 0

    @functools.partial(
        pl.kernel, mesh=_sc_mesh(), out_type=jax.ShapeDtypeStruct((n_out, HALF), U32),
        scratch_types=[pltpu.VMEM((nchunk, SC_CHUNK), I32)] + _row_buffers(),
        name="moe_combine")
    def k(tab_hbm, idx_hbm, out_hbm, idx_v, buf0, buf1, isem, g0, g1, s0, s1):
        wid = lax.axis_index("s") * SC_CORES + lax.axis_index("c")
        bufs, gsem, ssem = (buf0, buf1), (g0, g1), (s0, s1)
        idx_load = pltpu.make_async_copy(idx_hbm.at[pl.ds(wid * nchunk, nchunk)], idx_v, isem)

        def gather(j, b):
            return pltpu.make_async_copy(tab_hbm.at[idx_v.at[j]], bufs[b], gsem[b])

        def store(j, b):
            return pltpu.make_async_copy(bufs[b], out_hbm.at[pl.ds(wid * per_w + j * SC_CHUNK, SC_CHUNK)], ssem[b])

        idx_load.start()
        idx_load.wait()
        gather(0, 0).start()

        @pl.loop(0, nchunk // 2)
        def _(i):
            for b in range(2):
                j = 2 * i + b

                @pl.when(j >= 1)
                def _():
                    store(j - 1, 1 - b).wait()

                @pl.when(j + 1 < nchunk)
                def _():
                    gather(j + 1, 1 - b).start()

                gather(j, b).wait()
                store(j, b).start()

        store(nchunk - 1, 1).wait()

    return k(yb, dest2d)


def _lambda_kernel(p_ref, o_ref):
    p = p_ref[...]
    s1 = jnp.sum(p[0:1] * p[1:2], axis=-1, keepdims=True)
    s2 = jnp.sum(p[2:3] * p[3:4], axis=-1, keepdims=True)
    o_ref[...] = jnp.broadcast_to(jnp.exp(s1) - jnp.exp(s2) + LAMBDA_INIT, (1, LANES))


def kernel(x, c, w_ada, b_ada, norm1_g, w_in, lambda_q1, lambda_k1, lambda_q2, lambda_k2, diff_norm_g, w_alpha_up, b_alpha, gla_norm_g, w_branch_diff, w_branch_gla, w_out, norm2_g, w_router, b_router, w_gate_up, b_gate_up, w_down, b_down, final_norm_g):
    w_in0 = w_in[0]
    c_a, c_g = W_A, W_A + W_G
    w_a = w_in0[:, :c_a].astype(BF16)
    w_g = w_in0[:, c_a:c_g].astype(BF16)
    w_lr = jnp.pad(w_in0[:, c_g:c_g + GLA_RANK], ((0, 0), (0, LANES - GLA_RANK))).astype(BF16)
    w_gate = w_in0[:, c_g + GLA_RANK:].astype(BF16)
    w_up = jnp.pad(w_alpha_up[0], ((0, LANES - GLA_RANK), (0, 0)))
    lam_in = jnp.concatenate([lambda_q1, lambda_k1, lambda_q2, lambda_k2], axis=0)
    slopes = jnp.asarray(2.0 ** (-8.0 * np.arange(1, DIFF_HEADS + 1) / DIFF_HEADS), dtype=F32)

    mod = _modulation(c, w_ada[0], b_ada[0])
    mod4 = mod.reshape(N_MOD, BATCH, 1, D_MODEL)
    lam = pl.pallas_call(
        _lambda_kernel, out_shape=jax.ShapeDtypeStruct((1, LANES), F32), name="lambda")(lam_in)[0, :1]

    qkv_a, qkv_g, gates, glr = _in_proj(x, mod4, norm1_g, w_a, w_g, w_gate, w_lr)
    pos = jnp.arange(SEQ, dtype=I32)
    p_hi, p_lo = (pos >> 3).astype(F32), (pos & 7).astype(F32)
    kaug = jnp.zeros((SEQ, LANES), F32).at[:, 0].set(p_hi).at[:, 1].set(p_hi).at[:, 2].set(p_lo).at[:, 3].set(
        p_lo).astype(BF16)
    o_a = _diff_attention(qkv_a, slopes, lam, kaug, diff_norm_g)
    o_b = _gla(qkv_g, glr, w_up, b_alpha, gla_norm_g)

    x1, h2p, eidx, rank, wts, cnt = _merge_route(
        o_a.reshape(N_TOK, -1), o_b.reshape(N_TOK, -1), gates.reshape(N_TOK, W_GATE),
        x.reshape(N_TOK, D_MODEL), mod4, norm2_g,
        w_branch_diff[0].astype(BF16), w_branch_gla[0].astype(BF16), w_out[0].astype(BF16),
        w_router[0].T, b_router[0].reshape(N_EXPERTS, 1))

    dest8, be, nv = _route_plan(cnt, eidx, rank)
    dest = dest8[:TOP_K].reshape(-1, SC_CHUNK)

    xb = _dispatch_rows(h2p, dest)
    yb = _expert_ffn(be[0, :N_BLK], nv[0, :1], xb, w_gate_up[0], b_gate_up[0], w_down[0], b_down[0])
    yg = _combine_rows(yb, dest)

    out = _final(x1, yg, wts[:TOP_K].T, mod4, final_norm_g.reshape(1, D_MODEL))
    return out.reshape(BATCH, SEQ, D_MODEL)
```

```python
import functools
import math

import jax
import jax.numpy as jnp
import numpy as np
from jax import lax
from jax.experimental import pallas as pl
from jax.experimental.pallas import tpu as pltpu
from jax.experimental.pallas import tpu_sc as plsc

F32 = jnp.float32
BF16 = jnp.bfloat16
U32 = jnp.uint32
I32 = jnp.int32

D_MODEL = 1024
BATCH = 16
SEQ = 2048
N_TOK = BATCH * SEQ
CHUNK = 64
DIFF_HEADS = 4
DIFF_DH = 64
GLA_HEADS = 4
GLA_DK = 64
GLA_DV = 128
GLA_RANK = 16
GLA_GATE_NORM = 16.0
N_EXPERTS = 32
TOP_K = 4
D_FF = D_MODEL
SWIGLU_LIMIT = 7.0
SWIGLU_ALPHA = 1.702
N_MOD = 6
EPS = 1e-6
LAMBDA_INIT = 0.8 - 0.6 * math.exp(-0.3 * 0)

LANES = 128
HALF = D_MODEL // 2
ROW_W = HALF
ROW_DT = U32

TM_IN = 512
TQ = 512
VT_ROWS = 2 * DIFF_DH + 16
LOG2E = math.log2(math.e)
TM_MERGE = 512
MERGE_PARTS = 1
BLK = 512
MOE_GROUPS = 2
N_GRP = N_TOK // MOE_GROUPS
N_BLK = (N_GRP * TOP_K) // BLK + N_EXPERTS
P_ROWS = N_BLK * BLK
TM_FIN = 512
VMEM_LIMIT = 56 * 1024 * 1024


def _nt_dot(a, b):
    return lax.dot_general(a, b, (((1,), (1,)), ((), ())), preferred_element_type=F32)


def _tn_dot(a, b):
    return lax.dot_general(a, b, (((0,), (0,)), ((), ())), preferred_element_type=F32)


def _split3(x):
    hi = x.astype(BF16)
    r1 = x - hi.astype(F32)
    mid = r1.astype(BF16)
    lo = (r1 - mid.astype(F32)).astype(BF16)
    return hi, mid, lo


def _pack_halves(y):
    return pltpu.pack_elementwise([y[:, :HALF], y[:, HALF:]], packed_dtype=BF16)


def _unpack_halves(u):
    lo = pltpu.unpack_elementwise(u, index=0, packed_dtype=BF16, unpacked_dtype=F32)
    hi = pltpu.unpack_elementwise(u, index=1, packed_dtype=BF16, unpacked_dtype=F32)
    return lo, hi


def _mod_kernel(c_ref, w_ref, b_ref, o_ref):
    c = c_ref[...]
    s = c * jax.nn.sigmoid(c)
    o_ref[0] = jnp.dot(s.astype(BF16), w_ref[...].astype(BF16),
                       preferred_element_type=F32) + b_ref[...]


def _modulation(c, w_ada, b_ada):
    return pl.pallas_call(
        _mod_kernel,
        grid=(N_MOD,),
        in_specs=[
            pl.BlockSpec((BATCH, D_MODEL), lambda j: (0, 0)),
            pl.BlockSpec((D_MODEL, D_MODEL), lambda j: (0, j)),
            pl.BlockSpec((1, D_MODEL), lambda j: (0, j)),
        ],
        out_specs=pl.BlockSpec((1, BATCH, D_MODEL), lambda j: (j, 0, 0)),
        out_shape=jax.ShapeDtypeStruct((N_MOD, BATCH, D_MODEL), F32),
        compiler_params=pltpu.CompilerParams(dimension_semantics=("arbitrary",)),
        name="adaln_mod",
    )(c, w_ada, b_ada.reshape(1, N_MOD * D_MODEL))


W_A = 3 * DIFF_HEADS * 2 * DIFF_DH
W_G = 2 * GLA_HEADS * GLA_DK + 2 * GLA_HEADS * GLA_DV
W_GATE = 2 * D_MODEL


def _in_kernel(x_ref, sh_ref, sc_ref, g_ref, wa_ref, wg_ref, wgate_ref, wlr_ref,
               oa_ref, og_ref, ogate_ref, olr_ref):
    x = x_ref[...]
    inv = lax.rsqrt(jnp.mean(x * x, axis=-1, keepdims=True) + EPS)
    h = (x * inv * g_ref[...]) * (1.0 + sc_ref[...]) + sh_ref[...]
    hb = h.astype(BF16)
    oa_ref[...] = jnp.dot(hb, wa_ref[...], preferred_element_type=F32).astype(BF16)
    og_ref[...] = jnp.dot(hb, wg_ref[...], preferred_element_type=F32).astype(BF16)
    ogate_ref[...] = jnp.dot(hb, wgate_ref[...], preferred_element_type=F32).astype(BF16)
    olr_ref[...] = jnp.dot(hb, wlr_ref[...], preferred_element_type=F32)


def _in_proj(x, mod4, norm1_g, w_a, w_g, w_gate, w_lr):
    nrow = SEQ // TM_IN
    full = lambda shape: pl.BlockSpec(shape, lambda b, i: (0,) * len(shape))
    return pl.pallas_call(
        _in_kernel,
        grid=(BATCH, nrow),
        in_specs=[
            pl.BlockSpec((None, TM_IN, D_MODEL), lambda b, i: (b, i, 0)),
            pl.BlockSpec((None, None, 1, D_MODEL), lambda b, i: (0, b, 0, 0)),
            pl.BlockSpec((None, None, 1, D_MODEL), lambda b, i: (1, b, 0, 0)),
            full((1, D_MODEL)),
            full((D_MODEL, W_A)),
            full((D_MODEL, W_G)),
            full((D_MODEL, W_GATE)),
            full((D_MODEL, LANES)),
        ],
        out_specs=[
            pl.BlockSpec((None, TM_IN, W_A), lambda b, i: (b, i, 0)),
            pl.BlockSpec((None, TM_IN, W_G), lambda b, i: (b, i, 0)),
            pl.BlockSpec((None, TM_IN, W_GATE), lambda b, i: (b, i, 0)),
            pl.BlockSpec((None, TM_IN, LANES), lambda b, i: (b, i, 0)),
        ],
        out_shape=[
            jax.ShapeDtypeStruct((BATCH, SEQ, W_A), BF16),
            jax.ShapeDtypeStruct((BATCH, SEQ, W_G), BF16),
            jax.ShapeDtypeStruct((BATCH, SEQ, W_GATE), BF16),
            jax.ShapeDtypeStruct((BATCH, SEQ, LANES), F32),
        ],
        compiler_params=pltpu.CompilerParams(
            dimension_semantics=("arbitrary", "arbitrary"), vmem_limit_bytes=VMEM_LIMIT),
        name="in_proj",
    )(x, mod4, mod4, norm1_g, w_a, w_g, w_gate, w_lr)


def _attn_kernel(slope_ref, lam_ref, q_ref, k_ref, v_ref, kaug_ref, g_ref, o_ref,
                 sa_ref, sb_ref, corr_ref, vt_ref, m_ref, acc_ref):
    c_alibi = slope_ref[pl.program_id(1)] * LOG2E
    lam = lam_ref[0]
    lane = lax.broadcasted_iota(I32, (TQ, 2 * DIFF_DH), 1)
    c_vec = jnp.full((TQ, 2 * DIFF_DH), c_alibi, F32)
    c_hi = c_vec.astype(BF16).astype(F32)
    c_lo = (c_vec - c_hi).astype(BF16).astype(F32)
    zero = jnp.zeros((TQ, 2 * DIFF_DH), BF16)
    aug = jnp.where(lane == 0, 8.0 * c_hi, jnp.where(lane == 1, 8.0 * c_lo,
                    jnp.where(lane == 2, c_hi, jnp.where(lane == 3, c_lo, 0.0)))).astype(BF16)

    def blk(j):
        return slice(j * TQ, (j + 1) * TQ)

    def stacked_queries(qi):
        q = (q_ref[blk(qi), :].astype(F32) * (DIFF_DH ** -0.5 * LOG2E)).astype(BF16)
        return jnp.concatenate([
            jnp.concatenate([jnp.where(lane < DIFF_DH, q, zero), aug], axis=1),
            jnp.concatenate([jnp.where(lane >= DIFF_DH, q, zero), aug], axis=1)], axis=0)

    def scores(lhs, j):
        kk = jnp.concatenate([k_ref[blk(j), :], kaug_ref[blk(j), :]], axis=1)
        return _nt_dot(kk, lhs)

    ones_rows = jnp.where(lax.broadcasted_iota(I32, (VT_ROWS - 2 * DIFF_DH, TQ), 0) == 0, 1.0, 0.0).astype(BF16)
    for j in range(SEQ // TQ):
        vt_ref[j, :2 * DIFF_DH, :] = v_ref[blk(j), :].astype(F32).T.astype(BF16)
        vt_ref[j, 2 * DIFF_DH:, :] = ones_rows

    def update(s, j, first):
        if first:
            m_new = jnp.max(s, axis=0, keepdims=True)
            p = jnp.exp2((s - m_new).astype(BF16))
            acc_ref[...] = jnp.dot(vt_ref[j], p, preferred_element_type=F32)
        else:
            m = m_ref[...]
            m_new = jnp.maximum(m, jnp.max(s, axis=0, keepdims=True))
            alpha = jnp.exp2(m - m_new)
            p = jnp.exp2((s - m_new).astype(BF16))
            acc_ref[...] = alpha * acc_ref[...] + jnp.dot(vt_ref[j], p, preferred_element_type=F32)
        m_ref[...] = m_new

    kr = lax.broadcasted_iota(I32, (TQ, TQ), 0)
    qc = lax.broadcasted_iota(I32, (TQ, TQ), 1)
    ahead = jnp.maximum(kr - qc, 0).astype(F32)
    corr_ref[...] = jnp.where((qc >> 6) >= (kr >> 6), (-2.0 * c_alibi) * ahead, -jnp.inf)

    pairs = [(qi, j) for qi in range(SEQ // TQ) for j in range(qi + 1)]
    bufs = (sa_ref, sb_ref)
    lhs = stacked_queries(0)
    bufs[0][...] = scores(lhs, 0)
    for t, (qi, j) in enumerate(pairs):
        if t + 1 < len(pairs):
            nqi, nj = pairs[t + 1]
            if nqi != qi:
                lhs = stacked_queries(nqi)
            bufs[(t + 1) % 2][...] = scores(lhs, nj)
        s = bufs[t % 2][...]
        if j == qi:
            corr = corr_ref[...]
            s = s + jnp.concatenate([corr, corr], axis=1)
        update(s, j, first=(j == 0))
        if j == qi:
            ot = acc_ref[:2 * DIFF_DH, :] / acc_ref[2 * DIFF_DH:2 * DIFF_DH + 1, :]
            o = (ot[:, :TQ] - lam * ot[:, TQ:]).T
            inv = lax.rsqrt(jnp.mean(o * o, axis=-1, keepdims=True) + EPS)
            o_ref[blk(qi), :] = (o * inv * g_ref[...] * (1.0 - LAMBDA_INIT)).astype(BF16)


def _diff_attention(qkv_a, slopes, lam, kaug, diff_norm_g):
    return pl.pallas_call(
        _attn_kernel,
        grid=(BATCH, DIFF_HEADS),
        in_specs=[
            pl.BlockSpec(memory_space=pltpu.SMEM),
            pl.BlockSpec(memory_space=pltpu.SMEM),
            pl.BlockSpec((None, SEQ, LANES), lambda b, h: (b, 0, h)),
            pl.BlockSpec((None, SEQ, LANES), lambda b, h: (b, 0, DIFF_HEADS + h)),
            pl.BlockSpec((None, SEQ, LANES), lambda b, h: (b, 0, 2 * DIFF_HEADS + h)),
            pl.BlockSpec((SEQ, LANES), lambda b, h: (0, 0)),
            pl.BlockSpec((1, LANES), lambda b, h: (0, 0)),
        ],
        out_specs=pl.BlockSpec((None, SEQ, LANES), lambda b, h: (b, 0, h)),
        out_shape=jax.ShapeDtypeStruct((BATCH, SEQ, DIFF_HEADS * 2 * DIFF_DH), BF16),
        scratch_shapes=[
            pltpu.VMEM((TQ, 2 * TQ), F32), pltpu.VMEM((TQ, 2 * TQ), F32),
            pltpu.VMEM((TQ, TQ), F32),
            pltpu.VMEM((SEQ // TQ, VT_ROWS, TQ), BF16),
            pltpu.VMEM((1, 2 * TQ), F32),
            pltpu.VMEM((VT_ROWS, 2 * TQ), F32),
        ],
        compiler_params=pltpu.CompilerParams(
            dimension_semantics=("arbitrary", "arbitrary")),
        name="diff_attn",
    )(slopes, lam, qkv_a, qkv_a, qkv_a, kaug, diff_norm_g)


N_CHUNK = SEQ // CHUNK
PAIR = 2 * GLA_DK
PAIR_V = 2 * GLA_DV
CS_ROWS = 256
GLA_UNROLL = 8


def _gla_kernel(q_ref, k_ref, v_ref, r_ref, lr_ref, wup_ref, bup_ref, g_ref, o_ref,
                gcum_ref, state_ref):
    w_hi, w_mid, _ = _split3(wup_ref[...])
    rr = lax.broadcasted_iota(I32, (CS_ROWS, CS_ROWS), 0)
    cc = lax.broadcasted_iota(I32, (CS_ROWS, CS_ROWS), 1)
    tri = jnp.where(((rr >> 6) == (cc >> 6)) & (cc <= rr), 1.0, 0.0).astype(BF16)
    for blk in range(SEQ // CS_ROWS):
        rows = pl.ds(blk * CS_ROWS, CS_ROWS)
        a_hi, a_mid, _ = _split3(lr_ref[rows, :])
        z = (jnp.dot(a_hi, w_hi, preferred_element_type=F32)
             + jnp.dot(a_hi, w_mid, preferred_element_type=F32)
             + jnp.dot(a_mid, w_hi, preferred_element_type=F32)) + bup_ref[...]
        la = (jnp.minimum(z, 0.0) - jnp.log(1.0 + jnp.exp(-jnp.abs(z)))) * (1.0 / GLA_GATE_NORM)
        l_hi, l_mid, l_lo = _split3(la)
        gcum_ref[rows, :] = (jnp.dot(tri, l_hi, preferred_element_type=F32)
                             + jnp.dot(tri, l_mid, preferred_element_type=F32)
                             + jnp.dot(tri, l_lo, preferred_element_type=F32))

    state_ref[...] = jnp.zeros_like(state_ref)
    lane_k = lax.broadcasted_iota(I32, (1, PAIR), 1)
    row_v = lax.broadcasted_iota(I32, (PAIR_V, PAIR), 0)
    col_k = lax.broadcasted_iota(I32, (PAIR_V, PAIR), 1)
    same_head = (row_v >= GLA_DV) == (col_k >= GLA_DK)
    cr = lax.broadcasted_iota(I32, (CHUNK, CHUNK), 0)
    cs = lax.broadcasted_iota(I32, (CHUNK, CHUNK), 1)
    causal = cs <= cr
    scale = GLA_DK ** -0.5

    def chunk(n):
        rows = pl.ds(pl.multiple_of(n * CHUNK, CHUNK), CHUNK)
        gc = gcum_ref[rows, :]
        g_last = gcum_ref[pl.ds(n * CHUNK + CHUNK - 1, 1), :]
        qf = q_ref[rows, :].astype(F32) * scale
        kf = k_ref[rows, :].astype(F32)
        q_s = (qf * jnp.exp(gc)).astype(BF16)
        k_s = (kf * jnp.exp(-gc)).astype(BF16)
        k_d = (kf * jnp.exp(g_last - gc)).astype(BF16)
        decay = jnp.exp(g_last)
        for pr in range(GLA_HEADS // 2):
            kl = slice(pr * PAIR, (pr + 1) * PAIR)
            vl = slice(pr * PAIR_V, (pr + 1) * PAIR_V)
            qs_p, ks_p, kd_p = q_s[:, kl], k_s[:, kl], k_d[:, kl]
            v_p = v_ref[rows, vl]
            st = state_ref[pr]
            o_inter = _nt_dot(qs_p, st.astype(BF16))
            d_st = _tn_dot(v_p, kd_p)
            state_ref[pr] = st * decay[:, kl] + jnp.where(same_head, d_st, 0.0)
            for sub in range(2):
                hd = 2 * pr + sub
                in_head = (lane_k >= sub * GLA_DK) & (lane_k < (sub + 1) * GLA_DK)
                a = _nt_dot(jnp.where(in_head, qs_p, jnp.zeros_like(qs_p)), ks_p)
                a = jnp.where(causal, a, 0.0).astype(BF16)
                vs = slice(hd * GLA_DV, (hd + 1) * GLA_DV)
                o = (jnp.dot(a, v_ref[rows, vs], preferred_element_type=F32)
                     + o_inter[:, sub * GLA_DV:(sub + 1) * GLA_DV])
                inv = lax.rsqrt(jnp.mean(o * o, axis=-1, keepdims=True) + EPS)
                r = r_ref[rows, vs].astype(F32)
                o_ref[rows, vs] = (o * inv * g_ref[...] * (r * jax.nn.sigmoid(r))).astype(BF16)

    def chunk_group(t, _):
        for u in range(GLA_UNROLL):
            chunk(t * GLA_UNROLL + u)
        return 0

    lax.fori_loop(0, N_CHUNK // GLA_UNROLL, chunk_group, 0)


def _gla(qkv_g, glr, w_up, b_up, gla_norm_g):
    qk_w = GLA_HEADS * GLA_DK
    v_w = GLA_HEADS * GLA_DV
    return pl.pallas_call(
        _gla_kernel,
        grid=(BATCH,),
        in_specs=[
            pl.BlockSpec((None, SEQ, qk_w), lambda b: (b, 0, 0)),
            pl.BlockSpec((None, SEQ, qk_w), lambda b: (b, 0, 1)),
            pl.BlockSpec((None, SEQ, v_w), lambda b: (b, 0, 1)),
            pl.BlockSpec((None, SEQ, v_w), lambda b: (b, 0, 2)),
            pl.BlockSpec((None, SEQ, LANES), lambda b: (b, 0, 0)),
            pl.BlockSpec((LANES, qk_w), lambda b: (0, 0)),
            pl.BlockSpec((1, qk_w), lambda b: (0, 0)),
            pl.BlockSpec((1, GLA_DV), lambda b: (0, 0)),
        ],
        out_specs=pl.BlockSpec((None, SEQ, v_w), lambda b: (b, 0, 0)),
        out_shape=jax.ShapeDtypeStruct((BATCH, SEQ, v_w), BF16),
        scratch_shapes=[
            pltpu.VMEM((SEQ, qk_w), F32),
            pltpu.VMEM((GLA_HEADS // 2, PAIR_V, PAIR), F32),
        ],
        compiler_params=pltpu.CompilerParams(
            dimension_semantics=("arbitrary",), vmem_limit_bytes=VMEM_LIMIT),
        name="gla",
    )(qkv_g, qkv_g, qkv_g, qkv_g, glr, w_up, b_up, gla_norm_g)


def _merge_kernel(oa_ref, ob_ref, gate_ref, x_ref, gt1_ref, sh2_ref, sc2_ref, g2_ref,
                  wpa_ref, wpb_ref, wo_ref, wr_ref, br_ref,
                  x1_ref, h2p_ref, eidx_ref, rank_ref, wts_ref, cnt_ref,
                  upper_ref, carry_ref):
    i = pl.program_id(0)

    @pl.when(i == 0)
    def _():
        rr = lax.broadcasted_iota(I32, (TM_MERGE, TM_MERGE), 0)
        cc = lax.broadcasted_iota(I32, (TM_MERGE, TM_MERGE), 1)
        upper_ref[...] = jnp.where(rr < cc, 1.0, 0.0).astype(BF16)
        carry_ref[...] = jnp.zeros_like(carry_ref)

    w_hi, w_mid, _ = _split3(wr_ref[...])

    def token_rows(rows):
        ga = gate_ref[rows, :D_MODEL].astype(F32)
        gb = gate_ref[rows, D_MODEL:].astype(F32)
        merged = (jax.nn.sigmoid(ga) * jnp.dot(oa_ref[rows, :], wpa_ref[...], preferred_element_type=F32)
                  + jax.nn.sigmoid(gb) * jnp.dot(ob_ref[rows, :], wpb_ref[...], preferred_element_type=F32))
        y = jnp.dot(merged.astype(BF16), wo_ref[...], preferred_element_type=F32)
        x1 = x_ref[rows, :] + gt1_ref[...] * y
        x1_ref[rows, :] = x1
        inv = lax.rsqrt(jnp.mean(x1 * x1, axis=-1, keepdims=True) + EPS)
        h2 = (x1 * inv * g2_ref[...]) * (1.0 + sc2_ref[...]) + sh2_ref[...]
        h2p_ref[rows, :] = _pack_halves(h2)
        h_hi, h_mid, _ = _split3(h2)
        return _nt_dot(w_hi, h_hi) + _nt_dot(w_hi, h_mid) + _nt_dot(w_mid, h_hi)

    part = TM_MERGE // MERGE_PARTS
    logits = jnp.concatenate(
        [token_rows(slice(r * part, (r + 1) * part)) for r in range(MERGE_PARTS)], axis=1) + br_ref[...]

    eio = lax.broadcasted_iota(I32, (N_EXPERTS, TM_MERGE), 0)
    vals, idxs, sels = [], [], []
    cur = logits
    for _k in range(TOP_K):
        m = jnp.max(cur, axis=0, keepdims=True)
        idx = jnp.min(jnp.where(cur == m, eio, N_EXPERTS), axis=0, keepdims=True)
        sel = eio == idx
        vals.append(m)
        idxs.append(idx)
        sels.append(sel)
        cur = jnp.where(sel, -jnp.inf, cur)
    es = [jnp.exp(v - vals[0]) for v in vals]
    tot = es[0] + es[1] + es[2] + es[3]
    onehot = jnp.zeros((N_EXPERTS, TM_MERGE), F32)
    for sel in sels:
        onehot = onehot + jnp.where(sel, 1.0, 0.0)
    before = jnp.dot(onehot.astype(BF16), upper_ref[...], preferred_element_type=F32) + carry_ref[:, 0:1]
    ranks = [jnp.sum(jnp.where(sel, before, 0.0), axis=0, keepdims=True) for sel in sels]
    carry_ref[...] = carry_ref[...] + jnp.sum(onehot, axis=1, keepdims=True)
    cnt_ref[...] = carry_ref[...]

    zi = jnp.zeros((8 - TOP_K, TM_MERGE), I32)
    zf = jnp.zeros((8 - TOP_K, TM_MERGE), F32)
    eidx_ref[...] = jnp.concatenate(idxs + [zi], axis=0)
    rank_ref[...] = jnp.concatenate([r.astype(I32) for r in ranks] + [zi], axis=0)
    wts_ref[...] = jnp.concatenate([e / tot for e in es] + [zf], axis=0)


def _merge_route(grp, o_a, o_b, gates, x2d, mod4, norm2_g, w_pa, w_pb, w_o, w_rt, b_r):
    ntile = N_GRP // TM_MERGE
    first = grp * ntile
    per_b = SEQ // TM_MERGE
    full = lambda shape: pl.BlockSpec(shape, lambda i: (0,) * len(shape))
    row_in = lambda w: pl.BlockSpec((TM_MERGE, w), lambda i: (first + i, 0))
    row = lambda w: pl.BlockSpec((TM_MERGE, w), lambda i: (i, 0))
    modspec = lambda j: pl.BlockSpec((None, None, 1, D_MODEL), lambda i: (j, (first + i) // per_b, 0, 0))
    col = pl.BlockSpec((8, TM_MERGE), lambda i: (0, i))
    return pl.pallas_call(
        _merge_kernel,
        grid=(ntile,),
        in_specs=[
            row_in(DIFF_HEADS * 2 * DIFF_DH), row_in(GLA_HEADS * GLA_DV), row_in(W_GATE), row_in(D_MODEL),
            modspec(2), modspec(3), modspec(4),
            full((1, D_MODEL)),
            full((DIFF_HEADS * 2 * DIFF_DH, D_MODEL)), full((GLA_HEADS * GLA_DV, D_MODEL)),
            full((D_MODEL, D_MODEL)),
            full((N_EXPERTS, D_MODEL)), full((N_EXPERTS, 1)),
        ],
        out_specs=[
            row(D_MODEL), row(ROW_W), col, col, col,
            pl.BlockSpec((N_EXPERTS, LANES), lambda i: (0, 0)),
        ],
        out_shape=[
            jax.ShapeDtypeStruct((N_GRP, D_MODEL), F32),
            jax.ShapeDtypeStruct((N_GRP, ROW_W), ROW_DT),
            jax.ShapeDtypeStruct((8, N_GRP), I32),
            jax.ShapeDtypeStruct((8, N_GRP), I32),
            jax.ShapeDtypeStruct((8, N_GRP), F32),
            jax.ShapeDtypeStruct((N_EXPERTS, LANES), F32),
        ],
        scratch_shapes=[
            pltpu.VMEM((TM_MERGE, TM_MERGE), BF16),
            pltpu.VMEM((N_EXPERTS, LANES), F32),
        ],
        compiler_params=pltpu.CompilerParams(
            dimension_semantics=("arbitrary",), vmem_limit_bytes=VMEM_LIMIT),
        name="merge_route",
    )(o_a, o_b, gates, x2d, mod4, mod4, mod4, norm2_g, w_pa, w_pb, w_o, w_rt, b_r)


TP = 4096
NB_PAD = ((N_BLK + LANES - 1) // LANES) * LANES


def _plan_kernel(cnt_ref, eidx_ref, rank_ref, dest_ref, be_ref, nv_ref):
    cnt = cnt_ref[...]
    padded = jnp.floor((cnt + (BLK - 1.0)) * (1.0 / BLK)) * BLK
    er = lax.broadcasted_iota(I32, (N_EXPERTS, N_EXPERTS), 0)
    ec = lax.broadcasted_iota(I32, (N_EXPERTS, N_EXPERTS), 1)
    lower = jnp.where(ec < er, 1.0, 0.0).astype(BF16)
    p_hi, p_mid, p_lo = _split3(padded)
    starts = (jnp.dot(lower, p_hi, preferred_element_type=F32)
              + jnp.dot(lower, p_mid, preferred_element_type=F32)
              + jnp.dot(lower, p_lo, preferred_element_type=F32))
    ends = starts + padded
    blk_start = (lax.broadcasted_iota(I32, (1, NB_PAD), 1) * BLK).astype(F32)
    n_before = jnp.sum(jnp.where(ends[:, 0:1] <= blk_start, 1.0, 0.0), axis=0, keepdims=True)
    be_ref[...] = jnp.minimum(n_before, N_EXPERTS - 1.0).astype(I32)
    nv_ref[...] = (jnp.max(ends, axis=0, keepdims=True) * (1.0 / BLK)).astype(I32)

    eio = lax.broadcasted_iota(I32, (N_EXPERTS, TP), 0)
    rows = []
    for k in range(TOP_K):
        onehot = eio == eidx_ref[k:k + 1, :]
        base = jnp.sum(jnp.where(onehot, starts[:, 0:1], 0.0), axis=0, keepdims=True)
        rows.append(base.astype(I32) + rank_ref[k:k + 1, :])
    dest_ref[...] = jnp.concatenate(rows + [jnp.zeros((8 - TOP_K, TP), I32)], axis=0)


def _route_plan(cnt, eidx, rank):
    return pl.pallas_call(
        _plan_kernel,
        grid=(N_GRP // TP,),
        in_specs=[
            pl.BlockSpec((N_EXPERTS, LANES), lambda i: (0, 0)),
            pl.BlockSpec((8, TP), lambda i: (0, i)),
            pl.BlockSpec((8, TP), lambda i: (0, i)),
        ],
        out_specs=[
            pl.BlockSpec((8, TP), lambda i: (0, i)),
            pl.BlockSpec((1, NB_PAD), lambda i: (0, 0)),
            pl.BlockSpec((1, LANES), lambda i: (0, 0)),
        ],
        out_shape=[
            jax.ShapeDtypeStruct((8, N_GRP), I32),
            jax.ShapeDtypeStruct((1, NB_PAD), I32),
            jax.ShapeDtypeStruct((1, LANES), I32),
        ],
        compiler_params=pltpu.CompilerParams(dimension_semantics=("arbitrary",)),
        name="route_plan",
    )(cnt, eidx, rank)


def _ffn_kernel(be_ref, nv_ref, x_ref, wgu_ref, bgu_ref, wd_ref, bd_ref, y_ref):
    i = pl.program_id(0)
    valid = i < nv_ref[0]

    @pl.when(valid)
    def _():
        lo, hi = _unpack_halves(x_ref[...])
        x = jnp.concatenate([lo, hi], axis=1).astype(BF16)
        mm = lambda a, w: lax.dot_general(a, w, (((1,), (0,)), ((), ())), preferred_element_type=F32)
        gu = mm(x, wgu_ref[...]) + bgu_ref[...]
        gate = jnp.minimum(gu[:, :D_FF], SWIGLU_LIMIT)
        up = jnp.clip(gu[:, D_FF:], -SWIGLU_LIMIT, SWIGLU_LIMIT)
        act = (up + 1.0) * (gate * jax.nn.sigmoid(SWIGLU_ALPHA * gate))
        y = mm(act.astype(BF16), wd_ref[...]) + bd_ref[...]
        y_ref[...] = _pack_halves(y)

    @pl.when(jnp.logical_not(valid))
    def _():
        y_ref[...] = jnp.zeros_like(y_ref)


def _expert_ffn(blk_expert, n_valid, xb, w_gate_up, b_gate_up, w_down, b_down):
    grid_spec = pltpu.PrefetchScalarGridSpec(
        num_scalar_prefetch=2,
        grid=(N_BLK,),
        in_specs=[
            pl.BlockSpec((BLK, ROW_W), lambda i, be, nv: (i, 0)),
            pl.BlockSpec((None, D_MODEL, 2 * D_FF), lambda i, be, nv: (be[i], 0, 0)),
            pl.BlockSpec((None, 1, 2 * D_FF), lambda i, be, nv: (be[i], 0, 0)),
            pl.BlockSpec((None, D_FF, D_MODEL), lambda i, be, nv: (be[i], 0, 0)),
            pl.BlockSpec((None, 1, D_MODEL), lambda i, be, nv: (be[i], 0, 0)),
        ],
        out_specs=pl.BlockSpec((BLK, ROW_W), lambda i, be, nv: (i, 0)),
    )
    return pl.pallas_call(
        _ffn_kernel,
        grid_spec=grid_spec,
        out_shape=jax.ShapeDtypeStruct((P_ROWS, ROW_W), ROW_DT),
        compiler_params=pltpu.CompilerParams(
            dimension_semantics=("arbitrary",), vmem_limit_bytes=VMEM_LIMIT),
        name="expert_ffn",
    )(blk_expert, n_valid, xb, w_gate_up, b_gate_up.reshape(N_EXPERTS, 1, 2 * D_FF),
      w_down, b_down.reshape(N_EXPERTS, 1, D_MODEL))


def _final_kernel(x1_ref, y0_ref, y1_ref, y2_ref, y3_ref, w_ref, gt2_ref, g_ref, *rest):
    o_ref = rest[-1]
    w = w_ref[...]
    ylo = jnp.zeros((TM_FIN, HALF), F32)
    yhi = jnp.zeros((TM_FIN, HALF), F32)
    for k, y_ref in enumerate((y0_ref, y1_ref, y2_ref, y3_ref)):
        lo, hi = _unpack_halves(y_ref[...])
        wk = w[:, k:k + 1]
        ylo = ylo + wk * lo
        yhi = yhi + wk * hi
    y = jnp.concatenate([ylo, yhi], axis=1)
    x2 = x1_ref[...] + gt2_ref[...] * y
    inv = lax.rsqrt(jnp.mean(x2 * x2, axis=-1, keepdims=True) + EPS)
    o_ref[...] = x2 * inv * g_ref[...]


def _final(grp, x1, yg, w4, mod4, final_norm_g, out_so_far):
    per_b = SEQ // TM_FIN
    ntile = N_GRP // TM_FIN
    first = grp * ntile
    slot = lambda k: pl.BlockSpec((TM_FIN, ROW_W), lambda i: (k * ntile + i, 0))
    in_specs = [
        pl.BlockSpec((TM_FIN, D_MODEL), lambda i: (i, 0)),
        slot(0), slot(1), slot(2), slot(3),
        pl.BlockSpec((TM_FIN, TOP_K), lambda i: (i, 0)),
        pl.BlockSpec((None, None, 1, D_MODEL), lambda i: (5, (first + i) // per_b, 0, 0)),
        pl.BlockSpec((1, D_MODEL), lambda i: (0, 0)),
    ]
    args = [x1, yg, yg, yg, yg, w4, mod4, final_norm_g]
    aliases = {}
    if out_so_far is not None:
        in_specs.append(pl.BlockSpec(memory_space=pl.ANY))
        args.append(out_so_far)
        aliases = {len(args) - 1: 0}
    return pl.pallas_call(
        _final_kernel,
        grid=(ntile,),
        in_specs=in_specs,
        out_specs=pl.BlockSpec((TM_FIN, D_MODEL), lambda i: (first + i, 0)),
        out_shape=jax.ShapeDtypeStruct((N_TOK, D_MODEL), F32),
        input_output_aliases=aliases,
        compiler_params=pltpu.CompilerParams(
            dimension_semantics=("arbitrary",), vmem_limit_bytes=VMEM_LIMIT),
        name="combine_final",
    )(*args)


SC_CORES = 2
SC_SUBCORES = 16
SC_WORKERS = SC_CORES * SC_SUBCORES
SC_CHUNK = 64


def _sc_mesh():
    return plsc.VectorSubcoreMesh(core_axis_name="c", subcore_axis_name="s")


def _row_buffers():
    return ([pltpu.VMEM((SC_CHUNK, HALF), U32)] * 2 + [pltpu.SemaphoreType.DMA] * 5)


def _dispatch_rows(h2p, dest2d):
    per_w = N_GRP // SC_WORKERS
    nchunk = per_w // SC_CHUNK
    rows_per_k = N_GRP // SC_CHUNK
    assert nchunk % 2 == 0

    @functools.partial(
        pl.kernel, mesh=_sc_mesh(), out_type=jax.ShapeDtypeStruct((P_ROWS, HALF), U32),
        scratch_types=[pltpu.VMEM((TOP_K * nchunk, SC_CHUNK), I32)] + _row_buffers(),
        name="moe_dispatch")
    def k(src_hbm, dest_hbm, out_hbm, idx_v, buf0, buf1, isem, l0, l1, s0, s1):
        wid = lax.axis_index("s") * SC_CORES + lax.axis_index("c")
        bufs, lsem, ssem = (buf0, buf1), (l0, l1), (s0, s1)
        idx_loads = [
            pltpu.make_async_copy(dest_hbm.at[pl.ds(kk * rows_per_k + wid * nchunk, nchunk)],
                                  idx_v.at[pl.ds(kk * nchunk, nchunk)], isem) for kk in range(TOP_K)]

        def load(j, b):
            return pltpu.make_async_copy(src_hbm.at[pl.ds(wid * per_w + j * SC_CHUNK, SC_CHUNK)], bufs[b], lsem[b])

        def scatters(j, b):
            return [pltpu.make_async_copy(bufs[b], out_hbm.at[idx_v.at[kk * nchunk + j]], ssem[b])
                    for kk in range(TOP_K)]

        for cp in idx_loads:
            cp.start()
        load(0, 0).start()
        for cp in idx_loads:
            cp.wait()

        @pl.loop(0, nchunk // 2)
        def _(i):
            for b in range(2):
                j = 2 * i + b

                @pl.when(j >= 1)
                def _():
                    for cp in scatters(j - 1, 1 - b):
                        cp.wait()

                @pl.when(j + 1 < nchunk)
                def _():
                    load(j + 1, 1 - b).start()

                load(j, b).wait()
                for cp in scatters(j, b):
                    cp.start()

        for cp in scatters(nchunk - 1, 1):
            cp.wait()

    return k(h2p, dest2d)


def _combine_rows(yb, dest2d):
    n_out = TOP_K * N_GRP
    per_w = n_out // SC_WORKERS
    nchunk = per_w // SC_CHUNK
    assert nchunk % 2 == 0

    @functools.partial(
        pl.kernel, mesh=_sc_mesh(), out_type=jax.ShapeDtypeStruct((n_out, HALF), U32),
        scratch_types=[pltpu.VMEM((nchunk, SC_CHUNK), I32)] + _row_buffers(),
        name="moe_combine")
    def k(tab_hbm, idx_hbm, out_hbm, idx_v, buf0, buf1, isem, g0, g1, s0, s1):
        wid = lax.axis_index("s") * SC_CORES + lax.axis_index("c")
        bufs, gsem, ssem = (buf0, buf1), (g0, g1), (s0, s1)
        idx_load = pltpu.make_async_copy(idx_hbm.at[pl.ds(wid * nchunk, nchunk)], idx_v, isem)

        def gather(j, b):
            return pltpu.make_async_copy(tab_hbm.at[idx_v.at[j]], bufs[b], gsem[b])

        def store(j, b):
            return pltpu.make_async_copy(bufs[b], out_hbm.at[pl.ds(wid * per_w + j * SC_CHUNK, SC_CHUNK)], ssem[b])

        idx_load.start()
        idx_load.wait()
        gather(0, 0).start()

        @pl.loop(0, nchunk // 2)
        def _(i):
            for b in range(2):
                j = 2 * i + b

                @pl.when(j >= 1)
                def _():
                    store(j - 1, 1 - b).wait()

                @pl.when(j + 1 < nchunk)
                def _():
                    gather(j + 1, 1 - b).start()

                gather(j, b).wait()
                store(j, b).start()

        store(nchunk - 1, 1).wait()

    return k(yb, dest2d)


def _lambda_kernel(p_ref, o_ref):
    p = p_ref[...]
    s1 = jnp.sum(p[0:1] * p[1:2], axis=-1, keepdims=True)
    s2 = jnp.sum(p[2:3] * p[3:4], axis=-1, keepdims=True)
    o_ref[...] = jnp.broadcast_to(jnp.exp(s1) - jnp.exp(s2) + LAMBDA_INIT, (1, LANES))


def kernel(x, c, w_ada, b_ada, norm1_g, w_in, lambda_q1, lambda_k1, lambda_q2, lambda_k2, diff_norm_g, w_alpha_up, b_alpha, gla_norm_g, w_branch_diff, w_branch_gla, w_out, norm2_g, w_router, b_router, w_gate_up, b_gate_up, w_down, b_down, final_norm_g):
    w_in0 = w_in[0]
    c_a, c_g = W_A, W_A + W_G
    w_a = w_in0[:, :c_a].astype(BF16)
    w_g = w_in0[:, c_a:c_g].astype(BF16)
    w_lr = jnp.pad(w_in0[:, c_g:c_g + GLA_RANK], ((0, 0), (0, LANES - GLA_RANK))).astype(BF16)
    w_gate = w_in0[:, c_g + GLA_RANK:].astype(BF16)
    w_up = jnp.pad(w_alpha_up[0], ((0, LANES - GLA_RANK), (0, 0)))
    lam_in = jnp.concatenate([lambda_q1, lambda_k1, lambda_q2, lambda_k2], axis=0)
    slopes = jnp.asarray(2.0 ** (-8.0 * np.arange(1, DIFF_HEADS + 1) / DIFF_HEADS), dtype=F32)

    mod = _modulation(c, w_ada[0], b_ada[0])
    mod4 = mod.reshape(N_MOD, BATCH, 1, D_MODEL)
    lam = pl.pallas_call(
        _lambda_kernel, out_shape=jax.ShapeDtypeStruct((1, LANES), F32), name="lambda")(lam_in)[0, :1]

    qkv_a, qkv_g, gates, glr = _in_proj(x, mod4, norm1_g, w_a, w_g, w_gate, w_lr)
    pos = jnp.arange(SEQ, dtype=I32)
    p_hi, p_lo = (pos >> 3).astype(F32), (pos & 7).astype(F32)
    kaug = jnp.zeros((SEQ, LANES), F32).at[:, 0].set(p_hi).at[:, 1].set(p_hi).at[:, 2].set(p_lo).at[:, 3].set(
        p_lo).astype(BF16)
    o_a = _diff_attention(qkv_a, slopes, lam, kaug, diff_norm_g)
    o_b = _gla(qkv_g, glr, w_up, b_alpha, gla_norm_g)

    merge_args = (o_a.reshape(N_TOK, -1), o_b.reshape(N_TOK, -1), gates.reshape(N_TOK, W_GATE),
                  x.reshape(N_TOK, D_MODEL), mod4, norm2_g,
                  w_branch_diff[0].astype(BF16), w_branch_gla[0].astype(BF16), w_out[0].astype(BF16),
                  w_router[0].T, b_router[0].reshape(N_EXPERTS, 1))
    fin_g = final_norm_g.reshape(1, D_MODEL)

    routed = []
    for grp in range(MOE_GROUPS):
        x1, h2p, eidx, rank, wts, cnt = _merge_route(grp, *merge_args)
        dest8, be, nv = _route_plan(cnt, eidx, rank)
        dest = dest8[:TOP_K].reshape(-1, SC_CHUNK)
        routed.append((x1, wts, dest, be, nv, _dispatch_rows(h2p, dest)))
    gathered = []
    for x1, wts, dest, be, nv, xb in routed:
        yb = _expert_ffn(be[0, :N_BLK], nv[0, :1], xb, w_gate_up[0], b_gate_up[0], w_down[0], b_down[0])
        gathered.append(_combine_rows(yb, dest))
    out = None
    for grp, ((x1, wts, *_), yg) in enumerate(zip(routed, gathered)):
        out = _final(grp, x1, yg, wts[:TOP_K].T, mod4, fin_g, out)
    return out.reshape(BATCH, SEQ, D_MODEL)
```

```python
import functools
import math

import jax
import jax.numpy as jnp
import numpy as np
from jax import lax
from jax.experimental import pallas as pl
from jax.experimental.pallas import tpu as pltpu
from jax.experimental.pallas import tpu_sc as plsc

F32 = jnp.float32
BF16 = jnp.bfloat16
U32 = jnp.uint32
I32 = jnp.int32

D_MODEL = 1024
BATCH = 16
SEQ = 2048
N_TOK = BATCH * SEQ
CHUNK = 64
DIFF_HEADS = 4
DIFF_DH = 64
GLA_HEADS = 4
GLA_DK = 64
GLA_DV = 128
GLA_RANK = 16
GLA_GATE_NORM = 16.0
N_EXPERTS = 32
TOP_K = 4
D_FF = D_MODEL
SWIGLU_LIMIT = 7.0
SWIGLU_ALPHA = 1.702
N_MOD = 6
EPS = 1e-6
LAMBDA_INIT = 0.8 - 0.6 * math.exp(-0.3 * 0)

LANES = 128
HALF = D_MODEL // 2
ROW_W = HALF
ROW_DT = U32

TM_IN = 512
TQ = 512
VT_ROWS = 2 * DIFF_DH + 16
LOG2E = math.log2(math.e)
TM_MERGE = 512
MERGE_PARTS = 1
BLK = 512
MOE_GROUPS = 1
N_GRP = N_TOK // MOE_GROUPS
N_BLK = (N_GRP * TOP_K) // BLK + N_EXPERTS
P_ROWS = N_BLK * BLK
TM_FIN = 512
VMEM_LIMIT = 56 * 1024 * 1024


def _nt_dot(a, b):
    return lax.dot_general(a, b, (((1,), (1,)), ((), ())), preferred_element_type=F32)


def _tn_dot(a, b):
    return lax.dot_general(a, b, (((0,), (0,)), ((), ())), preferred_element_type=F32)


def _split3(x):
    hi = x.astype(BF16)
    r1 = x - hi.astype(F32)
    mid = r1.astype(BF16)
    lo = (r1 - mid.astype(F32)).astype(BF16)
    return hi, mid, lo


def _pack_halves(y):
    return pltpu.pack_elementwise([y[:, :HALF], y[:, HALF:]], packed_dtype=BF16)


def _unpack_halves(u):
    lo = pltpu.unpack_elementwise(u, index=0, packed_dtype=BF16, unpacked_dtype=F32)
    hi = pltpu.unpack_elementwise(u, index=1, packed_dtype=BF16, unpacked_dtype=F32)
    return lo, hi


def _mod_kernel(c_ref, w_ref, b_ref, o_ref):
    c = c_ref[...]
    s = c * jax.nn.sigmoid(c)
    o_ref[0] = jnp.dot(s.astype(BF16), w_ref[...].astype(BF16),
                       preferred_element_type=F32) + b_ref[...]


def _modulation(c, w_ada, b_ada):
    return pl.pallas_call(
        _mod_kernel,
        grid=(N_MOD,),
        in_specs=[
            pl.BlockSpec((BATCH, D_MODEL), lambda j: (0, 0)),
            pl.BlockSpec((D_MODEL, D_MODEL), lambda j: (0, j)),
            pl.BlockSpec((1, D_MODEL), lambda j: (0, j)),
        ],
        out_specs=pl.BlockSpec((1, BATCH, D_MODEL), lambda j: (j, 0, 0)),
        out_shape=jax.ShapeDtypeStruct((N_MOD, BATCH, D_MODEL), F32),
        compiler_params=pltpu.CompilerParams(dimension_semantics=("arbitrary",)),
        name="adaln_mod",
    )(c, w_ada, b_ada.reshape(1, N_MOD * D_MODEL))


W_A = 3 * DIFF_HEADS * 2 * DIFF_DH
W_G = 2 * GLA_HEADS * GLA_DK + 2 * GLA_HEADS * GLA_DV
W_GATE = 2 * D_MODEL


def _in_kernel(x_ref, sh_ref, sc_ref, g_ref, wa_ref, wg_ref, wgate_ref, wlr_ref,
               oa_ref, og_ref, ogate_ref, olr_ref):
    x = x_ref[...]
    inv = lax.rsqrt(jnp.mean(x * x, axis=-1, keepdims=True) + EPS)
    h = (x * inv * g_ref[...]) * (1.0 + sc_ref[...]) + sh_ref[...]
    hb = h.astype(BF16)
    oa_ref[...] = jnp.dot(hb, wa_ref[...], preferred_element_type=F32).astype(BF16)
    og_ref[...] = jnp.dot(hb, wg_ref[...], preferred_element_type=F32).astype(BF16)
    ogate_ref[...] = jnp.dot(hb, wgate_ref[...], preferred_element_type=F32).astype(BF16)
    olr_ref[...] = jnp.dot(hb, wlr_ref[...], preferred_element_type=F32)


def _in_proj(x, mod4, norm1_g, w_a, w_g, w_gate, w_lr):
    nrow = SEQ // TM_IN
    full = lambda shape: pl.BlockSpec(shape, lambda b, i: (0,) * len(shape))
    return pl.pallas_call(
        _in_kernel,
        grid=(BATCH, nrow),
        in_specs=[
            pl.BlockSpec((None, TM_IN, D_MODEL), lambda b, i: (b, i, 0)),
            pl.BlockSpec((None, None, 1, D_MODEL), lambda b, i: (0, b, 0, 0)),
            pl.BlockSpec((None, None, 1, D_MODEL), lambda b, i: (1, b, 0, 0)),
            full((1, D_MODEL)),
            full((D_MODEL, W_A)),
            full((D_MODEL, W_G)),
            full((D_MODEL, W_GATE)),
            full((D_MODEL, LANES)),
        ],
        out_specs=[
            pl.BlockSpec((None, TM_IN, W_A), lambda b, i: (b, i, 0)),
            pl.BlockSpec((None, TM_IN, W_G), lambda b, i: (b, i, 0)),
            pl.BlockSpec((None, TM_IN, W_GATE), lambda b, i: (b, i, 0)),
            pl.BlockSpec((None, TM_IN, LANES), lambda b, i: (b, i, 0)),
        ],
        out_shape=[
            jax.ShapeDtypeStruct((BATCH, SEQ, W_A), BF16),
            jax.ShapeDtypeStruct((BATCH, SEQ, W_G), BF16),
            jax.ShapeDtypeStruct((BATCH, SEQ, W_GATE), BF16),
            jax.ShapeDtypeStruct((BATCH, SEQ, LANES), F32),
        ],
        compiler_params=pltpu.CompilerParams(
            dimension_semantics=("arbitrary", "arbitrary"), vmem_limit_bytes=VMEM_LIMIT),
        name="in_proj",
    )(x, mod4, mod4, norm1_g, w_a, w_g, w_gate, w_lr)


def _attn_kernel(slope_ref, lam_ref, q_ref, k_ref, v_ref, kaug_ref, g_ref, o_ref,
                 sa_ref, sb_ref, corr_ref, vt_ref, m_ref, acc_ref):
    c_alibi = slope_ref[pl.program_id(1)] * LOG2E
    lam = lam_ref[0]
    lane = lax.broadcasted_iota(I32, (TQ, 2 * DIFF_DH), 1)
    c_vec = jnp.full((TQ, 2 * DIFF_DH), c_alibi, F32)
    c_hi = c_vec.astype(BF16).astype(F32)
    c_lo = (c_vec - c_hi).astype(BF16).astype(F32)
    zero = jnp.zeros((TQ, 2 * DIFF_DH), BF16)
    aug = jnp.where(lane == 0, 8.0 * c_hi, jnp.where(lane == 1, 8.0 * c_lo,
                    jnp.where(lane == 2, c_hi, jnp.where(lane == 3, c_lo, 0.0)))).astype(BF16)

    def blk(j):
        return slice(j * TQ, (j + 1) * TQ)

    def stacked_queries(qi):
        q = (q_ref[blk(qi), :].astype(F32) * (DIFF_DH ** -0.5 * LOG2E)).astype(BF16)
        return jnp.concatenate([
            jnp.concatenate([jnp.where(lane < DIFF_DH, q, zero), aug], axis=1),
            jnp.concatenate([jnp.where(lane >= DIFF_DH, q, zero), aug], axis=1)], axis=0)

    def scores(lhs, j):
        kk = jnp.concatenate([k_ref[blk(j), :], kaug_ref[blk(j), :]], axis=1)
        return _nt_dot(kk, lhs)

    ones_rows = jnp.where(lax.broadcasted_iota(I32, (VT_ROWS - 2 * DIFF_DH, TQ), 0) == 0, 1.0, 0.0).astype(BF16)
    for j in range(SEQ // TQ):
        vt_ref[j, :2 * DIFF_DH, :] = v_ref[blk(j), :].astype(F32).T.astype(BF16)
        vt_ref[j, 2 * DIFF_DH:, :] = ones_rows

    def update(s, j, first):
        if first:
            m_new = jnp.max(s, axis=0, keepdims=True)
            p = jnp.exp2((s - m_new).astype(BF16))
            acc_ref[...] = jnp.dot(vt_ref[j], p, preferred_element_type=F32)
        else:
            m = m_ref[...]
            m_new = jnp.maximum(m, jnp.max(s, axis=0, keepdims=True))
            alpha = jnp.exp2(m - m_new)
            p = jnp.exp2((s - m_new).astype(BF16))
            acc_ref[...] = alpha * acc_ref[...] + jnp.dot(vt_ref[j], p, preferred_element_type=F32)
        m_ref[...] = m_new

    kr = lax.broadcasted_iota(I32, (TQ, TQ), 0)
    qc = lax.broadcasted_iota(I32, (TQ, TQ), 1)
    ahead = jnp.maximum(kr - qc, 0).astype(F32)
    corr_ref[...] = jnp.where((qc >> 6) >= (kr >> 6), (-2.0 * c_alibi) * ahead, -jnp.inf)

    pairs = [(qi, j) for qi in range(SEQ // TQ) for j in range(qi + 1)]
    bufs = (sa_ref, sb_ref)
    lhs = stacked_queries(0)
    bufs[0][...] = scores(lhs, 0)
    for t, (qi, j) in enumerate(pairs):
        if t + 1 < len(pairs):
            nqi, nj = pairs[t + 1]
            if nqi != qi:
                lhs = stacked_queries(nqi)
            bufs[(t + 1) % 2][...] = scores(lhs, nj)
        s = bufs[t % 2][...]
        if j == qi:
            corr = corr_ref[...]
            s = s + jnp.concatenate([corr, corr], axis=1)
        update(s, j, first=(j == 0))
        if j == qi:
            ot = acc_ref[:2 * DIFF_DH, :] / acc_ref[2 * DIFF_DH:2 * DIFF_DH + 1, :]
            o = (ot[:, :TQ] - lam * ot[:, TQ:]).T
            inv = lax.rsqrt(jnp.mean(o * o, axis=-1, keepdims=True) + EPS)
            o_ref[blk(qi), :] = (o * inv * g_ref[...] * (1.0 - LAMBDA_INIT)).astype(BF16)


def _diff_attention(qkv_a, slopes, lam, kaug, diff_norm_g):
    return pl.pallas_call(
        _attn_kernel,
        grid=(BATCH, DIFF_HEADS),
        in_specs=[
            pl.BlockSpec(memory_space=pltpu.SMEM),
            pl.BlockSpec(memory_space=pltpu.SMEM),
            pl.BlockSpec((None, SEQ, LANES), lambda b, h: (b, 0, h)),
            pl.BlockSpec((None, SEQ, LANES), lambda b, h: (b, 0, DIFF_HEADS + h)),
            pl.BlockSpec((None, SEQ, LANES), lambda b, h: (b, 0, 2 * DIFF_HEADS + h)),
            pl.BlockSpec((SEQ, LANES), lambda b, h: (0, 0)),
            pl.BlockSpec((1, LANES), lambda b, h: (0, 0)),
        ],
        out_specs=pl.BlockSpec((None, SEQ, LANES), lambda b, h: (b, 0, h)),
        out_shape=jax.ShapeDtypeStruct((BATCH, SEQ, DIFF_HEADS * 2 * DIFF_DH), BF16),
        scratch_shapes=[
            pltpu.VMEM((TQ, 2 * TQ), F32), pltpu.VMEM((TQ, 2 * TQ), F32),
            pltpu.VMEM((TQ, TQ), F32),
            pltpu.VMEM((SEQ // TQ, VT_ROWS, TQ), BF16),
            pltpu.VMEM((1, 2 * TQ), F32),
            pltpu.VMEM((VT_ROWS, 2 * TQ), F32),
        ],
        compiler_params=pltpu.CompilerParams(
            dimension_semantics=("arbitrary", "arbitrary")),
        name="diff_attn",
    )(slopes, lam, qkv_a, qkv_a, qkv_a, kaug, diff_norm_g)


N_CHUNK = SEQ // CHUNK
PAIR = 2 * GLA_DK
PAIR_V = 2 * GLA_DV
CS_ROWS = 256
GLA_UNROLL = 8


def _gla_kernel(q_ref, k_ref, v_ref, r_ref, lr_ref, wup_ref, bup_ref, g_ref, o_ref,
                gcum_ref, state_ref):
    w_hi, w_mid, _ = _split3(wup_ref[...])
    rr = lax.broadcasted_iota(I32, (CS_ROWS, CS_ROWS), 0)
    cc = lax.broadcasted_iota(I32, (CS_ROWS, CS_ROWS), 1)
    tri = jnp.where(((rr >> 6) == (cc >> 6)) & (cc <= rr), 1.0, 0.0).astype(BF16)
    for blk in range(SEQ // CS_ROWS):
        rows = pl.ds(blk * CS_ROWS, CS_ROWS)
        a_hi, a_mid, _ = _split3(lr_ref[rows, :])
        z = (jnp.dot(a_hi, w_hi, preferred_element_type=F32)
             + jnp.dot(a_hi, w_mid, preferred_element_type=F32)
             + jnp.dot(a_mid, w_hi, preferred_element_type=F32)) + bup_ref[...]
        la = (jnp.minimum(z, 0.0) - jnp.log(1.0 + jnp.exp(-jnp.abs(z)))) * (1.0 / GLA_GATE_NORM)
        l_hi, l_mid, l_lo = _split3(la)
        gcum_ref[rows, :] = (jnp.dot(tri, l_hi, preferred_element_type=F32)
                             + jnp.dot(tri, l_mid, preferred_element_type=F32)
                             + jnp.dot(tri, l_lo, preferred_element_type=F32))

    state_ref[...] = jnp.zeros_like(state_ref)
    lane_k = lax.broadcasted_iota(I32, (1, PAIR), 1)
    row_v = lax.broadcasted_iota(I32, (PAIR_V, PAIR), 0)
    col_k = lax.broadcasted_iota(I32, (PAIR_V, PAIR), 1)
    same_head = (row_v >= GLA_DV) == (col_k >= GLA_DK)
    cr = lax.broadcasted_iota(I32, (CHUNK, CHUNK), 0)
    cs = lax.broadcasted_iota(I32, (CHUNK, CHUNK), 1)
    causal = cs <= cr
    scale = GLA_DK ** -0.5

    def chunk(n):
        rows = pl.ds(pl.multiple_of(n * CHUNK, CHUNK), CHUNK)
        gc = gcum_ref[rows, :]
        g_last = gcum_ref[pl.ds(n * CHUNK + CHUNK - 1, 1), :]
        qf = q_ref[rows, :].astype(F32) * scale
        kf = k_ref[rows, :].astype(F32)
        q_s = (qf * jnp.exp(gc)).astype(BF16)
        k_s = (kf * jnp.exp(-gc)).astype(BF16)
        k_d = (kf * jnp.exp(g_last - gc)).astype(BF16)
        decay = jnp.exp(g_last)
        for pr in range(GLA_HEADS // 2):
            kl = slice(pr * PAIR, (pr + 1) * PAIR)
            vl = slice(pr * PAIR_V, (pr + 1) * PAIR_V)
            qs_p, ks_p, kd_p = q_s[:, kl], k_s[:, kl], k_d[:, kl]
            v_p = v_ref[rows, vl]
            st = state_ref[pr]
            o_inter = _nt_dot(qs_p, st.astype(BF16))
            d_st = _tn_dot(v_p, kd_p)
            state_ref[pr] = st * decay[:, kl] + jnp.where(same_head, d_st, 0.0)
            for sub in range(2):
                hd = 2 * pr + sub
                in_head = (lane_k >= sub * GLA_DK) & (lane_k < (sub + 1) * GLA_DK)
                a = _nt_dot(jnp.where(in_head, qs_p, jnp.zeros_like(qs_p)), ks_p)
                a = jnp.where(causal, a, 0.0).astype(BF16)
                vs = slice(hd * GLA_DV, (hd + 1) * GLA_DV)
                o = (jnp.dot(a, v_ref[rows, vs], preferred_element_type=F32)
                     + o_inter[:, sub * GLA_DV:(sub + 1) * GLA_DV])
                inv = lax.rsqrt(jnp.mean(o * o, axis=-1, keepdims=True) + EPS)
                r = r_ref[rows, vs].astype(F32)
                o_ref[rows, vs] = (o * inv * g_ref[...] * (r * jax.nn.sigmoid(r))).astype(BF16)

    def chunk_group(t, _):
        for u in range(GLA_UNROLL):
            chunk(t * GLA_UNROLL + u)
        return 0

    lax.fori_loop(0, N_CHUNK // GLA_UNROLL, chunk_group, 0)


def _gla(qkv_g, glr, w_up, b_up, gla_norm_g):
    qk_w = GLA_HEADS * GLA_DK
    v_w = GLA_HEADS * GLA_DV
    return pl.pallas_call(
        _gla_kernel,
        grid=(BATCH,),
        in_specs=[
            pl.BlockSpec((None, SEQ, qk_w), lambda b: (b, 0, 0)),
            pl.BlockSpec((None, SEQ, qk_w), lambda b: (b, 0, 1)),
            pl.BlockSpec((None, SEQ, v_w), lambda b: (b, 0, 1)),
            pl.BlockSpec((None, SEQ, v_w), lambda b: (b, 0, 2)),
            pl.BlockSpec((None, SEQ, LANES), lambda b: (b, 0, 0)),
            pl.BlockSpec((LANES, qk_w), lambda b: (0, 0)),
            pl.BlockSpec((1, qk_w), lambda b: (0, 0)),
            pl.BlockSpec((1, GLA_DV), lambda b: (0, 0)),
        ],
        out_specs=pl.BlockSpec((None, SEQ, v_w), lambda b: (b, 0, 0)),
        out_shape=jax.ShapeDtypeStruct((BATCH, SEQ, v_w), BF16),
        scratch_shapes=[
            pltpu.VMEM((SEQ, qk_w), F32),
            pltpu.VMEM((GLA_HEADS // 2, PAIR_V, PAIR), F32),
        ],
        compiler_params=pltpu.CompilerParams(
            dimension_semantics=("arbitrary",), vmem_limit_bytes=VMEM_LIMIT),
        name="gla",
    )(qkv_g, qkv_g, qkv_g, qkv_g, glr, w_up, b_up, gla_norm_g)


def _merge_kernel(oa_ref, ob_ref, gate_ref, x_ref, gt1_ref, sh2_ref, sc2_ref, g2_ref,
                  wpa_ref, wpb_ref, wo_ref, wr_ref, br_ref,
                  x1_ref, h2p_ref, eidx_ref, rank_ref, wts_ref, cnt_ref,
                  upper_ref, carry_ref):
    i = pl.program_id(0)

    @pl.when(i == 0)
    def _():
        rr = lax.broadcasted_iota(I32, (TM_MERGE, TM_MERGE), 0)
        cc = lax.broadcasted_iota(I32, (TM_MERGE, TM_MERGE), 1)
        upper_ref[...] = jnp.where(rr < cc, 1.0, 0.0).astype(BF16)
        carry_ref[...] = jnp.zeros_like(carry_ref)

    w_hi, w_mid, _ = _split3(wr_ref[...])

    def token_rows(rows):
        ga = gate_ref[rows, :D_MODEL].astype(F32)
        gb = gate_ref[rows, D_MODEL:].astype(F32)
        merged = (jax.nn.sigmoid(ga) * jnp.dot(oa_ref[rows, :], wpa_ref[...], preferred_element_type=F32)
                  + jax.nn.sigmoid(gb) * jnp.dot(ob_ref[rows, :], wpb_ref[...], preferred_element_type=F32))
        y = jnp.dot(merged.astype(BF16), wo_ref[...], preferred_element_type=F32)
        x1 = x_ref[rows, :] + gt1_ref[...] * y
        x1_ref[rows, :] = x1
        inv = lax.rsqrt(jnp.mean(x1 * x1, axis=-1, keepdims=True) + EPS)
        h2 = (x1 * inv * g2_ref[...]) * (1.0 + sc2_ref[...]) + sh2_ref[...]
        h2p_ref[rows, :] = _pack_halves(h2)
        h_hi, h_mid, _ = _split3(h2)
        return _nt_dot(w_hi, h_hi) + _nt_dot(w_hi, h_mid) + _nt_dot(w_mid, h_hi)

    part = TM_MERGE // MERGE_PARTS
    logits = jnp.concatenate(
        [token_rows(slice(r * part, (r + 1) * part)) for r in range(MERGE_PARTS)], axis=1) + br_ref[...]

    eio = lax.broadcasted_iota(I32, (N_EXPERTS, TM_MERGE), 0)
    vals, idxs, sels = [], [], []
    cur = logits
    for _k in range(TOP_K):
        m = jnp.max(cur, axis=0, keepdims=True)
        idx = jnp.min(jnp.where(cur == m, eio, N_EXPERTS), axis=0, keepdims=True)
        sel = eio == idx
        vals.append(m)
        idxs.append(idx)
        sels.append(sel)
        cur = jnp.where(sel, -jnp.inf, cur)
    es = [jnp.exp(v - vals[0]) for v in vals]
    tot = es[0] + es[1] + es[2] + es[3]
    onehot = jnp.zeros((N_EXPERTS, TM_MERGE), F32)
    for sel in sels:
        onehot = onehot + jnp.where(sel, 1.0, 0.0)
    before = jnp.dot(onehot.astype(BF16), upper_ref[...], preferred_element_type=F32) + carry_ref[:, 0:1]
    ranks = [jnp.sum(jnp.where(sel, before, 0.0), axis=0, keepdims=True) for sel in sels]
    carry_ref[...] = carry_ref[...] + jnp.sum(onehot, axis=1, keepdims=True)
    cnt_ref[...] = carry_ref[...]

    zi = jnp.zeros((8 - TOP_K, TM_MERGE), I32)
    zf = jnp.zeros((8 - TOP_K, TM_MERGE), F32)
    eidx_ref[...] = jnp.concatenate(idxs + [zi], axis=0)
    rank_ref[...] = jnp.concatenate([r.astype(I32) for r in ranks] + [zi], axis=0)
    wts_ref[...] = jnp.concatenate([e / tot for e in es] + [zf], axis=0)


def _merge_route(grp, o_a, o_b, gates, x2d, mod4, norm2_g, w_pa, w_pb, w_o, w_rt, b_r):
    ntile = N_GRP // TM_MERGE
    first = grp * ntile
    per_b = SEQ // TM_MERGE
    full = lambda shape: pl.BlockSpec(shape, lambda i: (0,) * len(shape))
    row_in = lambda w: pl.BlockSpec((TM_MERGE, w), lambda i: (first + i, 0))
    row = lambda w: pl.BlockSpec((TM_MERGE, w), lambda i: (i, 0))
    modspec = lambda j: pl.BlockSpec((None, None, 1, D_MODEL), lambda i: (j, (first + i) // per_b, 0, 0))
    col = pl.BlockSpec((8, TM_MERGE), lambda i: (0, i))
    return pl.pallas_call(
        _merge_kernel,
        grid=(ntile,),
        in_specs=[
            row_in(DIFF_HEADS * 2 * DIFF_DH), row_in(GLA_HEADS * GLA_DV), row_in(W_GATE), row_in(D_MODEL),
            modspec(2), modspec(3), modspec(4),
            full((1, D_MODEL)),
            full((DIFF_HEADS * 2 * DIFF_DH, D_MODEL)), full((GLA_HEADS * GLA_DV, D_MODEL)),
            full((D_MODEL, D_MODEL)),
            full((N_EXPERTS, D_MODEL)), full((N_EXPERTS, 1)),
        ],
        out_specs=[
            row(D_MODEL), row(ROW_W), col, col, col,
            pl.BlockSpec((N_EXPERTS, LANES), lambda i: (0, 0)),
        ],
        out_shape=[
            jax.ShapeDtypeStruct((N_GRP, D_MODEL), F32),
            jax.ShapeDtypeStruct((N_GRP, ROW_W), ROW_DT),
            jax.ShapeDtypeStruct((8, N_GRP), I32),
            jax.ShapeDtypeStruct((8, N_GRP), I32),
            jax.ShapeDtypeStruct((8, N_GRP), F32),
            jax.ShapeDtypeStruct((N_EXPERTS, LANES), F32),
        ],
        scratch_shapes=[
            pltpu.VMEM((TM_MERGE, TM_MERGE), BF16),
            pltpu.VMEM((N_EXPERTS, LANES), F32),
        ],
        compiler_params=pltpu.CompilerParams(
            dimension_semantics=("arbitrary",), vmem_limit_bytes=VMEM_LIMIT),
        name="merge_route",
    )(o_a, o_b, gates, x2d, mod4, mod4, mod4, norm2_g, w_pa, w_pb, w_o, w_rt, b_r)


TP = 4096
NB_PAD = ((N_BLK + LANES - 1) // LANES) * LANES


def _plan_kernel(cnt_ref, eidx_ref, rank_ref, dest_ref, be_ref, nv_ref):
    cnt = cnt_ref[...]
    padded = jnp.floor((cnt + (BLK - 1.0)) * (1.0 / BLK)) * BLK
    er = lax.broadcasted_iota(I32, (N_EXPERTS, N_EXPERTS), 0)
    ec = lax.broadcasted_iota(I32, (N_EXPERTS, N_EXPERTS), 1)
    lower = jnp.where(ec < er, 1.0, 0.0).astype(BF16)
    p_hi, p_mid, p_lo = _split3(padded)
    starts = (jnp.dot(lower, p_hi, preferred_element_type=F32)
              + jnp.dot(lower, p_mid, preferred_element_type=F32)
              + jnp.dot(lower, p_lo, preferred_element_type=F32))
    ends = starts + padded
    blk_start = (lax.broadcasted_iota(I32, (1, NB_PAD), 1) * BLK).astype(F32)
    n_before = jnp.sum(jnp.where(ends[:, 0:1] <= blk_start, 1.0, 0.0), axis=0, keepdims=True)
    be_ref[...] = jnp.minimum(n_before, N_EXPERTS - 1.0).astype(I32)
    nv_ref[...] = (jnp.max(ends, axis=0, keepdims=True) * (1.0 / BLK)).astype(I32)

    eio = lax.broadcasted_iota(I32, (N_EXPERTS, TP), 0)
    rows = []
    for k in range(TOP_K):
        onehot = eio == eidx_ref[k:k + 1, :]
        base = jnp.sum(jnp.where(onehot, starts[:, 0:1], 0.0), axis=0, keepdims=True)
        rows.append(base.astype(I32) + rank_ref[k:k + 1, :])
    dest_ref[...] = jnp.concatenate(rows + [jnp.zeros((8 - TOP_K, TP), I32)], axis=0)


def _route_plan(cnt, eidx, rank):
    return pl.pallas_call(
        _plan_kernel,
        grid=(N_GRP // TP,),
        in_specs=[
            pl.BlockSpec((N_EXPERTS, LANES), lambda i: (0, 0)),
            pl.BlockSpec((8, TP), lambda i: (0, i)),
            pl.BlockSpec((8, TP), lambda i: (0, i)),
        ],
        out_specs=[
            pl.BlockSpec((8, TP), lambda i: (0, i)),
            pl.BlockSpec((1, NB_PAD), lambda i: (0, 0)),
            pl.BlockSpec((1, LANES), lambda i: (0, 0)),
        ],
        out_shape=[
            jax.ShapeDtypeStruct((8, N_GRP), I32),
            jax.ShapeDtypeStruct((1, NB_PAD), I32),
            jax.ShapeDtypeStruct((1, LANES), I32),
        ],
        compiler_params=pltpu.CompilerParams(dimension_semantics=("arbitrary",)),
        name="route_plan",
    )(cnt, eidx, rank)


def _ffn_kernel(be_ref, nv_ref, first_ref, slot_ref, nxt_ref, x_ref, wgu_hbm, bgu_ref, wd_hbm, bd_ref, y_ref,
                wgu_buf, wd_buf, sem):
    i = pl.program_id(0)
    valid = i < nv_ref[0]
    s = slot_ref[i]

    def weight_copies(e, slot):
        return (pltpu.make_async_copy(wgu_hbm.at[e], wgu_buf.at[slot], sem.at[slot, 0]),
                pltpu.make_async_copy(wd_hbm.at[e], wd_buf.at[slot], sem.at[slot, 1]))

    @pl.when(i == 0)
    def _():
        for cp in weight_copies(be_ref[0], 0):
            cp.start()

    @pl.when(valid & (first_ref[i] == 1))
    def _():
        for cp in weight_copies(be_ref[i], s):
            cp.wait()

        @pl.when(nxt_ref[i] >= 0)
        def _():
            for cp in weight_copies(nxt_ref[i], 1 - s):
                cp.start()

    @pl.when(valid)
    def _():
        lo, hi = _unpack_halves(x_ref[...])
        x = jnp.concatenate([lo, hi], axis=1).astype(BF16)
        mm = lambda a, w: lax.dot_general(a, w, (((1,), (0,)), ((), ())), preferred_element_type=F32)
        gu = mm(x, wgu_buf[s]) + bgu_ref[...]
        gate = jnp.minimum(gu[:, :D_FF], SWIGLU_LIMIT)
        up = jnp.clip(gu[:, D_FF:], -SWIGLU_LIMIT, SWIGLU_LIMIT)
        act = (up + 1.0) * (gate * jax.nn.sigmoid(SWIGLU_ALPHA * gate))
        y = mm(act.astype(BF16), wd_buf[s]) + bd_ref[...]
        y_ref[...] = _pack_halves(y)

    @pl.when(jnp.logical_not(valid))
    def _():
        y_ref[...] = jnp.zeros_like(y_ref)


def _expert_ffn(blk_expert, n_valid, xb, w_gate_up, b_gate_up, w_down, b_down):
    idx = jnp.arange(N_BLK, dtype=I32)
    used = idx < n_valid[0]
    first = (used & ((idx == 0) | (blk_expert != jnp.roll(blk_expert, 1)))).astype(I32)
    slot = (jnp.cumsum(first) - 1) & 1
    later = used[None, :] & (idx[None, :] > idx[:, None]) & (blk_expert[None, :] != blk_expert[:, None])
    nxt_pos = jnp.min(jnp.where(later, idx[None, :], N_BLK), axis=1)
    nxt = jnp.where(nxt_pos < N_BLK, blk_expert[jnp.minimum(nxt_pos, N_BLK - 1)], -1).astype(I32)

    blockwise = lambda i, *_: (i, 0)
    per_expert = lambda i, be, *_: (be[i], 0, 0)
    grid_spec = pltpu.PrefetchScalarGridSpec(
        num_scalar_prefetch=5,
        grid=(N_BLK,),
        in_specs=[
            pl.BlockSpec((BLK, ROW_W), blockwise),
            pl.BlockSpec(memory_space=pl.ANY),
            pl.BlockSpec((None, 1, 2 * D_FF), per_expert),
            pl.BlockSpec(memory_space=pl.ANY),
            pl.BlockSpec((None, 1, D_MODEL), per_expert),
        ],
        out_specs=pl.BlockSpec((BLK, ROW_W), blockwise),
        scratch_shapes=[
            pltpu.VMEM((2, D_MODEL, 2 * D_FF), F32),
            pltpu.VMEM((2, D_FF, D_MODEL), F32),
            pltpu.SemaphoreType.DMA((2, 2)),
        ],
    )
    return pl.pallas_call(
        _ffn_kernel,
        grid_spec=grid_spec,
        out_shape=jax.ShapeDtypeStruct((P_ROWS, ROW_W), ROW_DT),
        compiler_params=pltpu.CompilerParams(
            dimension_semantics=("arbitrary",), vmem_limit_bytes=VMEM_LIMIT),
        name="expert_ffn",
    )(blk_expert, n_valid, first, slot.astype(I32), nxt, xb, w_gate_up,
      b_gate_up.reshape(N_EXPERTS, 1, 2 * D_FF), w_down, b_down.reshape(N_EXPERTS, 1, D_MODEL))


def _final_kernel(x1_ref, y0_ref, y1_ref, y2_ref, y3_ref, w_ref, gt2_ref, g_ref, *rest):
    o_ref = rest[-1]
    w = w_ref[...]
    ylo = jnp.zeros((TM_FIN, HALF), F32)
    yhi = jnp.zeros((TM_FIN, HALF), F32)
    for k, y_ref in enumerate((y0_ref, y1_ref, y2_ref, y3_ref)):
        lo, hi = _unpack_halves(y_ref[...])
        wk = w[:, k:k + 1]
        ylo = ylo + wk * lo
        yhi = yhi + wk * hi
    y = jnp.concatenate([ylo, yhi], axis=1)
    x2 = x1_ref[...] + gt2_ref[...] * y
    inv = lax.rsqrt(jnp.mean(x2 * x2, axis=-1, keepdims=True) + EPS)
    o_ref[...] = x2 * inv * g_ref[...]


def _final(grp, x1, yg, w4, mod4, final_norm_g, out_so_far):
    per_b = SEQ // TM_FIN
    ntile = N_GRP // TM_FIN
    first = grp * ntile
    slot = lambda k: pl.BlockSpec((TM_FIN, ROW_W), lambda i: (k * ntile + i, 0))
    in_specs = [
        pl.BlockSpec((TM_FIN, D_MODEL), lambda i: (i, 0)),
        slot(0), slot(1), slot(2), slot(3),
        pl.BlockSpec((TM_FIN, TOP_K), lambda i: (i, 0)),
        pl.BlockSpec((None, None, 1, D_MODEL), lambda i: (5, (first + i) // per_b, 0, 0)),
        pl.BlockSpec((1, D_MODEL), lambda i: (0, 0)),
    ]
    args = [x1, yg, yg, yg, yg, w4, mod4, final_norm_g]
    aliases = {}
    if out_so_far is not None:
        in_specs.append(pl.BlockSpec(memory_space=pl.ANY))
        args.append(out_so_far)
        aliases = {len(args) - 1: 0}
    return pl.pallas_call(
        _final_kernel,
        grid=(ntile,),
        in_specs=in_specs,
        out_specs=pl.BlockSpec((TM_FIN, D_MODEL), lambda i: (first + i, 0)),
        out_shape=jax.ShapeDtypeStruct((N_TOK, D_MODEL), F32),
        input_output_aliases=aliases,
        compiler_params=pltpu.CompilerParams(
            dimension_semantics=("arbitrary",), vmem_limit_bytes=VMEM_LIMIT),
        name="combine_final",
    )(*args)


SC_CORES = 2
SC_SUBCORES = 16
SC_WORKERS = SC_CORES * SC_SUBCORES
SC_CHUNK = 64


def _sc_mesh():
    return plsc.VectorSubcoreMesh(core_axis_name="c", subcore_axis_name="s")


def _row_buffers():
    return ([pltpu.VMEM((SC_CHUNK, HALF), U32)] * 2 + [pltpu.SemaphoreType.DMA] * 5)


def _dispatch_rows(h2p, dest2d):
    per_w = N_GRP // SC_WORKERS
    nchunk = per_w // SC_CHUNK
    rows_per_k = N_GRP // SC_CHUNK
    assert nchunk % 2 == 0

    @functools.partial(
        pl.kernel, mesh=_sc_mesh(), out_type=jax.ShapeDtypeStruct((P_ROWS, HALF), U32),
        scratch_types=[pltpu.VMEM((TOP_K * nchunk, SC_CHUNK), I32)] + _row_buffers(),
        name="moe_dispatch")
    def k(src_hbm, dest_hbm, out_hbm, idx_v, buf0, buf1, isem, l0, l1, s0, s1):
        wid = lax.axis_index("s") * SC_CORES + lax.axis_index("c")
        bufs, lsem, ssem = (buf0, buf1), (l0, l1), (s0, s1)
        idx_loads = [
            pltpu.make_async_copy(dest_hbm.at[pl.ds(kk * rows_per_k + wid * nchunk, nchunk)],
                                  idx_v.at[pl.ds(kk * nchunk, nchunk)], isem) for kk in range(TOP_K)]

        def load(j, b):
            return pltpu.make_async_copy(src_hbm.at[pl.ds(wid * per_w + j * SC_CHUNK, SC_CHUNK)], bufs[b], lsem[b])

        def scatters(j, b):
            return [pltpu.make_async_copy(bufs[b], out_hbm.at[idx_v.at[kk * nchunk + j]], ssem[b])
                    for kk in range(TOP_K)]

        for cp in idx_loads:
            cp.start()
        load(0, 0).start()
        for cp in idx_loads:
            cp.wait()

        @pl.loop(0, nchunk // 2)
        def _(i):
            for b in range(2):
                j = 2 * i + b

                @pl.when(j >= 1)
                def _():
                    for cp in scatters(j - 1, 1 - b):
                        cp.wait()

                @pl.when(j + 1 < nchunk)
                def _():
                    load(j + 1, 1 - b).start()

                load(j, b).wait()
                for cp in scatters(j, b):
                    cp.start()

        for cp in scatters(nchunk - 1, 1):
            cp.wait()

    return k(h2p, dest2d)


def _combine_rows(yb, dest2d):
    n_out = TOP_K * N_GRP
    per_w = n_out // SC_WORKERS
    nchunk = per_w // SC_CHUNK
    assert nchunk % 2 == 0

    @functools.partial(
        pl.kernel, mesh=_sc_mesh(), out_type=jax.ShapeDtypeStruct((n_out, HALF), U32),
        scratch_types=[pltpu.VMEM((nchunk, SC_CHUNK), I32)] + _row_buffers(),
        name="moe_combine")
    def k(tab_hbm, idx_hbm, out_hbm, idx_v, buf0, buf1, isem, g0, g1, s0, s1):
        wid = lax.axis_index("s") * SC_CORES + lax.axis_index("c")
        bufs, gsem, ssem = (buf0, buf1), (g0, g1), (s0, s1)
        idx_load = pltpu.make_async_copy(idx_hbm.at[pl.ds(wid * nchunk, nchunk)], idx_v, isem)

        def gather(j, b):
            return pltpu.make_async_copy(tab_hbm.at[idx_v.at[j]], bufs[b], gsem[b])

        def store(j, b):
            return pltpu.make_async_copy(bufs[b], out_hbm.at[pl.ds(wid * per_w + j * SC_CHUNK, SC_CHUNK)], ssem[b])

        idx_load.start()
        idx_load.wait()
        gather(0, 0).start()

        @pl.loop(0, nchunk // 2)
        def _(i):
            for b in range(2):
                j = 2 * i + b

                @pl.when(j >= 1)
                def _():
                    store(j - 1, 1 - b).wait()

                @pl.when(j + 1 < nchunk)
                def _():
                    gather(j + 1, 1 - b).start()

                gather(j, b).wait()
                store(j, b).start()

        store(nchunk - 1, 1).wait()

    return k(yb, dest2d)


def _lambda_kernel(p_ref, o_ref):
    p = p_ref[...]
    s1 = jnp.sum(p[0:1] * p[1:2], axis=-1, keepdims=True)
    s2 = jnp.sum(p[2:3] * p[3:4], axis=-1, keepdims=True)
    o_ref[...] = jnp.broadcast_to(jnp.exp(s1) - jnp.exp(s2) + LAMBDA_INIT, (1, LANES))


def kernel(x, c, w_ada, b_ada, norm1_g, w_in, lambda_q1, lambda_k1, lambda_q2, lambda_k2, diff_norm_g, w_alpha_up, b_alpha, gla_norm_g, w_branch_diff, w_branch_gla, w_out, norm2_g, w_router, b_router, w_gate_up, b_gate_up, w_down, b_down, final_norm_g):
    w_in0 = w_in[0]
    c_a, c_g = W_A, W_A + W_G
    w_a = w_in0[:, :c_a].astype(BF16)
    w_g = w_in0[:, c_a:c_g].astype(BF16)
    w_lr = jnp.pad(w_in0[:, c_g:c_g + GLA_RANK], ((0, 0), (0, LANES - GLA_RANK))).astype(BF16)
    w_gate = w_in0[:, c_g + GLA_RANK:].astype(BF16)
    w_up = jnp.pad(w_alpha_up[0], ((0, LANES - GLA_RANK), (0, 0)))
    lam_in = jnp.concatenate([lambda_q1, lambda_k1, lambda_q2, lambda_k2], axis=0)
    slopes = jnp.asarray(2.0 ** (-8.0 * np.arange(1, DIFF_HEADS + 1) / DIFF_HEADS), dtype=F32)

    mod = _modulation(c, w_ada[0], b_ada[0])
    mod4 = mod.reshape(N_MOD, BATCH, 1, D_MODEL)
    lam = pl.pallas_call(
        _lambda_kernel, out_shape=jax.ShapeDtypeStruct((1, LANES), F32), name="lambda")(lam_in)[0, :1]

    qkv_a, qkv_g, gates, glr = _in_proj(x, mod4, norm1_g, w_a, w_g, w_gate, w_lr)
    pos = jnp.arange(SEQ, dtype=I32)
    p_hi, p_lo = (pos >> 3).astype(F32), (pos & 7).astype(F32)
    kaug = jnp.zeros((SEQ, LANES), F32).at[:, 0].set(p_hi).at[:, 1].set(p_hi).at[:, 2].set(p_lo).at[:, 3].set(
        p_lo).astype(BF16)
    o_a = _diff_attention(qkv_a, slopes, lam, kaug, diff_norm_g)
    o_b = _gla(qkv_g, glr, w_up, b_alpha, gla_norm_g)

    merge_args = (o_a.reshape(N_TOK, -1), o_b.reshape(N_TOK, -1), gates.reshape(N_TOK, W_GATE),
                  x.reshape(N_TOK, D_MODEL), mod4, norm2_g,
                  w_branch_diff[0].astype(BF16), w_branch_gla[0].astype(BF16), w_out[0].astype(BF16),
                  w_router[0].T, b_router[0].reshape(N_EXPERTS, 1))
    fin_g = final_norm_g.reshape(1, D_MODEL)

    routed = []
    for grp in range(MOE_GROUPS):
        x1, h2p, eidx, rank, wts, cnt = _merge_route(grp, *merge_args)
        dest8, be, nv = _route_plan(cnt, eidx, rank)
        dest = dest8[:TOP_K].reshape(-1, SC_CHUNK)
        routed.append((x1, wts, dest, be, nv, _dispatch_rows(h2p, dest)))
    gathered = []
    for x1, wts, dest, be, nv, xb in routed:
        yb = _expert_ffn(be[0, :N_BLK], nv[0, :1], xb, w_gate_up[0], b_gate_up[0], w_down[0], b_down[0])
        gathered.append(_combine_rows(yb, dest))
    out = None
    for grp, ((x1, wts, *_), yg) in enumerate(zip(routed, gathered)):
        out = _final(grp, x1, yg, wts[:TOP_K].T, mod4, fin_g, out)
    return out.reshape(BATCH, SEQ, D_MODEL)
```

```python
import functools
import math

import jax
import jax.numpy as jnp
import numpy as np
from jax import lax
from jax.experimental import pallas as pl
from jax.experimental.pallas import tpu as pltpu
from jax.experimental.pallas import tpu_sc as plsc

F32 = jnp.float32
BF16 = jnp.bfloat16
U32 = jnp.uint32
I32 = jnp.int32

D_MODEL = 1024
BATCH = 16
SEQ = 2048
N_TOK = BATCH * SEQ
CHUNK = 64
DIFF_HEADS = 4
DIFF_DH = 64
GLA_HEADS = 4
GLA_DK = 64
GLA_DV = 128
GLA_RANK = 16
GLA_GATE_NORM = 16.0
N_EXPERTS = 32
TOP_K = 4
D_FF = D_MODEL
SWIGLU_LIMIT = 7.0
SWIGLU_ALPHA = 1.702
N_MOD = 6
EPS = 1e-6
LAMBDA_INIT = 0.8 - 0.6 * math.exp(-0.3 * 0)

LANES = 128
HALF = D_MODEL // 2
ROW_W = HALF
ROW_DT = U32

TM_IN = 512
TQ = 512
VT_ROWS = 2 * DIFF_DH + 16
LOG2E = math.log2(math.e)
TM_MERGE = 512
BLK = 512
MOE_GROUPS = 1
N_GRP = N_TOK // MOE_GROUPS
N_BLK = (N_GRP * TOP_K) // BLK + N_EXPERTS
P_ROWS = N_BLK * BLK
TM_FIN = 512
VMEM_LIMIT = 56 * 1024 * 1024


def _nt_dot(a, b):
    return lax.dot_general(a, b, (((1,), (1,)), ((), ())), preferred_element_type=F32)


def _tn_dot(a, b):
    return lax.dot_general(a, b, (((0,), (0,)), ((), ())), preferred_element_type=F32)


def _split3(x):
    hi = x.astype(BF16)
    r1 = x - hi.astype(F32)
    mid = r1.astype(BF16)
    lo = (r1 - mid.astype(F32)).astype(BF16)
    return hi, mid, lo


def _pack_halves(y):
    return pltpu.pack_elementwise([y[:, :HALF], y[:, HALF:]], packed_dtype=BF16)


def _unpack_halves(u):
    lo = pltpu.unpack_elementwise(u, index=0, packed_dtype=BF16, unpacked_dtype=F32)
    hi = pltpu.unpack_elementwise(u, index=1, packed_dtype=BF16, unpacked_dtype=F32)
    return lo, hi


def _mod_kernel(c_ref, w_ref, b_ref, o_ref):
    c = c_ref[...]
    s = c * jax.nn.sigmoid(c)
    o_ref[0] = jnp.dot(s.astype(BF16), w_ref[...].astype(BF16),
                       preferred_element_type=F32) + b_ref[...]


def _modulation(c, w_ada, b_ada):
    return pl.pallas_call(
        _mod_kernel,
        grid=(N_MOD,),
        in_specs=[
            pl.BlockSpec((BATCH, D_MODEL), lambda j: (0, 0)),
            pl.BlockSpec((D_MODEL, D_MODEL), lambda j: (0, j)),
            pl.BlockSpec((1, D_MODEL), lambda j: (0, j)),
        ],
        out_specs=pl.BlockSpec((1, BATCH, D_MODEL), lambda j: (j, 0, 0)),
        out_shape=jax.ShapeDtypeStruct((N_MOD, BATCH, D_MODEL), F32),
        compiler_params=pltpu.CompilerParams(dimension_semantics=("arbitrary",)),
        name="adaln_mod",
    )(c, w_ada, b_ada.reshape(1, N_MOD * D_MODEL))


W_A = 3 * DIFF_HEADS * 2 * DIFF_DH
W_G = 2 * GLA_HEADS * GLA_DK + 2 * GLA_HEADS * GLA_DV
W_GATE = 2 * D_MODEL


def _in_kernel(x_ref, sh_ref, sc_ref, g_ref, wa_ref, wg_ref, wgate_ref, wlr_ref,
               oa_ref, og_ref, ogate_ref, olr_ref):
    x = x_ref[...]
    inv = lax.rsqrt(jnp.mean(x * x, axis=-1, keepdims=True) + EPS)
    h = (x * inv * g_ref[...]) * (1.0 + sc_ref[...]) + sh_ref[...]
    hb = h.astype(BF16)
    oa_ref[...] = jnp.dot(hb, wa_ref[...], preferred_element_type=F32).astype(BF16)
    og_ref[...] = jnp.dot(hb, wg_ref[...], preferred_element_type=F32).astype(BF16)
    ogate_ref[...] = jnp.dot(hb, wgate_ref[...], preferred_element_type=F32).astype(BF16)
    olr_ref[...] = jnp.dot(hb, wlr_ref[...], preferred_element_type=F32)


def _in_proj(x, mod4, norm1_g, w_a, w_g, w_gate, w_lr):
    nrow = SEQ // TM_IN
    full = lambda shape: pl.BlockSpec(shape, lambda b, i: (0,) * len(shape))
    return pl.pallas_call(
        _in_kernel,
        grid=(BATCH, nrow),
        in_specs=[
            pl.BlockSpec((None, TM_IN, D_MODEL), lambda b, i: (b, i, 0)),
            pl.BlockSpec((None, None, 1, D_MODEL), lambda b, i: (0, b, 0, 0)),
            pl.BlockSpec((None, None, 1, D_MODEL), lambda b, i: (1, b, 0, 0)),
            full((1, D_MODEL)),
            full((D_MODEL, W_A)),
            full((D_MODEL, W_G)),
            full((D_MODEL, W_GATE)),
            full((D_MODEL, LANES)),
        ],
        out_specs=[
            pl.BlockSpec((None, TM_IN, W_A), lambda b, i: (b, i, 0)),
            pl.BlockSpec((None, TM_IN, W_G), lambda b, i: (b, i, 0)),
            pl.BlockSpec((None, TM_IN, W_GATE), lambda b, i: (b, i, 0)),
            pl.BlockSpec((None, TM_IN, LANES), lambda b, i: (b, i, 0)),
        ],
        out_shape=[
            jax.ShapeDtypeStruct((BATCH, SEQ, W_A), BF16),
            jax.ShapeDtypeStruct((BATCH, SEQ, W_G), BF16),
            jax.ShapeDtypeStruct((BATCH, SEQ, W_GATE), BF16),
            jax.ShapeDtypeStruct((BATCH, SEQ, LANES), F32),
        ],
        compiler_params=pltpu.CompilerParams(
            dimension_semantics=("arbitrary", "arbitrary"), vmem_limit_bytes=VMEM_LIMIT),
        name="in_proj",
    )(x, mod4, mod4, norm1_g, w_a, w_g, w_gate, w_lr)


def _attn_kernel(slope_ref, lam_ref, q_ref, k_ref, v_ref, kaug_ref, g_ref, o_ref,
                 sa_ref, sb_ref, corr_ref, vt_ref, m_ref, acc_ref):
    c_alibi = slope_ref[pl.program_id(1)] * LOG2E
    lam = lam_ref[0]
    lane = lax.broadcasted_iota(I32, (TQ, 2 * DIFF_DH), 1)
    c_vec = jnp.full((TQ, 2 * DIFF_DH), c_alibi, F32)
    c_hi = c_vec.astype(BF16).astype(F32)
    c_lo = (c_vec - c_hi).astype(BF16).astype(F32)
    zero = jnp.zeros((TQ, 2 * DIFF_DH), BF16)
    aug = jnp.where(lane == 0, 8.0 * c_hi, jnp.where(lane == 1, 8.0 * c_lo,
                    jnp.where(lane == 2, c_hi, jnp.where(lane == 3, c_lo, 0.0)))).astype(BF16)

    def blk(j):
        return slice(j * TQ, (j + 1) * TQ)

    def stacked_queries(qi):
        q = (q_ref[blk(qi), :].astype(F32) * (DIFF_DH ** -0.5 * LOG2E)).astype(BF16)
        return jnp.concatenate([
            jnp.concatenate([jnp.where(lane < DIFF_DH, q, zero), aug], axis=1),
            jnp.concatenate([jnp.where(lane >= DIFF_DH, q, zero), aug], axis=1)], axis=0)

    def scores(lhs, j):
        kk = jnp.concatenate([k_ref[blk(j), :], kaug_ref[blk(j), :]], axis=1)
        return _nt_dot(kk, lhs)

    ones_rows = jnp.where(lax.broadcasted_iota(I32, (VT_ROWS - 2 * DIFF_DH, TQ), 0) == 0, 1.0, 0.0).astype(BF16)
    for j in range(SEQ // TQ):
        vt_ref[j, :2 * DIFF_DH, :] = v_ref[blk(j), :].astype(F32).T.astype(BF16)
        vt_ref[j, 2 * DIFF_DH:, :] = ones_rows

    def update(s, j, first):
        if first:
            m_new = jnp.max(s, axis=0, keepdims=True)
            p = jnp.exp2((s - m_new).astype(BF16))
            acc_ref[...] = jnp.dot(vt_ref[j], p, preferred_element_type=F32)
        else:
            m = m_ref[...]
            m_new = jnp.maximum(m, jnp.max(s, axis=0, keepdims=True))
            alpha = jnp.exp2(m - m_new)
            p = jnp.exp2((s - m_new).astype(BF16))
            acc_ref[...] = alpha * acc_ref[...] + jnp.dot(vt_ref[j], p, preferred_element_type=F32)
        m_ref[...] = m_new

    kr = lax.broadcasted_iota(I32, (TQ, TQ), 0)
    qc = lax.broadcasted_iota(I32, (TQ, TQ), 1)
    ahead = jnp.maximum(kr - qc, 0).astype(F32)
    corr_ref[...] = jnp.where((qc >> 6) >= (kr >> 6), (-2.0 * c_alibi) * ahead, -jnp.inf)

    pairs = [(qi, j) for qi in range(SEQ // TQ) for j in range(qi + 1)]
    bufs = (sa_ref, sb_ref)
    lhs = stacked_queries(0)
    bufs[0][...] = scores(lhs, 0)
    for t, (qi, j) in enumerate(pairs):
        if t + 1 < len(pairs):
            nqi, nj = pairs[t + 1]
            if nqi != qi:
                lhs = stacked_queries(nqi)
            bufs[(t + 1) % 2][...] = scores(lhs, nj)
        s = bufs[t % 2][...]
        if j == qi:
            corr = corr_ref[...]
            s = s + jnp.concatenate([corr, corr], axis=1)
        update(s, j, first=(j == 0))
        if j == qi:
            ot = acc_ref[:2 * DIFF_DH, :] / acc_ref[2 * DIFF_DH:2 * DIFF_DH + 1, :]
            o = (ot[:, :TQ] - lam * ot[:, TQ:]).T
            inv = lax.rsqrt(jnp.mean(o * o, axis=-1, keepdims=True) + EPS)
            o_ref[blk(qi), :] = (o * inv * g_ref[...] * (1.0 - LAMBDA_INIT)).astype(BF16)


def _diff_attention(qkv_a, slopes, lam, kaug, diff_norm_g):
    return pl.pallas_call(
        _attn_kernel,
        grid=(BATCH, DIFF_HEADS),
        in_specs=[
            pl.BlockSpec(memory_space=pltpu.SMEM),
            pl.BlockSpec(memory_space=pltpu.SMEM),
            pl.BlockSpec((None, SEQ, LANES), lambda b, h: (b, 0, h)),
            pl.BlockSpec((None, SEQ, LANES), lambda b, h: (b, 0, DIFF_HEADS + h)),
            pl.BlockSpec((None, SEQ, LANES), lambda b, h: (b, 0, 2 * DIFF_HEADS + h)),
            pl.BlockSpec((SEQ, LANES), lambda b, h: (0, 0)),
            pl.BlockSpec((1, LANES), lambda b, h: (0, 0)),
        ],
        out_specs=pl.BlockSpec((None, SEQ, LANES), lambda b, h: (b, 0, h)),
        out_shape=jax.ShapeDtypeStruct((BATCH, SEQ, DIFF_HEADS * 2 * DIFF_DH), BF16),
        scratch_shapes=[
            pltpu.VMEM((TQ, 2 * TQ), F32), pltpu.VMEM((TQ, 2 * TQ), F32),
            pltpu.VMEM((TQ, TQ), F32),
            pltpu.VMEM((SEQ // TQ, VT_ROWS, TQ), BF16),
            pltpu.VMEM((1, 2 * TQ), F32),
            pltpu.VMEM((VT_ROWS, 2 * TQ), F32),
        ],
        compiler_params=pltpu.CompilerParams(
            dimension_semantics=("arbitrary", "arbitrary")),
        name="diff_attn",
    )(slopes, lam, qkv_a, qkv_a, qkv_a, kaug, diff_norm_g)


N_CHUNK = SEQ // CHUNK
PAIR = 2 * GLA_DK
PAIR_V = 2 * GLA_DV
CS_ROWS = 256
GLA_UNROLL = 8


def _gla_kernel(q_ref, k_ref, v_ref, r_ref, lr_ref, wup_ref, bup_ref, g_ref, o_ref,
                gcum_ref, state_ref):
    w_hi, w_mid, _ = _split3(wup_ref[...])
    rr = lax.broadcasted_iota(I32, (CS_ROWS, CS_ROWS), 0)
    cc = lax.broadcasted_iota(I32, (CS_ROWS, CS_ROWS), 1)
    tri = jnp.where(((rr >> 6) == (cc >> 6)) & (cc <= rr), 1.0, 0.0).astype(BF16)
    for blk in range(SEQ // CS_ROWS):
        rows = pl.ds(blk * CS_ROWS, CS_ROWS)
        a_hi, a_mid, _ = _split3(lr_ref[rows, :])
        z = (jnp.dot(a_hi, w_hi, preferred_element_type=F32)
             + jnp.dot(a_hi, w_mid, preferred_element_type=F32)
             + jnp.dot(a_mid, w_hi, preferred_element_type=F32)) + bup_ref[...]
        la = (jnp.minimum(z, 0.0) - jnp.log(1.0 + jnp.exp(-jnp.abs(z)))) * (1.0 / GLA_GATE_NORM)
        l_hi, l_mid, l_lo = _split3(la)
        gcum_ref[rows, :] = (jnp.dot(tri, l_hi, preferred_element_type=F32)
                             + jnp.dot(tri, l_mid, preferred_element_type=F32)
                             + jnp.dot(tri, l_lo, preferred_element_type=F32))

    state_ref[...] = jnp.zeros_like(state_ref)
    lane_k = lax.broadcasted_iota(I32, (1, PAIR), 1)
    row_v = lax.broadcasted_iota(I32, (PAIR_V, PAIR), 0)
    col_k = lax.broadcasted_iota(I32, (PAIR_V, PAIR), 1)
    same_head = (row_v >= GLA_DV) == (col_k >= GLA_DK)
    cr = lax.broadcasted_iota(I32, (CHUNK, CHUNK), 0)
    cs = lax.broadcasted_iota(I32, (CHUNK, CHUNK), 1)
    causal = cs <= cr
    scale = GLA_DK ** -0.5

    def chunk(n):
        rows = pl.ds(pl.multiple_of(n * CHUNK, CHUNK), CHUNK)
        gc = gcum_ref[rows, :]
        g_last = gcum_ref[pl.ds(n * CHUNK + CHUNK - 1, 1), :]
        qf = q_ref[rows, :].astype(F32) * scale
        kf = k_ref[rows, :].astype(F32)
        q_s = (qf * jnp.exp(gc)).astype(BF16)
        k_s = (kf * jnp.exp(-gc)).astype(BF16)
        k_d = (kf * jnp.exp(g_last - gc)).astype(BF16)
        decay = jnp.exp(g_last)
        for pr in range(GLA_HEADS // 2):
            kl = slice(pr * PAIR, (pr + 1) * PAIR)
            vl = slice(pr * PAIR_V, (pr + 1) * PAIR_V)
            qs_p, ks_p, kd_p = q_s[:, kl], k_s[:, kl], k_d[:, kl]
            v_p = v_ref[rows, vl]
            st = state_ref[pr]
            o_inter = _nt_dot(qs_p, st.astype(BF16))
            d_st = _tn_dot(v_p, kd_p)
            state_ref[pr] = st * decay[:, kl] + jnp.where(same_head, d_st, 0.0)
            for sub in range(2):
                hd = 2 * pr + sub
                in_head = (lane_k >= sub * GLA_DK) & (lane_k < (sub + 1) * GLA_DK)
                a = _nt_dot(jnp.where(in_head, qs_p, jnp.zeros_like(qs_p)), ks_p)
                a = jnp.where(causal, a, 0.0).astype(BF16)
                vs = slice(hd * GLA_DV, (hd + 1) * GLA_DV)
                o = (jnp.dot(a, v_ref[rows, vs], preferred_element_type=F32)
                     + o_inter[:, sub * GLA_DV:(sub + 1) * GLA_DV])
                inv = lax.rsqrt(jnp.mean(o * o, axis=-1, keepdims=True) + EPS)
                r = r_ref[rows, vs].astype(F32)
                o_ref[rows, vs] = (o * inv * g_ref[...] * (r * jax.nn.sigmoid(r))).astype(BF16)

    def chunk_group(t, _):
        for u in range(GLA_UNROLL):
            chunk(t * GLA_UNROLL + u)
        return 0

    lax.fori_loop(0, N_CHUNK // GLA_UNROLL, chunk_group, 0)


def _gla(qkv_g, glr, w_up, b_up, gla_norm_g):
    qk_w = GLA_HEADS * GLA_DK
    v_w = GLA_HEADS * GLA_DV
    return pl.pallas_call(
        _gla_kernel,
        grid=(BATCH,),
        in_specs=[
            pl.BlockSpec((None, SEQ, qk_w), lambda b: (b, 0, 0)),
            pl.BlockSpec((None, SEQ, qk_w), lambda b: (b, 0, 1)),
            pl.BlockSpec((None, SEQ, v_w), lambda b: (b, 0, 1)),
            pl.BlockSpec((None, SEQ, v_w), lambda b: (b, 0, 2)),
            pl.BlockSpec((None, SEQ, LANES), lambda b: (b, 0, 0)),
            pl.BlockSpec((LANES, qk_w), lambda b: (0, 0)),
            pl.BlockSpec((1, qk_w), lambda b: (0, 0)),
            pl.BlockSpec((1, GLA_DV), lambda b: (0, 0)),
        ],
        out_specs=pl.BlockSpec((None, SEQ, v_w), lambda b: (b, 0, 0)),
        out_shape=jax.ShapeDtypeStruct((BATCH, SEQ, v_w), BF16),
        scratch_shapes=[
            pltpu.VMEM((SEQ, qk_w), F32),
            pltpu.VMEM((GLA_HEADS // 2, PAIR_V, PAIR), F32),
        ],
        compiler_params=pltpu.CompilerParams(
            dimension_semantics=("arbitrary",), vmem_limit_bytes=VMEM_LIMIT),
        name="gla",
    )(qkv_g, qkv_g, qkv_g, qkv_g, glr, w_up, b_up, gla_norm_g)


def _merge_kernel(oa_ref, ob_ref, gate_ref, x_ref, gt1_ref, sh2_ref, sc2_ref, g2_ref,
                  wpa_ref, wpb_ref, wo_ref, wr_ref, br_ref,
                  x1_ref, h2p_ref, eidx_ref, rank_ref, wts_ref, cnt_ref,
                  upper_ref, carry_ref):
    i = pl.program_id(0)

    @pl.when(i == 0)
    def _():
        rr = lax.broadcasted_iota(I32, (TM_MERGE, TM_MERGE), 0)
        cc = lax.broadcasted_iota(I32, (TM_MERGE, TM_MERGE), 1)
        upper_ref[...] = jnp.where(rr < cc, 1.0, 0.0).astype(BF16)
        carry_ref[...] = jnp.zeros_like(carry_ref)

    ga = gate_ref[:, :D_MODEL].astype(F32)
    gb = gate_ref[:, D_MODEL:].astype(F32)
    merged = (jax.nn.sigmoid(ga) * jnp.dot(oa_ref[...], wpa_ref[...], preferred_element_type=F32)
              + jax.nn.sigmoid(gb) * jnp.dot(ob_ref[...], wpb_ref[...], preferred_element_type=F32))
    y = jnp.dot(merged.astype(BF16), wo_ref[...], preferred_element_type=F32)
    x1 = x_ref[...] + gt1_ref[...] * y
    x1_ref[...] = x1
    inv = lax.rsqrt(jnp.mean(x1 * x1, axis=-1, keepdims=True) + EPS)
    h2 = (x1 * inv * g2_ref[...]) * (1.0 + sc2_ref[...]) + sh2_ref[...]
    h2p_ref[...] = _pack_halves(h2)

    h_hi, h_mid, _ = _split3(h2)
    w_hi, w_mid, _ = _split3(wr_ref[...])
    logits = (_nt_dot(w_hi, h_hi) + _nt_dot(w_hi, h_mid) + _nt_dot(w_mid, h_hi)) + br_ref[...]

    eio = lax.broadcasted_iota(I32, (N_EXPERTS, TM_MERGE), 0)
    vals, idxs, sels = [], [], []
    cur = logits
    for _k in range(TOP_K):
        m = jnp.max(cur, axis=0, keepdims=True)
        idx = jnp.min(jnp.where(cur == m, eio, N_EXPERTS), axis=0, keepdims=True)
        sel = eio == idx
        vals.append(m)
        idxs.append(idx)
        sels.append(sel)
        cur = jnp.where(sel, -jnp.inf, cur)
    es = [jnp.exp(v - vals[0]) for v in vals]
    tot = es[0] + es[1] + es[2] + es[3]
    onehot = jnp.zeros((N_EXPERTS, TM_MERGE), F32)
    for sel in sels:
        onehot = onehot + jnp.where(sel, 1.0, 0.0)
    before = jnp.dot(onehot.astype(BF16), upper_ref[...], preferred_element_type=F32) + carry_ref[:, 0:1]
    ranks = [jnp.sum(jnp.where(sel, before, 0.0), axis=0, keepdims=True) for sel in sels]
    carry_ref[...] = carry_ref[...] + jnp.sum(onehot, axis=1, keepdims=True)
    cnt_ref[...] = carry_ref[...]

    zi = jnp.zeros((8 - TOP_K, TM_MERGE), I32)
    zf = jnp.zeros((8 - TOP_K, TM_MERGE), F32)
    eidx_ref[...] = jnp.concatenate(idxs + [zi], axis=0)
    rank_ref[...] = jnp.concatenate([r.astype(I32) for r in ranks] + [zi], axis=0)
    wts_ref[...] = jnp.concatenate([e / tot for e in es] + [zf], axis=0)


def _merge_route(grp, o_a, o_b, gates, x2d, mod4, norm2_g, w_pa, w_pb, w_o, w_rt, b_r):
    ntile = N_GRP // TM_MERGE
    first = grp * ntile
    per_b = SEQ // TM_MERGE
    full = lambda shape: pl.BlockSpec(shape, lambda i: (0,) * len(shape))
    row_in = lambda w: pl.BlockSpec((TM_MERGE, w), lambda i: (first + i, 0))
    row = lambda w: pl.BlockSpec((TM_MERGE, w), lambda i: (i, 0))
    modspec = lambda j: pl.BlockSpec((None, None, 1, D_MODEL), lambda i: (j, (first + i) // per_b, 0, 0))
    col = pl.BlockSpec((8, TM_MERGE), lambda i: (0, i))
    return pl.pallas_call(
        _merge_kernel,
        grid=(ntile,),
        in_specs=[
            row_in(DIFF_HEADS * 2 * DIFF_DH), row_in(GLA_HEADS * GLA_DV), row_in(W_GATE), row_in(D_MODEL),
            modspec(2), modspec(3), modspec(4),
            full((1, D_MODEL)),
            full((DIFF_HEADS * 2 * DIFF_DH, D_MODEL)), full((GLA_HEADS * GLA_DV, D_MODEL)),
            full((D_MODEL, D_MODEL)),
            full((N_EXPERTS, D_MODEL)), full((N_EXPERTS, 1)),
        ],
        out_specs=[
            row(D_MODEL), row(ROW_W), col, col, col,
            pl.BlockSpec((N_EXPERTS, LANES), lambda i: (0, 0)),
        ],
        out_shape=[
            jax.ShapeDtypeStruct((N_GRP, D_MODEL), F32),
            jax.ShapeDtypeStruct((N_GRP, ROW_W), ROW_DT),
            jax.ShapeDtypeStruct((8, N_GRP), I32),
            jax.ShapeDtypeStruct((8, N_GRP), I32),
            jax.ShapeDtypeStruct((8, N_GRP), F32),
            jax.ShapeDtypeStruct((N_EXPERTS, LANES), F32),
        ],
        scratch_shapes=[
            pltpu.VMEM((TM_MERGE, TM_MERGE), BF16),
            pltpu.VMEM((N_EXPERTS, LANES), F32),
        ],
        compiler_params=pltpu.CompilerParams(
            dimension_semantics=("arbitrary",), vmem_limit_bytes=VMEM_LIMIT),
        name="merge_route",
    )(o_a, o_b, gates, x2d, mod4, mod4, mod4, norm2_g, w_pa, w_pb, w_o, w_rt, b_r)


TP = 4096
NB_PAD = ((N_BLK + LANES - 1) // LANES) * LANES


def _plan_kernel(cnt_ref, eidx_ref, rank_ref, dest_ref, be_ref, nv_ref):
    cnt = cnt_ref[...]
    padded = jnp.floor((cnt + (BLK - 1.0)) * (1.0 / BLK)) * BLK
    er = lax.broadcasted_iota(I32, (N_EXPERTS, N_EXPERTS), 0)
    ec = lax.broadcasted_iota(I32, (N_EXPERTS, N_EXPERTS), 1)
    lower = jnp.where(ec < er, 1.0, 0.0).astype(BF16)
    p_hi, p_mid, p_lo = _split3(padded)
    starts = (jnp.dot(lower, p_hi, preferred_element_type=F32)
              + jnp.dot(lower, p_mid, preferred_element_type=F32)
              + jnp.dot(lower, p_lo, preferred_element_type=F32))
    ends = starts + padded
    blk_start = (lax.broadcasted_iota(I32, (1, NB_PAD), 1) * BLK).astype(F32)
    n_before = jnp.sum(jnp.where(ends[:, 0:1] <= blk_start, 1.0, 0.0), axis=0, keepdims=True)
    be_ref[...] = jnp.minimum(n_before, N_EXPERTS - 1.0).astype(I32)
    nv_ref[...] = (jnp.max(ends, axis=0, keepdims=True) * (1.0 / BLK)).astype(I32)

    eio = lax.broadcasted_iota(I32, (N_EXPERTS, TP), 0)
    rows = []
    for k in range(TOP_K):
        onehot = eio == eidx_ref[k:k + 1, :]
        base = jnp.sum(jnp.where(onehot, starts[:, 0:1], 0.0), axis=0, keepdims=True)
        rows.append(base.astype(I32) + rank_ref[k:k + 1, :])
    dest_ref[...] = jnp.concatenate(rows + [jnp.zeros((8 - TOP_K, TP), I32)], axis=0)


def _route_plan(cnt, eidx, rank):
    return pl.pallas_call(
        _plan_kernel,
        grid=(N_GRP // TP,),
        in_specs=[
            pl.BlockSpec((N_EXPERTS, LANES), lambda i: (0, 0)),
            pl.BlockSpec((8, TP), lambda i: (0, i)),
            pl.BlockSpec((8, TP), lambda i: (0, i)),
        ],
        out_specs=[
            pl.BlockSpec((8, TP), lambda i: (0, i)),
            pl.BlockSpec((1, NB_PAD), lambda i: (0, 0)),
            pl.BlockSpec((1, LANES), lambda i: (0, 0)),
        ],
        out_shape=[
            jax.ShapeDtypeStruct((8, N_GRP), I32),
            jax.ShapeDtypeStruct((1, NB_PAD), I32),
            jax.ShapeDtypeStruct((1, LANES), I32),
        ],
        compiler_params=pltpu.CompilerParams(dimension_semantics=("arbitrary",)),
        name="route_plan",
    )(cnt, eidx, rank)


def _ffn_kernel(be_ref, nv_ref, first_ref, slot_ref, nxt_ref, x_ref, wgu_hbm, bgu_ref, wd_hbm, bd_ref, y_ref,
                wgu_buf, wd_buf, sem):
    i = pl.program_id(0)
    valid = i < nv_ref[0]
    s = slot_ref[i]

    def weight_copies(e, slot):
        return (pltpu.make_async_copy(wgu_hbm.at[e], wgu_buf.at[slot], sem.at[slot, 0]),
                pltpu.make_async_copy(wd_hbm.at[e], wd_buf.at[slot], sem.at[slot, 1]))

    @pl.when(i == 0)
    def _():
        for cp in weight_copies(be_ref[0], 0):
            cp.start()

    @pl.when(valid & (first_ref[i] == 1))
    def _():
        for cp in weight_copies(be_ref[i], s):
            cp.wait()

        @pl.when(nxt_ref[i] >= 0)
        def _():
            for cp in weight_copies(nxt_ref[i], 1 - s):
                cp.start()

    @pl.when(valid)
    def _():
        lo, hi = _unpack_halves(x_ref[...])
        x = jnp.concatenate([lo, hi], axis=1).astype(BF16)
        mm = lambda a, w: lax.dot_general(a, w, (((1,), (0,)), ((), ())), preferred_element_type=F32)
        gu = mm(x, wgu_buf[s]) + bgu_ref[...]
        gate = jnp.minimum(gu[:, :D_FF], SWIGLU_LIMIT)
        up = jnp.clip(gu[:, D_FF:], -SWIGLU_LIMIT, SWIGLU_LIMIT)
        act = (up + 1.0) * (gate * jax.nn.sigmoid(SWIGLU_ALPHA * gate))
        y = mm(act.astype(BF16), wd_buf[s]) + bd_ref[...]
        y_ref[...] = _pack_halves(y)

    @pl.when(jnp.logical_not(valid))
    def _():
        y_ref[...] = jnp.zeros_like(y_ref)


def _expert_ffn(blk_expert, n_valid, xb, w_gate_up, b_gate_up, w_down, b_down):
    idx = jnp.arange(N_BLK, dtype=I32)
    used = idx < n_valid[0]
    first = (used & ((idx == 0) | (blk_expert != jnp.roll(blk_expert, 1)))).astype(I32)
    slot = (jnp.cumsum(first) - 1) & 1
    later = used[None, :] & (idx[None, :] > idx[:, None]) & (blk_expert[None, :] != blk_expert[:, None])
    nxt_pos = jnp.min(jnp.where(later, idx[None, :], N_BLK), axis=1)
    nxt = jnp.where(nxt_pos < N_BLK, blk_expert[jnp.minimum(nxt_pos, N_BLK - 1)], -1).astype(I32)

    blockwise = lambda i, *_: (i, 0)
    per_expert = lambda i, be, *_: (be[i], 0, 0)
    grid_spec = pltpu.PrefetchScalarGridSpec(
        num_scalar_prefetch=5,
        grid=(N_BLK,),
        in_specs=[
            pl.BlockSpec((BLK, ROW_W), blockwise),
            pl.BlockSpec(memory_space=pl.ANY),
            pl.BlockSpec((None, 1, 2 * D_FF), per_expert),
            pl.BlockSpec(memory_space=pl.ANY),
            pl.BlockSpec((None, 1, D_MODEL), per_expert),
        ],
        out_specs=pl.BlockSpec((BLK, ROW_W), blockwise),
        scratch_shapes=[
            pltpu.VMEM((2, D_MODEL, 2 * D_FF), F32),
            pltpu.VMEM((2, D_FF, D_MODEL), F32),
            pltpu.SemaphoreType.DMA((2, 2)),
        ],
    )
    return pl.pallas_call(
        _ffn_kernel,
        grid_spec=grid_spec,
        out_shape=jax.ShapeDtypeStruct((P_ROWS, ROW_W), ROW_DT),
        compiler_params=pltpu.CompilerParams(
            dimension_semantics=("arbitrary",), vmem_limit_bytes=VMEM_LIMIT),
        name="expert_ffn",
    )(blk_expert, n_valid, first, slot.astype(I32), nxt, xb, w_gate_up,
      b_gate_up.reshape(N_EXPERTS, 1, 2 * D_FF), w_down, b_down.reshape(N_EXPERTS, 1, D_MODEL))


def _final_kernel(x1_ref, y0_ref, y1_ref, y2_ref, y3_ref, w_ref, gt2_ref, g_ref, *rest):
    o_ref = rest[-1]
    w = w_ref[...].T
    ylo = jnp.zeros((TM_FIN, HALF), F32)
    yhi = jnp.zeros((TM_FIN, HALF), F32)
    for k, y_ref in enumerate((y0_ref, y1_ref, y2_ref, y3_ref)):
        lo, hi = _unpack_halves(y_ref[...])
        wk = w[:, k:k + 1]
        ylo = ylo + wk * lo
        yhi = yhi + wk * hi
    y = jnp.concatenate([ylo, yhi], axis=1)
    x2 = x1_ref[...] + gt2_ref[...] * y
    inv = lax.rsqrt(jnp.mean(x2 * x2, axis=-1, keepdims=True) + EPS)
    o_ref[...] = x2 * inv * g_ref[...]


def _final(grp, x1, yg, w4, mod4, final_norm_g, out_so_far):
    per_b = SEQ // TM_FIN
    ntile = N_GRP // TM_FIN
    first = grp * ntile
    slot = lambda k: pl.BlockSpec((TM_FIN, ROW_W), lambda i: (k * ntile + i, 0))
    in_specs = [
        pl.BlockSpec((TM_FIN, D_MODEL), lambda i: (i, 0)),
        slot(0), slot(1), slot(2), slot(3),
        pl.BlockSpec((8, TM_FIN), lambda i: (0, i)),
        pl.BlockSpec((None, None, 1, D_MODEL), lambda i: (5, (first + i) // per_b, 0, 0)),
        pl.BlockSpec((1, D_MODEL), lambda i: (0, 0)),
    ]
    args = [x1, yg, yg, yg, yg, w4, mod4, final_norm_g]
    aliases = {}
    if out_so_far is not None:
        in_specs.append(pl.BlockSpec(memory_space=pl.ANY))
        args.append(out_so_far)
        aliases = {len(args) - 1: 0}
    return pl.pallas_call(
        _final_kernel,
        grid=(ntile,),
        in_specs=in_specs,
        out_specs=pl.BlockSpec((TM_FIN, D_MODEL), lambda i: (first + i, 0)),
        out_shape=jax.ShapeDtypeStruct((N_TOK, D_MODEL), F32),
        input_output_aliases=aliases,
        compiler_params=pltpu.CompilerParams(
            dimension_semantics=("arbitrary",), vmem_limit_bytes=VMEM_LIMIT),
        name="combine_final",
    )(*args)


SC_CORES = 2
SC_SUBCORES = 16
SC_WORKERS = SC_CORES * SC_SUBCORES
SC_CHUNK = 64


def _sc_mesh():
    return plsc.VectorSubcoreMesh(core_axis_name="c", subcore_axis_name="s")


def _row_buffers():
    return ([pltpu.VMEM((SC_CHUNK, HALF), U32)] * 2 + [pltpu.SemaphoreType.DMA] * 5)


def _dispatch_rows(h2p, dest2d):
    per_w = N_GRP // SC_WORKERS
    nchunk = per_w // SC_CHUNK
    rows_per_k = N_GRP // SC_CHUNK
    assert nchunk % 2 == 0

    @functools.partial(
        pl.kernel, mesh=_sc_mesh(), out_type=jax.ShapeDtypeStruct((P_ROWS, HALF), U32),
        scratch_types=[pltpu.VMEM((TOP_K * nchunk, SC_CHUNK), I32)] + _row_buffers(),
        name="moe_dispatch")
    def k(src_hbm, dest_hbm, out_hbm, idx_v, buf0, buf1, isem, l0, l1, s0, s1):
        wid = lax.axis_index("s") * SC_CORES + lax.axis_index("c")
        bufs, lsem, ssem = (buf0, buf1), (l0, l1), (s0, s1)
        idx_loads = [
            pltpu.make_async_copy(dest_hbm.at[pl.ds(kk * rows_per_k + wid * nchunk, nchunk)],
                                  idx_v.at[pl.ds(kk * nchunk, nchunk)], isem) for kk in range(TOP_K)]

        def load(j, b):
            return pltpu.make_async_copy(src_hbm.at[pl.ds(wid * per_w + j * SC_CHUNK, SC_CHUNK)], bufs[b], lsem[b])

        def scatters(j, b):
            return [pltpu.make_async_copy(bufs[b], out_hbm.at[idx_v.at[kk * nchunk + j]], ssem[b])
                    for kk in range(TOP_K)]

        for cp in idx_loads:
            cp.start()
        load(0, 0).start()
        for cp in idx_loads:
            cp.wait()

        @pl.loop(0, nchunk // 2)
        def _(i):
            for b in range(2):
                j = 2 * i + b

                @pl.when(j >= 1)
                def _():
                    for cp in scatters(j - 1, 1 - b):
                        cp.wait()

                @pl.when(j + 1 < nchunk)
                def _():
                    load(j + 1, 1 - b).start()

                load(j, b).wait()
                for cp in scatters(j, b):
                    cp.start()

        for cp in scatters(nchunk - 1, 1):
            cp.wait()

    return k(h2p, dest2d)


def _combine_rows(yb, dest2d):
    n_out = TOP_K * N_GRP
    per_w = n_out // SC_WORKERS
    nchunk = per_w // SC_CHUNK
    assert nchunk % 2 == 0

    @functools.partial(
        pl.kernel, mesh=_sc_mesh(), out_type=jax.ShapeDtypeStruct((n_out, HALF), U32),
        scratch_types=[pltpu.VMEM((nchunk, SC_CHUNK), I32)] + _row_buffers(),
        name="moe_combine")
    def k(tab_hbm, idx_hbm, out_hbm, idx_v, buf0, buf1, isem, g0, g1, s0, s1):
        wid = lax.axis_index("s") * SC_CORES + lax.axis_index("c")
        bufs, gsem, ssem = (buf0, buf1), (g0, g1), (s0, s1)
        idx_load = pltpu.make_async_copy(idx_hbm.at[pl.ds(wid * nchunk, nchunk)], idx_v, isem)

        def gather(j, b):
            return pltpu.make_async_copy(tab_hbm.at[idx_v.at[j]], bufs[b], gsem[b])

        def store(j, b):
            return pltpu.make_async_copy(bufs[b], out_hbm.at[pl.ds(wid * per_w + j * SC_CHUNK, SC_CHUNK)], ssem[b])

        idx_load.start()
        idx_load.wait()
        gather(0, 0).start()

        @pl.loop(0, nchunk // 2)
        def _(i):
            for b in range(2):
                j = 2 * i + b

                @pl.when(j >= 1)
                def _():
                    store(j - 1, 1 - b).wait()

                @pl.when(j + 1 < nchunk)
                def _():
                    gather(j + 1, 1 - b).start()

                gather(j, b).wait()
                store(j, b).start()

        store(nchunk - 1, 1).wait()

    return k(yb, dest2d)


def _lambda_kernel(p_ref, o_ref):
    p = p_ref[...]
    s1 = jnp.sum(p[0:1] * p[1:2], axis=-1, keepdims=True)
    s2 = jnp.sum(p[2:3] * p[3:4], axis=-1, keepdims=True)
    o_ref[...] = jnp.broadcast_to(jnp.exp(s1) - jnp.exp(s2) + LAMBDA_INIT, (1, LANES))


def kernel(x, c, w_ada, b_ada, norm1_g, w_in, lambda_q1, lambda_k1, lambda_q2, lambda_k2, diff_norm_g, w_alpha_up, b_alpha, gla_norm_g, w_branch_diff, w_branch_gla, w_out, norm2_g, w_router, b_router, w_gate_up, b_gate_up, w_down, b_down, final_norm_g):
    w_in0 = w_in[0]
    c_a, c_g = W_A, W_A + W_G
    w_a = w_in0[:, :c_a].astype(BF16)
    w_g = w_in0[:, c_a:c_g].astype(BF16)
    w_lr = jnp.pad(w_in0[:, c_g:c_g + GLA_RANK], ((0, 0), (0, LANES - GLA_RANK))).astype(BF16)
    w_gate = w_in0[:, c_g + GLA_RANK:].astype(BF16)
    w_up = jnp.pad(w_alpha_up[0], ((0, LANES - GLA_RANK), (0, 0)))
    lam_in = jnp.concatenate([lambda_q1, lambda_k1, lambda_q2, lambda_k2], axis=0)
    slopes = jnp.asarray(2.0 ** (-8.0 * np.arange(1, DIFF_HEADS + 1) / DIFF_HEADS), dtype=F32)

    mod = _modulation(c, w_ada[0], b_ada[0])
    mod4 = mod.reshape(N_MOD, BATCH, 1, D_MODEL)
    lam = pl.pallas_call(
        _lambda_kernel, out_shape=jax.ShapeDtypeStruct((1, LANES), F32), name="lambda")(lam_in)[0, :1]

    qkv_a, qkv_g, gates, glr = _in_proj(x, mod4, norm1_g, w_a, w_g, w_gate, w_lr)
    pos = jnp.arange(SEQ, dtype=I32)
    p_hi, p_lo = (pos >> 3).astype(F32), (pos & 7).astype(F32)
    kaug = jnp.zeros((SEQ, LANES), F32).at[:, 0].set(p_hi).at[:, 1].set(p_hi).at[:, 2].set(p_lo).at[:, 3].set(
        p_lo).astype(BF16)
    o_a = _diff_attention(qkv_a, slopes, lam, kaug, diff_norm_g)
    o_b = _gla(qkv_g, glr, w_up, b_alpha, gla_norm_g)

    merge_args = (o_a.reshape(N_TOK, -1), o_b.reshape(N_TOK, -1), gates.reshape(N_TOK, W_GATE),
                  x.reshape(N_TOK, D_MODEL), mod4, norm2_g,
                  w_branch_diff[0].astype(BF16), w_branch_gla[0].astype(BF16), w_out[0].astype(BF16),
                  w_router[0].T, b_router[0].reshape(N_EXPERTS, 1))
    fin_g = final_norm_g.reshape(1, D_MODEL)

    routed = []
    for grp in range(MOE_GROUPS):
        x1, h2p, eidx, rank, wts, cnt = _merge_route(grp, *merge_args)
        dest8, be, nv = _route_plan(cnt, eidx, rank)
        dest = dest8[:TOP_K].reshape(-1, SC_CHUNK)
        routed.append((x1, wts, dest, be, nv, _dispatch_rows(h2p, dest)))
    gathered = []
    for x1, wts, dest, be, nv, xb in routed:
        yb = _expert_ffn(be[0, :N_BLK], nv[0, :1], xb, w_gate_up[0], b_gate_up[0], w_down[0], b_down[0])
        gathered.append(_combine_rows(yb, dest))
    out = None
    for grp, ((x1, wts, *_), yg) in enumerate(zip(routed, gathered)):
        out = _final(grp, x1, yg, wts, mod4, fin_g, out)
    return out.reshape(BATCH, SEQ, D_MODEL)
```

```python
import functools
import math

import jax
import jax.numpy as jnp
import numpy as np
from jax import lax
from jax.experimental import pallas as pl
from jax.experimental.pallas import tpu as pltpu
from jax.experimental.pallas import tpu_sc as plsc

F32 = jnp.float32
BF16 = jnp.bfloat16
U32 = jnp.uint32
I32 = jnp.int32

D_MODEL = 1024
BATCH = 16
SEQ = 2048
N_TOK = BATCH * SEQ
CHUNK = 64
DIFF_HEADS = 4
DIFF_DH = 64
GLA_HEADS = 4
GLA_DK = 64
GLA_DV = 128
GLA_RANK = 16
GLA_GATE_NORM = 16.0
N_EXPERTS = 32
TOP_K = 4
D_FF = D_MODEL
SWIGLU_LIMIT = 7.0
SWIGLU_ALPHA = 1.702
N_MOD = 6
EPS = 1e-6
LAMBDA_INIT = 0.8 - 0.6 * math.exp(-0.3 * 0)

LANES = 128
HALF = D_MODEL // 2
ROW_W = HALF
ROW_DT = U32

TM_IN = 512
TQ = 512
ATTN_STRIPS = 4
ATTN_W = 2 * TQ // ATTN_STRIPS
VT_ROWS = 2 * DIFF_DH + 16
LOG2E = math.log2(math.e)
TM_MERGE = 512
BLK = 512
MOE_GROUPS = 1
N_GRP = N_TOK // MOE_GROUPS
N_BLK = (N_GRP * TOP_K) // BLK + N_EXPERTS
P_ROWS = N_BLK * BLK
TM_FIN = 512
VMEM_LIMIT = 56 * 1024 * 1024


def _nt_dot(a, b):
    return lax.dot_general(a, b, (((1,), (1,)), ((), ())), preferred_element_type=F32)


def _tn_dot(a, b):
    return lax.dot_general(a, b, (((0,), (0,)), ((), ())), preferred_element_type=F32)


def _split3(x):
    hi = x.astype(BF16)
    r1 = x - hi.astype(F32)
    mid = r1.astype(BF16)
    lo = (r1 - mid.astype(F32)).astype(BF16)
    return hi, mid, lo


def _pack_halves(y):
    return pltpu.pack_elementwise([y[:, :HALF], y[:, HALF:]], packed_dtype=BF16)


def _unpack_halves(u):
    lo = pltpu.unpack_elementwise(u, index=0, packed_dtype=BF16, unpacked_dtype=F32)
    hi = pltpu.unpack_elementwise(u, index=1, packed_dtype=BF16, unpacked_dtype=F32)
    return lo, hi


def _mod_kernel(c_ref, w_ref, b_ref, o_ref):
    c = c_ref[...]
    s = c * jax.nn.sigmoid(c)
    o_ref[0] = jnp.dot(s.astype(BF16), w_ref[...].astype(BF16),
                       preferred_element_type=F32) + b_ref[...]


def _modulation(c, w_ada, b_ada):
    return pl.pallas_call(
        _mod_kernel,
        grid=(N_MOD,),
        in_specs=[
            pl.BlockSpec((BATCH, D_MODEL), lambda j: (0, 0)),
            pl.BlockSpec((D_MODEL, D_MODEL), lambda j: (0, j)),
            pl.BlockSpec((1, D_MODEL), lambda j: (0, j)),
        ],
        out_specs=pl.BlockSpec((1, BATCH, D_MODEL), lambda j: (j, 0, 0)),
        out_shape=jax.ShapeDtypeStruct((N_MOD, BATCH, D_MODEL), F32),
        compiler_params=pltpu.CompilerParams(dimension_semantics=("arbitrary",)),
        name="adaln_mod",
    )(c, w_ada, b_ada.reshape(1, N_MOD * D_MODEL))


W_A = 3 * DIFF_HEADS * 2 * DIFF_DH
W_G = 2 * GLA_HEADS * GLA_DK + 2 * GLA_HEADS * GLA_DV
W_GATE = 2 * D_MODEL


def _in_kernel(x_ref, sh_ref, sc_ref, g_ref, wa_ref, wg_ref, wgate_ref, wlr_ref,
               oa_ref, og_ref, ogate_ref, olr_ref):
    x = x_ref[...]
    inv = lax.rsqrt(jnp.mean(x * x, axis=-1, keepdims=True) + EPS)
    h = (x * inv * g_ref[...]) * (1.0 + sc_ref[...]) + sh_ref[...]
    hb = h.astype(BF16)
    oa_ref[...] = jnp.dot(hb, wa_ref[...], preferred_element_type=F32).astype(BF16)
    og_ref[...] = jnp.dot(hb, wg_ref[...], preferred_element_type=F32).astype(BF16)
    ogate_ref[...] = jnp.dot(hb, wgate_ref[...], preferred_element_type=F32).astype(BF16)
    olr_ref[...] = jnp.dot(hb, wlr_ref[...], preferred_element_type=F32)


def _in_proj(x, mod4, norm1_g, w_a, w_g, w_gate, w_lr):
    nrow = SEQ // TM_IN
    full = lambda shape: pl.BlockSpec(shape, lambda b, i: (0,) * len(shape))
    return pl.pallas_call(
        _in_kernel,
        grid=(BATCH, nrow),
        in_specs=[
            pl.BlockSpec((None, TM_IN, D_MODEL), lambda b, i: (b, i, 0)),
            pl.BlockSpec((None, None, 1, D_MODEL), lambda b, i: (0, b, 0, 0)),
            pl.BlockSpec((None, None, 1, D_MODEL), lambda b, i: (1, b, 0, 0)),
            full((1, D_MODEL)),
            full((D_MODEL, W_A)),
            full((D_MODEL, W_G)),
            full((D_MODEL, W_GATE)),
            full((D_MODEL, LANES)),
        ],
        out_specs=[
            pl.BlockSpec((None, TM_IN, W_A), lambda b, i: (b, i, 0)),
            pl.BlockSpec((None, TM_IN, W_G), lambda b, i: (b, i, 0)),
            pl.BlockSpec((None, TM_IN, W_GATE), lambda b, i: (b, i, 0)),
            pl.BlockSpec((None, TM_IN, LANES), lambda b, i: (b, i, 0)),
        ],
        out_shape=[
            jax.ShapeDtypeStruct((BATCH, SEQ, W_A), BF16),
            jax.ShapeDtypeStruct((BATCH, SEQ, W_G), BF16),
            jax.ShapeDtypeStruct((BATCH, SEQ, W_GATE), BF16),
            jax.ShapeDtypeStruct((BATCH, SEQ, LANES), F32),
        ],
        compiler_params=pltpu.CompilerParams(
            dimension_semantics=("arbitrary", "arbitrary"), vmem_limit_bytes=VMEM_LIMIT),
        name="in_proj",
    )(x, mod4, mod4, norm1_g, w_a, w_g, w_gate, w_lr)


def _attn_kernel(slope_ref, lam_ref, q_ref, k_ref, v_ref, kaug_ref, g_ref, o_ref,
                 sa_ref, sb_ref, lha_ref, lhb_ref, corr_ref, vt_ref, m_ref, acc_ref):
    c_alibi = slope_ref[pl.program_id(1)] * LOG2E
    lam = lam_ref[0]
    lane = lax.broadcasted_iota(I32, (TQ, 2 * DIFF_DH), 1)
    c_vec = jnp.full((TQ, 2 * DIFF_DH), c_alibi, F32)
    c_hi = c_vec.astype(BF16).astype(F32)
    c_lo = (c_vec - c_hi).astype(BF16).astype(F32)
    zero = jnp.zeros((TQ, 2 * DIFF_DH), BF16)
    aug = jnp.where(lane == 0, 8.0 * c_hi, jnp.where(lane == 1, 8.0 * c_lo,
                    jnp.where(lane == 2, c_hi, jnp.where(lane == 3, c_lo, 0.0)))).astype(BF16)

    def blk(j):
        return slice(j * TQ, (j + 1) * TQ)

    def stacked_queries(qi):
        q = (q_ref[blk(qi), :].astype(F32) * (DIFF_DH ** -0.5 * LOG2E)).astype(BF16)
        return jnp.concatenate([
            jnp.concatenate([jnp.where(lane < DIFF_DH, q, zero), aug], axis=1),
            jnp.concatenate([jnp.where(lane >= DIFF_DH, q, zero), aug], axis=1)], axis=0)

    lhs_refs = (lha_ref, lhb_ref)

    def strip(c):
        return slice(c * ATTN_W, (c + 1) * ATTN_W)

    def keys(j):
        return jnp.concatenate([k_ref[blk(j), :], kaug_ref[blk(j), :]], axis=1)

    def scores(kk, qi, c):
        return _nt_dot(kk, lhs_refs[qi % 2][strip(c), :])

    ones_rows = jnp.where(lax.broadcasted_iota(I32, (VT_ROWS - 2 * DIFF_DH, TQ), 0) == 0, 1.0, 0.0).astype(BF16)
    for j in range(SEQ // TQ):
        vt_ref[j, :2 * DIFF_DH, :] = v_ref[blk(j), :].astype(F32).T.astype(BF16)
        vt_ref[j, 2 * DIFF_DH:, :] = ones_rows

    def update(s, j, c, first):
        if first:
            m_new = jnp.max(s, axis=0, keepdims=True)
            p = jnp.exp2((s - m_new).astype(BF16))
            acc_ref[:, strip(c)] = jnp.dot(vt_ref[j], p, preferred_element_type=F32)
        else:
            m = m_ref[:, strip(c)]
            m_new = jnp.maximum(m, jnp.max(s, axis=0, keepdims=True))
            alpha = jnp.exp2(m - m_new)
            p = jnp.exp2((s - m_new).astype(BF16))
            acc_ref[:, strip(c)] = (alpha * acc_ref[:, strip(c)]
                                    + jnp.dot(vt_ref[j], p, preferred_element_type=F32))
        m_ref[:, strip(c)] = m_new

    kr = lax.broadcasted_iota(I32, (TQ, TQ), 0)
    qc = lax.broadcasted_iota(I32, (TQ, TQ), 1)
    ahead = jnp.maximum(kr - qc, 0).astype(F32)
    corr_ref[...] = jnp.where((qc >> 6) >= (kr >> 6), (-2.0 * c_alibi) * ahead, -jnp.inf)

    pairs = [(qi, j) for qi in range(SEQ // TQ) for j in range(qi + 1)]
    bufs = (sa_ref, sb_ref)
    lhs_refs[0][...] = stacked_queries(0)
    kk = keys(0)
    for c in range(ATTN_STRIPS):
        bufs[0][:, strip(c)] = scores(kk, 0, c)
    for t, (qi, j) in enumerate(pairs):
        nxt = pairs[t + 1] if t + 1 < len(pairs) else None
        if nxt is not None:
            if nxt[0] != qi:
                lhs_refs[nxt[0] % 2][...] = stacked_queries(nxt[0])
            kk = keys(nxt[1])
        for c in range(ATTN_STRIPS):
            if nxt is not None:
                bufs[(t + 1) % 2][:, strip(c)] = scores(kk, nxt[0], c)
            s = bufs[t % 2][:, strip(c)]
            if j == qi:
                lo = (c * ATTN_W) % TQ
                s = s + corr_ref[:, lo:lo + ATTN_W]
            update(s, j, c, first=(j == 0))
        if j == qi:
            ot = acc_ref[:2 * DIFF_DH, :] / acc_ref[2 * DIFF_DH:2 * DIFF_DH + 1, :]
            o = (ot[:, :TQ] - lam * ot[:, TQ:]).T
            inv = lax.rsqrt(jnp.mean(o * o, axis=-1, keepdims=True) + EPS)
            o_ref[blk(qi), :] = (o * inv * g_ref[...] * (1.0 - LAMBDA_INIT)).astype(BF16)


def _diff_attention(qkv_a, slopes, lam, kaug, diff_norm_g):
    return pl.pallas_call(
        _attn_kernel,
        grid=(BATCH, DIFF_HEADS),
        in_specs=[
            pl.BlockSpec(memory_space=pltpu.SMEM),
            pl.BlockSpec(memory_space=pltpu.SMEM),
            pl.BlockSpec((None, SEQ, LANES), lambda b, h: (b, 0, h)),
            pl.BlockSpec((None, SEQ, LANES), lambda b, h: (b, 0, DIFF_HEADS + h)),
            pl.BlockSpec((None, SEQ, LANES), lambda b, h: (b, 0, 2 * DIFF_HEADS + h)),
            pl.BlockSpec((SEQ, LANES), lambda b, h: (0, 0)),
            pl.BlockSpec((1, LANES), lambda b, h: (0, 0)),
        ],
        out_specs=pl.BlockSpec((None, SEQ, LANES), lambda b, h: (b, 0, h)),
        out_shape=jax.ShapeDtypeStruct((BATCH, SEQ, DIFF_HEADS * 2 * DIFF_DH), BF16),
        scratch_shapes=[
            pltpu.VMEM((TQ, 2 * TQ), F32), pltpu.VMEM((TQ, 2 * TQ), F32),
            pltpu.VMEM((2 * TQ, 4 * DIFF_DH), BF16), pltpu.VMEM((2 * TQ, 4 * DIFF_DH), BF16),
            pltpu.VMEM((TQ, TQ), F32),
            pltpu.VMEM((SEQ // TQ, VT_ROWS, TQ), BF16),
            pltpu.VMEM((1, 2 * TQ), F32),
            pltpu.VMEM((VT_ROWS, 2 * TQ), F32),
        ],
        compiler_params=pltpu.CompilerParams(
            dimension_semantics=("arbitrary", "arbitrary")),
        name="diff_attn",
    )(slopes, lam, qkv_a, qkv_a, qkv_a, kaug, diff_norm_g)


N_CHUNK = SEQ // CHUNK
PAIR = 2 * GLA_DK
PAIR_V = 2 * GLA_DV
CS_ROWS = 256
GLA_UNROLL = 8


def _gla_kernel(q_ref, k_ref, v_ref, r_ref, lr_ref, wup_ref, bup_ref, g_ref, o_ref,
                gcum_ref, state_ref):
    w_hi, w_mid, _ = _split3(wup_ref[...])
    rr = lax.broadcasted_iota(I32, (CS_ROWS, CS_ROWS), 0)
    cc = lax.broadcasted_iota(I32, (CS_ROWS, CS_ROWS), 1)
    tri = jnp.where(((rr >> 6) == (cc >> 6)) & (cc <= rr), 1.0, 0.0).astype(BF16)
    for blk in range(SEQ // CS_ROWS):
        rows = pl.ds(blk * CS_ROWS, CS_ROWS)
        a_hi, a_mid, _ = _split3(lr_ref[rows, :])
        z = (jnp.dot(a_hi, w_hi, preferred_element_type=F32)
             + jnp.dot(a_hi, w_mid, preferred_element_type=F32)
             + jnp.dot(a_mid, w_hi, preferred_element_type=F32)) + bup_ref[...]
        la = (jnp.minimum(z, 0.0) - jnp.log(1.0 + jnp.exp(-jnp.abs(z)))) * (1.0 / GLA_GATE_NORM)
        l_hi, l_mid, l_lo = _split3(la)
        gcum_ref[rows, :] = (jnp.dot(tri, l_hi, preferred_element_type=F32)
                             + jnp.dot(tri, l_mid, preferred_element_type=F32)
                             + jnp.dot(tri, l_lo, preferred_element_type=F32))

    state_ref[...] = jnp.zeros_like(state_ref)
    lane_k = lax.broadcasted_iota(I32, (1, PAIR), 1)
    row_v = lax.broadcasted_iota(I32, (PAIR_V, PAIR), 0)
    col_k = lax.broadcasted_iota(I32, (PAIR_V, PAIR), 1)
    same_head = (row_v >= GLA_DV) == (col_k >= GLA_DK)
    cr = lax.broadcasted_iota(I32, (CHUNK, CHUNK), 0)
    cs = lax.broadcasted_iota(I32, (CHUNK, CHUNK), 1)
    causal = cs <= cr
    scale = GLA_DK ** -0.5

    def chunk(n):
        rows = pl.ds(pl.multiple_of(n * CHUNK, CHUNK), CHUNK)
        gc = gcum_ref[rows, :]
        g_last = gcum_ref[pl.ds(n * CHUNK + CHUNK - 1, 1), :]
        qf = q_ref[rows, :].astype(F32) * scale
        kf = k_ref[rows, :].astype(F32)
        q_s = (qf * jnp.exp(gc)).astype(BF16)
        k_s = (kf * jnp.exp(-gc)).astype(BF16)
        k_d = (kf * jnp.exp(g_last - gc)).astype(BF16)
        decay = jnp.exp(g_last)
        for pr in range(GLA_HEADS // 2):
            kl = slice(pr * PAIR, (pr + 1) * PAIR)
            vl = slice(pr * PAIR_V, (pr + 1) * PAIR_V)
            qs_p, ks_p, kd_p = q_s[:, kl], k_s[:, kl], k_d[:, kl]
            v_p = v_ref[rows, vl]
            st = state_ref[pr]
            o_inter = _nt_dot(qs_p, st.astype(BF16))
            d_st = _tn_dot(v_p, kd_p)
            state_ref[pr] = st * decay[:, kl] + jnp.where(same_head, d_st, 0.0)
            for sub in range(2):
                hd = 2 * pr + sub
                in_head = (lane_k >= sub * GLA_DK) & (lane_k < (sub + 1) * GLA_DK)
                a = _nt_dot(jnp.where(in_head, qs_p, jnp.zeros_like(qs_p)), ks_p)
                a = jnp.where(causal, a, 0.0).astype(BF16)
                vs = slice(hd * GLA_DV, (hd + 1) * GLA_DV)
                o = (jnp.dot(a, v_ref[rows, vs], preferred_element_type=F32)
                     + o_inter[:, sub * GLA_DV:(sub + 1) * GLA_DV])
                inv = lax.rsqrt(jnp.mean(o * o, axis=-1, keepdims=True) + EPS)
                r = r_ref[rows, vs].astype(F32)
                o_ref[rows, vs] = (o * inv * g_ref[...] * (r * jax.nn.sigmoid(r))).astype(BF16)

    def chunk_group(t, _):
        for u in range(GLA_UNROLL):
            chunk(t * GLA_UNROLL + u)
        return 0

    lax.fori_loop(0, N_CHUNK // GLA_UNROLL, chunk_group, 0)


def _gla(qkv_g, glr, w_up, b_up, gla_norm_g):
    qk_w = GLA_HEADS * GLA_DK
    v_w = GLA_HEADS * GLA_DV
    return pl.pallas_call(
        _gla_kernel,
        grid=(BATCH,),
        in_specs=[
            pl.BlockSpec((None, SEQ, qk_w), lambda b: (b, 0, 0)),
            pl.BlockSpec((None, SEQ, qk_w), lambda b: (b, 0, 1)),
            pl.BlockSpec((None, SEQ, v_w), lambda b: (b, 0, 1)),
            pl.BlockSpec((None, SEQ, v_w), lambda b: (b, 0, 2)),
            pl.BlockSpec((None, SEQ, LANES), lambda b: (b, 0, 0)),
            pl.BlockSpec((LANES, qk_w), lambda b: (0, 0)),
            pl.BlockSpec((1, qk_w), lambda b: (0, 0)),
            pl.BlockSpec((1, GLA_DV), lambda b: (0, 0)),
        ],
        out_specs=pl.BlockSpec((None, SEQ, v_w), lambda b: (b, 0, 0)),
        out_shape=jax.ShapeDtypeStruct((BATCH, SEQ, v_w), BF16),
        scratch_shapes=[
            pltpu.VMEM((SEQ, qk_w), F32),
            pltpu.VMEM((GLA_HEADS // 2, PAIR_V, PAIR), F32),
        ],
        compiler_params=pltpu.CompilerParams(
            dimension_semantics=("arbitrary",), vmem_limit_bytes=VMEM_LIMIT),
        name="gla",
    )(qkv_g, qkv_g, qkv_g, qkv_g, glr, w_up, b_up, gla_norm_g)


def _merge_kernel(oa_ref, ob_ref, gate_ref, x_ref, gt1_ref, sh2_ref, sc2_ref, g2_ref,
                  wpa_ref, wpb_ref, wo_ref, wr_ref, br_ref,
                  x1_ref, h2p_ref, eidx_ref, rank_ref, wts_ref, cnt_ref,
                  upper_ref, carry_ref):
    i = pl.program_id(0)

    @pl.when(i == 0)
    def _():
        rr = lax.broadcasted_iota(I32, (TM_MERGE, TM_MERGE), 0)
        cc = lax.broadcasted_iota(I32, (TM_MERGE, TM_MERGE), 1)
        upper_ref[...] = jnp.where(rr < cc, 1.0, 0.0).astype(BF16)
        carry_ref[...] = jnp.zeros_like(carry_ref)

    ga = gate_ref[:, :D_MODEL].astype(F32)
    gb = gate_ref[:, D_MODEL:].astype(F32)
    merged = (jax.nn.sigmoid(ga) * jnp.dot(oa_ref[...], wpa_ref[...], preferred_element_type=F32)
              + jax.nn.sigmoid(gb) * jnp.dot(ob_ref[...], wpb_ref[...], preferred_element_type=F32))
    y = jnp.dot(merged.astype(BF16), wo_ref[...], preferred_element_type=F32)
    x1 = x_ref[...] + gt1_ref[...] * y
    x1_ref[...] = x1
    inv = lax.rsqrt(jnp.mean(x1 * x1, axis=-1, keepdims=True) + EPS)
    h2 = (x1 * inv * g2_ref[...]) * (1.0 + sc2_ref[...]) + sh2_ref[...]
    h2p_ref[...] = _pack_halves(h2)

    h_hi, h_mid, _ = _split3(h2)
    w_hi, w_mid, _ = _split3(wr_ref[...])
    logits = (_nt_dot(w_hi, h_hi) + _nt_dot(w_hi, h_mid) + _nt_dot(w_mid, h_hi)) + br_ref[...]

    eio = lax.broadcasted_iota(I32, (N_EXPERTS, TM_MERGE), 0)
    vals, idxs, sels = [], [], []
    cur = logits
    for _k in range(TOP_K):
        m = jnp.max(cur, axis=0, keepdims=True)
        idx = jnp.min(jnp.where(cur == m, eio, N_EXPERTS), axis=0, keepdims=True)
        sel = eio == idx
        vals.append(m)
        idxs.append(idx)
        sels.append(sel)
        cur = jnp.where(sel, -jnp.inf, cur)
    es = [jnp.exp(v - vals[0]) for v in vals]
    tot = es[0] + es[1] + es[2] + es[3]
    onehot = jnp.zeros((N_EXPERTS, TM_MERGE), F32)
    for sel in sels:
        onehot = onehot + jnp.where(sel, 1.0, 0.0)
    before = jnp.dot(onehot.astype(BF16), upper_ref[...], preferred_element_type=F32) + carry_ref[:, 0:1]
    ranks = [jnp.sum(jnp.where(sel, before, 0.0), axis=0, keepdims=True) for sel in sels]
    carry_ref[...] = carry_ref[...] + jnp.sum(onehot, axis=1, keepdims=True)
    cnt_ref[...] = carry_ref[...]

    zi = jnp.zeros((8 - TOP_K, TM_MERGE), I32)
    zf = jnp.zeros((8 - TOP_K, TM_MERGE), F32)
    eidx_ref[...] = jnp.concatenate(idxs + [zi], axis=0)
    rank_ref[...] = jnp.concatenate([r.astype(I32) for r in ranks] + [zi], axis=0)
    wts_ref[...] = jnp.concatenate([e / tot for e in es] + [zf], axis=0)


def _merge_route(grp, o_a, o_b, gates, x2d, mod4, norm2_g, w_pa, w_pb, w_o, w_rt, b_r):
    ntile = N_GRP // TM_MERGE
    first = grp * ntile
    per_b = SEQ // TM_MERGE
    full = lambda shape: pl.BlockSpec(shape, lambda i: (0,) * len(shape))
    row_in = lambda w: pl.BlockSpec((TM_MERGE, w), lambda i: (first + i, 0))
    row = lambda w: pl.BlockSpec((TM_MERGE, w), lambda i: (i, 0))
    modspec = lambda j: pl.BlockSpec((None, None, 1, D_MODEL), lambda i: (j, (first + i) // per_b, 0, 0))
    col = pl.BlockSpec((8, TM_MERGE), lambda i: (0, i))
    return pl.pallas_call(
        _merge_kernel,
        grid=(ntile,),
        in_specs=[
            row_in(DIFF_HEADS * 2 * DIFF_DH), row_in(GLA_HEADS * GLA_DV), row_in(W_GATE), row_in(D_MODEL),
            modspec(2), modspec(3), modspec(4),
            full((1, D_MODEL)),
            full((DIFF_HEADS * 2 * DIFF_DH, D_MODEL)), full((GLA_HEADS * GLA_DV, D_MODEL)),
            full((D_MODEL, D_MODEL)),
            full((N_EXPERTS, D_MODEL)), full((N_EXPERTS, 1)),
        ],
        out_specs=[
            row(D_MODEL), row(ROW_W), col, col, col,
            pl.BlockSpec((N_EXPERTS, LANES), lambda i: (0, 0)),
        ],
        out_shape=[
            jax.ShapeDtypeStruct((N_GRP, D_MODEL), F32),
            jax.ShapeDtypeStruct((N_GRP, ROW_W), ROW_DT),
            jax.ShapeDtypeStruct((8, N_GRP), I32),
            jax.ShapeDtypeStruct((8, N_GRP), I32),
            jax.ShapeDtypeStruct((8, N_GRP), F32),
            jax.ShapeDtypeStruct((N_EXPERTS, LANES), F32),
        ],
        scratch_shapes=[
            pltpu.VMEM((TM_MERGE, TM_MERGE), BF16),
            pltpu.VMEM((N_EXPERTS, LANES), F32),
        ],
        compiler_params=pltpu.CompilerParams(
            dimension_semantics=("arbitrary",), vmem_limit_bytes=VMEM_LIMIT),
        name="merge_route",
    )(o_a, o_b, gates, x2d, mod4, mod4, mod4, norm2_g, w_pa, w_pb, w_o, w_rt, b_r)


TP = 4096
NB_PAD = ((N_BLK + LANES - 1) // LANES) * LANES


def _plan_kernel(cnt_ref, eidx_ref, rank_ref, dest_ref, be_ref, nv_ref):
    cnt = cnt_ref[...]
    padded = jnp.floor((cnt + (BLK - 1.0)) * (1.0 / BLK)) * BLK
    er = lax.broadcasted_iota(I32, (N_EXPERTS, N_EXPERTS), 0)
    ec = lax.broadcasted_iota(I32, (N_EXPERTS, N_EXPERTS), 1)
    lower = jnp.where(ec < er, 1.0, 0.0).astype(BF16)
    p_hi, p_mid, p_lo = _split3(padded)
    starts = (jnp.dot(lower, p_hi, preferred_element_type=F32)
              + jnp.dot(lower, p_mid, preferred_element_type=F32)
              + jnp.dot(lower, p_lo, preferred_element_type=F32))
    ends = starts + padded
    blk_start = (lax.broadcasted_iota(I32, (1, NB_PAD), 1) * BLK).astype(F32)
    n_before = jnp.sum(jnp.where(ends[:, 0:1] <= blk_start, 1.0, 0.0), axis=0, keepdims=True)
    be_ref[...] = jnp.minimum(n_before, N_EXPERTS - 1.0).astype(I32)
    nv_ref[...] = (jnp.max(ends, axis=0, keepdims=True) * (1.0 / BLK)).astype(I32)

    eio = lax.broadcasted_iota(I32, (N_EXPERTS, TP), 0)
    rows = []
    for k in range(TOP_K):
        onehot = eio == eidx_ref[k:k + 1, :]
        base = jnp.sum(jnp.where(onehot, starts[:, 0:1], 0.0), axis=0, keepdims=True)
        rows.append(base.astype(I32) + rank_ref[k:k + 1, :])
    dest_ref[...] = jnp.concatenate(rows + [jnp.zeros((8 - TOP_K, TP), I32)], axis=0)


def _route_plan(cnt, eidx, rank):
    return pl.pallas_call(
        _plan_kernel,
        grid=(N_GRP // TP,),
        in_specs=[
            pl.BlockSpec((N_EXPERTS, LANES), lambda i: (0, 0)),
            pl.BlockSpec((8, TP), lambda i: (0, i)),
            pl.BlockSpec((8, TP), lambda i: (0, i)),
        ],
        out_specs=[
            pl.BlockSpec((8, TP), lambda i: (0, i)),
            pl.BlockSpec((1, NB_PAD), lambda i: (0, 0)),
            pl.BlockSpec((1, LANES), lambda i: (0, 0)),
        ],
        out_shape=[
            jax.ShapeDtypeStruct((8, N_GRP), I32),
            jax.ShapeDtypeStruct((1, NB_PAD), I32),
            jax.ShapeDtypeStruct((1, LANES), I32),
        ],
        compiler_params=pltpu.CompilerParams(dimension_semantics=("arbitrary",)),
        name="route_plan",
    )(cnt, eidx, rank)


def _ffn_kernel(be_ref, nv_ref, first_ref, slot_ref, nxt_ref, x_ref, wgu_hbm, bgu_ref, wd_hbm, bd_ref, y_ref,
                wgu_buf, wd_buf, sem):
    i = pl.program_id(0)
    valid = i < nv_ref[0]
    s = slot_ref[i]

    def weight_copies(e, slot):
        return (pltpu.make_async_copy(wgu_hbm.at[e], wgu_buf.at[slot], sem.at[slot, 0]),
                pltpu.make_async_copy(wd_hbm.at[e], wd_buf.at[slot], sem.at[slot, 1]))

    @pl.when(i == 0)
    def _():
        for cp in weight_copies(be_ref[0], 0):
            cp.start()

    @pl.when(valid & (first_ref[i] == 1))
    def _():
        for cp in weight_copies(be_ref[i], s):
            cp.wait()

        @pl.when(nxt_ref[i] >= 0)
        def _():
            for cp in weight_copies(nxt_ref[i], 1 - s):
                cp.start()

    @pl.when(valid)
    def _():
        lo, hi = _unpack_halves(x_ref[...])
        x = jnp.concatenate([lo, hi], axis=1).astype(BF16)
        mm = lambda a, w: lax.dot_general(a, w, (((1,), (0,)), ((), ())), preferred_element_type=F32)
        gu = mm(x, wgu_buf[s]) + bgu_ref[...]
        gate = jnp.minimum(gu[:, :D_FF], SWIGLU_LIMIT)
        up = jnp.clip(gu[:, D_FF:], -SWIGLU_LIMIT, SWIGLU_LIMIT)
        act = (up + 1.0) * (gate * jax.nn.sigmoid(SWIGLU_ALPHA * gate))
        y = mm(act.astype(BF16), wd_buf[s]) + bd_ref[...]
        y_ref[...] = _pack_halves(y)

    @pl.when(jnp.logical_not(valid))
    def _():
        y_ref[...] = jnp.zeros_like(y_ref)


def _expert_ffn(blk_expert, n_valid, xb, w_gate_up, b_gate_up, w_down, b_down):
    idx = jnp.arange(N_BLK, dtype=I32)
    used = idx < n_valid[0]
    first = (used & ((idx == 0) | (blk_expert != jnp.roll(blk_expert, 1)))).astype(I32)
    slot = (jnp.cumsum(first) - 1) & 1
    later = used[None, :] & (idx[None, :] > idx[:, None]) & (blk_expert[None, :] != blk_expert[:, None])
    nxt_pos = jnp.min(jnp.where(later, idx[None, :], N_BLK), axis=1)
    nxt = jnp.where(nxt_pos < N_BLK, blk_expert[jnp.minimum(nxt_pos, N_BLK - 1)], -1).astype(I32)

    blockwise = lambda i, *_: (i, 0)
    per_expert = lambda i, be, *_: (be[i], 0, 0)
    grid_spec = pltpu.PrefetchScalarGridSpec(
        num_scalar_prefetch=5,
        grid=(N_BLK,),
        in_specs=[
            pl.BlockSpec((BLK, ROW_W), blockwise),
            pl.BlockSpec(memory_space=pl.ANY),
            pl.BlockSpec((None, 1, 2 * D_FF), per_expert),
            pl.BlockSpec(memory_space=pl.ANY),
            pl.BlockSpec((None, 1, D_MODEL), per_expert),
        ],
        out_specs=pl.BlockSpec((BLK, ROW_W), blockwise),
        scratch_shapes=[
            pltpu.VMEM((2, D_MODEL, 2 * D_FF), F32),
            pltpu.VMEM((2, D_FF, D_MODEL), F32),
            pltpu.SemaphoreType.DMA((2, 2)),
        ],
    )
    return pl.pallas_call(
        _ffn_kernel,
        grid_spec=grid_spec,
        out_shape=jax.ShapeDtypeStruct((P_ROWS, ROW_W), ROW_DT),
        compiler_params=pltpu.CompilerParams(
            dimension_semantics=("arbitrary",), vmem_limit_bytes=VMEM_LIMIT),
        name="expert_ffn",
    )(blk_expert, n_valid, first, slot.astype(I32), nxt, xb, w_gate_up,
      b_gate_up.reshape(N_EXPERTS, 1, 2 * D_FF), w_down, b_down.reshape(N_EXPERTS, 1, D_MODEL))


def _final_kernel(x1_ref, y0_ref, y1_ref, y2_ref, y3_ref, w_ref, gt2_ref, g_ref, *rest):
    o_ref = rest[-1]
    w = w_ref[...].T
    ylo = jnp.zeros((TM_FIN, HALF), F32)
    yhi = jnp.zeros((TM_FIN, HALF), F32)
    for k, y_ref in enumerate((y0_ref, y1_ref, y2_ref, y3_ref)):
        lo, hi = _unpack_halves(y_ref[...])
        wk = w[:, k:k + 1]
        ylo = ylo + wk * lo
        yhi = yhi + wk * hi
    y = jnp.concatenate([ylo, yhi], axis=1)
    x2 = x1_ref[...] + gt2_ref[...] * y
    inv = lax.rsqrt(jnp.mean(x2 * x2, axis=-1, keepdims=True) + EPS)
    o_ref[...] = x2 * inv * g_ref[...]


def _final(grp, x1, yg, w4, mod4, final_norm_g, out_so_far):
    per_b = SEQ // TM_FIN
    ntile = N_GRP // TM_FIN
    first = grp * ntile
    slot = lambda k: pl.BlockSpec((TM_FIN, ROW_W), lambda i: (k * ntile + i, 0))
    in_specs = [
        pl.BlockSpec((TM_FIN, D_MODEL), lambda i: (i, 0)),
        slot(0), slot(1), slot(2), slot(3),
        pl.BlockSpec((8, TM_FIN), lambda i: (0, i)),
        pl.BlockSpec((None, None, 1, D_MODEL), lambda i: (5, (first + i) // per_b, 0, 0)),
        pl.BlockSpec((1, D_MODEL), lambda i: (0, 0)),
    ]
    args = [x1, yg, yg, yg, yg, w4, mod4, final_norm_g]
    aliases = {}
    if out_so_far is not None:
        in_specs.append(pl.BlockSpec(memory_space=pl.ANY))
        args.append(out_so_far)
        aliases = {len(args) - 1: 0}
    return pl.pallas_call(
        _final_kernel,
        grid=(ntile,),
        in_specs=in_specs,
        out_specs=pl.BlockSpec((TM_FIN, D_MODEL), lambda i: (first + i, 0)),
        out_shape=jax.ShapeDtypeStruct((N_TOK, D_MODEL), F32),
        input_output_aliases=aliases,
        compiler_params=pltpu.CompilerParams(
            dimension_semantics=("arbitrary",), vmem_limit_bytes=VMEM_LIMIT),
        name="combine_final",
    )(*args)


SC_CORES = 2
SC_SUBCORES = 16
SC_WORKERS = SC_CORES * SC_SUBCORES
SC_CHUNK = 64


def _sc_mesh():
    return plsc.VectorSubcoreMesh(core_axis_name="c", subcore_axis_name="s")


def _row_buffers():
    return ([pltpu.VMEM((SC_CHUNK, HALF), U32)] * 2 + [pltpu.SemaphoreType.DMA] * 5)


def _dispatch_rows(h2p, dest2d):
    per_w = N_GRP // SC_WORKERS
    nchunk = per_w // SC_CHUNK
    rows_per_k = N_GRP // SC_CHUNK
    assert nchunk % 2 == 0

    @functools.partial(
        pl.kernel, mesh=_sc_mesh(), out_type=jax.ShapeDtypeStruct((P_ROWS, HALF), U32),
        scratch_types=[pltpu.VMEM((TOP_K * nchunk, SC_CHUNK), I32)] + _row_buffers(),
        name="moe_dispatch")
    def k(src_hbm, dest_hbm, out_hbm, idx_v, buf0, buf1, isem, l0, l1, s0, s1):
        wid = lax.axis_index("s") * SC_CORES + lax.axis_index("c")
        bufs, lsem, ssem = (buf0, buf1), (l0, l1), (s0, s1)
        idx_loads = [
            pltpu.make_async_copy(dest_hbm.at[pl.ds(kk * rows_per_k + wid * nchunk, nchunk)],
                                  idx_v.at[pl.ds(kk * nchunk, nchunk)], isem) for kk in range(TOP_K)]

        def load(j, b):
            return pltpu.make_async_copy(src_hbm.at[pl.ds(wid * per_w + j * SC_CHUNK, SC_CHUNK)], bufs[b], lsem[b])

        def scatters(j, b):
            return [pltpu.make_async_copy(bufs[b], out_hbm.at[idx_v.at[kk * nchunk + j]], ssem[b])
                    for kk in range(TOP_K)]

        for cp in idx_loads:
            cp.start()
        load(0, 0).start()
        for cp in idx_loads:
            cp.wait()

        @pl.loop(0, nchunk // 2)
        def _(i):
            for b in range(2):
                j = 2 * i + b

                @pl.when(j >= 1)
                def _():
                    for cp in scatters(j - 1, 1 - b):
                        cp.wait()

                @pl.when(j + 1 < nchunk)
                def _():
                    load(j + 1, 1 - b).start()

                load(j, b).wait()
                for cp in scatters(j, b):
                    cp.start()

        for cp in scatters(nchunk - 1, 1):
            cp.wait()

    return k(h2p, dest2d)


def _combine_rows(yb, dest2d):
    n_out = TOP_K * N_GRP
    per_w = n_out // SC_WORKERS
    nchunk = per_w // SC_CHUNK
    assert nchunk % 2 == 0

    @functools.partial(
        pl.kernel, mesh=_sc_mesh(), out_type=jax.ShapeDtypeStruct((n_out, HALF), U32),
        scratch_types=[pltpu.VMEM((nchunk, SC_CHUNK), I32)] + _row_buffers(),
        name="moe_combine")
    def k(tab_hbm, idx_hbm, out_hbm, idx_v, buf0, buf1, isem, g0, g1, s0, s1):
        wid = lax.axis_index("s") * SC_CORES + lax.axis_index("c")
        bufs, gsem, ssem = (buf0, buf1), (g0, g1), (s0, s1)
        idx_load = pltpu.make_async_copy(idx_hbm.at[pl.ds(wid * nchunk, nchunk)], idx_v, isem)

        def gather(j, b):
            return pltpu.make_async_copy(tab_hbm.at[idx_v.at[j]], bufs[b], gsem[b])

        def store(j, b):
            return pltpu.make_async_copy(bufs[b], out_hbm.at[pl.ds(wid * per_w + j * SC_CHUNK, SC_CHUNK)], ssem[b])

        idx_load.start()
        idx_load.wait()
        gather(0, 0).start()

        @pl.loop(0, nchunk // 2)
        def _(i):
            for b in range(2):
                j = 2 * i + b

                @pl.when(j >= 1)
                def _():
                    store(j - 1, 1 - b).wait()

                @pl.when(j + 1 < nchunk)
                def _():
                    gather(j + 1, 1 - b).start()

                gather(j, b).wait()
                store(j, b).start()

        store(nchunk - 1, 1).wait()

    return k(yb, dest2d)


def _lambda_kernel(p_ref, o_ref):
    p = p_ref[...]
    s1 = jnp.sum(p[0:1] * p[1:2], axis=-1, keepdims=True)
    s2 = jnp.sum(p[2:3] * p[3:4], axis=-1, keepdims=True)
    o_ref[...] = jnp.broadcast_to(jnp.exp(s1) - jnp.exp(s2) + LAMBDA_INIT, (1, LANES))


def kernel(x, c, w_ada, b_ada, norm1_g, w_in, lambda_q1, lambda_k1, lambda_q2, lambda_k2, diff_norm_g, w_alpha_up, b_alpha, gla_norm_g, w_branch_diff, w_branch_gla, w_out, norm2_g, w_router, b_router, w_gate_up, b_gate_up, w_down, b_down, final_norm_g):
    w_in0 = w_in[0]
    c_a, c_g = W_A, W_A + W_G
    w_a = w_in0[:, :c_a].astype(BF16)
    w_g = w_in0[:, c_a:c_g].astype(BF16)
    w_lr = jnp.pad(w_in0[:, c_g:c_g + GLA_RANK], ((0, 0), (0, LANES - GLA_RANK))).astype(BF16)
    w_gate = w_in0[:, c_g + GLA_RANK:].astype(BF16)
    w_up = jnp.pad(w_alpha_up[0], ((0, LANES - GLA_RANK), (0, 0)))
    lam_in = jnp.concatenate([lambda_q1, lambda_k1, lambda_q2, lambda_k2], axis=0)
    slopes = jnp.asarray(2.0 ** (-8.0 * np.arange(1, DIFF_HEADS + 1) / DIFF_HEADS), dtype=F32)

    mod = _modulation(c, w_ada[0], b_ada[0])
    mod4 = mod.reshape(N_MOD, BATCH, 1, D_MODEL)
    lam = pl.pallas_call(
        _lambda_kernel, out_shape=jax.ShapeDtypeStruct((1, LANES), F32), name="lambda")(lam_in)[0, :1]

    qkv_a, qkv_g, gates, glr = _in_proj(x, mod4, norm1_g, w_a, w_g, w_gate, w_lr)
    pos = jnp.arange(SEQ, dtype=I32)
    p_hi, p_lo = (pos >> 3).astype(F32), (pos & 7).astype(F32)
    kaug = jnp.zeros((SEQ, LANES), F32).at[:, 0].set(p_hi).at[:, 1].set(p_hi).at[:, 2].set(p_lo).at[:, 3].set(
        p_lo).astype(BF16)
    o_a = _diff_attention(qkv_a, slopes, lam, kaug, diff_norm_g)
    o_b = _gla(qkv_g, glr, w_up, b_alpha, gla_norm_g)

    merge_args = (o_a.reshape(N_TOK, -1), o_b.reshape(N_TOK, -1), gates.reshape(N_TOK, W_GATE),
                  x.reshape(N_TOK, D_MODEL), mod4, norm2_g,
                  w_branch_diff[0].astype(BF16), w_branch_gla[0].astype(BF16), w_out[0].astype(BF16),
                  w_router[0].T, b_router[0].reshape(N_EXPERTS, 1))
    fin_g = final_norm_g.reshape(1, D_MODEL)

    routed = []
    for grp in range(MOE_GROUPS):
        x1, h2p, eidx, rank, wts, cnt = _merge_route(grp, *merge_args)
        dest8, be, nv = _route_plan(cnt, eidx, rank)
        dest = dest8[:TOP_K].reshape(-1, SC_CHUNK)
        routed.append((x1, wts, dest, be, nv, _dispatch_rows(h2p, dest)))
    gathered = []
    for x1, wts, dest, be, nv, xb in routed:
        yb = _expert_ffn(be[0, :N_BLK], nv[0, :1], xb, w_gate_up[0], b_gate_up[0], w_down[0], b_down[0])
        gathered.append(_combine_rows(yb, dest))
    out = None
    for grp, ((x1, wts, *_), yg) in enumerate(zip(routed, gathered)):
        out = _final(grp, x1, yg, wts, mod4, fin_g, out)
    return out.reshape(BATCH, SEQ, D_MODEL)
```

```python
import functools
import math

import jax
import jax.numpy as jnp
import numpy as np
from jax import lax
from jax.experimental import pallas as pl
from jax.experimental.pallas import tpu as pltpu
from jax.experimental.pallas import tpu_sc as plsc

F32 = jnp.float32
BF16 = jnp.bfloat16
U32 = jnp.uint32
I32 = jnp.int32

D_MODEL = 1024
BATCH = 16
SEQ = 2048
N_TOK = BATCH * SEQ
CHUNK = 64
DIFF_HEADS = 4
DIFF_DH = 64
GLA_HEADS = 4
GLA_DK = 64
GLA_DV = 128
GLA_RANK = 16
GLA_GATE_NORM = 16.0
N_EXPERTS = 32
TOP_K = 4
D_FF = D_MODEL
SWIGLU_LIMIT = 7.0
SWIGLU_ALPHA = 1.702
N_MOD = 6
EPS = 1e-6
LAMBDA_INIT = 0.8 - 0.6 * math.exp(-0.3 * 0)

LANES = 128
HALF = D_MODEL // 2
ROW_W = HALF
ROW_DT = U32

TM_IN = 512
TQ = 512
ATTN_STRIPS = 4
ATTN_W = 2 * TQ // ATTN_STRIPS
VT_ROWS = 2 * DIFF_DH + 16
LOG2E = math.log2(math.e)
TM_MERGE = 512
BLK = 512
MOE_GROUPS = 1
N_GRP = N_TOK // MOE_GROUPS
N_BLK = (N_GRP * TOP_K) // BLK + N_EXPERTS
P_ROWS = N_BLK * BLK
TM_FIN = 512
VMEM_LIMIT = 56 * 1024 * 1024
CHUNK_SHIFT = CHUNK.bit_length() - 1
POS_LO_BITS = 3
POS_LO = 1 << POS_LO_BITS


def _nt_dot(a, b):
    return lax.dot_general(a, b, (((1,), (1,)), ((), ())), preferred_element_type=F32)


def _tn_dot(a, b):
    return lax.dot_general(a, b, (((0,), (0,)), ((), ())), preferred_element_type=F32)


def _split3(x):
    hi = x.astype(BF16)
    r1 = x - hi.astype(F32)
    mid = r1.astype(BF16)
    lo = (r1 - mid.astype(F32)).astype(BF16)
    return hi, mid, lo


def _pack_halves(y):
    return pltpu.pack_elementwise([y[:, :HALF], y[:, HALF:]], packed_dtype=BF16)


def _unpack_halves(u):
    lo = pltpu.unpack_elementwise(u, index=0, packed_dtype=BF16, unpacked_dtype=F32)
    hi = pltpu.unpack_elementwise(u, index=1, packed_dtype=BF16, unpacked_dtype=F32)
    return lo, hi


def _mod_kernel(c_ref, w_ref, b_ref, o_ref):
    c = c_ref[...]
    s = c * jax.nn.sigmoid(c)
    o_ref[0] = jnp.dot(s.astype(BF16), w_ref[...].astype(BF16),
                       preferred_element_type=F32) + b_ref[...]


def _modulation(c, w_ada, b_ada):
    return pl.pallas_call(
        _mod_kernel,
        grid=(N_MOD,),
        in_specs=[
            pl.BlockSpec((BATCH, D_MODEL), lambda j: (0, 0)),
            pl.BlockSpec((D_MODEL, D_MODEL), lambda j: (0, j)),
            pl.BlockSpec((1, D_MODEL), lambda j: (0, j)),
        ],
        out_specs=pl.BlockSpec((1, BATCH, D_MODEL), lambda j: (j, 0, 0)),
        out_shape=jax.ShapeDtypeStruct((N_MOD, BATCH, D_MODEL), F32),
        compiler_params=pltpu.CompilerParams(dimension_semantics=("arbitrary",)),
        name="adaln_mod",
    )(c, w_ada, b_ada.reshape(1, N_MOD * D_MODEL))


W_A = 3 * DIFF_HEADS * 2 * DIFF_DH
W_G = 2 * GLA_HEADS * GLA_DK + 2 * GLA_HEADS * GLA_DV
W_GATE = 2 * D_MODEL


def _in_kernel(x_ref, sh_ref, sc_ref, g_ref, wa_ref, wg_ref, wgate_ref, wlr_ref,
               oa_ref, og_ref, ogate_ref, olr_ref):
    x = x_ref[...]
    inv = lax.rsqrt(jnp.mean(x * x, axis=-1, keepdims=True) + EPS)
    h = (x * inv * g_ref[...]) * (1.0 + sc_ref[...]) + sh_ref[...]
    hb = h.astype(BF16)
    oa_ref[...] = jnp.dot(hb, wa_ref[...], preferred_element_type=F32).astype(BF16)
    og_ref[...] = jnp.dot(hb, wg_ref[...], preferred_element_type=F32).astype(BF16)
    ogate_ref[...] = jnp.dot(hb, wgate_ref[...], preferred_element_type=F32).astype(BF16)
    olr_ref[...] = jnp.dot(hb, wlr_ref[...], preferred_element_type=F32)


def _in_proj(x, mod4, norm1_g, w_a, w_g, w_gate, w_lr):
    nrow = SEQ // TM_IN
    full = lambda shape: pl.BlockSpec(shape, lambda b, i: (0,) * len(shape))
    return pl.pallas_call(
        _in_kernel,
        grid=(BATCH, nrow),
        in_specs=[
            pl.BlockSpec((None, TM_IN, D_MODEL), lambda b, i: (b, i, 0)),
            pl.BlockSpec((None, None, 1, D_MODEL), lambda b, i: (0, b, 0, 0)),
            pl.BlockSpec((None, None, 1, D_MODEL), lambda b, i: (1, b, 0, 0)),
            full((1, D_MODEL)),
            full((D_MODEL, W_A)),
            full((D_MODEL, W_G)),
            full((D_MODEL, W_GATE)),
            full((D_MODEL, LANES)),
        ],
        out_specs=[
            pl.BlockSpec((None, TM_IN, W_A), lambda b, i: (b, i, 0)),
            pl.BlockSpec((None, TM_IN, W_G), lambda b, i: (b, i, 0)),
            pl.BlockSpec((None, TM_IN, W_GATE), lambda b, i: (b, i, 0)),
            pl.BlockSpec((None, TM_IN, LANES), lambda b, i: (b, i, 0)),
        ],
        out_shape=[
            jax.ShapeDtypeStruct((BATCH, SEQ, W_A), BF16),
            jax.ShapeDtypeStruct((BATCH, SEQ, W_G), BF16),
            jax.ShapeDtypeStruct((BATCH, SEQ, W_GATE), BF16),
            jax.ShapeDtypeStruct((BATCH, SEQ, LANES), F32),
        ],
        compiler_params=pltpu.CompilerParams(
            dimension_semantics=("arbitrary", "arbitrary"), vmem_limit_bytes=VMEM_LIMIT),
        name="in_proj",
    )(x, mod4, mod4, norm1_g, w_a, w_g, w_gate, w_lr)


def _attn_kernel(slope_ref, lam_ref, q_ref, k_ref, v_ref, kaug_ref, g_ref, o_ref,
                 sa_ref, sb_ref, lha_ref, lhb_ref, corr_ref, vt_ref, m_ref, acc_ref):
    c_alibi = slope_ref[pl.program_id(1)] * LOG2E
    lam = lam_ref[0]
    lane = lax.broadcasted_iota(I32, (TQ, 2 * DIFF_DH), 1)
    c_vec = jnp.full((TQ, 2 * DIFF_DH), c_alibi, F32)
    c_hi = c_vec.astype(BF16).astype(F32)
    c_lo = (c_vec - c_hi).astype(BF16).astype(F32)
    zero = jnp.zeros((TQ, 2 * DIFF_DH), BF16)
    aug = jnp.where(lane == 0, POS_LO * c_hi, jnp.where(lane == 1, POS_LO * c_lo,
                    jnp.where(lane == 2, c_hi, jnp.where(lane == 3, c_lo, 0.0)))).astype(BF16)

    def blk(j):
        return slice(j * TQ, (j + 1) * TQ)

    def stacked_queries(qi):
        q = (q_ref[blk(qi), :].astype(F32) * (DIFF_DH ** -0.5 * LOG2E)).astype(BF16)
        return jnp.concatenate([
            jnp.concatenate([jnp.where(lane < DIFF_DH, q, zero), aug], axis=1),
            jnp.concatenate([jnp.where(lane >= DIFF_DH, q, zero), aug], axis=1)], axis=0)

    lhs_refs = (lha_ref, lhb_ref)

    def strip(c):
        return slice(c * ATTN_W, (c + 1) * ATTN_W)

    def keys(j):
        return jnp.concatenate([k_ref[blk(j), :], kaug_ref[blk(j), :]], axis=1)

    def scores(kk, qi, c):
        return _nt_dot(kk, lhs_refs[qi % 2][strip(c), :])

    ones_rows = jnp.where(lax.broadcasted_iota(I32, (VT_ROWS - 2 * DIFF_DH, TQ), 0) == 0, 1.0, 0.0).astype(BF16)
    for j in range(SEQ // TQ):
        vt_ref[j, :2 * DIFF_DH, :] = v_ref[blk(j), :].astype(F32).T.astype(BF16)
        vt_ref[j, 2 * DIFF_DH:, :] = ones_rows

    def update(s, j, c, first):
        nk = s.shape[0]
        if first:
            m_new = jnp.max(s, axis=0, keepdims=True)
            p = jnp.exp2((s - m_new).astype(BF16))
            acc_ref[:, strip(c)] = jnp.dot(vt_ref[j, :, :nk], p, preferred_element_type=F32)
        else:
            m = m_ref[:, strip(c)]
            m_new = jnp.maximum(m, jnp.max(s, axis=0, keepdims=True))
            alpha = jnp.exp2(m - m_new)
            p = jnp.exp2((s - m_new).astype(BF16))
            acc_ref[:, strip(c)] = (alpha * acc_ref[:, strip(c)]
                                    + jnp.dot(vt_ref[j, :, :nk], p, preferred_element_type=F32))
        m_ref[:, strip(c)] = m_new

    def n_keys(qi, j, c):
        return TQ if j < qi else min(TQ, (c * ATTN_W) % TQ + ATTN_W)

    kr = lax.broadcasted_iota(I32, (TQ, TQ), 0)
    qc = lax.broadcasted_iota(I32, (TQ, TQ), 1)
    ahead = jnp.maximum(kr - qc, 0).astype(F32)
    corr_ref[...] = jnp.where((qc >> CHUNK_SHIFT) >= (kr >> CHUNK_SHIFT), (-2.0 * c_alibi) * ahead, -jnp.inf)

    pairs = [(qi, j) for qi in range(SEQ // TQ) for j in range(qi + 1)]
    bufs = (sa_ref, sb_ref)
    lhs_refs[0][...] = stacked_queries(0)
    kk = keys(0)
    for c in range(ATTN_STRIPS):
        nk = n_keys(0, 0, c)
        bufs[0][:nk, strip(c)] = scores(kk[:nk], 0, c)
    for t, (qi, j) in enumerate(pairs):
        nxt = pairs[t + 1] if t + 1 < len(pairs) else None
        if nxt is not None:
            if nxt[0] != qi:
                lhs_refs[nxt[0] % 2][...] = stacked_queries(nxt[0])
            kk = keys(nxt[1])
        for c in range(ATTN_STRIPS):
            if nxt is not None:
                nk = n_keys(*nxt, c)
                bufs[(t + 1) % 2][:nk, strip(c)] = scores(kk[:nk], nxt[0], c)
            nk = n_keys(qi, j, c)
            s = bufs[t % 2][:nk, strip(c)]
            if j == qi:
                lo = (c * ATTN_W) % TQ
                s = s + corr_ref[:nk, lo:lo + ATTN_W]
            update(s, j, c, first=(j == 0))
        if j == qi:
            ot = acc_ref[:2 * DIFF_DH, :] / acc_ref[2 * DIFF_DH:2 * DIFF_DH + 1, :]
            o = (ot[:, :TQ] - lam * ot[:, TQ:]).T
            inv = lax.rsqrt(jnp.mean(o * o, axis=-1, keepdims=True) + EPS)
            o_ref[blk(qi), :] = (o * inv * g_ref[...] * (1.0 - LAMBDA_INIT)).astype(BF16)


def _diff_attention(qkv_a, slopes, lam, kaug, diff_norm_g):
    return pl.pallas_call(
        _attn_kernel,
        grid=(BATCH, DIFF_HEADS),
        in_specs=[
            pl.BlockSpec(memory_space=pltpu.SMEM),
            pl.BlockSpec(memory_space=pltpu.SMEM),
            pl.BlockSpec((None, SEQ, LANES), lambda b, h: (b, 0, h)),
            pl.BlockSpec((None, SEQ, LANES), lambda b, h: (b, 0, DIFF_HEADS + h)),
            pl.BlockSpec((None, SEQ, LANES), lambda b, h: (b, 0, 2 * DIFF_HEADS + h)),
            pl.BlockSpec((SEQ, LANES), lambda b, h: (0, 0)),
            pl.BlockSpec((1, LANES), lambda b, h: (0, 0)),
        ],
        out_specs=pl.BlockSpec((None, SEQ, LANES), lambda b, h: (b, 0, h)),
        out_shape=jax.ShapeDtypeStruct((BATCH, SEQ, DIFF_HEADS * 2 * DIFF_DH), BF16),
        scratch_shapes=[
            pltpu.VMEM((TQ, 2 * TQ), F32), pltpu.VMEM((TQ, 2 * TQ), F32),
            pltpu.VMEM((2 * TQ, 4 * DIFF_DH), BF16), pltpu.VMEM((2 * TQ, 4 * DIFF_DH), BF16),
            pltpu.VMEM((TQ, TQ), F32),
            pltpu.VMEM((SEQ // TQ, VT_ROWS, TQ), BF16),
            pltpu.VMEM((1, 2 * TQ), F32),
            pltpu.VMEM((VT_ROWS, 2 * TQ), F32),
        ],
        compiler_params=pltpu.CompilerParams(
            dimension_semantics=("arbitrary", "arbitrary")),
        name="diff_attn",
    )(slopes, lam, qkv_a, qkv_a, qkv_a, kaug, diff_norm_g)


N_CHUNK = SEQ // CHUNK
PAIR = 2 * GLA_DK
PAIR_V = 2 * GLA_DV
CS_ROWS = 256
GLA_UNROLL = 8


def _gla_kernel(q_ref, k_ref, v_ref, r_ref, lr_ref, wup_ref, bup_ref, g_ref, o_ref,
                gcum_ref, state_ref):
    w_hi, w_mid, _ = _split3(wup_ref[...])
    rr = lax.broadcasted_iota(I32, (CS_ROWS, CS_ROWS), 0)
    cc = lax.broadcasted_iota(I32, (CS_ROWS, CS_ROWS), 1)
    tri = jnp.where(((rr >> CHUNK_SHIFT) == (cc >> CHUNK_SHIFT)) & (cc <= rr), 1.0, 0.0).astype(BF16)
    for blk in range(SEQ // CS_ROWS):
        rows = pl.ds(blk * CS_ROWS, CS_ROWS)
        a_hi, a_mid, _ = _split3(lr_ref[rows, :])
        z = (jnp.dot(a_hi, w_hi, preferred_element_type=F32)
             + jnp.dot(a_hi, w_mid, preferred_element_type=F32)
             + jnp.dot(a_mid, w_hi, preferred_element_type=F32)) + bup_ref[...]
        la = (jnp.minimum(z, 0.0) - jnp.log(1.0 + jnp.exp(-jnp.abs(z)))) * (1.0 / GLA_GATE_NORM)
        l_hi, l_mid, l_lo = _split3(la)
        gcum_ref[rows, :] = (jnp.dot(tri, l_hi, preferred_element_type=F32)
                             + jnp.dot(tri, l_mid, preferred_element_type=F32)
                             + jnp.dot(tri, l_lo, preferred_element_type=F32))

    state_ref[...] = jnp.zeros_like(state_ref)
    lane_k = lax.broadcasted_iota(I32, (1, PAIR), 1)
    row_v = lax.broadcasted_iota(I32, (PAIR_V, PAIR), 0)
    col_k = lax.broadcasted_iota(I32, (PAIR_V, PAIR), 1)
    same_head = (row_v >= GLA_DV) == (col_k >= GLA_DK)
    cr = lax.broadcasted_iota(I32, (CHUNK, CHUNK), 0)
    cs = lax.broadcasted_iota(I32, (CHUNK, CHUNK), 1)
    causal = cs <= cr
    scale = GLA_DK ** -0.5

    def chunk(n):
        rows = pl.ds(pl.multiple_of(n * CHUNK, CHUNK), CHUNK)
        gc = gcum_ref[rows, :]
        g_last = gcum_ref[pl.ds(n * CHUNK + CHUNK - 1, 1), :]
        qf = q_ref[rows, :].astype(F32) * scale
        kf = k_ref[rows, :].astype(F32)
        q_s = (qf * jnp.exp(gc)).astype(BF16)
        k_s = (kf * jnp.exp(-gc)).astype(BF16)
        k_d = (kf * jnp.exp(g_last - gc)).astype(BF16)
        decay = jnp.exp(g_last)
        for pr in range(GLA_HEADS // 2):
            kl = slice(pr * PAIR, (pr + 1) * PAIR)
            vl = slice(pr * PAIR_V, (pr + 1) * PAIR_V)
            qs_p, ks_p, kd_p = q_s[:, kl], k_s[:, kl], k_d[:, kl]
            v_p = v_ref[rows, vl]
            st = state_ref[pr]
            o_inter = _nt_dot(qs_p, st.astype(BF16))
            d_st = _tn_dot(v_p, kd_p)
            state_ref[pr] = st * decay[:, kl] + jnp.where(same_head, d_st, 0.0)
            for sub in range(2):
                hd = 2 * pr + sub
                in_head = (lane_k >= sub * GLA_DK) & (lane_k < (sub + 1) * GLA_DK)
                a = _nt_dot(jnp.where(in_head, qs_p, jnp.zeros_like(qs_p)), ks_p)
                a = jnp.where(causal, a, 0.0).astype(BF16)
                vs = slice(hd * GLA_DV, (hd + 1) * GLA_DV)
                o = (jnp.dot(a, v_ref[rows, vs], preferred_element_type=F32)
                     + o_inter[:, sub * GLA_DV:(sub + 1) * GLA_DV])
                inv = lax.rsqrt(jnp.mean(o * o, axis=-1, keepdims=True) + EPS)
                r = r_ref[rows, vs].astype(F32)
                o_ref[rows, vs] = (o * inv * g_ref[...] * (r * jax.nn.sigmoid(r))).astype(BF16)

    def chunk_group(t, _):
        for u in range(GLA_UNROLL):
            chunk(t * GLA_UNROLL + u)
        return 0

    lax.fori_loop(0, N_CHUNK // GLA_UNROLL, chunk_group, 0)


def _gla(qkv_g, glr, w_up, b_up, gla_norm_g):
    qk_w = GLA_HEADS * GLA_DK
    v_w = GLA_HEADS * GLA_DV
    return pl.pallas_call(
        _gla_kernel,
        grid=(BATCH,),
        in_specs=[
            pl.BlockSpec((None, SEQ, qk_w), lambda b: (b, 0, 0)),
            pl.BlockSpec((None, SEQ, qk_w), lambda b: (b, 0, 1)),
            pl.BlockSpec((None, SEQ, v_w), lambda b: (b, 0, 1)),
            pl.BlockSpec((None, SEQ, v_w), lambda b: (b, 0, 2)),
            pl.BlockSpec((None, SEQ, LANES), lambda b: (b, 0, 0)),
            pl.BlockSpec((LANES, qk_w), lambda b: (0, 0)),
            pl.BlockSpec((1, qk_w), lambda b: (0, 0)),
            pl.BlockSpec((1, GLA_DV), lambda b: (0, 0)),
        ],
        out_specs=pl.BlockSpec((None, SEQ, v_w), lambda b: (b, 0, 0)),
        out_shape=jax.ShapeDtypeStruct((BATCH, SEQ, v_w), BF16),
        scratch_shapes=[
            pltpu.VMEM((SEQ, qk_w), F32),
            pltpu.VMEM((GLA_HEADS // 2, PAIR_V, PAIR), F32),
        ],
        compiler_params=pltpu.CompilerParams(
            dimension_semantics=("arbitrary",), vmem_limit_bytes=VMEM_LIMIT),
        name="gla",
    )(qkv_g, qkv_g, qkv_g, qkv_g, glr, w_up, b_up, gla_norm_g)


def _merge_kernel(oa_ref, ob_ref, gate_ref, x_ref, gt1_ref, sh2_ref, sc2_ref, g2_ref,
                  wpa_ref, wpb_ref, wo_ref, wr_ref, br_ref,
                  x1_ref, h2p_ref, eidx_ref, rank_ref, wts_ref, cnt_ref,
                  upper_ref, carry_ref):
    i = pl.program_id(0)

    @pl.when(i == 0)
    def _():
        rr = lax.broadcasted_iota(I32, (TM_MERGE, TM_MERGE), 0)
        cc = lax.broadcasted_iota(I32, (TM_MERGE, TM_MERGE), 1)
        upper_ref[...] = jnp.where(rr < cc, 1.0, 0.0).astype(BF16)
        carry_ref[...] = jnp.zeros_like(carry_ref)

    ga = gate_ref[:, :D_MODEL].astype(F32)
    gb = gate_ref[:, D_MODEL:].astype(F32)
    merged = (jax.nn.sigmoid(ga) * jnp.dot(oa_ref[...], wpa_ref[...], preferred_element_type=F32)
              + jax.nn.sigmoid(gb) * jnp.dot(ob_ref[...], wpb_ref[...], preferred_element_type=F32))
    y = jnp.dot(merged.astype(BF16), wo_ref[...], preferred_element_type=F32)
    x1 = x_ref[...] + gt1_ref[...] * y
    x1_ref[...] = x1
    inv = lax.rsqrt(jnp.mean(x1 * x1, axis=-1, keepdims=True) + EPS)
    h2 = (x1 * inv * g2_ref[...]) * (1.0 + sc2_ref[...]) + sh2_ref[...]
    h2p_ref[...] = _pack_halves(h2)

    h_hi, h_mid, _ = _split3(h2)
    w_hi, w_mid, _ = _split3(wr_ref[...])
    logits = (_nt_dot(w_hi, h_hi) + _nt_dot(w_hi, h_mid) + _nt_dot(w_mid, h_hi)) + br_ref[...]

    eio = lax.broadcasted_iota(I32, (N_EXPERTS, TM_MERGE), 0)
    vals, idxs, sels = [], [], []
    cur = logits
    for _k in range(TOP_K):
        m = jnp.max(cur, axis=0, keepdims=True)
        idx = jnp.min(jnp.where(cur == m, eio, N_EXPERTS), axis=0, keepdims=True)
        sel = eio == idx
        vals.append(m)
        idxs.append(idx)
        sels.append(sel)
        cur = jnp.where(sel, -jnp.inf, cur)
    es = [jnp.exp(v - vals[0]) for v in vals]
    tot = es[0] + es[1] + es[2] + es[3]
    onehot = jnp.zeros((N_EXPERTS, TM_MERGE), F32)
    for sel in sels:
        onehot = onehot + jnp.where(sel, 1.0, 0.0)
    before = jnp.dot(onehot.astype(BF16), upper_ref[...], preferred_element_type=F32) + carry_ref[:, 0:1]
    ranks = [jnp.sum(jnp.where(sel, before, 0.0), axis=0, keepdims=True) for sel in sels]
    carry_ref[...] = carry_ref[...] + jnp.sum(onehot, axis=1, keepdims=True)
    cnt_ref[...] = carry_ref[...]

    zi = jnp.zeros((8 - TOP_K, TM_MERGE), I32)
    zf = jnp.zeros((8 - TOP_K, TM_MERGE), F32)
    eidx_ref[...] = jnp.concatenate(idxs + [zi], axis=0)
    rank_ref[...] = jnp.concatenate([r.astype(I32) for r in ranks] + [zi], axis=0)
    wts_ref[...] = jnp.concatenate([e / tot for e in es] + [zf], axis=0)


def _merge_route(grp, o_a, o_b, gates, x2d, mod4, norm2_g, w_pa, w_pb, w_o, w_rt, b_r):
    ntile = N_GRP // TM_MERGE
    first = grp * ntile
    per_b = SEQ // TM_MERGE
    full = lambda shape: pl.BlockSpec(shape, lambda i: (0,) * len(shape))
    row_in = lambda w: pl.BlockSpec((TM_MERGE, w), lambda i: (first + i, 0))
    row = lambda w: pl.BlockSpec((TM_MERGE, w), lambda i: (i, 0))
    modspec = lambda j: pl.BlockSpec((None, None, 1, D_MODEL), lambda i: (j, (first + i) // per_b, 0, 0))
    col = pl.BlockSpec((8, TM_MERGE), lambda i: (0, i))
    return pl.pallas_call(
        _merge_kernel,
        grid=(ntile,),
        in_specs=[
            row_in(DIFF_HEADS * 2 * DIFF_DH), row_in(GLA_HEADS * GLA_DV), row_in(W_GATE), row_in(D_MODEL),
            modspec(2), modspec(3), modspec(4),
            full((1, D_MODEL)),
            full((DIFF_HEADS * 2 * DIFF_DH, D_MODEL)), full((GLA_HEADS * GLA_DV, D_MODEL)),
            full((D_MODEL, D_MODEL)),
            full((N_EXPERTS, D_MODEL)), full((N_EXPERTS, 1)),
        ],
        out_specs=[
            row(D_MODEL), row(ROW_W), col, col, col,
            pl.BlockSpec((N_EXPERTS, LANES), lambda i: (0, 0)),
        ],
        out_shape=[
            jax.ShapeDtypeStruct((N_GRP, D_MODEL), F32),
            jax.ShapeDtypeStruct((N_GRP, ROW_W), ROW_DT),
            jax.ShapeDtypeStruct((8, N_GRP), I32),
            jax.ShapeDtypeStruct((8, N_GRP), I32),
            jax.ShapeDtypeStruct((8, N_GRP), F32),
            jax.ShapeDtypeStruct((N_EXPERTS, LANES), F32),
        ],
        scratch_shapes=[
            pltpu.VMEM((TM_MERGE, TM_MERGE), BF16),
            pltpu.VMEM((N_EXPERTS, LANES), F32),
        ],
        compiler_params=pltpu.CompilerParams(
            dimension_semantics=("arbitrary",), vmem_limit_bytes=VMEM_LIMIT),
        name="merge_route",
    )(o_a, o_b, gates, x2d, mod4, mod4, mod4, norm2_g, w_pa, w_pb, w_o, w_rt, b_r)


TP = 4096
NB_PAD = ((N_BLK + LANES - 1) // LANES) * LANES


def _plan_kernel(cnt_ref, eidx_ref, rank_ref, dest_ref, be_ref, nv_ref):
    cnt = cnt_ref[...]
    padded = jnp.floor((cnt + (BLK - 1.0)) * (1.0 / BLK)) * BLK
    er = lax.broadcasted_iota(I32, (N_EXPERTS, N_EXPERTS), 0)
    ec = lax.broadcasted_iota(I32, (N_EXPERTS, N_EXPERTS), 1)
    lower = jnp.where(ec < er, 1.0, 0.0).astype(BF16)
    p_hi, p_mid, p_lo = _split3(padded)
    starts = (jnp.dot(lower, p_hi, preferred_element_type=F32)
              + jnp.dot(lower, p_mid, preferred_element_type=F32)
              + jnp.dot(lower, p_lo, preferred_element_type=F32))
    ends = starts + padded
    blk_start = (lax.broadcasted_iota(I32, (1, NB_PAD), 1) * BLK).astype(F32)
    n_before = jnp.sum(jnp.where(ends[:, 0:1] <= blk_start, 1.0, 0.0), axis=0, keepdims=True)
    be_ref[...] = jnp.minimum(n_before, N_EXPERTS - 1.0).astype(I32)
    nv_ref[...] = (jnp.max(ends, axis=0, keepdims=True) * (1.0 / BLK)).astype(I32)

    eio = lax.broadcasted_iota(I32, (N_EXPERTS, TP), 0)
    rows = []
    for k in range(TOP_K):
        onehot = eio == eidx_ref[k:k + 1, :]
        base = jnp.sum(jnp.where(onehot, starts[:, 0:1], 0.0), axis=0, keepdims=True)
        rows.append(base.astype(I32) + rank_ref[k:k + 1, :])
    dest_ref[...] = jnp.concatenate(rows + [jnp.zeros((8 - TOP_K, TP), I32)], axis=0)


def _route_plan(cnt, eidx, rank):
    return pl.pallas_call(
        _plan_kernel,
        grid=(N_GRP // TP,),
        in_specs=[
            pl.BlockSpec((N_EXPERTS, LANES), lambda i: (0, 0)),
            pl.BlockSpec((8, TP), lambda i: (0, i)),
            pl.BlockSpec((8, TP), lambda i: (0, i)),
        ],
        out_specs=[
            pl.BlockSpec((8, TP), lambda i: (0, i)),
            pl.BlockSpec((1, NB_PAD), lambda i: (0, 0)),
            pl.BlockSpec((1, LANES), lambda i: (0, 0)),
        ],
        out_shape=[
            jax.ShapeDtypeStruct((8, N_GRP), I32),
            jax.ShapeDtypeStruct((1, NB_PAD), I32),
            jax.ShapeDtypeStruct((1, LANES), I32),
        ],
        compiler_params=pltpu.CompilerParams(dimension_semantics=("arbitrary",)),
        name="route_plan",
    )(cnt, eidx, rank)


def _ffn_kernel(be_ref, nv_ref, first_ref, slot_ref, nxt_ref, x_ref, wgu_hbm, bgu_ref, wd_hbm, bd_ref, y_ref,
                wgu_buf, wd_buf, sem):
    i = pl.program_id(0)
    valid = i < nv_ref[0]
    s = slot_ref[i]

    def weight_copies(e, slot):
        return (pltpu.make_async_copy(wgu_hbm.at[e], wgu_buf.at[slot], sem.at[slot, 0]),
                pltpu.make_async_copy(wd_hbm.at[e], wd_buf.at[slot], sem.at[slot, 1]))

    @pl.when(i == 0)
    def _():
        for cp in weight_copies(be_ref[0], 0):
            cp.start()

    @pl.when(valid & (first_ref[i] == 1))
    def _():
        for cp in weight_copies(be_ref[i], s):
            cp.wait()

        @pl.when(nxt_ref[i] >= 0)
        def _():
            for cp in weight_copies(nxt_ref[i], 1 - s):
                cp.start()

    @pl.when(valid)
    def _():
        lo, hi = _unpack_halves(x_ref[...])
        x = jnp.concatenate([lo, hi], axis=1).astype(BF16)
        mm = lambda a, w: lax.dot_general(a, w, (((1,), (0,)), ((), ())), preferred_element_type=F32)
        gu = mm(x, wgu_buf[s]) + bgu_ref[...]
        gate = jnp.minimum(gu[:, :D_FF], SWIGLU_LIMIT)
        up = jnp.clip(gu[:, D_FF:], -SWIGLU_LIMIT, SWIGLU_LIMIT)
        act = (up + 1.0) * (gate * jax.nn.sigmoid(SWIGLU_ALPHA * gate))
        y = mm(act.astype(BF16), wd_buf[s]) + bd_ref[...]
        y_ref[...] = _pack_halves(y)

    @pl.when(jnp.logical_not(valid))
    def _():
        y_ref[...] = jnp.zeros_like(y_ref)


def _expert_ffn(blk_expert, n_valid, xb, w_gate_up, b_gate_up, w_down, b_down):
    idx = jnp.arange(N_BLK, dtype=I32)
    used = idx < n_valid[0]
    first = (used & ((idx == 0) | (blk_expert != jnp.roll(blk_expert, 1)))).astype(I32)
    slot = (jnp.cumsum(first) - 1) & 1
    later = used[None, :] & (idx[None, :] > idx[:, None]) & (blk_expert[None, :] != blk_expert[:, None])
    nxt_pos = jnp.min(jnp.where(later, idx[None, :], N_BLK), axis=1)
    nxt = jnp.where(nxt_pos < N_BLK, blk_expert[jnp.minimum(nxt_pos, N_BLK - 1)], -1).astype(I32)

    blockwise = lambda i, *_: (i, 0)
    per_expert = lambda i, be, *_: (be[i], 0, 0)
    grid_spec = pltpu.PrefetchScalarGridSpec(
        num_scalar_prefetch=5,
        grid=(N_BLK,),
        in_specs=[
            pl.BlockSpec((BLK, ROW_W), blockwise),
            pl.BlockSpec(memory_space=pl.ANY),
            pl.BlockSpec((None, 1, 2 * D_FF), per_expert),
            pl.BlockSpec(memory_space=pl.ANY),
            pl.BlockSpec((None, 1, D_MODEL), per_expert),
        ],
        out_specs=pl.BlockSpec((BLK, ROW_W), blockwise),
        scratch_shapes=[
            pltpu.VMEM((2, D_MODEL, 2 * D_FF), F32),
            pltpu.VMEM((2, D_FF, D_MODEL), F32),
            pltpu.SemaphoreType.DMA((2, 2)),
        ],
    )
    return pl.pallas_call(
        _ffn_kernel,
        grid_spec=grid_spec,
        out_shape=jax.ShapeDtypeStruct((P_ROWS, ROW_W), ROW_DT),
        compiler_params=pltpu.CompilerParams(
            dimension_semantics=("arbitrary",), vmem_limit_bytes=VMEM_LIMIT),
        name="expert_ffn",
    )(blk_expert, n_valid, first, slot.astype(I32), nxt, xb, w_gate_up,
      b_gate_up.reshape(N_EXPERTS, 1, 2 * D_FF), w_down, b_down.reshape(N_EXPERTS, 1, D_MODEL))


def _final_kernel(x1_ref, y0_ref, y1_ref, y2_ref, y3_ref, w_ref, gt2_ref, g_ref, *rest):
    o_ref = rest[-1]
    w = w_ref[...].T
    ylo = jnp.zeros((TM_FIN, HALF), F32)
    yhi = jnp.zeros((TM_FIN, HALF), F32)
    for k, y_ref in enumerate((y0_ref, y1_ref, y2_ref, y3_ref)):
        lo, hi = _unpack_halves(y_ref[...])
        wk = w[:, k:k + 1]
        ylo = ylo + wk * lo
        yhi = yhi + wk * hi
    y = jnp.concatenate([ylo, yhi], axis=1)
    x2 = x1_ref[...] + gt2_ref[...] * y
    inv = lax.rsqrt(jnp.mean(x2 * x2, axis=-1, keepdims=True) + EPS)
    o_ref[...] = x2 * inv * g_ref[...]


def _final(grp, x1, yg, w4, mod4, final_norm_g, out_so_far):
    per_b = SEQ // TM_FIN
    ntile = N_GRP // TM_FIN
    first = grp * ntile
    slot = lambda k: pl.BlockSpec((TM_FIN, ROW_W), lambda i: (k * ntile + i, 0))
    in_specs = [
        pl.BlockSpec((TM_FIN, D_MODEL), lambda i: (i, 0)),
        slot(0), slot(1), slot(2), slot(3),
        pl.BlockSpec((8, TM_FIN), lambda i: (0, i)),
        pl.BlockSpec((None, None, 1, D_MODEL), lambda i: (5, (first + i) // per_b, 0, 0)),
        pl.BlockSpec((1, D_MODEL), lambda i: (0, 0)),
    ]
    args = [x1, yg, yg, yg, yg, w4, mod4, final_norm_g]
    aliases = {}
    if out_so_far is not None:
        in_specs.append(pl.BlockSpec(memory_space=pl.ANY))
        args.append(out_so_far)
        aliases = {len(args) - 1: 0}
    return pl.pallas_call(
        _final_kernel,
        grid=(ntile,),
        in_specs=in_specs,
        out_specs=pl.BlockSpec((TM_FIN, D_MODEL), lambda i: (first + i, 0)),
        out_shape=jax.ShapeDtypeStruct((N_TOK, D_MODEL), F32),
        input_output_aliases=aliases,
        compiler_params=pltpu.CompilerParams(
            dimension_semantics=("arbitrary",), vmem_limit_bytes=VMEM_LIMIT),
        name="combine_final",
    )(*args)


SC_CORES = 2
SC_SUBCORES = 16
SC_WORKERS = SC_CORES * SC_SUBCORES
SC_CHUNK = 64


def _sc_mesh():
    return plsc.VectorSubcoreMesh(core_axis_name="c", subcore_axis_name="s")


def _row_buffers():
    return ([pltpu.VMEM((SC_CHUNK, HALF), U32)] * 2 + [pltpu.SemaphoreType.DMA] * 5)


def _dispatch_rows(h2p, dest2d):
    per_w = N_GRP // SC_WORKERS
    nchunk = per_w // SC_CHUNK
    rows_per_k = N_GRP // SC_CHUNK
    assert nchunk % 2 == 0

    @functools.partial(
        pl.kernel, mesh=_sc_mesh(), out_type=jax.ShapeDtypeStruct((P_ROWS, HALF), U32),
        scratch_types=[pltpu.VMEM((TOP_K * nchunk, SC_CHUNK), I32)] + _row_buffers(),
        name="moe_dispatch")
    def k(src_hbm, dest_hbm, out_hbm, idx_v, buf0, buf1, isem, l0, l1, s0, s1):
        wid = lax.axis_index("s") * SC_CORES + lax.axis_index("c")
        bufs, lsem, ssem = (buf0, buf1), (l0, l1), (s0, s1)
        idx_loads = [
            pltpu.make_async_copy(dest_hbm.at[pl.ds(kk * rows_per_k + wid * nchunk, nchunk)],
                                  idx_v.at[pl.ds(kk * nchunk, nchunk)], isem) for kk in range(TOP_K)]

        def load(j, b):
            return pltpu.make_async_copy(src_hbm.at[pl.ds(wid * per_w + j * SC_CHUNK, SC_CHUNK)], bufs[b], lsem[b])

        def scatters(j, b):
            return [pltpu.make_async_copy(bufs[b], out_hbm.at[idx_v.at[kk * nchunk + j]], ssem[b])
                    for kk in range(TOP_K)]

        for cp in idx_loads:
            cp.start()
        load(0, 0).start()
        for cp in idx_loads:
            cp.wait()

        @pl.loop(0, nchunk // 2)
        def _(i):
            for b in range(2):
                j = 2 * i + b

                @pl.when(j >= 1)
                def _():
                    for cp in scatters(j - 1, 1 - b):
                        cp.wait()

                @pl.when(j + 1 < nchunk)
                def _():
                    load(j + 1, 1 - b).start()

                load(j, b).wait()
                for cp in scatters(j, b):
                    cp.start()

        for cp in scatters(nchunk - 1, 1):
            cp.wait()

    return k(h2p, dest2d)


def _combine_rows(yb, dest2d):
    n_out = TOP_K * N_GRP
    per_w = n_out // SC_WORKERS
    nchunk = per_w // SC_CHUNK
    assert nchunk % 2 == 0

    @functools.partial(
        pl.kernel, mesh=_sc_mesh(), out_type=jax.ShapeDtypeStruct((n_out, HALF), U32),
        scratch_types=[pltpu.VMEM((nchunk, SC_CHUNK), I32)] + _row_buffers(),
        name="moe_combine")
    def k(tab_hbm, idx_hbm, out_hbm, idx_v, buf0, buf1, isem, g0, g1, s0, s1):
        wid = lax.axis_index("s") * SC_CORES + lax.axis_index("c")
        bufs, gsem, ssem = (buf0, buf1), (g0, g1), (s0, s1)
        idx_load = pltpu.make_async_copy(idx_hbm.at[pl.ds(wid * nchunk, nchunk)], idx_v, isem)

        def gather(j, b):
            return pltpu.make_async_copy(tab_hbm.at[idx_v.at[j]], bufs[b], gsem[b])

        def store(j, b):
            return pltpu.make_async_copy(bufs[b], out_hbm.at[pl.ds(wid * per_w + j * SC_CHUNK, SC_CHUNK)], ssem[b])

        idx_load.start()
        idx_load.wait()
        gather(0, 0).start()

        @pl.loop(0, nchunk // 2)
        def _(i):
            for b in range(2):
                j = 2 * i + b

                @pl.when(j >= 1)
                def _():
                    store(j - 1, 1 - b).wait()

                @pl.when(j + 1 < nchunk)
                def _():
                    gather(j + 1, 1 - b).start()

                gather(j, b).wait()
                store(j, b).start()

        store(nchunk - 1, 1).wait()

    return k(yb, dest2d)


def _lambda_kernel(p_ref, o_ref):
    p = p_ref[...]
    s1 = jnp.sum(p[0:1] * p[1:2], axis=-1, keepdims=True)
    s2 = jnp.sum(p[2:3] * p[3:4], axis=-1, keepdims=True)
    o_ref[...] = jnp.broadcast_to(jnp.exp(s1) - jnp.exp(s2) + LAMBDA_INIT, (1, LANES))


def kernel(x, c, w_ada, b_ada, norm1_g, w_in, lambda_q1, lambda_k1, lambda_q2, lambda_k2, diff_norm_g, w_alpha_up, b_alpha, gla_norm_g, w_branch_diff, w_branch_gla, w_out, norm2_g, w_router, b_router, w_gate_up, b_gate_up, w_down, b_down, final_norm_g):
    w_in0 = w_in[0]
    c_a, c_g = W_A, W_A + W_G
    w_a = w_in0[:, :c_a].astype(BF16)
    w_g = w_in0[:, c_a:c_g].astype(BF16)
    w_lr = jnp.pad(w_in0[:, c_g:c_g + GLA_RANK], ((0, 0), (0, LANES - GLA_RANK))).astype(BF16)
    w_gate = w_in0[:, c_g + GLA_RANK:].astype(BF16)
    w_up = jnp.pad(w_alpha_up[0], ((0, LANES - GLA_RANK), (0, 0)))
    lam_in = jnp.concatenate([lambda_q1, lambda_k1, lambda_q2, lambda_k2], axis=0)
    slopes = jnp.asarray(2.0 ** (-8.0 * np.arange(1, DIFF_HEADS + 1) / DIFF_HEADS), dtype=F32)

    mod = _modulation(c, w_ada[0], b_ada[0])
    mod4 = mod.reshape(N_MOD, BATCH, 1, D_MODEL)
    lam = pl.pallas_call(
        _lambda_kernel, out_shape=jax.ShapeDtypeStruct((1, LANES), F32), name="lambda")(lam_in)[0, :1]

    qkv_a, qkv_g, gates, glr = _in_proj(x, mod4, norm1_g, w_a, w_g, w_gate, w_lr)
    pos = jnp.arange(SEQ, dtype=I32)
    p_hi = (pos >> POS_LO_BITS).astype(F32)
    p_lo = (pos & (POS_LO - 1)).astype(F32)
    kaug = jnp.zeros((SEQ, LANES), F32).at[:, 0].set(p_hi).at[:, 1].set(p_hi).at[:, 2].set(p_lo).at[:, 3].set(
        p_lo).astype(BF16)
    o_a = _diff_attention(qkv_a, slopes, lam, kaug, diff_norm_g)
    o_b = _gla(qkv_g, glr, w_up, b_alpha, gla_norm_g)

    merge_args = (o_a.reshape(N_TOK, -1), o_b.reshape(N_TOK, -1), gates.reshape(N_TOK, W_GATE),
                  x.reshape(N_TOK, D_MODEL), mod4, norm2_g,
                  w_branch_diff[0].astype(BF16), w_branch_gla[0].astype(BF16), w_out[0].astype(BF16),
                  w_router[0].T, b_router[0].reshape(N_EXPERTS, 1))
    fin_g = final_norm_g.reshape(1, D_MODEL)

    routed = []
    for grp in range(MOE_GROUPS):
        x1, h2p, eidx, rank, wts, cnt = _merge_route(grp, *merge_args)
        dest8, be, nv = _route_plan(cnt, eidx, rank)
        dest = dest8[:TOP_K].reshape(-1, SC_CHUNK)
        routed.append((x1, wts, dest, be, nv, _dispatch_rows(h2p, dest)))
    gathered = []
    for x1, wts, dest, be, nv, xb in routed:
        yb = _expert_ffn(be[0, :N_BLK], nv[0, :1], xb, w_gate_up[0], b_gate_up[0], w_down[0], b_down[0])
        gathered.append(_combine_rows(yb, dest))
    out = None
    for grp, ((x1, wts, *_), yg) in enumerate(zip(routed, gathered)):
        out = _final(grp, x1, yg, wts, mod4, fin_g, out)
    return out.reshape(BATCH, SEQ, D_MODEL)
```

```python
import functools
import math

import jax
import jax.numpy as jnp
import numpy as np
from jax import lax
from jax.experimental import pallas as pl
from jax.experimental.pallas import tpu as pltpu
from jax.experimental.pallas import tpu_sc as plsc

F32 = jnp.float32
BF16 = jnp.bfloat16
U32 = jnp.uint32
I32 = jnp.int32

D_MODEL = 1024
BATCH = 16
SEQ = 2048
N_TOK = BATCH * SEQ
CHUNK = 64
DIFF_HEADS = 4
DIFF_DH = 64
GLA_HEADS = 4
GLA_DK = 64
GLA_DV = 128
GLA_RANK = 16
GLA_GATE_NORM = 16.0
N_EXPERTS = 32
TOP_K = 4
D_FF = D_MODEL
SWIGLU_LIMIT = 7.0
SWIGLU_ALPHA = 1.702
N_MOD = 6
EPS = 1e-6
LAMBDA_INIT = 0.8 - 0.6 * math.exp(-0.3 * 0)

LANES = 128
HALF = D_MODEL // 2
ROW_W = HALF
ROW_DT = U32

TM_IN = 512
TQ = 512
ATTN_STRIPS = 4
ATTN_W = 2 * TQ // ATTN_STRIPS
VT_ROWS = 2 * DIFF_DH + 16
LOG2E = math.log2(math.e)
TM_MERGE = 512
BLK = 512
MOE_GROUPS = 1
N_GRP = N_TOK // MOE_GROUPS
N_BLK = (N_GRP * TOP_K) // BLK + N_EXPERTS
P_ROWS = N_BLK * BLK
TM_FIN = 512
VMEM_LIMIT = 56 * 1024 * 1024
CHUNK_SHIFT = CHUNK.bit_length() - 1
POS_LO_BITS = 3
POS_LO = 1 << POS_LO_BITS


def _nt_dot(a, b):
    return lax.dot_general(a, b, (((1,), (1,)), ((), ())), preferred_element_type=F32)


def _tn_dot(a, b):
    return lax.dot_general(a, b, (((0,), (0,)), ((), ())), preferred_element_type=F32)


def _split3(x):
    hi = x.astype(BF16)
    r1 = x - hi.astype(F32)
    mid = r1.astype(BF16)
    lo = (r1 - mid.astype(F32)).astype(BF16)
    return hi, mid, lo


def _pack_halves(y):
    return pltpu.pack_elementwise([y[:, :HALF], y[:, HALF:]], packed_dtype=BF16)


def _unpack_halves(u):
    lo = pltpu.unpack_elementwise(u, index=0, packed_dtype=BF16, unpacked_dtype=F32)
    hi = pltpu.unpack_elementwise(u, index=1, packed_dtype=BF16, unpacked_dtype=F32)
    return lo, hi


def _mod_kernel(c_ref, w_ref, b_ref, o_ref):
    c = c_ref[...]
    s = c * jax.nn.sigmoid(c)
    o_ref[0] = jnp.dot(s.astype(BF16), w_ref[...].astype(BF16),
                       preferred_element_type=F32) + b_ref[...]


def _modulation(c, w_ada, b_ada):
    return pl.pallas_call(
        _mod_kernel,
        grid=(N_MOD,),
        in_specs=[
            pl.BlockSpec((BATCH, D_MODEL), lambda j: (0, 0)),
            pl.BlockSpec((D_MODEL, D_MODEL), lambda j: (0, j)),
            pl.BlockSpec((1, D_MODEL), lambda j: (0, j)),
        ],
        out_specs=pl.BlockSpec((1, BATCH, D_MODEL), lambda j: (j, 0, 0)),
        out_shape=jax.ShapeDtypeStruct((N_MOD, BATCH, D_MODEL), F32),
        compiler_params=pltpu.CompilerParams(dimension_semantics=("arbitrary",)),
        name="adaln_mod",
    )(c, w_ada, b_ada.reshape(1, N_MOD * D_MODEL))


W_A = 3 * DIFF_HEADS * 2 * DIFF_DH
W_G = 2 * GLA_HEADS * GLA_DK + 2 * GLA_HEADS * GLA_DV
W_GATE = 2 * D_MODEL


def _in_kernel(x_ref, sh_ref, sc_ref, g_ref, wa_ref, wg_ref, wgate_ref, wlr_ref,
               oa_ref, og_ref, ogate_ref, olr_ref):
    x = x_ref[...]
    inv = lax.rsqrt(jnp.mean(x * x, axis=-1, keepdims=True) + EPS)
    h = (x * inv * g_ref[...]) * (1.0 + sc_ref[...]) + sh_ref[...]
    hb = h.astype(BF16)
    oa_ref[...] = jnp.dot(hb, wa_ref[...], preferred_element_type=F32).astype(BF16)
    og_ref[...] = jnp.dot(hb, wg_ref[...], preferred_element_type=F32).astype(BF16)
    ogate_ref[...] = jnp.dot(hb, wgate_ref[...], preferred_element_type=F32).astype(BF16)
    olr_ref[...] = jnp.dot(hb, wlr_ref[...], preferred_element_type=F32)


def _in_proj(x, mod4, norm1_g, w_a, w_g, w_gate, w_lr):
    nrow = SEQ // TM_IN
    full = lambda shape: pl.BlockSpec(shape, lambda b, i: (0,) * len(shape))
    return pl.pallas_call(
        _in_kernel,
        grid=(BATCH, nrow),
        in_specs=[
            pl.BlockSpec((None, TM_IN, D_MODEL), lambda b, i: (b, i, 0)),
            pl.BlockSpec((None, None, 1, D_MODEL), lambda b, i: (0, b, 0, 0)),
            pl.BlockSpec((None, None, 1, D_MODEL), lambda b, i: (1, b, 0, 0)),
            full((1, D_MODEL)),
            full((D_MODEL, W_A)),
            full((D_MODEL, W_G)),
            full((D_MODEL, W_GATE)),
            full((D_MODEL, LANES)),
        ],
        out_specs=[
            pl.BlockSpec((None, TM_IN, W_A), lambda b, i: (b, i, 0)),
            pl.BlockSpec((None, TM_IN, W_G), lambda b, i: (b, i, 0)),
            pl.BlockSpec((None, TM_IN, W_GATE), lambda b, i: (b, i, 0)),
            pl.BlockSpec((None, TM_IN, LANES), lambda b, i: (b, i, 0)),
        ],
        out_shape=[
            jax.ShapeDtypeStruct((BATCH, SEQ, W_A), BF16),
            jax.ShapeDtypeStruct((BATCH, SEQ, W_G), BF16),
            jax.ShapeDtypeStruct((BATCH, SEQ, W_GATE), BF16),
            jax.ShapeDtypeStruct((BATCH, SEQ, LANES), F32),
        ],
        compiler_params=pltpu.CompilerParams(
            dimension_semantics=("arbitrary", "arbitrary"), vmem_limit_bytes=VMEM_LIMIT),
        name="in_proj",
    )(x, mod4, mod4, norm1_g, w_a, w_g, w_gate, w_lr)


def _attn_kernel(slope_ref, lam_ref, q_ref, k_ref, v_ref, kaug_ref, g_ref, o_ref,
                 sa_ref, sb_ref, lha_ref, lhb_ref, corr_ref, vt_ref, m_ref, acc_ref):
    c_alibi = slope_ref[pl.program_id(1)] * LOG2E
    lam = lam_ref[0]
    lane = lax.broadcasted_iota(I32, (TQ, 2 * DIFF_DH), 1)
    c_vec = jnp.full((TQ, 2 * DIFF_DH), c_alibi, F32)
    c_hi = c_vec.astype(BF16).astype(F32)
    c_lo = (c_vec - c_hi).astype(BF16).astype(F32)
    zero = jnp.zeros((TQ, 2 * DIFF_DH), BF16)
    aug = jnp.where(lane == 0, POS_LO * c_hi, jnp.where(lane == 1, POS_LO * c_lo,
                    jnp.where(lane == 2, c_hi, jnp.where(lane == 3, c_lo, 0.0)))).astype(BF16)

    def blk(j):
        return slice(j * TQ, (j + 1) * TQ)

    def stacked_queries(qi):
        q = (q_ref[blk(qi), :].astype(F32) * (DIFF_DH ** -0.5 * LOG2E)).astype(BF16)
        return jnp.concatenate([
            jnp.concatenate([jnp.where(lane < DIFF_DH, q, zero), aug], axis=1),
            jnp.concatenate([jnp.where(lane >= DIFF_DH, q, zero), aug], axis=1)], axis=0)

    lhs_refs = (lha_ref, lhb_ref)

    def strip(c):
        return slice(c * ATTN_W, (c + 1) * ATTN_W)

    def keys(j):
        return jnp.concatenate([k_ref[blk(j), :], kaug_ref[blk(j), :]], axis=1)

    def scores(kk, qi, c):
        return _nt_dot(kk, lhs_refs[qi % 2][strip(c), :])

    ones_rows = jnp.where(lax.broadcasted_iota(I32, (VT_ROWS - 2 * DIFF_DH, TQ), 0) == 0, 1.0, 0.0).astype(BF16)
    for j in range(SEQ // TQ):
        vt_ref[j, :2 * DIFF_DH, :] = v_ref[blk(j), :].astype(F32).T.astype(BF16)
        vt_ref[j, 2 * DIFF_DH:, :] = ones_rows

    def update(s, j, c, first):
        nk = s.shape[0]
        if first:
            m_new = jnp.max(s, axis=0, keepdims=True)
            p = jnp.exp2((s - m_new).astype(BF16))
            acc_ref[:, strip(c)] = jnp.dot(vt_ref[j, :, :nk], p, preferred_element_type=F32)
        else:
            m = m_ref[:, strip(c)]
            m_new = jnp.maximum(m, jnp.max(s, axis=0, keepdims=True))
            alpha = jnp.exp2(m - m_new)
            p = jnp.exp2((s - m_new).astype(BF16))
            acc_ref[:, strip(c)] = (alpha * acc_ref[:, strip(c)]
                                    + jnp.dot(vt_ref[j, :, :nk], p, preferred_element_type=F32))
        m_ref[:, strip(c)] = m_new

    def n_keys(qi, j, c):
        return TQ if j < qi else min(TQ, (c * ATTN_W) % TQ + ATTN_W)

    kr = lax.broadcasted_iota(I32, (TQ, TQ), 0)
    qc = lax.broadcasted_iota(I32, (TQ, TQ), 1)
    ahead = jnp.maximum(kr - qc, 0).astype(F32)
    corr_ref[...] = jnp.where((qc >> CHUNK_SHIFT) >= (kr >> CHUNK_SHIFT), (-2.0 * c_alibi) * ahead, -jnp.inf)

    pairs = [(qi, j) for qi in range(SEQ // TQ) for j in range(qi + 1)]
    bufs = (sa_ref, sb_ref)
    lhs_refs[0][...] = stacked_queries(0)
    kk = keys(0)
    for c in range(ATTN_STRIPS):
        nk = n_keys(0, 0, c)
        bufs[0][:nk, strip(c)] = scores(kk[:nk], 0, c)
    for t, (qi, j) in enumerate(pairs):
        nxt = pairs[t + 1] if t + 1 < len(pairs) else None
        if nxt is not None:
            if nxt[0] != qi:
                lhs_refs[nxt[0] % 2][...] = stacked_queries(nxt[0])
            kk = keys(nxt[1])
        for c in range(ATTN_STRIPS):
            if nxt is not None:
                nk = n_keys(*nxt, c)
                bufs[(t + 1) % 2][:nk, strip(c)] = scores(kk[:nk], nxt[0], c)
            nk = n_keys(qi, j, c)
            s = bufs[t % 2][:nk, strip(c)]
            if j == qi:
                lo = (c * ATTN_W) % TQ
                s = s + corr_ref[:nk, lo:lo + ATTN_W]
            update(s, j, c, first=(j == 0))
        if j == qi:
            ot = acc_ref[:2 * DIFF_DH, :] / acc_ref[2 * DIFF_DH:2 * DIFF_DH + 1, :]
            o = (ot[:, :TQ] - lam * ot[:, TQ:]).T
            inv = lax.rsqrt(jnp.mean(o * o, axis=-1, keepdims=True) + EPS)
            o_ref[blk(qi), :] = (o * inv * g_ref[...] * (1.0 - LAMBDA_INIT)).astype(BF16)


def _diff_attention(qkv_a, slopes, lam, kaug, diff_norm_g):
    return pl.pallas_call(
        _attn_kernel,
        grid=(BATCH, DIFF_HEADS),
        in_specs=[
            pl.BlockSpec(memory_space=pltpu.SMEM),
            pl.BlockSpec(memory_space=pltpu.SMEM),
            pl.BlockSpec((None, SEQ, LANES), lambda b, h: (b, 0, h)),
            pl.BlockSpec((None, SEQ, LANES), lambda b, h: (b, 0, DIFF_HEADS + h)),
            pl.BlockSpec((None, SEQ, LANES), lambda b, h: (b, 0, 2 * DIFF_HEADS + h)),
            pl.BlockSpec((SEQ, LANES), lambda b, h: (0, 0)),
            pl.BlockSpec((1, LANES), lambda b, h: (0, 0)),
        ],
        out_specs=pl.BlockSpec((None, SEQ, LANES), lambda b, h: (b, 0, h)),
        out_shape=jax.ShapeDtypeStruct((BATCH, SEQ, DIFF_HEADS * 2 * DIFF_DH), BF16),
        scratch_shapes=[
            pltpu.VMEM((TQ, 2 * TQ), F32), pltpu.VMEM((TQ, 2 * TQ), F32),
            pltpu.VMEM((2 * TQ, 4 * DIFF_DH), BF16), pltpu.VMEM((2 * TQ, 4 * DIFF_DH), BF16),
            pltpu.VMEM((TQ, TQ), F32),
            pltpu.VMEM((SEQ // TQ, VT_ROWS, TQ), BF16),
            pltpu.VMEM((1, 2 * TQ), F32),
            pltpu.VMEM((VT_ROWS, 2 * TQ), F32),
        ],
        compiler_params=pltpu.CompilerParams(
            dimension_semantics=("arbitrary", "arbitrary")),
        name="diff_attn",
    )(slopes, lam, qkv_a, qkv_a, qkv_a, kaug, diff_norm_g)


N_CHUNK = SEQ // CHUNK
PAIR = 2 * GLA_DK
PAIR_V = 2 * GLA_DV
CS_ROWS = 256
GLA_UNROLL = 8


def _gla_kernel(q_ref, k_ref, v_ref, r_ref, lr_ref, wup_ref, bup_ref, g_ref, o_ref,
                gcum_ref, state_ref):
    w_hi, w_mid, _ = _split3(wup_ref[...])
    rr = lax.broadcasted_iota(I32, (CS_ROWS, CS_ROWS), 0)
    cc = lax.broadcasted_iota(I32, (CS_ROWS, CS_ROWS), 1)
    tri = jnp.where(((rr >> CHUNK_SHIFT) == (cc >> CHUNK_SHIFT)) & (cc <= rr), 1.0, 0.0).astype(BF16)
    for blk in range(SEQ // CS_ROWS):
        rows = pl.ds(blk * CS_ROWS, CS_ROWS)
        a_hi, a_mid, _ = _split3(lr_ref[rows, :])
        z = (jnp.dot(a_hi, w_hi, preferred_element_type=F32)
             + jnp.dot(a_hi, w_mid, preferred_element_type=F32)
             + jnp.dot(a_mid, w_hi, preferred_element_type=F32)) + bup_ref[...]
        la = (jnp.minimum(z, 0.0) - jnp.log(1.0 + jnp.exp(-jnp.abs(z)))) * (1.0 / GLA_GATE_NORM)
        l_hi, l_mid, l_lo = _split3(la)
        gcum_ref[rows, :] = (jnp.dot(tri, l_hi, preferred_element_type=F32)
                             + jnp.dot(tri, l_mid, preferred_element_type=F32)
                             + jnp.dot(tri, l_lo, preferred_element_type=F32))

    state_ref[...] = jnp.zeros_like(state_ref)
    lane_k = lax.broadcasted_iota(I32, (1, PAIR), 1)
    row_v = lax.broadcasted_iota(I32, (PAIR_V, PAIR), 0)
    col_k = lax.broadcasted_iota(I32, (PAIR_V, PAIR), 1)
    same_head = (row_v >= GLA_DV) == (col_k >= GLA_DK)
    cr = lax.broadcasted_iota(I32, (CHUNK, CHUNK), 0)
    cs = lax.broadcasted_iota(I32, (CHUNK, CHUNK), 1)
    causal = cs <= cr
    scale = GLA_DK ** -0.5

    def chunk(n):
        rows = pl.ds(pl.multiple_of(n * CHUNK, CHUNK), CHUNK)
        gc = gcum_ref[rows, :]
        g_last = gcum_ref[pl.ds(n * CHUNK + CHUNK - 1, 1), :]
        qf = q_ref[rows, :].astype(F32) * scale
        kf = k_ref[rows, :].astype(F32)
        q_s = (qf * jnp.exp(gc)).astype(BF16)
        k_s = (kf * jnp.exp(-gc)).astype(BF16)
        k_d = (kf * jnp.exp(g_last - gc)).astype(BF16)
        decay = jnp.exp(g_last)
        for pr in range(GLA_HEADS // 2):
            kl = slice(pr * PAIR, (pr + 1) * PAIR)
            vl = slice(pr * PAIR_V, (pr + 1) * PAIR_V)
            qs_p, ks_p, kd_p = q_s[:, kl], k_s[:, kl], k_d[:, kl]
            v_p = v_ref[rows, vl]
            st = state_ref[pr]
            o_inter = _nt_dot(qs_p, st.astype(BF16))
            d_st = _tn_dot(v_p, kd_p)
            state_ref[pr] = st * decay[:, kl] + jnp.where(same_head, d_st, 0.0)
            for sub in range(2):
                hd = 2 * pr + sub
                in_head = (lane_k >= sub * GLA_DK) & (lane_k < (sub + 1) * GLA_DK)
                a = _nt_dot(jnp.where(in_head, qs_p, jnp.zeros_like(qs_p)), ks_p)
                a = jnp.where(causal, a, 0.0).astype(BF16)
                vs = slice(hd * GLA_DV, (hd + 1) * GLA_DV)
                o = (jnp.dot(a, v_ref[rows, vs], preferred_element_type=F32)
                     + o_inter[:, sub * GLA_DV:(sub + 1) * GLA_DV])
                inv = lax.rsqrt(jnp.mean(o * o, axis=-1, keepdims=True) + EPS)
                r = r_ref[rows, vs].astype(F32)
                o_ref[rows, vs] = (o * inv * g_ref[...] * (r * jax.nn.sigmoid(r))).astype(BF16)

    def chunk_group(t, _):
        for u in range(GLA_UNROLL):
            chunk(t * GLA_UNROLL + u)
        return 0

    lax.fori_loop(0, N_CHUNK // GLA_UNROLL, chunk_group, 0)


def _gla(qkv_g, glr, w_up, b_up, gla_norm_g):
    qk_w = GLA_HEADS * GLA_DK
    v_w = GLA_HEADS * GLA_DV
    return pl.pallas_call(
        _gla_kernel,
        grid=(BATCH,),
        in_specs=[
            pl.BlockSpec((None, SEQ, qk_w), lambda b: (b, 0, 0)),
            pl.BlockSpec((None, SEQ, qk_w), lambda b: (b, 0, 1)),
            pl.BlockSpec((None, SEQ, v_w), lambda b: (b, 0, 1)),
            pl.BlockSpec((None, SEQ, v_w), lambda b: (b, 0, 2)),
            pl.BlockSpec((None, SEQ, LANES), lambda b: (b, 0, 0)),
            pl.BlockSpec((LANES, qk_w), lambda b: (0, 0)),
            pl.BlockSpec((1, qk_w), lambda b: (0, 0)),
            pl.BlockSpec((1, GLA_DV), lambda b: (0, 0)),
        ],
        out_specs=pl.BlockSpec((None, SEQ, v_w), lambda b: (b, 0, 0)),
        out_shape=jax.ShapeDtypeStruct((BATCH, SEQ, v_w), BF16),
        scratch_shapes=[
            pltpu.VMEM((SEQ, qk_w), F32),
            pltpu.VMEM((GLA_HEADS // 2, PAIR_V, PAIR), F32),
        ],
        compiler_params=pltpu.CompilerParams(
            dimension_semantics=("arbitrary",), vmem_limit_bytes=VMEM_LIMIT),
        name="gla",
    )(qkv_g, qkv_g, qkv_g, qkv_g, glr, w_up, b_up, gla_norm_g)


def _merge_kernel(oa_ref, ob_ref, gate_ref, x_ref, gt1_ref, sh2_ref, sc2_ref, g2_ref,
                  wpa_ref, wpb_ref, wo_ref, wr_ref, br_ref,
                  x1_ref, h2p_ref, eidx_ref, rank_ref, wts_ref, cnt_ref,
                  upper_ref, carry_ref):
    i = pl.program_id(0)

    @pl.when(i == 0)
    def _():
        rr = lax.broadcasted_iota(I32, (TM_MERGE, TM_MERGE), 0)
        cc = lax.broadcasted_iota(I32, (TM_MERGE, TM_MERGE), 1)
        upper_ref[...] = jnp.where(rr < cc, 1.0, 0.0).astype(BF16)
        carry_ref[...] = jnp.zeros_like(carry_ref)

    ga = gate_ref[:, :D_MODEL].astype(F32)
    gb = gate_ref[:, D_MODEL:].astype(F32)
    merged = (jax.nn.sigmoid(ga) * jnp.dot(oa_ref[...], wpa_ref[...], preferred_element_type=F32)
              + jax.nn.sigmoid(gb) * jnp.dot(ob_ref[...], wpb_ref[...], preferred_element_type=F32))
    y = jnp.dot(merged.astype(BF16), wo_ref[...], preferred_element_type=F32)
    x1 = x_ref[...] + gt1_ref[...] * y
    x1_ref[...] = x1
    inv = lax.rsqrt(jnp.mean(x1 * x1, axis=-1, keepdims=True) + EPS)
    h2 = (x1 * inv * g2_ref[...]) * (1.0 + sc2_ref[...]) + sh2_ref[...]
    h2p_ref[...] = _pack_halves(h2)

    h_hi, h_mid, _ = _split3(h2)
    w_hi, w_mid, _ = _split3(wr_ref[...])
    logits = (_nt_dot(w_hi, h_hi) + _nt_dot(w_hi, h_mid) + _nt_dot(w_mid, h_hi)) + br_ref[...]

    eio = lax.broadcasted_iota(I32, (N_EXPERTS, TM_MERGE), 0)
    vals, idxs, sels = [], [], []
    cur = logits
    for _k in range(TOP_K):
        m = jnp.max(cur, axis=0, keepdims=True)
        idx = jnp.min(jnp.where(cur == m, eio, N_EXPERTS), axis=0, keepdims=True)
        sel = eio == idx
        vals.append(m)
        idxs.append(idx)
        sels.append(sel)
        cur = jnp.where(sel, -jnp.inf, cur)
    es = [jnp.exp(v - vals[0]) for v in vals]
    tot = es[0] + es[1] + es[2] + es[3]
    onehot = jnp.zeros((N_EXPERTS, TM_MERGE), F32)
    for sel in sels:
        onehot = onehot + jnp.where(sel, 1.0, 0.0)
    before = jnp.dot(onehot.astype(BF16), upper_ref[...], preferred_element_type=F32) + carry_ref[:, 0:1]
    ranks = [jnp.sum(jnp.where(sel, before, 0.0), axis=0, keepdims=True) for sel in sels]
    carry_ref[...] = carry_ref[...] + jnp.sum(onehot, axis=1, keepdims=True)
    cnt_ref[...] = carry_ref[...]

    zi = jnp.zeros((8 - TOP_K, TM_MERGE), I32)
    zf = jnp.zeros((8 - TOP_K, TM_MERGE), F32)
    eidx_ref[...] = jnp.concatenate(idxs + [zi], axis=0)
    rank_ref[...] = jnp.concatenate([r.astype(I32) for r in ranks] + [zi], axis=0)
    wts_ref[...] = jnp.concatenate([e / tot for e in es] + [zf], axis=0)


def _merge_route(grp, o_a, o_b, gates, x2d, mod4, norm2_g, w_pa, w_pb, w_o, w_rt, b_r):
    ntile = N_GRP // TM_MERGE
    first = grp * ntile
    per_b = SEQ // TM_MERGE
    full = lambda shape: pl.BlockSpec(shape, lambda i: (0,) * len(shape))
    row_in = lambda w: pl.BlockSpec((TM_MERGE, w), lambda i: (first + i, 0))
    row = lambda w: pl.BlockSpec((TM_MERGE, w), lambda i: (i, 0))
    modspec = lambda j: pl.BlockSpec((None, None, 1, D_MODEL), lambda i: (j, (first + i) // per_b, 0, 0))
    col = pl.BlockSpec((8, TM_MERGE), lambda i: (0, i))
    return pl.pallas_call(
        _merge_kernel,
        grid=(ntile,),
        in_specs=[
            row_in(DIFF_HEADS * 2 * DIFF_DH), row_in(GLA_HEADS * GLA_DV), row_in(W_GATE), row_in(D_MODEL),
            modspec(2), modspec(3), modspec(4),
            full((1, D_MODEL)),
            full((DIFF_HEADS * 2 * DIFF_DH, D_MODEL)), full((GLA_HEADS * GLA_DV, D_MODEL)),
            full((D_MODEL, D_MODEL)),
            full((N_EXPERTS, D_MODEL)), full((N_EXPERTS, 1)),
        ],
        out_specs=[
            row(D_MODEL), row(ROW_W), col, col, col,
            pl.BlockSpec((N_EXPERTS, LANES), lambda i: (0, 0)),
        ],
        out_shape=[
            jax.ShapeDtypeStruct((N_GRP, D_MODEL), F32),
            jax.ShapeDtypeStruct((N_GRP, ROW_W), ROW_DT),
            jax.ShapeDtypeStruct((8, N_GRP), I32),
            jax.ShapeDtypeStruct((8, N_GRP), I32),
            jax.ShapeDtypeStruct((8, N_GRP), F32),
            jax.ShapeDtypeStruct((N_EXPERTS, LANES), F32),
        ],
        scratch_shapes=[
            pltpu.VMEM((TM_MERGE, TM_MERGE), BF16),
            pltpu.VMEM((N_EXPERTS, LANES), F32),
        ],
        compiler_params=pltpu.CompilerParams(
            dimension_semantics=("arbitrary",), vmem_limit_bytes=VMEM_LIMIT),
        name="merge_route",
    )(o_a, o_b, gates, x2d, mod4, mod4, mod4, norm2_g, w_pa, w_pb, w_o, w_rt, b_r)


TP = 4096
NB_PAD = ((N_BLK + LANES - 1) // LANES) * LANES


def _plan_kernel(cnt_ref, eidx_ref, rank_ref, dest_ref, be_ref, nv_ref, rows_ref):
    cnt = cnt_ref[...]
    padded = jnp.floor((cnt + (BLK - 1.0)) * (1.0 / BLK)) * BLK
    er = lax.broadcasted_iota(I32, (N_EXPERTS, N_EXPERTS), 0)
    ec = lax.broadcasted_iota(I32, (N_EXPERTS, N_EXPERTS), 1)
    lower = jnp.where(ec < er, 1.0, 0.0).astype(BF16)
    p_hi, p_mid, p_lo = _split3(padded)
    starts = (jnp.dot(lower, p_hi, preferred_element_type=F32)
              + jnp.dot(lower, p_mid, preferred_element_type=F32)
              + jnp.dot(lower, p_lo, preferred_element_type=F32))
    ends = starts + padded
    blk_start = (lax.broadcasted_iota(I32, (1, NB_PAD), 1) * BLK).astype(F32)
    n_before = jnp.sum(jnp.where(ends[:, 0:1] <= blk_start, 1.0, 0.0), axis=0, keepdims=True)
    be_ref[...] = jnp.minimum(n_before, N_EXPERTS - 1.0).astype(I32)
    nv_ref[...] = (jnp.max(ends, axis=0, keepdims=True) * (1.0 / BLK)).astype(I32)
    owner = (starts[:, 0:1] <= blk_start) & (blk_start < ends[:, 0:1])
    filled = jnp.sum(jnp.where(owner, (starts + cnt)[:, 0:1] - blk_start, 0.0), axis=0, keepdims=True)
    rows_ref[...] = jnp.clip(filled, 0.0, float(BLK)).astype(I32)

    eio = lax.broadcasted_iota(I32, (N_EXPERTS, TP), 0)
    rows = []
    for k in range(TOP_K):
        onehot = eio == eidx_ref[k:k + 1, :]
        base = jnp.sum(jnp.where(onehot, starts[:, 0:1], 0.0), axis=0, keepdims=True)
        rows.append(base.astype(I32) + rank_ref[k:k + 1, :])
    dest_ref[...] = jnp.concatenate(rows + [jnp.zeros((8 - TOP_K, TP), I32)], axis=0)


def _route_plan(cnt, eidx, rank):
    return pl.pallas_call(
        _plan_kernel,
        grid=(N_GRP // TP,),
        in_specs=[
            pl.BlockSpec((N_EXPERTS, LANES), lambda i: (0, 0)),
            pl.BlockSpec((8, TP), lambda i: (0, i)),
            pl.BlockSpec((8, TP), lambda i: (0, i)),
        ],
        out_specs=[
            pl.BlockSpec((8, TP), lambda i: (0, i)),
            pl.BlockSpec((1, NB_PAD), lambda i: (0, 0)),
            pl.BlockSpec((1, LANES), lambda i: (0, 0)),
            pl.BlockSpec((1, NB_PAD), lambda i: (0, 0)),
        ],
        out_shape=[
            jax.ShapeDtypeStruct((8, N_GRP), I32),
            jax.ShapeDtypeStruct((1, NB_PAD), I32),
            jax.ShapeDtypeStruct((1, LANES), I32),
            jax.ShapeDtypeStruct((1, NB_PAD), I32),
        ],
        compiler_params=pltpu.CompilerParams(dimension_semantics=("arbitrary",)),
        name="route_plan",
    )(cnt, eidx, rank)


def _ffn_kernel(be_ref, nv_ref, first_ref, slot_ref, nxt_ref, rows_ref, x_ref, wgu_hbm, bgu_ref, wd_hbm, bd_ref,
                y_ref, wgu_buf, wd_buf, sem):
    i = pl.program_id(0)
    valid = i < nv_ref[0]
    half_only = rows_ref[i] <= BLK // 2
    s = slot_ref[i]

    def weight_copies(e, slot):
        return (pltpu.make_async_copy(wgu_hbm.at[e], wgu_buf.at[slot], sem.at[slot, 0]),
                pltpu.make_async_copy(wd_hbm.at[e], wd_buf.at[slot], sem.at[slot, 1]))

    @pl.when(i == 0)
    def _():
        for cp in weight_copies(be_ref[0], 0):
            cp.start()

    @pl.when(valid & (first_ref[i] == 1))
    def _():
        for cp in weight_copies(be_ref[i], s):
            cp.wait()

        @pl.when(nxt_ref[i] >= 0)
        def _():
            for cp in weight_copies(nxt_ref[i], 1 - s):
                cp.start()

    def expert_rows(n):
        lo, hi = _unpack_halves(x_ref[:n, :])
        x = jnp.concatenate([lo, hi], axis=1).astype(BF16)
        mm = lambda a, w: lax.dot_general(a, w, (((1,), (0,)), ((), ())), preferred_element_type=F32)
        gu = mm(x, wgu_buf[s]) + bgu_ref[...]
        gate = jnp.minimum(gu[:, :D_FF], SWIGLU_LIMIT)
        up = jnp.clip(gu[:, D_FF:], -SWIGLU_LIMIT, SWIGLU_LIMIT)
        act = (up + 1.0) * (gate * jax.nn.sigmoid(SWIGLU_ALPHA * gate))
        y = mm(act.astype(BF16), wd_buf[s]) + bd_ref[...]
        y_ref[:n, :] = _pack_halves(y)

    @pl.when(valid & jnp.logical_not(half_only))
    def _():
        expert_rows(BLK)

    @pl.when(valid & half_only)
    def _():
        expert_rows(BLK // 2)
        y_ref[BLK // 2:, :] = jnp.zeros((BLK // 2, ROW_W), ROW_DT)

    @pl.when(jnp.logical_not(valid))
    def _():
        y_ref[...] = jnp.zeros_like(y_ref)


def _expert_ffn(blk_expert, n_valid, blk_rows, xb, w_gate_up, b_gate_up, w_down, b_down):
    idx = jnp.arange(N_BLK, dtype=I32)
    used = idx < n_valid[0]
    first = (used & ((idx == 0) | (blk_expert != jnp.roll(blk_expert, 1)))).astype(I32)
    slot = (jnp.cumsum(first) - 1) & 1
    later = used[None, :] & (idx[None, :] > idx[:, None]) & (blk_expert[None, :] != blk_expert[:, None])
    nxt_pos = jnp.min(jnp.where(later, idx[None, :], N_BLK), axis=1)
    nxt = jnp.where(nxt_pos < N_BLK, blk_expert[jnp.minimum(nxt_pos, N_BLK - 1)], -1).astype(I32)

    blockwise = lambda i, *_: (i, 0)
    per_expert = lambda i, be, *_: (be[i], 0, 0)
    grid_spec = pltpu.PrefetchScalarGridSpec(
        num_scalar_prefetch=6,
        grid=(N_BLK,),
        in_specs=[
            pl.BlockSpec((BLK, ROW_W), blockwise),
            pl.BlockSpec(memory_space=pl.ANY),
            pl.BlockSpec((None, 1, 2 * D_FF), per_expert),
            pl.BlockSpec(memory_space=pl.ANY),
            pl.BlockSpec((None, 1, D_MODEL), per_expert),
        ],
        out_specs=pl.BlockSpec((BLK, ROW_W), blockwise),
        scratch_shapes=[
            pltpu.VMEM((2, D_MODEL, 2 * D_FF), F32),
            pltpu.VMEM((2, D_FF, D_MODEL), F32),
            pltpu.SemaphoreType.DMA((2, 2)),
        ],
    )
    return pl.pallas_call(
        _ffn_kernel,
        grid_spec=grid_spec,
        out_shape=jax.ShapeDtypeStruct((P_ROWS, ROW_W), ROW_DT),
        compiler_params=pltpu.CompilerParams(
            dimension_semantics=("arbitrary",), vmem_limit_bytes=VMEM_LIMIT),
        name="expert_ffn",
    )(blk_expert, n_valid, first, slot.astype(I32), nxt, blk_rows, xb, w_gate_up,
      b_gate_up.reshape(N_EXPERTS, 1, 2 * D_FF), w_down, b_down.reshape(N_EXPERTS, 1, D_MODEL))


def _final_kernel(x1_ref, y0_ref, y1_ref, y2_ref, y3_ref, w_ref, gt2_ref, g_ref, *rest):
    o_ref = rest[-1]
    w = w_ref[...].T
    ylo = jnp.zeros((TM_FIN, HALF), F32)
    yhi = jnp.zeros((TM_FIN, HALF), F32)
    for k, y_ref in enumerate((y0_ref, y1_ref, y2_ref, y3_ref)):
        lo, hi = _unpack_halves(y_ref[...])
        wk = w[:, k:k + 1]
        ylo = ylo + wk * lo
        yhi = yhi + wk * hi
    y = jnp.concatenate([ylo, yhi], axis=1)
    x2 = x1_ref[...] + gt2_ref[...] * y
    inv = lax.rsqrt(jnp.mean(x2 * x2, axis=-1, keepdims=True) + EPS)
    o_ref[...] = x2 * inv * g_ref[...]


def _final(grp, x1, yg, w4, mod4, final_norm_g, out_so_far):
    per_b = SEQ // TM_FIN
    ntile = N_GRP // TM_FIN
    first = grp * ntile
    slot = lambda k: pl.BlockSpec((TM_FIN, ROW_W), lambda i: (k * ntile + i, 0))
    in_specs = [
        pl.BlockSpec((TM_FIN, D_MODEL), lambda i: (i, 0)),
        slot(0), slot(1), slot(2), slot(3),
        pl.BlockSpec((8, TM_FIN), lambda i: (0, i)),
        pl.BlockSpec((None, None, 1, D_MODEL), lambda i: (5, (first + i) // per_b, 0, 0)),
        pl.BlockSpec((1, D_MODEL), lambda i: (0, 0)),
    ]
    args = [x1, yg, yg, yg, yg, w4, mod4, final_norm_g]
    aliases = {}
    if out_so_far is not None:
        in_specs.append(pl.BlockSpec(memory_space=pl.ANY))
        args.append(out_so_far)
        aliases = {len(args) - 1: 0}
    return pl.pallas_call(
        _final_kernel,
        grid=(ntile,),
        in_specs=in_specs,
        out_specs=pl.BlockSpec((TM_FIN, D_MODEL), lambda i: (first + i, 0)),
        out_shape=jax.ShapeDtypeStruct((N_TOK, D_MODEL), F32),
        input_output_aliases=aliases,
        compiler_params=pltpu.CompilerParams(
            dimension_semantics=("arbitrary",), vmem_limit_bytes=VMEM_LIMIT),
        name="combine_final",
    )(*args)


SC_CORES = 2
SC_SUBCORES = 16
SC_WORKERS = SC_CORES * SC_SUBCORES
SC_CHUNK = 64


def _sc_mesh():
    return plsc.VectorSubcoreMesh(core_axis_name="c", subcore_axis_name="s")


def _row_buffers():
    return ([pltpu.VMEM((SC_CHUNK, HALF), U32)] * 2 + [pltpu.SemaphoreType.DMA] * 5)


def _dispatch_rows(h2p, dest2d):
    per_w = N_GRP // SC_WORKERS
    nchunk = per_w // SC_CHUNK
    rows_per_k = N_GRP // SC_CHUNK
    assert nchunk % 2 == 0

    @functools.partial(
        pl.kernel, mesh=_sc_mesh(), out_type=jax.ShapeDtypeStruct((P_ROWS, HALF), U32),
        scratch_types=[pltpu.VMEM((TOP_K * nchunk, SC_CHUNK), I32)] + _row_buffers(),
        name="moe_dispatch")
    def k(src_hbm, dest_hbm, out_hbm, idx_v, buf0, buf1, isem, l0, l1, s0, s1):
        wid = lax.axis_index("s") * SC_CORES + lax.axis_index("c")
        bufs, lsem, ssem = (buf0, buf1), (l0, l1), (s0, s1)
        idx_loads = [
            pltpu.make_async_copy(dest_hbm.at[pl.ds(kk * rows_per_k + wid * nchunk, nchunk)],
                                  idx_v.at[pl.ds(kk * nchunk, nchunk)], isem) for kk in range(TOP_K)]

        def load(j, b):
            return pltpu.make_async_copy(src_hbm.at[pl.ds(wid * per_w + j * SC_CHUNK, SC_CHUNK)], bufs[b], lsem[b])

        def scatters(j, b):
            return [pltpu.make_async_copy(bufs[b], out_hbm.at[idx_v.at[kk * nchunk + j]], ssem[b])
                    for kk in range(TOP_K)]

        for cp in idx_loads:
            cp.start()
        load(0, 0).start()
        for cp in idx_loads:
            cp.wait()

        @pl.loop(0, nchunk // 2)
        def _(i):
            for b in range(2):
                j = 2 * i + b

                @pl.when(j >= 1)
                def _():
                    for cp in scatters(j - 1, 1 - b):
                        cp.wait()

                @pl.when(j + 1 < nchunk)
                def _():
                    load(j + 1, 1 - b).start()

                load(j, b).wait()
                for cp in scatters(j, b):
                    cp.start()

        for cp in scatters(nchunk - 1, 1):
            cp.wait()

    return k(h2p, dest2d)


def _combine_rows(yb, dest2d):
    n_out = TOP_K * N_GRP
    per_w = n_out // SC_WORKERS
    nchunk = per_w // SC_CHUNK
    assert nchunk % 2 == 0

    @functools.partial(
        pl.kernel, mesh=_sc_mesh(), out_type=jax.ShapeDtypeStruct((n_out, HALF), U32),
        scratch_types=[pltpu.VMEM((nchunk, SC_CHUNK), I32)] + _row_buffers(),
        name="moe_combine")
    def k(tab_hbm, idx_hbm, out_hbm, idx_v, buf0, buf1, isem, g0, g1, s0, s1):
        wid = lax.axis_index("s") * SC_CORES + lax.axis_index("c")
        bufs, gsem, ssem = (buf0, buf1), (g0, g1), (s0, s1)
        idx_load = pltpu.make_async_copy(idx_hbm.at[pl.ds(wid * nchunk, nchunk)], idx_v, isem)

        def gather(j, b):
            return pltpu.make_async_copy(tab_hbm.at[idx_v.at[j]], bufs[b], gsem[b])

        def store(j, b):
            return pltpu.make_async_copy(bufs[b], out_hbm.at[pl.ds(wid * per_w + j * SC_CHUNK, SC_CHUNK)], ssem[b])

        idx_load.start()
        idx_load.wait()
        gather(0, 0).start()

        @pl.loop(0, nchunk // 2)
        def _(i):
            for b in range(2):
                j = 2 * i + b

                @pl.when(j >= 1)
                def _():
                    store(j - 1, 1 - b).wait()

                @pl.when(j + 1 < nchunk)
                def _():
                    gather(j + 1, 1 - b).start()

                gather(j, b).wait()
                store(j, b).start()

        store(nchunk - 1, 1).wait()

    return k(yb, dest2d)


def _lambda_kernel(p_ref, o_ref):
    p = p_ref[...]
    s1 = jnp.sum(p[0:1] * p[1:2], axis=-1, keepdims=True)
    s2 = jnp.sum(p[2:3] * p[3:4], axis=-1, keepdims=True)
    o_ref[...] = jnp.broadcast_to(jnp.exp(s1) - jnp.exp(s2) + LAMBDA_INIT, (1, LANES))


def kernel(x, c, w_ada, b_ada, norm1_g, w_in, lambda_q1, lambda_k1, lambda_q2, lambda_k2, diff_norm_g, w_alpha_up, b_alpha, gla_norm_g, w_branch_diff, w_branch_gla, w_out, norm2_g, w_router, b_router, w_gate_up, b_gate_up, w_down, b_down, final_norm_g):
    w_in0 = w_in[0]
    c_a, c_g = W_A, W_A + W_G
    w_a = w_in0[:, :c_a].astype(BF16)
    w_g = w_in0[:, c_a:c_g].astype(BF16)
    w_lr = jnp.pad(w_in0[:, c_g:c_g + GLA_RANK], ((0, 0), (0, LANES - GLA_RANK))).astype(BF16)
    w_gate = w_in0[:, c_g + GLA_RANK:].astype(BF16)
    w_up = jnp.pad(w_alpha_up[0], ((0, LANES - GLA_RANK), (0, 0)))
    lam_in = jnp.concatenate([lambda_q1, lambda_k1, lambda_q2, lambda_k2], axis=0)
    slopes = jnp.asarray(2.0 ** (-8.0 * np.arange(1, DIFF_HEADS + 1) / DIFF_HEADS), dtype=F32)

    mod = _modulation(c, w_ada[0], b_ada[0])
    mod4 = mod.reshape(N_MOD, BATCH, 1, D_MODEL)
    lam = pl.pallas_call(
        _lambda_kernel, out_shape=jax.ShapeDtypeStruct((1, LANES), F32), name="lambda")(lam_in)[0, :1]

    qkv_a, qkv_g, gates, glr = _in_proj(x, mod4, norm1_g, w_a, w_g, w_gate, w_lr)
    pos = jnp.arange(SEQ, dtype=I32)
    p_hi = (pos >> POS_LO_BITS).astype(F32)
    p_lo = (pos & (POS_LO - 1)).astype(F32)
    kaug = jnp.zeros((SEQ, LANES), F32).at[:, 0].set(p_hi).at[:, 1].set(p_hi).at[:, 2].set(p_lo).at[:, 3].set(
        p_lo).astype(BF16)
    o_a = _diff_attention(qkv_a, slopes, lam, kaug, diff_norm_g)
    o_b = _gla(qkv_g, glr, w_up, b_alpha, gla_norm_g)

    merge_args = (o_a.reshape(N_TOK, -1), o_b.reshape(N_TOK, -1), gates.reshape(N_TOK, W_GATE),
                  x.reshape(N_TOK, D_MODEL), mod4, norm2_g,
                  w_branch_diff[0].astype(BF16), w_branch_gla[0].astype(BF16), w_out[0].astype(BF16),
                  w_router[0].T, b_router[0].reshape(N_EXPERTS, 1))
    fin_g = final_norm_g.reshape(1, D_MODEL)

    routed = []
    for grp in range(MOE_GROUPS):
        x1, h2p, eidx, rank, wts, cnt = _merge_route(grp, *merge_args)
        dest8, be, nv, rows = _route_plan(cnt, eidx, rank)
        dest = dest8[:TOP_K].reshape(-1, SC_CHUNK)
        routed.append((x1, wts, dest, be, nv, rows, _dispatch_rows(h2p, dest)))
    gathered = []
    for x1, wts, dest, be, nv, rows, xb in routed:
        yb = _expert_ffn(be[0, :N_BLK], nv[0, :1], rows[0, :N_BLK], xb,
                         w_gate_up[0], b_gate_up[0], w_down[0], b_down[0])
        gathered.append(_combine_rows(yb, dest))
    out = None
    for grp, ((x1, wts, *_), yg) in enumerate(zip(routed, gathered)):
        out = _final(grp, x1, yg, wts, mod4, fin_g, out)
    return out.reshape(BATCH, SEQ, D_MODEL)
```

```python
import functools
import math

import jax
import jax.numpy as jnp
import numpy as np
from jax import lax
from jax.experimental import pallas as pl
from jax.experimental.pallas import tpu as pltpu
from jax.experimental.pallas import tpu_sc as plsc

F32 = jnp.float32
BF16 = jnp.bfloat16
U32 = jnp.uint32
I32 = jnp.int32

D_MODEL = 1024
BATCH = 16
SEQ = 2048
N_TOK = BATCH * SEQ
CHUNK = 64
DIFF_HEADS = 4
DIFF_DH = 64
GLA_HEADS = 4
GLA_DK = 64
GLA_DV = 128
GLA_RANK = 16
GLA_GATE_NORM = 16.0
N_EXPERTS = 32
TOP_K = 4
D_FF = D_MODEL
SWIGLU_LIMIT = 7.0
SWIGLU_ALPHA = 1.702
N_MOD = 6
EPS = 1e-6
LAMBDA_INIT = 0.8 - 0.6 * math.exp(-0.3 * 0)

LANES = 128
HALF = D_MODEL // 2
ROW_W = HALF
ROW_DT = U32

TM_IN = 512
TQ = 512
ATTN_STRIPS = 4
ATTN_W = 2 * TQ // ATTN_STRIPS
VT_ROWS = 2 * DIFF_DH + 16
LOG2E = math.log2(math.e)
TM_MERGE = 512
BLK = 1024
BLK_STEP = 256
MOE_GROUPS = 1
N_GRP = N_TOK // MOE_GROUPS
N_BLK = (N_GRP * TOP_K) // BLK + N_EXPERTS
P_ROWS = N_BLK * BLK
TM_FIN = 512
VMEM_LIMIT = 56 * 1024 * 1024
CHUNK_SHIFT = CHUNK.bit_length() - 1
POS_LO_BITS = 3
POS_LO = 1 << POS_LO_BITS


def _nt_dot(a, b):
    return lax.dot_general(a, b, (((1,), (1,)), ((), ())), preferred_element_type=F32)


def _tn_dot(a, b):
    return lax.dot_general(a, b, (((0,), (0,)), ((), ())), preferred_element_type=F32)


def _split3(x):
    hi = x.astype(BF16)
    r1 = x - hi.astype(F32)
    mid = r1.astype(BF16)
    lo = (r1 - mid.astype(F32)).astype(BF16)
    return hi, mid, lo


def _pack_halves(y):
    return pltpu.pack_elementwise([y[:, :HALF], y[:, HALF:]], packed_dtype=BF16)


def _unpack_halves(u):
    lo = pltpu.unpack_elementwise(u, index=0, packed_dtype=BF16, unpacked_dtype=F32)
    hi = pltpu.unpack_elementwise(u, index=1, packed_dtype=BF16, unpacked_dtype=F32)
    return lo, hi


def _mod_kernel(c_ref, w_ref, b_ref, o_ref):
    c = c_ref[...]
    s = c * jax.nn.sigmoid(c)
    o_ref[0] = jnp.dot(s.astype(BF16), w_ref[...].astype(BF16),
                       preferred_element_type=F32) + b_ref[...]


def _modulation(c, w_ada, b_ada):
    return pl.pallas_call(
        _mod_kernel,
        grid=(N_MOD,),
        in_specs=[
            pl.BlockSpec((BATCH, D_MODEL), lambda j: (0, 0)),
            pl.BlockSpec((D_MODEL, D_MODEL), lambda j: (0, j)),
            pl.BlockSpec((1, D_MODEL), lambda j: (0, j)),
        ],
        out_specs=pl.BlockSpec((1, BATCH, D_MODEL), lambda j: (j, 0, 0)),
        out_shape=jax.ShapeDtypeStruct((N_MOD, BATCH, D_MODEL), F32),
        compiler_params=pltpu.CompilerParams(dimension_semantics=("arbitrary",)),
        name="adaln_mod",
    )(c, w_ada, b_ada.reshape(1, N_MOD * D_MODEL))


W_A = 3 * DIFF_HEADS * 2 * DIFF_DH
W_G = 2 * GLA_HEADS * GLA_DK + 2 * GLA_HEADS * GLA_DV
W_GATE = 2 * D_MODEL


def _in_kernel(x_ref, sh_ref, sc_ref, g_ref, wa_ref, wg_ref, wgate_ref, wlr_ref,
               oa_ref, og_ref, ogate_ref, olr_ref):
    x = x_ref[...]
    inv = lax.rsqrt(jnp.mean(x * x, axis=-1, keepdims=True) + EPS)
    h = (x * inv * g_ref[...]) * (1.0 + sc_ref[...]) + sh_ref[...]
    hb = h.astype(BF16)
    oa_ref[...] = jnp.dot(hb, wa_ref[...], preferred_element_type=F32).astype(BF16)
    og_ref[...] = jnp.dot(hb, wg_ref[...], preferred_element_type=F32).astype(BF16)
    ogate_ref[...] = jnp.dot(hb, wgate_ref[...], preferred_element_type=F32).astype(BF16)
    olr_ref[...] = jnp.dot(hb, wlr_ref[...], preferred_element_type=F32)


def _in_proj(x, mod4, norm1_g, w_a, w_g, w_gate, w_lr):
    nrow = SEQ // TM_IN
    full = lambda shape: pl.BlockSpec(shape, lambda b, i: (0,) * len(shape))
    return pl.pallas_call(
        _in_kernel,
        grid=(BATCH, nrow),
        in_specs=[
            pl.BlockSpec((None, TM_IN, D_MODEL), lambda b, i: (b, i, 0)),
            pl.BlockSpec((None, None, 1, D_MODEL), lambda b, i: (0, b, 0, 0)),
            pl.BlockSpec((None, None, 1, D_MODEL), lambda b, i: (1, b, 0, 0)),
            full((1, D_MODEL)),
            full((D_MODEL, W_A)),
            full((D_MODEL, W_G)),
            full((D_MODEL, W_GATE)),
            full((D_MODEL, LANES)),
        ],
        out_specs=[
            pl.BlockSpec((None, TM_IN, W_A), lambda b, i: (b, i, 0)),
            pl.BlockSpec((None, TM_IN, W_G), lambda b, i: (b, i, 0)),
            pl.BlockSpec((None, TM_IN, W_GATE), lambda b, i: (b, i, 0)),
            pl.BlockSpec((None, TM_IN, LANES), lambda b, i: (b, i, 0)),
        ],
        out_shape=[
            jax.ShapeDtypeStruct((BATCH, SEQ, W_A), BF16),
            jax.ShapeDtypeStruct((BATCH, SEQ, W_G), BF16),
            jax.ShapeDtypeStruct((BATCH, SEQ, W_GATE), BF16),
            jax.ShapeDtypeStruct((BATCH, SEQ, LANES), F32),
        ],
        compiler_params=pltpu.CompilerParams(
            dimension_semantics=("arbitrary", "arbitrary"), vmem_limit_bytes=VMEM_LIMIT),
        name="in_proj",
    )(x, mod4, mod4, norm1_g, w_a, w_g, w_gate, w_lr)


def _attn_kernel(slope_ref, lam_ref, q_ref, k_ref, v_ref, kaug_ref, g_ref, o_ref,
                 sa_ref, sb_ref, lha_ref, lhb_ref, corr_ref, vt_ref, m_ref, acc_ref):
    c_alibi = slope_ref[pl.program_id(1)] * LOG2E
    lam = lam_ref[0]
    lane = lax.broadcasted_iota(I32, (TQ, 2 * DIFF_DH), 1)
    c_vec = jnp.full((TQ, 2 * DIFF_DH), c_alibi, F32)
    c_hi = c_vec.astype(BF16).astype(F32)
    c_lo = (c_vec - c_hi).astype(BF16).astype(F32)
    zero = jnp.zeros((TQ, 2 * DIFF_DH), BF16)
    aug = jnp.where(lane == 0, POS_LO * c_hi, jnp.where(lane == 1, POS_LO * c_lo,
                    jnp.where(lane == 2, c_hi, jnp.where(lane == 3, c_lo, 0.0)))).astype(BF16)

    def blk(j):
        return slice(j * TQ, (j + 1) * TQ)

    def stacked_queries(qi):
        q = (q_ref[blk(qi), :].astype(F32) * (DIFF_DH ** -0.5 * LOG2E)).astype(BF16)
        return jnp.concatenate([
            jnp.concatenate([jnp.where(lane < DIFF_DH, q, zero), aug], axis=1),
            jnp.concatenate([jnp.where(lane >= DIFF_DH, q, zero), aug], axis=1)], axis=0)

    lhs_refs = (lha_ref, lhb_ref)

    def strip(c):
        return slice(c * ATTN_W, (c + 1) * ATTN_W)

    def keys(j):
        return jnp.concatenate([k_ref[blk(j), :], kaug_ref[blk(j), :]], axis=1)

    def scores(kk, qi, c):
        return _nt_dot(kk, lhs_refs[qi % 2][strip(c), :])

    ones_rows = jnp.where(lax.broadcasted_iota(I32, (VT_ROWS - 2 * DIFF_DH, TQ), 0) == 0, 1.0, 0.0).astype(BF16)
    for j in range(SEQ // TQ):
        vt_ref[j, :2 * DIFF_DH, :] = v_ref[blk(j), :].astype(F32).T.astype(BF16)
        vt_ref[j, 2 * DIFF_DH:, :] = ones_rows

    def update(s, j, c, first):
        nk = s.shape[0]
        if first:
            m_new = jnp.max(s, axis=0, keepdims=True)
            p = jnp.exp2((s - m_new).astype(BF16))
            acc_ref[:, strip(c)] = jnp.dot(vt_ref[j, :, :nk], p, preferred_element_type=F32)
        else:
            m = m_ref[:, strip(c)]
            m_new = jnp.maximum(m, jnp.max(s, axis=0, keepdims=True))
            alpha = jnp.exp2(m - m_new)
            p = jnp.exp2((s - m_new).astype(BF16))
            acc_ref[:, strip(c)] = (alpha * acc_ref[:, strip(c)]
                                    + jnp.dot(vt_ref[j, :, :nk], p, preferred_element_type=F32))
        m_ref[:, strip(c)] = m_new

    def n_keys(qi, j, c):
        return TQ if j < qi else min(TQ, (c * ATTN_W) % TQ + ATTN_W)

    kr = lax.broadcasted_iota(I32, (TQ, TQ), 0)
    qc = lax.broadcasted_iota(I32, (TQ, TQ), 1)
    ahead = jnp.maximum(kr - qc, 0).astype(F32)
    corr_ref[...] = jnp.where((qc >> CHUNK_SHIFT) >= (kr >> CHUNK_SHIFT), (-2.0 * c_alibi) * ahead, -jnp.inf)

    pairs = [(qi, j) for qi in range(SEQ // TQ) for j in range(qi + 1)]
    bufs = (sa_ref, sb_ref)
    lhs_refs[0][...] = stacked_queries(0)
    kk = keys(0)
    for c in range(ATTN_STRIPS):
        nk = n_keys(0, 0, c)
        bufs[0][:nk, strip(c)] = scores(kk[:nk], 0, c)
    for t, (qi, j) in enumerate(pairs):
        nxt = pairs[t + 1] if t + 1 < len(pairs) else None
        if nxt is not None:
            if nxt[0] != qi:
                lhs_refs[nxt[0] % 2][...] = stacked_queries(nxt[0])
            kk = keys(nxt[1])
        for c in range(ATTN_STRIPS):
            if nxt is not None:
                nk = n_keys(*nxt, c)
                bufs[(t + 1) % 2][:nk, strip(c)] = scores(kk[:nk], nxt[0], c)
            nk = n_keys(qi, j, c)
            s = bufs[t % 2][:nk, strip(c)]
            if j == qi:
                lo = (c * ATTN_W) % TQ
                s = s + corr_ref[:nk, lo:lo + ATTN_W]
            update(s, j, c, first=(j == 0))
        if j == qi:
            ot = acc_ref[:2 * DIFF_DH, :] / acc_ref[2 * DIFF_DH:2 * DIFF_DH + 1, :]
            o = (ot[:, :TQ] - lam * ot[:, TQ:]).T
            inv = lax.rsqrt(jnp.mean(o * o, axis=-1, keepdims=True) + EPS)
            o_ref[blk(qi), :] = (o * inv * g_ref[...] * (1.0 - LAMBDA_INIT)).astype(BF16)


def _diff_attention(qkv_a, slopes, lam, kaug, diff_norm_g):
    return pl.pallas_call(
        _attn_kernel,
        grid=(BATCH, DIFF_HEADS),
        in_specs=[
            pl.BlockSpec(memory_space=pltpu.SMEM),
            pl.BlockSpec(memory_space=pltpu.SMEM),
            pl.BlockSpec((None, SEQ, LANES), lambda b, h: (b, 0, h)),
            pl.BlockSpec((None, SEQ, LANES), lambda b, h: (b, 0, DIFF_HEADS + h)),
            pl.BlockSpec((None, SEQ, LANES), lambda b, h: (b, 0, 2 * DIFF_HEADS + h)),
            pl.BlockSpec((SEQ, LANES), lambda b, h: (0, 0)),
            pl.BlockSpec((1, LANES), lambda b, h: (0, 0)),
        ],
        out_specs=pl.BlockSpec((None, SEQ, LANES), lambda b, h: (b, 0, h)),
        out_shape=jax.ShapeDtypeStruct((BATCH, SEQ, DIFF_HEADS * 2 * DIFF_DH), BF16),
        scratch_shapes=[
            pltpu.VMEM((TQ, 2 * TQ), F32), pltpu.VMEM((TQ, 2 * TQ), F32),
            pltpu.VMEM((2 * TQ, 4 * DIFF_DH), BF16), pltpu.VMEM((2 * TQ, 4 * DIFF_DH), BF16),
            pltpu.VMEM((TQ, TQ), F32),
            pltpu.VMEM((SEQ // TQ, VT_ROWS, TQ), BF16),
            pltpu.VMEM((1, 2 * TQ), F32),
            pltpu.VMEM((VT_ROWS, 2 * TQ), F32),
        ],
        compiler_params=pltpu.CompilerParams(
            dimension_semantics=("arbitrary", "arbitrary")),
        name="diff_attn",
    )(slopes, lam, qkv_a, qkv_a, qkv_a, kaug, diff_norm_g)


N_CHUNK = SEQ // CHUNK
PAIR = 2 * GLA_DK
PAIR_V = 2 * GLA_DV
CS_ROWS = 256
GLA_UNROLL = 8


def _gla_kernel(q_ref, k_ref, v_ref, r_ref, lr_ref, wup_ref, bup_ref, g_ref, o_ref,
                gcum_ref, state_ref):
    w_hi, w_mid, _ = _split3(wup_ref[...])
    rr = lax.broadcasted_iota(I32, (CS_ROWS, CS_ROWS), 0)
    cc = lax.broadcasted_iota(I32, (CS_ROWS, CS_ROWS), 1)
    tri = jnp.where(((rr >> CHUNK_SHIFT) == (cc >> CHUNK_SHIFT)) & (cc <= rr), 1.0, 0.0).astype(BF16)
    for blk in range(SEQ // CS_ROWS):
        rows = pl.ds(blk * CS_ROWS, CS_ROWS)
        a_hi, a_mid, _ = _split3(lr_ref[rows, :])
        z = (jnp.dot(a_hi, w_hi, preferred_element_type=F32)
             + jnp.dot(a_hi, w_mid, preferred_element_type=F32)
             + jnp.dot(a_mid, w_hi, preferred_element_type=F32)) + bup_ref[...]
        la = (jnp.minimum(z, 0.0) - jnp.log(1.0 + jnp.exp(-jnp.abs(z)))) * (1.0 / GLA_GATE_NORM)
        l_hi, l_mid, l_lo = _split3(la)
        gcum_ref[rows, :] = (jnp.dot(tri, l_hi, preferred_element_type=F32)
                             + jnp.dot(tri, l_mid, preferred_element_type=F32)
                             + jnp.dot(tri, l_lo, preferred_element_type=F32))

    state_ref[...] = jnp.zeros_like(state_ref)
    lane_k = lax.broadcasted_iota(I32, (1, PAIR), 1)
    row_v = lax.broadcasted_iota(I32, (PAIR_V, PAIR), 0)
    col_k = lax.broadcasted_iota(I32, (PAIR_V, PAIR), 1)
    same_head = (row_v >= GLA_DV) == (col_k >= GLA_DK)
    cr = lax.broadcasted_iota(I32, (CHUNK, CHUNK), 0)
    cs = lax.broadcasted_iota(I32, (CHUNK, CHUNK), 1)
    causal = cs <= cr
    scale = GLA_DK ** -0.5

    def chunk(n):
        rows = pl.ds(pl.multiple_of(n * CHUNK, CHUNK), CHUNK)
        gc = gcum_ref[rows, :]
        g_last = gcum_ref[pl.ds(n * CHUNK + CHUNK - 1, 1), :]
        qf = q_ref[rows, :].astype(F32) * scale
        kf = k_ref[rows, :].astype(F32)
        q_s = (qf * jnp.exp(gc)).astype(BF16)
        k_s = (kf * jnp.exp(-gc)).astype(BF16)
        k_d = (kf * jnp.exp(g_last - gc)).astype(BF16)
        decay = jnp.exp(g_last)
        for pr in range(GLA_HEADS // 2):
            kl = slice(pr * PAIR, (pr + 1) * PAIR)
            vl = slice(pr * PAIR_V, (pr + 1) * PAIR_V)
            qs_p, ks_p, kd_p = q_s[:, kl], k_s[:, kl], k_d[:, kl]
            v_p = v_ref[rows, vl]
            st = state_ref[pr]
            o_inter = _nt_dot(qs_p, st.astype(BF16))
            d_st = _tn_dot(v_p, kd_p)
            state_ref[pr] = st * decay[:, kl] + jnp.where(same_head, d_st, 0.0)
            for sub in range(2):
                hd = 2 * pr + sub
                in_head = (lane_k >= sub * GLA_DK) & (lane_k < (sub + 1) * GLA_DK)
                a = _nt_dot(jnp.where(in_head, qs_p, jnp.zeros_like(qs_p)), ks_p)
                a = jnp.where(causal, a, 0.0).astype(BF16)
                vs = slice(hd * GLA_DV, (hd + 1) * GLA_DV)
                o = (jnp.dot(a, v_ref[rows, vs], preferred_element_type=F32)
                     + o_inter[:, sub * GLA_DV:(sub + 1) * GLA_DV])
                inv = lax.rsqrt(jnp.mean(o * o, axis=-1, keepdims=True) + EPS)
                r = r_ref[rows, vs].astype(F32)
                o_ref[rows, vs] = (o * inv * g_ref[...] * (r * jax.nn.sigmoid(r))).astype(BF16)

    def chunk_group(t, _):
        for u in range(GLA_UNROLL):
            chunk(t * GLA_UNROLL + u)
        return 0

    lax.fori_loop(0, N_CHUNK // GLA_UNROLL, chunk_group, 0)


def _gla(qkv_g, glr, w_up, b_up, gla_norm_g):
    qk_w = GLA_HEADS * GLA_DK
    v_w = GLA_HEADS * GLA_DV
    return pl.pallas_call(
        _gla_kernel,
        grid=(BATCH,),
        in_specs=[
            pl.BlockSpec((None, SEQ, qk_w), lambda b: (b, 0, 0)),
            pl.BlockSpec((None, SEQ, qk_w), lambda b: (b, 0, 1)),
            pl.BlockSpec((None, SEQ, v_w), lambda b: (b, 0, 1)),
            pl.BlockSpec((None, SEQ, v_w), lambda b: (b, 0, 2)),
            pl.BlockSpec((None, SEQ, LANES), lambda b: (b, 0, 0)),
            pl.BlockSpec((LANES, qk_w), lambda b: (0, 0)),
            pl.BlockSpec((1, qk_w), lambda b: (0, 0)),
            pl.BlockSpec((1, GLA_DV), lambda b: (0, 0)),
        ],
        out_specs=pl.BlockSpec((None, SEQ, v_w), lambda b: (b, 0, 0)),
        out_shape=jax.ShapeDtypeStruct((BATCH, SEQ, v_w), BF16),
        scratch_shapes=[
            pltpu.VMEM((SEQ, qk_w), F32),
            pltpu.VMEM((GLA_HEADS // 2, PAIR_V, PAIR), F32),
        ],
        compiler_params=pltpu.CompilerParams(
            dimension_semantics=("arbitrary",), vmem_limit_bytes=VMEM_LIMIT),
        name="gla",
    )(qkv_g, qkv_g, qkv_g, qkv_g, glr, w_up, b_up, gla_norm_g)


def _merge_kernel(oa_ref, ob_ref, gate_ref, x_ref, gt1_ref, sh2_ref, sc2_ref, g2_ref,
                  wpa_ref, wpb_ref, wo_ref, wr_ref, br_ref,
                  x1_ref, h2p_ref, eidx_ref, rank_ref, wts_ref, cnt_ref,
                  upper_ref, carry_ref):
    i = pl.program_id(0)

    @pl.when(i == 0)
    def _():
        rr = lax.broadcasted_iota(I32, (TM_MERGE, TM_MERGE), 0)
        cc = lax.broadcasted_iota(I32, (TM_MERGE, TM_MERGE), 1)
        upper_ref[...] = jnp.where(rr < cc, 1.0, 0.0).astype(BF16)
        carry_ref[...] = jnp.zeros_like(carry_ref)

    ga = gate_ref[:, :D_MODEL].astype(F32)
    gb = gate_ref[:, D_MODEL:].astype(F32)
    merged = (jax.nn.sigmoid(ga) * jnp.dot(oa_ref[...], wpa_ref[...], preferred_element_type=F32)
              + jax.nn.sigmoid(gb) * jnp.dot(ob_ref[...], wpb_ref[...], preferred_element_type=F32))
    y = jnp.dot(merged.astype(BF16), wo_ref[...], preferred_element_type=F32)
    x1 = x_ref[...] + gt1_ref[...] * y
    x1_ref[...] = x1
    inv = lax.rsqrt(jnp.mean(x1 * x1, axis=-1, keepdims=True) + EPS)
    h2 = (x1 * inv * g2_ref[...]) * (1.0 + sc2_ref[...]) + sh2_ref[...]
    h2p_ref[...] = _pack_halves(h2)

    h_hi, h_mid, _ = _split3(h2)
    w_hi, w_mid, _ = _split3(wr_ref[...])
    logits = (_nt_dot(w_hi, h_hi) + _nt_dot(w_hi, h_mid) + _nt_dot(w_mid, h_hi)) + br_ref[...]

    eio = lax.broadcasted_iota(I32, (N_EXPERTS, TM_MERGE), 0)
    vals, idxs, sels = [], [], []
    cur = logits
    for _k in range(TOP_K):
        m = jnp.max(cur, axis=0, keepdims=True)
        idx = jnp.min(jnp.where(cur == m, eio, N_EXPERTS), axis=0, keepdims=True)
        sel = eio == idx
        vals.append(m)
        idxs.append(idx)
        sels.append(sel)
        cur = jnp.where(sel, -jnp.inf, cur)
    es = [jnp.exp(v - vals[0]) for v in vals]
    tot = es[0] + es[1] + es[2] + es[3]
    onehot = jnp.zeros((N_EXPERTS, TM_MERGE), F32)
    for sel in sels:
        onehot = onehot + jnp.where(sel, 1.0, 0.0)
    before = jnp.dot(onehot.astype(BF16), upper_ref[...], preferred_element_type=F32) + carry_ref[:, 0:1]
    ranks = [jnp.sum(jnp.where(sel, before, 0.0), axis=0, keepdims=True) for sel in sels]
    carry_ref[...] = carry_ref[...] + jnp.sum(onehot, axis=1, keepdims=True)
    cnt_ref[...] = carry_ref[...]

    zi = jnp.zeros((8 - TOP_K, TM_MERGE), I32)
    zf = jnp.zeros((8 - TOP_K, TM_MERGE), F32)
    eidx_ref[...] = jnp.concatenate(idxs + [zi], axis=0)
    rank_ref[...] = jnp.concatenate([r.astype(I32) for r in ranks] + [zi], axis=0)
    wts_ref[...] = jnp.concatenate([e / tot for e in es] + [zf], axis=0)


def _merge_route(grp, o_a, o_b, gates, x2d, mod4, norm2_g, w_pa, w_pb, w_o, w_rt, b_r):
    ntile = N_GRP // TM_MERGE
    first = grp * ntile
    per_b = SEQ // TM_MERGE
    full = lambda shape: pl.BlockSpec(shape, lambda i: (0,) * len(shape))
    row_in = lambda w: pl.BlockSpec((TM_MERGE, w), lambda i: (first + i, 0))
    row = lambda w: pl.BlockSpec((TM_MERGE, w), lambda i: (i, 0))
    modspec = lambda j: pl.BlockSpec((None, None, 1, D_MODEL), lambda i: (j, (first + i) // per_b, 0, 0))
    col = pl.BlockSpec((8, TM_MERGE), lambda i: (0, i))
    return pl.pallas_call(
        _merge_kernel,
        grid=(ntile,),
        in_specs=[
            row_in(DIFF_HEADS * 2 * DIFF_DH), row_in(GLA_HEADS * GLA_DV), row_in(W_GATE), row_in(D_MODEL),
            modspec(2), modspec(3), modspec(4),
            full((1, D_MODEL)),
            full((DIFF_HEADS * 2 * DIFF_DH, D_MODEL)), full((GLA_HEADS * GLA_DV, D_MODEL)),
            full((D_MODEL, D_MODEL)),
            full((N_EXPERTS, D_MODEL)), full((N_EXPERTS, 1)),
        ],
        out_specs=[
            row(D_MODEL), row(ROW_W), col, col, col,
            pl.BlockSpec((N_EXPERTS, LANES), lambda i: (0, 0)),
        ],
        out_shape=[
            jax.ShapeDtypeStruct((N_GRP, D_MODEL), F32),
            jax.ShapeDtypeStruct((N_GRP, ROW_W), ROW_DT),
            jax.ShapeDtypeStruct((8, N_GRP), I32),
            jax.ShapeDtypeStruct((8, N_GRP), I32),
            jax.ShapeDtypeStruct((8, N_GRP), F32),
            jax.ShapeDtypeStruct((N_EXPERTS, LANES), F32),
        ],
        scratch_shapes=[
            pltpu.VMEM((TM_MERGE, TM_MERGE), BF16),
            pltpu.VMEM((N_EXPERTS, LANES), F32),
        ],
        compiler_params=pltpu.CompilerParams(
            dimension_semantics=("arbitrary",), vmem_limit_bytes=VMEM_LIMIT),
        name="merge_route",
    )(o_a, o_b, gates, x2d, mod4, mod4, mod4, norm2_g, w_pa, w_pb, w_o, w_rt, b_r)


TP = 4096
NB_PAD = ((N_BLK + LANES - 1) // LANES) * LANES


def _plan_kernel(cnt_ref, eidx_ref, rank_ref, dest_ref, be_ref, nv_ref, rows_ref):
    cnt = cnt_ref[...]
    padded = jnp.floor((cnt + (BLK - 1.0)) * (1.0 / BLK)) * BLK
    er = lax.broadcasted_iota(I32, (N_EXPERTS, N_EXPERTS), 0)
    ec = lax.broadcasted_iota(I32, (N_EXPERTS, N_EXPERTS), 1)
    lower = jnp.where(ec < er, 1.0, 0.0).astype(BF16)
    p_hi, p_mid, p_lo = _split3(padded)
    starts = (jnp.dot(lower, p_hi, preferred_element_type=F32)
              + jnp.dot(lower, p_mid, preferred_element_type=F32)
              + jnp.dot(lower, p_lo, preferred_element_type=F32))
    ends = starts + padded
    blk_start = (lax.broadcasted_iota(I32, (1, NB_PAD), 1) * BLK).astype(F32)
    n_before = jnp.sum(jnp.where(ends[:, 0:1] <= blk_start, 1.0, 0.0), axis=0, keepdims=True)
    be_ref[...] = jnp.minimum(n_before, N_EXPERTS - 1.0).astype(I32)
    nv_ref[...] = (jnp.max(ends, axis=0, keepdims=True) * (1.0 / BLK)).astype(I32)
    owner = (starts[:, 0:1] <= blk_start) & (blk_start < ends[:, 0:1])
    filled = jnp.sum(jnp.where(owner, (starts + cnt)[:, 0:1] - blk_start, 0.0), axis=0, keepdims=True)
    rows_ref[...] = jnp.clip(filled, 0.0, float(BLK)).astype(I32)

    eio = lax.broadcasted_iota(I32, (N_EXPERTS, TP), 0)
    rows = []
    for k in range(TOP_K):
        onehot = eio == eidx_ref[k:k + 1, :]
        base = jnp.sum(jnp.where(onehot, starts[:, 0:1], 0.0), axis=0, keepdims=True)
        rows.append(base.astype(I32) + rank_ref[k:k + 1, :])
    dest_ref[...] = jnp.concatenate(rows + [jnp.zeros((8 - TOP_K, TP), I32)], axis=0)


def _route_plan(cnt, eidx, rank):
    return pl.pallas_call(
        _plan_kernel,
        grid=(N_GRP // TP,),
        in_specs=[
            pl.BlockSpec((N_EXPERTS, LANES), lambda i: (0, 0)),
            pl.BlockSpec((8, TP), lambda i: (0, i)),
            pl.BlockSpec((8, TP), lambda i: (0, i)),
        ],
        out_specs=[
            pl.BlockSpec((8, TP), lambda i: (0, i)),
            pl.BlockSpec((1, NB_PAD), lambda i: (0, 0)),
            pl.BlockSpec((1, LANES), lambda i: (0, 0)),
            pl.BlockSpec((1, NB_PAD), lambda i: (0, 0)),
        ],
        out_shape=[
            jax.ShapeDtypeStruct((8, N_GRP), I32),
            jax.ShapeDtypeStruct((1, NB_PAD), I32),
            jax.ShapeDtypeStruct((1, LANES), I32),
            jax.ShapeDtypeStruct((1, NB_PAD), I32),
        ],
        compiler_params=pltpu.CompilerParams(dimension_semantics=("arbitrary",)),
        name="route_plan",
    )(cnt, eidx, rank)


def _ffn_kernel(be_ref, nv_ref, first_ref, slot_ref, nxt_ref, rows_ref, x_ref, wgu_hbm, bgu_ref, wd_hbm, bd_ref,
                y_ref, wgu_buf, wd_buf, sem):
    i = pl.program_id(0)
    valid = i < nv_ref[0]
    filled = jnp.maximum(rows_ref[i], 1)
    s = slot_ref[i]

    def weight_copies(e, slot):
        return (pltpu.make_async_copy(wgu_hbm.at[e], wgu_buf.at[slot], sem.at[slot, 0]),
                pltpu.make_async_copy(wd_hbm.at[e], wd_buf.at[slot], sem.at[slot, 1]))

    @pl.when(i == 0)
    def _():
        for cp in weight_copies(be_ref[0], 0):
            cp.start()

    @pl.when(valid & (first_ref[i] == 1))
    def _():
        for cp in weight_copies(be_ref[i], s):
            cp.wait()

        @pl.when(nxt_ref[i] >= 0)
        def _():
            for cp in weight_copies(nxt_ref[i], 1 - s):
                cp.start()

    def expert_rows(n):
        lo, hi = _unpack_halves(x_ref[:n, :])
        x = jnp.concatenate([lo, hi], axis=1).astype(BF16)
        mm = lambda a, w: lax.dot_general(a, w, (((1,), (0,)), ((), ())), preferred_element_type=F32)
        gu = mm(x, wgu_buf[s]) + bgu_ref[...]
        gate = jnp.minimum(gu[:, :D_FF], SWIGLU_LIMIT)
        up = jnp.clip(gu[:, D_FF:], -SWIGLU_LIMIT, SWIGLU_LIMIT)
        act = (up + 1.0) * (gate * jax.nn.sigmoid(SWIGLU_ALPHA * gate))
        y = mm(act.astype(BF16), wd_buf[s]) + bd_ref[...]
        y_ref[:n, :] = _pack_halves(y)

    for n in range(BLK_STEP, BLK + 1, BLK_STEP):
        @pl.when(valid & (filled > n - BLK_STEP) & (filled <= n))
        def _(n=n):
            expert_rows(n)
            if n < BLK:
                y_ref[n:, :] = jnp.zeros((BLK - n, ROW_W), ROW_DT)

    @pl.when(jnp.logical_not(valid))
    def _():
        y_ref[...] = jnp.zeros_like(y_ref)


def _expert_ffn(blk_expert, n_valid, blk_rows, xb, w_gate_up, b_gate_up, w_down, b_down):
    idx = jnp.arange(N_BLK, dtype=I32)
    used = idx < n_valid[0]
    first = (used & ((idx == 0) | (blk_expert != jnp.roll(blk_expert, 1)))).astype(I32)
    slot = (jnp.cumsum(first) - 1) & 1
    later = used[None, :] & (idx[None, :] > idx[:, None]) & (blk_expert[None, :] != blk_expert[:, None])
    nxt_pos = jnp.min(jnp.where(later, idx[None, :], N_BLK), axis=1)
    nxt = jnp.where(nxt_pos < N_BLK, blk_expert[jnp.minimum(nxt_pos, N_BLK - 1)], -1).astype(I32)

    blockwise = lambda i, *_: (i, 0)
    per_expert = lambda i, be, *_: (be[i], 0, 0)
    grid_spec = pltpu.PrefetchScalarGridSpec(
        num_scalar_prefetch=6,
        grid=(N_BLK,),
        in_specs=[
            pl.BlockSpec((BLK, ROW_W), blockwise),
            pl.BlockSpec(memory_space=pl.ANY),
            pl.BlockSpec((None, 1, 2 * D_FF), per_expert),
            pl.BlockSpec(memory_space=pl.ANY),
            pl.BlockSpec((None, 1, D_MODEL), per_expert),
        ],
        out_specs=pl.BlockSpec((BLK, ROW_W), blockwise),
        scratch_shapes=[
            pltpu.VMEM((2, D_MODEL, 2 * D_FF), F32),
            pltpu.VMEM((2, D_FF, D_MODEL), F32),
            pltpu.SemaphoreType.DMA((2, 2)),
        ],
    )
    return pl.pallas_call(
        _ffn_kernel,
        grid_spec=grid_spec,
        out_shape=jax.ShapeDtypeStruct((P_ROWS, ROW_W), ROW_DT),
        compiler_params=pltpu.CompilerParams(
            dimension_semantics=("arbitrary",), vmem_limit_bytes=VMEM_LIMIT),
        name="expert_ffn",
    )(blk_expert, n_valid, first, slot.astype(I32), nxt, blk_rows, xb, w_gate_up,
      b_gate_up.reshape(N_EXPERTS, 1, 2 * D_FF), w_down, b_down.reshape(N_EXPERTS, 1, D_MODEL))


def _final_kernel(x1_ref, y0_ref, y1_ref, y2_ref, y3_ref, w_ref, gt2_ref, g_ref, *rest):
    o_ref = rest[-1]
    w = w_ref[...].T
    ylo = jnp.zeros((TM_FIN, HALF), F32)
    yhi = jnp.zeros((TM_FIN, HALF), F32)
    for k, y_ref in enumerate((y0_ref, y1_ref, y2_ref, y3_ref)):
        lo, hi = _unpack_halves(y_ref[...])
        wk = w[:, k:k + 1]
        ylo = ylo + wk * lo
        yhi = yhi + wk * hi
    y = jnp.concatenate([ylo, yhi], axis=1)
    x2 = x1_ref[...] + gt2_ref[...] * y
    inv = lax.rsqrt(jnp.mean(x2 * x2, axis=-1, keepdims=True) + EPS)
    o_ref[...] = x2 * inv * g_ref[...]


def _final(grp, x1, yg, w4, mod4, final_norm_g, out_so_far):
    per_b = SEQ // TM_FIN
    ntile = N_GRP // TM_FIN
    first = grp * ntile
    slot = lambda k: pl.BlockSpec((TM_FIN, ROW_W), lambda i: (k * ntile + i, 0))
    in_specs = [
        pl.BlockSpec((TM_FIN, D_MODEL), lambda i: (i, 0)),
        slot(0), slot(1), slot(2), slot(3),
        pl.BlockSpec((8, TM_FIN), lambda i: (0, i)),
        pl.BlockSpec((None, None, 1, D_MODEL), lambda i: (5, (first + i) // per_b, 0, 0)),
        pl.BlockSpec((1, D_MODEL), lambda i: (0, 0)),
    ]
    args = [x1, yg, yg, yg, yg, w4, mod4, final_norm_g]
    aliases = {}
    if out_so_far is not None:
        in_specs.append(pl.BlockSpec(memory_space=pl.ANY))
        args.append(out_so_far)
        aliases = {len(args) - 1: 0}
    return pl.pallas_call(
        _final_kernel,
        grid=(ntile,),
        in_specs=in_specs,
        out_specs=pl.BlockSpec((TM_FIN, D_MODEL), lambda i: (first + i, 0)),
        out_shape=jax.ShapeDtypeStruct((N_TOK, D_MODEL), F32),
        input_output_aliases=aliases,
        compiler_params=pltpu.CompilerParams(
            dimension_semantics=("arbitrary",), vmem_limit_bytes=VMEM_LIMIT),
        name="combine_final",
    )(*args)


SC_CORES = 2
SC_SUBCORES = 16
SC_WORKERS = SC_CORES * SC_SUBCORES
SC_CHUNK = 64


def _sc_mesh():
    return plsc.VectorSubcoreMesh(core_axis_name="c", subcore_axis_name="s")


def _row_buffers():
    return ([pltpu.VMEM((SC_CHUNK, HALF), U32)] * 2 + [pltpu.SemaphoreType.DMA] * 5)


def _dispatch_rows(h2p, dest2d):
    per_w = N_GRP // SC_WORKERS
    nchunk = per_w // SC_CHUNK
    rows_per_k = N_GRP // SC_CHUNK
    assert nchunk % 2 == 0

    @functools.partial(
        pl.kernel, mesh=_sc_mesh(), out_type=jax.ShapeDtypeStruct((P_ROWS, HALF), U32),
        scratch_types=[pltpu.VMEM((TOP_K * nchunk, SC_CHUNK), I32)] + _row_buffers(),
        name="moe_dispatch")
    def k(src_hbm, dest_hbm, out_hbm, idx_v, buf0, buf1, isem, l0, l1, s0, s1):
        wid = lax.axis_index("s") * SC_CORES + lax.axis_index("c")
        bufs, lsem, ssem = (buf0, buf1), (l0, l1), (s0, s1)
        idx_loads = [
            pltpu.make_async_copy(dest_hbm.at[pl.ds(kk * rows_per_k + wid * nchunk, nchunk)],
                                  idx_v.at[pl.ds(kk * nchunk, nchunk)], isem) for kk in range(TOP_K)]

        def load(j, b):
            return pltpu.make_async_copy(src_hbm.at[pl.ds(wid * per_w + j * SC_CHUNK, SC_CHUNK)], bufs[b], lsem[b])

        def scatters(j, b):
            return [pltpu.make_async_copy(bufs[b], out_hbm.at[idx_v.at[kk * nchunk + j]], ssem[b])
                    for kk in range(TOP_K)]

        for cp in idx_loads:
            cp.start()
        load(0, 0).start()
        for cp in idx_loads:
            cp.wait()

        @pl.loop(0, nchunk // 2)
        def _(i):
            for b in range(2):
                j = 2 * i + b

                @pl.when(j >= 1)
                def _():
                    for cp in scatters(j - 1, 1 - b):
                        cp.wait()

                @pl.when(j + 1 < nchunk)
                def _():
                    load(j + 1, 1 - b).start()

                load(j, b).wait()
                for cp in scatters(j, b):
                    cp.start()

        for cp in scatters(nchunk - 1, 1):
            cp.wait()

    return k(h2p, dest2d)


def _combine_rows(yb, dest2d):
    n_out = TOP_K * N_GRP
    per_w = n_out // SC_WORKERS
    nchunk = per_w // SC_CHUNK
    assert nchunk % 2 == 0

    @functools.partial(
        pl.kernel, mesh=_sc_mesh(), out_type=jax.ShapeDtypeStruct((n_out, HALF), U32),
        scratch_types=[pltpu.VMEM((nchunk, SC_CHUNK), I32)] + _row_buffers(),
        name="moe_combine")
    def k(tab_hbm, idx_hbm, out_hbm, idx_v, buf0, buf1, isem, g0, g1, s0, s1):
        wid = lax.axis_index("s") * SC_CORES + lax.axis_index("c")
        bufs, gsem, ssem = (buf0, buf1), (g0, g1), (s0, s1)
        idx_load = pltpu.make_async_copy(idx_hbm.at[pl.ds(wid * nchunk, nchunk)], idx_v, isem)

        def gather(j, b):
            return pltpu.make_async_copy(tab_hbm.at[idx_v.at[j]], bufs[b], gsem[b])

        def store(j, b):
            return pltpu.make_async_copy(bufs[b], out_hbm.at[pl.ds(wid * per_w + j * SC_CHUNK, SC_CHUNK)], ssem[b])

        idx_load.start()
        idx_load.wait()
        gather(0, 0).start()

        @pl.loop(0, nchunk // 2)
        def _(i):
            for b in range(2):
                j = 2 * i + b

                @pl.when(j >= 1)
                def _():
                    store(j - 1, 1 - b).wait()

                @pl.when(j + 1 < nchunk)
                def _():
                    gather(j + 1, 1 - b).start()

                gather(j, b).wait()
                store(j, b).start()

        store(nchunk - 1, 1).wait()

    return k(yb, dest2d)


def _lambda_kernel(p_ref, o_ref):
    p = p_ref[...]
    s1 = jnp.sum(p[0:1] * p[1:2], axis=-1, keepdims=True)
    s2 = jnp.sum(p[2:3] * p[3:4], axis=-1, keepdims=True)
    o_ref[...] = jnp.broadcast_to(jnp.exp(s1) - jnp.exp(s2) + LAMBDA_INIT, (1, LANES))


def kernel(x, c, w_ada, b_ada, norm1_g, w_in, lambda_q1, lambda_k1, lambda_q2, lambda_k2, diff_norm_g, w_alpha_up, b_alpha, gla_norm_g, w_branch_diff, w_branch_gla, w_out, norm2_g, w_router, b_router, w_gate_up, b_gate_up, w_down, b_down, final_norm_g):
    w_in0 = w_in[0]
    c_a, c_g = W_A, W_A + W_G
    w_a = w_in0[:, :c_a].astype(BF16)
    w_g = w_in0[:, c_a:c_g].astype(BF16)
    w_lr = jnp.pad(w_in0[:, c_g:c_g + GLA_RANK], ((0, 0), (0, LANES - GLA_RANK))).astype(BF16)
    w_gate = w_in0[:, c_g + GLA_RANK:].astype(BF16)
    w_up = jnp.pad(w_alpha_up[0], ((0, LANES - GLA_RANK), (0, 0)))
    lam_in = jnp.concatenate([lambda_q1, lambda_k1, lambda_q2, lambda_k2], axis=0)
    slopes = jnp.asarray(2.0 ** (-8.0 * np.arange(1, DIFF_HEADS + 1) / DIFF_HEADS), dtype=F32)

    mod = _modulation(c, w_ada[0], b_ada[0])
    mod4 = mod.reshape(N_MOD, BATCH, 1, D_MODEL)
    lam = pl.pallas_call(
        _lambda_kernel, out_shape=jax.ShapeDtypeStruct((1, LANES), F32), name="lambda")(lam_in)[0, :1]

    qkv_a, qkv_g, gates, glr = _in_proj(x, mod4, norm1_g, w_a, w_g, w_gate, w_lr)
    pos = jnp.arange(SEQ, dtype=I32)
    p_hi = (pos >> POS_LO_BITS).astype(F32)
    p_lo = (pos & (POS_LO - 1)).astype(F32)
    kaug = jnp.zeros((SEQ, LANES), F32).at[:, 0].set(p_hi).at[:, 1].set(p_hi).at[:, 2].set(p_lo).at[:, 3].set(
        p_lo).astype(BF16)
    o_a = _diff_attention(qkv_a, slopes, lam, kaug, diff_norm_g)
    o_b = _gla(qkv_g, glr, w_up, b_alpha, gla_norm_g)

    merge_args = (o_a.reshape(N_TOK, -1), o_b.reshape(N_TOK, -1), gates.reshape(N_TOK, W_GATE),
                  x.reshape(N_TOK, D_MODEL), mod4, norm2_g,
                  w_branch_diff[0].astype(BF16), w_branch_gla[0].astype(BF16), w_out[0].astype(BF16),
                  w_router[0].T, b_router[0].reshape(N_EXPERTS, 1))
    fin_g = final_norm_g.reshape(1, D_MODEL)

    routed = []
    for grp in range(MOE_GROUPS):
        x1, h2p, eidx, rank, wts, cnt = _merge_route(grp, *merge_args)
        dest8, be, nv, rows = _route_plan(cnt, eidx, rank)
        dest = dest8[:TOP_K].reshape(-1, SC_CHUNK)
        routed.append((x1, wts, dest, be, nv, rows, _dispatch_rows(h2p, dest)))
    gathered = []
    for x1, wts, dest, be, nv, rows, xb in routed:
        yb = _expert_ffn(be[0, :N_BLK], nv[0, :1], rows[0, :N_BLK], xb,
                         w_gate_up[0], b_gate_up[0], w_down[0], b_down[0])
        gathered.append(_combine_rows(yb, dest))
    out = None
    for grp, ((x1, wts, *_), yg) in enumerate(zip(routed, gathered)):
        out = _final(grp, x1, yg, wts, mod4, fin_g, out)
    return out.reshape(BATCH, SEQ, D_MODEL)
```

```python
import functools
import math

import jax
import jax.numpy as jnp
import numpy as np
from jax import lax
from jax.experimental import pallas as pl
from jax.experimental.pallas import tpu as pltpu
from jax.experimental.pallas import tpu_sc as plsc

F32 = jnp.float32
BF16 = jnp.bfloat16
U32 = jnp.uint32
I32 = jnp.int32

D_MODEL = 1024
BATCH = 16
SEQ = 2048
N_TOK = BATCH * SEQ
CHUNK = 64
DIFF_HEADS = 4
DIFF_DH = 64
GLA_HEADS = 4
GLA_DK = 64
GLA_DV = 128
GLA_RANK = 16
GLA_GATE_NORM = 16.0
N_EXPERTS = 32
TOP_K = 4
D_FF = D_MODEL
SWIGLU_LIMIT = 7.0
SWIGLU_ALPHA = 1.702
N_MOD = 6
EPS = 1e-6
LAMBDA_INIT = 0.8 - 0.6 * math.exp(-0.3 * 0)

LANES = 128
HALF = D_MODEL // 2
ROW_W = HALF
ROW_DT = U32

TM_IN = 1024
TQ = 512
ATTN_STRIPS = 4
ATTN_W = 2 * TQ // ATTN_STRIPS
VT_ROWS = 2 * DIFF_DH + 16
LOG2E = math.log2(math.e)
TM_MERGE = 512
BLK = 1024
BLK_STEP = 256
MOE_GROUPS = 1
N_GRP = N_TOK // MOE_GROUPS
N_BLK = (N_GRP * TOP_K) // BLK + N_EXPERTS
P_ROWS = N_BLK * BLK
TM_FIN = 512
VMEM_LIMIT = 56 * 1024 * 1024
CHUNK_SHIFT = CHUNK.bit_length() - 1
POS_LO_BITS = 3
POS_LO = 1 << POS_LO_BITS


def _nt_dot(a, b):
    return lax.dot_general(a, b, (((1,), (1,)), ((), ())), preferred_element_type=F32)


def _tn_dot(a, b):
    return lax.dot_general(a, b, (((0,), (0,)), ((), ())), preferred_element_type=F32)


def _split3(x):
    hi = x.astype(BF16)
    r1 = x - hi.astype(F32)
    mid = r1.astype(BF16)
    lo = (r1 - mid.astype(F32)).astype(BF16)
    return hi, mid, lo


def _pack_halves(y):
    return pltpu.pack_elementwise([y[:, :HALF], y[:, HALF:]], packed_dtype=BF16)


def _unpack_halves(u):
    lo = pltpu.unpack_elementwise(u, index=0, packed_dtype=BF16, unpacked_dtype=F32)
    hi = pltpu.unpack_elementwise(u, index=1, packed_dtype=BF16, unpacked_dtype=F32)
    return lo, hi


def _mod_kernel(c_ref, w_ref, b_ref, o_ref):
    c = c_ref[...]
    s = c * jax.nn.sigmoid(c)
    o_ref[0] = jnp.dot(s.astype(BF16), w_ref[...].astype(BF16),
                       preferred_element_type=F32) + b_ref[...]


def _modulation(c, w_ada, b_ada):
    return pl.pallas_call(
        _mod_kernel,
        grid=(N_MOD,),
        in_specs=[
            pl.BlockSpec((BATCH, D_MODEL), lambda j: (0, 0)),
            pl.BlockSpec((D_MODEL, D_MODEL), lambda j: (0, j)),
            pl.BlockSpec((1, D_MODEL), lambda j: (0, j)),
        ],
        out_specs=pl.BlockSpec((1, BATCH, D_MODEL), lambda j: (j, 0, 0)),
        out_shape=jax.ShapeDtypeStruct((N_MOD, BATCH, D_MODEL), F32),
        compiler_params=pltpu.CompilerParams(dimension_semantics=("arbitrary",)),
        name="adaln_mod",
    )(c, w_ada, b_ada.reshape(1, N_MOD * D_MODEL))


W_A = 3 * DIFF_HEADS * 2 * DIFF_DH
W_G = 2 * GLA_HEADS * GLA_DK + 2 * GLA_HEADS * GLA_DV
W_GATE = 2 * D_MODEL


def _in_kernel(x_ref, sh_ref, sc_ref, g_ref, wa_ref, wg_ref, wgate_ref, wlr_ref,
               oa_ref, og_ref, ogate_ref, olr_ref):
    x = x_ref[...]
    inv = lax.rsqrt(jnp.mean(x * x, axis=-1, keepdims=True) + EPS)
    h = (x * inv * g_ref[...]) * (1.0 + sc_ref[...]) + sh_ref[...]
    hb = h.astype(BF16)
    oa_ref[...] = jnp.dot(hb, wa_ref[...], preferred_element_type=F32).astype(BF16)
    og_ref[...] = jnp.dot(hb, wg_ref[...], preferred_element_type=F32).astype(BF16)
    ogate_ref[...] = jnp.dot(hb, wgate_ref[...], preferred_element_type=F32).astype(BF16)
    olr_ref[...] = jnp.dot(hb, wlr_ref[...], preferred_element_type=F32)


def _in_proj(x, mod4, norm1_g, w_a, w_g, w_gate, w_lr):
    nrow = SEQ // TM_IN
    full = lambda shape: pl.BlockSpec(shape, lambda b, i: (0,) * len(shape), pipeline_mode=pl.Buffered(1))
    return pl.pallas_call(
        _in_kernel,
        grid=(BATCH, nrow),
        in_specs=[
            pl.BlockSpec((None, TM_IN, D_MODEL), lambda b, i: (b, i, 0)),
            pl.BlockSpec((None, None, 1, D_MODEL), lambda b, i: (0, b, 0, 0)),
            pl.BlockSpec((None, None, 1, D_MODEL), lambda b, i: (1, b, 0, 0)),
            full((1, D_MODEL)),
            full((D_MODEL, W_A)),
            full((D_MODEL, W_G)),
            full((D_MODEL, W_GATE)),
            full((D_MODEL, LANES)),
        ],
        out_specs=[
            pl.BlockSpec((None, TM_IN, W_A), lambda b, i: (b, i, 0)),
            pl.BlockSpec((None, TM_IN, W_G), lambda b, i: (b, i, 0)),
            pl.BlockSpec((None, TM_IN, W_GATE), lambda b, i: (b, i, 0)),
            pl.BlockSpec((None, TM_IN, LANES), lambda b, i: (b, i, 0)),
        ],
        out_shape=[
            jax.ShapeDtypeStruct((BATCH, SEQ, W_A), BF16),
            jax.ShapeDtypeStruct((BATCH, SEQ, W_G), BF16),
            jax.ShapeDtypeStruct((BATCH, SEQ, W_GATE), BF16),
            jax.ShapeDtypeStruct((BATCH, SEQ, LANES), F32),
        ],
        compiler_params=pltpu.CompilerParams(
            dimension_semantics=("arbitrary", "arbitrary"), vmem_limit_bytes=VMEM_LIMIT),
        name="in_proj",
    )(x, mod4, mod4, norm1_g, w_a, w_g, w_gate, w_lr)


def _attn_kernel(slope_ref, lam_ref, q_ref, k_ref, v_ref, kaug_ref, g_ref, o_ref,
                 sa_ref, sb_ref, lha_ref, lhb_ref, corr_ref, vt_ref, m_ref, acc_ref):
    c_alibi = slope_ref[pl.program_id(1)] * LOG2E
    lam = lam_ref[0]
    lane = lax.broadcasted_iota(I32, (TQ, 2 * DIFF_DH), 1)
    c_vec = jnp.full((TQ, 2 * DIFF_DH), c_alibi, F32)
    c_hi = c_vec.astype(BF16).astype(F32)
    c_lo = (c_vec - c_hi).astype(BF16).astype(F32)
    zero = jnp.zeros((TQ, 2 * DIFF_DH), BF16)
    aug = jnp.where(lane == 0, POS_LO * c_hi, jnp.where(lane == 1, POS_LO * c_lo,
                    jnp.where(lane == 2, c_hi, jnp.where(lane == 3, c_lo, 0.0)))).astype(BF16)

    def blk(j):
        return slice(j * TQ, (j + 1) * TQ)

    def stacked_queries(qi):
        q = (q_ref[blk(qi), :].astype(F32) * (DIFF_DH ** -0.5 * LOG2E)).astype(BF16)
        return jnp.concatenate([
            jnp.concatenate([jnp.where(lane < DIFF_DH, q, zero), aug], axis=1),
            jnp.concatenate([jnp.where(lane >= DIFF_DH, q, zero), aug], axis=1)], axis=0)

    lhs_refs = (lha_ref, lhb_ref)

    def strip(c):
        return slice(c * ATTN_W, (c + 1) * ATTN_W)

    def keys(j):
        return jnp.concatenate([k_ref[blk(j), :], kaug_ref[blk(j), :]], axis=1)

    def scores(kk, qi, c):
        return _nt_dot(kk, lhs_refs[qi % 2][strip(c), :])

    ones_rows = jnp.where(lax.broadcasted_iota(I32, (VT_ROWS - 2 * DIFF_DH, TQ), 0) == 0, 1.0, 0.0).astype(BF16)
    for j in range(SEQ // TQ):
        vt_ref[j, :2 * DIFF_DH, :] = v_ref[blk(j), :].astype(F32).T.astype(BF16)
        vt_ref[j, 2 * DIFF_DH:, :] = ones_rows

    def update(s, j, c, first):
        nk = s.shape[0]
        if first:
            m_new = jnp.max(s, axis=0, keepdims=True)
            p = jnp.exp2((s - m_new).astype(BF16))
            acc_ref[:, strip(c)] = jnp.dot(vt_ref[j, :, :nk], p, preferred_element_type=F32)
        else:
            m = m_ref[:, strip(c)]
            m_new = jnp.maximum(m, jnp.max(s, axis=0, keepdims=True))
            alpha = jnp.exp2(m - m_new)
            p = jnp.exp2((s - m_new).astype(BF16))
            acc_ref[:, strip(c)] = (alpha * acc_ref[:, strip(c)]
                                    + jnp.dot(vt_ref[j, :, :nk], p, preferred_element_type=F32))
        m_ref[:, strip(c)] = m_new

    def n_keys(qi, j, c):
        return TQ if j < qi else min(TQ, (c * ATTN_W) % TQ + ATTN_W)

    kr = lax.broadcasted_iota(I32, (TQ, TQ), 0)
    qc = lax.broadcasted_iota(I32, (TQ, TQ), 1)
    ahead = jnp.maximum(kr - qc, 0).astype(F32)
    corr_ref[...] = jnp.where((qc >> CHUNK_SHIFT) >= (kr >> CHUNK_SHIFT), (-2.0 * c_alibi) * ahead, -jnp.inf)

    pairs = [(qi, j) for qi in range(SEQ // TQ) for j in range(qi + 1)]
    bufs = (sa_ref, sb_ref)
    lhs_refs[0][...] = stacked_queries(0)
    kk = keys(0)
    for c in range(ATTN_STRIPS):
        nk = n_keys(0, 0, c)
        bufs[0][:nk, strip(c)] = scores(kk[:nk], 0, c)
    for t, (qi, j) in enumerate(pairs):
        nxt = pairs[t + 1] if t + 1 < len(pairs) else None
        if nxt is not None:
            if nxt[0] != qi:
                lhs_refs[nxt[0] % 2][...] = stacked_queries(nxt[0])
            kk = keys(nxt[1])
        for c in range(ATTN_STRIPS):
            if nxt is not None:
                nk = n_keys(*nxt, c)
                bufs[(t + 1) % 2][:nk, strip(c)] = scores(kk[:nk], nxt[0], c)
            nk = n_keys(qi, j, c)
            s = bufs[t % 2][:nk, strip(c)]
            if j == qi:
                lo = (c * ATTN_W) % TQ
                s = s + corr_ref[:nk, lo:lo + ATTN_W]
            update(s, j, c, first=(j == 0))
        if j == qi:
            ot = acc_ref[:2 * DIFF_DH, :] / acc_ref[2 * DIFF_DH:2 * DIFF_DH + 1, :]
            o = (ot[:, :TQ] - lam * ot[:, TQ:]).T
            inv = lax.rsqrt(jnp.mean(o * o, axis=-1, keepdims=True) + EPS)
            o_ref[blk(qi), :] = (o * inv * g_ref[...] * (1.0 - LAMBDA_INIT)).astype(BF16)


def _diff_attention(qkv_a, slopes, lam, kaug, diff_norm_g):
    return pl.pallas_call(
        _attn_kernel,
        grid=(BATCH, DIFF_HEADS),
        in_specs=[
            pl.BlockSpec(memory_space=pltpu.SMEM),
            pl.BlockSpec(memory_space=pltpu.SMEM),
            pl.BlockSpec((None, SEQ, LANES), lambda b, h: (b, 0, h)),
            pl.BlockSpec((None, SEQ, LANES), lambda b, h: (b, 0, DIFF_HEADS + h)),
            pl.BlockSpec((None, SEQ, LANES), lambda b, h: (b, 0, 2 * DIFF_HEADS + h)),
            pl.BlockSpec((SEQ, LANES), lambda b, h: (0, 0)),
            pl.BlockSpec((1, LANES), lambda b, h: (0, 0)),
        ],
        out_specs=pl.BlockSpec((None, SEQ, LANES), lambda b, h: (b, 0, h)),
        out_shape=jax.ShapeDtypeStruct((BATCH, SEQ, DIFF_HEADS * 2 * DIFF_DH), BF16),
        scratch_shapes=[
            pltpu.VMEM((TQ, 2 * TQ), F32), pltpu.VMEM((TQ, 2 * TQ), F32),
            pltpu.VMEM((2 * TQ, 4 * DIFF_DH), BF16), pltpu.VMEM((2 * TQ, 4 * DIFF_DH), BF16),
            pltpu.VMEM((TQ, TQ), F32),
            pltpu.VMEM((SEQ // TQ, VT_ROWS, TQ), BF16),
            pltpu.VMEM((1, 2 * TQ), F32),
            pltpu.VMEM((VT_ROWS, 2 * TQ), F32),
        ],
        compiler_params=pltpu.CompilerParams(
            dimension_semantics=("arbitrary", "arbitrary")),
        name="diff_attn",
    )(slopes, lam, qkv_a, qkv_a, qkv_a, kaug, diff_norm_g)


N_CHUNK = SEQ // CHUNK
PAIR = 2 * GLA_DK
PAIR_V = 2 * GLA_DV
CS_ROWS = 256
GLA_UNROLL = 8


def _gla_kernel(q_ref, k_ref, v_ref, r_ref, lr_ref, wup_ref, bup_ref, g_ref, o_ref,
                gcum_ref, state_ref):
    w_hi, w_mid, _ = _split3(wup_ref[...])
    rr = lax.broadcasted_iota(I32, (CS_ROWS, CS_ROWS), 0)
    cc = lax.broadcasted_iota(I32, (CS_ROWS, CS_ROWS), 1)
    tri = jnp.where(((rr >> CHUNK_SHIFT) == (cc >> CHUNK_SHIFT)) & (cc <= rr), 1.0, 0.0).astype(BF16)
    for blk in range(SEQ // CS_ROWS):
        rows = pl.ds(blk * CS_ROWS, CS_ROWS)
        a_hi, a_mid, _ = _split3(lr_ref[rows, :])
        z = (jnp.dot(a_hi, w_hi, preferred_element_type=F32)
             + jnp.dot(a_hi, w_mid, preferred_element_type=F32)
             + jnp.dot(a_mid, w_hi, preferred_element_type=F32)) + bup_ref[...]
        la = (jnp.minimum(z, 0.0) - jnp.log(1.0 + jnp.exp(-jnp.abs(z)))) * (1.0 / GLA_GATE_NORM)
        l_hi, l_mid, l_lo = _split3(la)
        gcum_ref[rows, :] = (jnp.dot(tri, l_hi, preferred_element_type=F32)
                             + jnp.dot(tri, l_mid, preferred_element_type=F32)
                             + jnp.dot(tri, l_lo, preferred_element_type=F32))

    state_ref[...] = jnp.zeros_like(state_ref)
    lane_k = lax.broadcasted_iota(I32, (1, PAIR), 1)
    row_v = lax.broadcasted_iota(I32, (PAIR_V, PAIR), 0)
    col_k = lax.broadcasted_iota(I32, (PAIR_V, PAIR), 1)
    same_head = (row_v >= GLA_DV) == (col_k >= GLA_DK)
    cr = lax.broadcasted_iota(I32, (CHUNK, CHUNK), 0)
    cs = lax.broadcasted_iota(I32, (CHUNK, CHUNK), 1)
    causal = cs <= cr
    scale = GLA_DK ** -0.5

    def chunk(n):
        rows = pl.ds(pl.multiple_of(n * CHUNK, CHUNK), CHUNK)
        gc = gcum_ref[rows, :]
        g_last = gcum_ref[pl.ds(n * CHUNK + CHUNK - 1, 1), :]
        qf = q_ref[rows, :].astype(F32) * scale
        kf = k_ref[rows, :].astype(F32)
        q_s = (qf * jnp.exp(gc)).astype(BF16)
        k_s = (kf * jnp.exp(-gc)).astype(BF16)
        k_d = (kf * jnp.exp(g_last - gc)).astype(BF16)
        decay = jnp.exp(g_last)
        for pr in range(GLA_HEADS // 2):
            kl = slice(pr * PAIR, (pr + 1) * PAIR)
            vl = slice(pr * PAIR_V, (pr + 1) * PAIR_V)
            qs_p, ks_p, kd_p = q_s[:, kl], k_s[:, kl], k_d[:, kl]
            v_p = v_ref[rows, vl]
            st = state_ref[pr]
            o_inter = _nt_dot(qs_p, st.astype(BF16))
            d_st = _tn_dot(v_p, kd_p)
            state_ref[pr] = st * decay[:, kl] + jnp.where(same_head, d_st, 0.0)
            for sub in range(2):
                hd = 2 * pr + sub
                in_head = (lane_k >= sub * GLA_DK) & (lane_k < (sub + 1) * GLA_DK)
                a = _nt_dot(jnp.where(in_head, qs_p, jnp.zeros_like(qs_p)), ks_p)
                a = jnp.where(causal, a, 0.0).astype(BF16)
                vs = slice(hd * GLA_DV, (hd + 1) * GLA_DV)
                o = (jnp.dot(a, v_ref[rows, vs], preferred_element_type=F32)
                     + o_inter[:, sub * GLA_DV:(sub + 1) * GLA_DV])
                inv = lax.rsqrt(jnp.mean(o * o, axis=-1, keepdims=True) + EPS)
                r = r_ref[rows, vs].astype(F32)
                o_ref[rows, vs] = (o * inv * g_ref[...] * (r * jax.nn.sigmoid(r))).astype(BF16)

    def chunk_group(t, _):
        for u in range(GLA_UNROLL):
            chunk(t * GLA_UNROLL + u)
        return 0

    lax.fori_loop(0, N_CHUNK // GLA_UNROLL, chunk_group, 0)


def _gla(qkv_g, glr, w_up, b_up, gla_norm_g):
    qk_w = GLA_HEADS * GLA_DK
    v_w = GLA_HEADS * GLA_DV
    return pl.pallas_call(
        _gla_kernel,
        grid=(BATCH,),
        in_specs=[
            pl.BlockSpec((None, SEQ, qk_w), lambda b: (b, 0, 0)),
            pl.BlockSpec((None, SEQ, qk_w), lambda b: (b, 0, 1)),
            pl.BlockSpec((None, SEQ, v_w), lambda b: (b, 0, 1)),
            pl.BlockSpec((None, SEQ, v_w), lambda b: (b, 0, 2)),
            pl.BlockSpec((None, SEQ, LANES), lambda b: (b, 0, 0)),
            pl.BlockSpec((LANES, qk_w), lambda b: (0, 0)),
            pl.BlockSpec((1, qk_w), lambda b: (0, 0)),
            pl.BlockSpec((1, GLA_DV), lambda b: (0, 0)),
        ],
        out_specs=pl.BlockSpec((None, SEQ, v_w), lambda b: (b, 0, 0)),
        out_shape=jax.ShapeDtypeStruct((BATCH, SEQ, v_w), BF16),
        scratch_shapes=[
            pltpu.VMEM((SEQ, qk_w), F32),
            pltpu.VMEM((GLA_HEADS // 2, PAIR_V, PAIR), F32),
        ],
        compiler_params=pltpu.CompilerParams(
            dimension_semantics=("arbitrary",), vmem_limit_bytes=VMEM_LIMIT),
        name="gla",
    )(qkv_g, qkv_g, qkv_g, qkv_g, glr, w_up, b_up, gla_norm_g)


def _merge_kernel(oa_ref, ob_ref, gate_ref, x_ref, gt1_ref, sh2_ref, sc2_ref, g2_ref,
                  wpa_ref, wpb_ref, wo_ref, wr_ref, br_ref,
                  x1_ref, h2p_ref, eidx_ref, rank_ref, wts_ref, cnt_ref,
                  upper_ref, carry_ref):
    i = pl.program_id(0)

    @pl.when(i == 0)
    def _():
        rr = lax.broadcasted_iota(I32, (TM_MERGE, TM_MERGE), 0)
        cc = lax.broadcasted_iota(I32, (TM_MERGE, TM_MERGE), 1)
        upper_ref[...] = jnp.where(rr < cc, 1.0, 0.0).astype(BF16)
        carry_ref[...] = jnp.zeros_like(carry_ref)

    ga = gate_ref[:, :D_MODEL].astype(F32)
    gb = gate_ref[:, D_MODEL:].astype(F32)
    merged = (jax.nn.sigmoid(ga) * jnp.dot(oa_ref[...], wpa_ref[...], preferred_element_type=F32)
              + jax.nn.sigmoid(gb) * jnp.dot(ob_ref[...], wpb_ref[...], preferred_element_type=F32))
    y = jnp.dot(merged.astype(BF16), wo_ref[...], preferred_element_type=F32)
    x1 = x_ref[...] + gt1_ref[...] * y
    x1_ref[...] = x1
    inv = lax.rsqrt(jnp.mean(x1 * x1, axis=-1, keepdims=True) + EPS)
    h2 = (x1 * inv * g2_ref[...]) * (1.0 + sc2_ref[...]) + sh2_ref[...]
    h2p_ref[...] = _pack_halves(h2)

    h_hi, h_mid, _ = _split3(h2)
    w_hi, w_mid, _ = _split3(wr_ref[...])
    logits = (_nt_dot(w_hi, h_hi) + _nt_dot(w_hi, h_mid) + _nt_dot(w_mid, h_hi)) + br_ref[...]

    eio = lax.broadcasted_iota(I32, (N_EXPERTS, TM_MERGE), 0)
    vals, idxs, sels = [], [], []
    cur = logits
    for _k in range(TOP_K):
        m = jnp.max(cur, axis=0, keepdims=True)
        idx = jnp.min(jnp.where(cur == m, eio, N_EXPERTS), axis=0, keepdims=True)
        sel = eio == idx
        vals.append(m)
        idxs.append(idx)
        sels.append(sel)
        cur = jnp.where(sel, -jnp.inf, cur)
    es = [jnp.exp(v - vals[0]) for v in vals]
    tot = es[0] + es[1] + es[2] + es[3]
    onehot = jnp.zeros((N_EXPERTS, TM_MERGE), F32)
    for sel in sels:
        onehot = onehot + jnp.where(sel, 1.0, 0.0)
    before = jnp.dot(onehot.astype(BF16), upper_ref[...], preferred_element_type=F32) + carry_ref[:, 0:1]
    ranks = [jnp.sum(jnp.where(sel, before, 0.0), axis=0, keepdims=True) for sel in sels]
    carry_ref[...] = carry_ref[...] + jnp.sum(onehot, axis=1, keepdims=True)
    cnt_ref[...] = carry_ref[...]

    zi = jnp.zeros((8 - TOP_K, TM_MERGE), I32)
    zf = jnp.zeros((8 - TOP_K, TM_MERGE), F32)
    eidx_ref[...] = jnp.concatenate(idxs + [zi], axis=0)
    rank_ref[...] = jnp.concatenate([r.astype(I32) for r in ranks] + [zi], axis=0)
    wts_ref[...] = jnp.concatenate([e / tot for e in es] + [zf], axis=0)


def _merge_route(grp, o_a, o_b, gates, x2d, mod4, norm2_g, w_pa, w_pb, w_o, w_rt, b_r):
    ntile = N_GRP // TM_MERGE
    first = grp * ntile
    per_b = SEQ // TM_MERGE
    full = lambda shape: pl.BlockSpec(shape, lambda i: (0,) * len(shape))
    row_in = lambda w: pl.BlockSpec((TM_MERGE, w), lambda i: (first + i, 0))
    row = lambda w: pl.BlockSpec((TM_MERGE, w), lambda i: (i, 0))
    modspec = lambda j: pl.BlockSpec((None, None, 1, D_MODEL), lambda i: (j, (first + i) // per_b, 0, 0))
    col = pl.BlockSpec((8, TM_MERGE), lambda i: (0, i))
    return pl.pallas_call(
        _merge_kernel,
        grid=(ntile,),
        in_specs=[
            row_in(DIFF_HEADS * 2 * DIFF_DH), row_in(GLA_HEADS * GLA_DV), row_in(W_GATE), row_in(D_MODEL),
            modspec(2), modspec(3), modspec(4),
            full((1, D_MODEL)),
            full((DIFF_HEADS * 2 * DIFF_DH, D_MODEL)), full((GLA_HEADS * GLA_DV, D_MODEL)),
            full((D_MODEL, D_MODEL)),
            full((N_EXPERTS, D_MODEL)), full((N_EXPERTS, 1)),
        ],
        out_specs=[
            row(D_MODEL), row(ROW_W), col, col, col,
            pl.BlockSpec((N_EXPERTS, LANES), lambda i: (0, 0)),
        ],
        out_shape=[
            jax.ShapeDtypeStruct((N_GRP, D_MODEL), F32),
            jax.ShapeDtypeStruct((N_GRP, ROW_W), ROW_DT),
            jax.ShapeDtypeStruct((8, N_GRP), I32),
            jax.ShapeDtypeStruct((8, N_GRP), I32),
            jax.ShapeDtypeStruct((8, N_GRP), F32),
            jax.ShapeDtypeStruct((N_EXPERTS, LANES), F32),
        ],
        scratch_shapes=[
            pltpu.VMEM((TM_MERGE, TM_MERGE), BF16),
            pltpu.VMEM((N_EXPERTS, LANES), F32),
        ],
        compiler_params=pltpu.CompilerParams(
            dimension_semantics=("arbitrary",), vmem_limit_bytes=VMEM_LIMIT),
        name="merge_route",
    )(o_a, o_b, gates, x2d, mod4, mod4, mod4, norm2_g, w_pa, w_pb, w_o, w_rt, b_r)


TP = 4096
NB_PAD = ((N_BLK + LANES - 1) // LANES) * LANES


def _plan_kernel(cnt_ref, eidx_ref, rank_ref, dest_ref, be_ref, nv_ref, rows_ref):
    cnt = cnt_ref[...]
    padded = jnp.floor((cnt + (BLK - 1.0)) * (1.0 / BLK)) * BLK
    er = lax.broadcasted_iota(I32, (N_EXPERTS, N_EXPERTS), 0)
    ec = lax.broadcasted_iota(I32, (N_EXPERTS, N_EXPERTS), 1)
    lower = jnp.where(ec < er, 1.0, 0.0).astype(BF16)
    p_hi, p_mid, p_lo = _split3(padded)
    starts = (jnp.dot(lower, p_hi, preferred_element_type=F32)
              + jnp.dot(lower, p_mid, preferred_element_type=F32)
              + jnp.dot(lower, p_lo, preferred_element_type=F32))
    ends = starts + padded
    blk_start = (lax.broadcasted_iota(I32, (1, NB_PAD), 1) * BLK).astype(F32)
    n_before = jnp.sum(jnp.where(ends[:, 0:1] <= blk_start, 1.0, 0.0), axis=0, keepdims=True)
    be_ref[...] = jnp.minimum(n_before, N_EXPERTS - 1.0).astype(I32)
    nv_ref[...] = (jnp.max(ends, axis=0, keepdims=True) * (1.0 / BLK)).astype(I32)
    owner = (starts[:, 0:1] <= blk_start) & (blk_start < ends[:, 0:1])
    filled = jnp.sum(jnp.where(owner, (starts + cnt)[:, 0:1] - blk_start, 0.0), axis=0, keepdims=True)
    rows_ref[...] = jnp.clip(filled, 0.0, float(BLK)).astype(I32)

    eio = lax.broadcasted_iota(I32, (N_EXPERTS, TP), 0)
    rows = []
    for k in range(TOP_K):
        onehot = eio == eidx_ref[k:k + 1, :]
        base = jnp.sum(jnp.where(onehot, starts[:, 0:1], 0.0), axis=0, keepdims=True)
        rows.append(base.astype(I32) + rank_ref[k:k + 1, :])
    dest_ref[...] = jnp.concatenate(rows + [jnp.zeros((8 - TOP_K, TP), I32)], axis=0)


def _route_plan(cnt, eidx, rank):
    return pl.pallas_call(
        _plan_kernel,
        grid=(N_GRP // TP,),
        in_specs=[
            pl.BlockSpec((N_EXPERTS, LANES), lambda i: (0, 0)),
            pl.BlockSpec((8, TP), lambda i: (0, i)),
            pl.BlockSpec((8, TP), lambda i: (0, i)),
        ],
        out_specs=[
            pl.BlockSpec((8, TP), lambda i: (0, i)),
            pl.BlockSpec((1, NB_PAD), lambda i: (0, 0)),
            pl.BlockSpec((1, LANES), lambda i: (0, 0)),
            pl.BlockSpec((1, NB_PAD), lambda i: (0, 0)),
        ],
        out_shape=[
            jax.ShapeDtypeStruct((8, N_GRP), I32),
            jax.ShapeDtypeStruct((1, NB_PAD), I32),
            jax.ShapeDtypeStruct((1, LANES), I32),
            jax.ShapeDtypeStruct((1, NB_PAD), I32),
        ],
        compiler_params=pltpu.CompilerParams(dimension_semantics=("arbitrary",)),
        name="route_plan",
    )(cnt, eidx, rank)


def _ffn_kernel(be_ref, nv_ref, first_ref, slot_ref, nxt_ref, rows_ref, x_ref, wgu_hbm, bgu_ref, wd_hbm, bd_ref,
                y_ref, wgu_buf, wd_buf, sem):
    i = pl.program_id(0)
    valid = i < nv_ref[0]
    filled = jnp.maximum(rows_ref[i], 1)
    s = slot_ref[i]

    def weight_copies(e, slot):
        return (pltpu.make_async_copy(wgu_hbm.at[e], wgu_buf.at[slot], sem.at[slot, 0]),
                pltpu.make_async_copy(wd_hbm.at[e], wd_buf.at[slot], sem.at[slot, 1]))

    @pl.when(i == 0)
    def _():
        for cp in weight_copies(be_ref[0], 0):
            cp.start()

    @pl.when(valid & (first_ref[i] == 1))
    def _():
        for cp in weight_copies(be_ref[i], s):
            cp.wait()

        @pl.when(nxt_ref[i] >= 0)
        def _():
            for cp in weight_copies(nxt_ref[i], 1 - s):
                cp.start()

    def expert_rows(n):
        lo, hi = _unpack_halves(x_ref[:n, :])
        x = jnp.concatenate([lo, hi], axis=1).astype(BF16)
        mm = lambda a, w: lax.dot_general(a, w, (((1,), (0,)), ((), ())), preferred_element_type=F32)
        gu = mm(x, wgu_buf[s]) + bgu_ref[...]
        gate = jnp.minimum(gu[:, :D_FF], SWIGLU_LIMIT)
        up = jnp.clip(gu[:, D_FF:], -SWIGLU_LIMIT, SWIGLU_LIMIT)
        act = (up + 1.0) * (gate * jax.nn.sigmoid(SWIGLU_ALPHA * gate))
        y = mm(act.astype(BF16), wd_buf[s]) + bd_ref[...]
        y_ref[:n, :] = _pack_halves(y)

    for n in range(BLK_STEP, BLK + 1, BLK_STEP):
        @pl.when(valid & (filled > n - BLK_STEP) & (filled <= n))
        def _(n=n):
            expert_rows(n)
            if n < BLK:
                y_ref[n:, :] = jnp.zeros((BLK - n, ROW_W), ROW_DT)

    @pl.when(jnp.logical_not(valid))
    def _():
        y_ref[...] = jnp.zeros_like(y_ref)


def _expert_ffn(blk_expert, n_valid, blk_rows, xb, w_gate_up, b_gate_up, w_down, b_down):
    idx = jnp.arange(N_BLK, dtype=I32)
    used = idx < n_valid[0]
    first = (used & ((idx == 0) | (blk_expert != jnp.roll(blk_expert, 1)))).astype(I32)
    slot = (jnp.cumsum(first) - 1) & 1
    later = used[None, :] & (idx[None, :] > idx[:, None]) & (blk_expert[None, :] != blk_expert[:, None])
    nxt_pos = jnp.min(jnp.where(later, idx[None, :], N_BLK), axis=1)
    nxt = jnp.where(nxt_pos < N_BLK, blk_expert[jnp.minimum(nxt_pos, N_BLK - 1)], -1).astype(I32)

    blockwise = lambda i, *_: (i, 0)
    per_expert = lambda i, be, *_: (be[i], 0, 0)
    grid_spec = pltpu.PrefetchScalarGridSpec(
        num_scalar_prefetch=6,
        grid=(N_BLK,),
        in_specs=[
            pl.BlockSpec((BLK, ROW_W), blockwise),
            pl.BlockSpec(memory_space=pl.ANY),
            pl.BlockSpec((None, 1, 2 * D_FF), per_expert),
            pl.BlockSpec(memory_space=pl.ANY),
            pl.BlockSpec((None, 1, D_MODEL), per_expert),
        ],
        out_specs=pl.BlockSpec((BLK, ROW_W), blockwise),
        scratch_shapes=[
            pltpu.VMEM((2, D_MODEL, 2 * D_FF), F32),
            pltpu.VMEM((2, D_FF, D_MODEL), F32),
            pltpu.SemaphoreType.DMA((2, 2)),
        ],
    )
    return pl.pallas_call(
        _ffn_kernel,
        grid_spec=grid_spec,
        out_shape=jax.ShapeDtypeStruct((P_ROWS, ROW_W), ROW_DT),
        compiler_params=pltpu.CompilerParams(
            dimension_semantics=("arbitrary",), vmem_limit_bytes=VMEM_LIMIT),
        name="expert_ffn",
    )(blk_expert, n_valid, first, slot.astype(I32), nxt, blk_rows, xb, w_gate_up,
      b_gate_up.reshape(N_EXPERTS, 1, 2 * D_FF), w_down, b_down.reshape(N_EXPERTS, 1, D_MODEL))


def _final_kernel(x1_ref, y0_ref, y1_ref, y2_ref, y3_ref, w_ref, gt2_ref, g_ref, *rest):
    o_ref = rest[-1]
    w = w_ref[...].T
    ylo = jnp.zeros((TM_FIN, HALF), F32)
    yhi = jnp.zeros((TM_FIN, HALF), F32)
    for k, y_ref in enumerate((y0_ref, y1_ref, y2_ref, y3_ref)):
        lo, hi = _unpack_halves(y_ref[...])
        wk = w[:, k:k + 1]
        ylo = ylo + wk * lo
        yhi = yhi + wk * hi
    y = jnp.concatenate([ylo, yhi], axis=1)
    x2 = x1_ref[...] + gt2_ref[...] * y
    inv = lax.rsqrt(jnp.mean(x2 * x2, axis=-1, keepdims=True) + EPS)
    o_ref[...] = x2 * inv * g_ref[...]


def _final(grp, x1, yg, w4, mod4, final_norm_g, out_so_far):
    per_b = SEQ // TM_FIN
    ntile = N_GRP // TM_FIN
    first = grp * ntile
    slot = lambda k: pl.BlockSpec((TM_FIN, ROW_W), lambda i: (k * ntile + i, 0))
    in_specs = [
        pl.BlockSpec((TM_FIN, D_MODEL), lambda i: (i, 0)),
        slot(0), slot(1), slot(2), slot(3),
        pl.BlockSpec((8, TM_FIN), lambda i: (0, i)),
        pl.BlockSpec((None, None, 1, D_MODEL), lambda i: (5, (first + i) // per_b, 0, 0)),
        pl.BlockSpec((1, D_MODEL), lambda i: (0, 0)),
    ]
    args = [x1, yg, yg, yg, yg, w4, mod4, final_norm_g]
    aliases = {}
    if out_so_far is not None:
        in_specs.append(pl.BlockSpec(memory_space=pl.ANY))
        args.append(out_so_far)
        aliases = {len(args) - 1: 0}
    return pl.pallas_call(
        _final_kernel,
        grid=(ntile,),
        in_specs=in_specs,
        out_specs=pl.BlockSpec((TM_FIN, D_MODEL), lambda i: (first + i, 0)),
        out_shape=jax.ShapeDtypeStruct((N_TOK, D_MODEL), F32),
        input_output_aliases=aliases,
        compiler_params=pltpu.CompilerParams(
            dimension_semantics=("arbitrary",), vmem_limit_bytes=VMEM_LIMIT),
        name="combine_final",
    )(*args)


SC_CORES = 2
SC_SUBCORES = 16
SC_WORKERS = SC_CORES * SC_SUBCORES
SC_CHUNK = 64


def _sc_mesh():
    return plsc.VectorSubcoreMesh(core_axis_name="c", subcore_axis_name="s")


def _row_buffers():
    return ([pltpu.VMEM((SC_CHUNK, HALF), U32)] * 2 + [pltpu.SemaphoreType.DMA] * 5)


def _dispatch_rows(h2p, dest2d):
    per_w = N_GRP // SC_WORKERS
    nchunk = per_w // SC_CHUNK
    rows_per_k = N_GRP // SC_CHUNK
    assert nchunk % 2 == 0

    @functools.partial(
        pl.kernel, mesh=_sc_mesh(), out_type=jax.ShapeDtypeStruct((P_ROWS, HALF), U32),
        scratch_types=[pltpu.VMEM((TOP_K * nchunk, SC_CHUNK), I32)] + _row_buffers(),
        name="moe_dispatch")
    def k(src_hbm, dest_hbm, out_hbm, idx_v, buf0, buf1, isem, l0, l1, s0, s1):
        wid = lax.axis_index("s") * SC_CORES + lax.axis_index("c")
        bufs, lsem, ssem = (buf0, buf1), (l0, l1), (s0, s1)
        idx_loads = [
            pltpu.make_async_copy(dest_hbm.at[pl.ds(kk * rows_per_k + wid * nchunk, nchunk)],
                                  idx_v.at[pl.ds(kk * nchunk, nchunk)], isem) for kk in range(TOP_K)]

        def load(j, b):
            return pltpu.make_async_copy(src_hbm.at[pl.ds(wid * per_w + j * SC_CHUNK, SC_CHUNK)], bufs[b], lsem[b])

        def scatters(j, b):
            return [pltpu.make_async_copy(bufs[b], out_hbm.at[idx_v.at[kk * nchunk + j]], ssem[b])
                    for kk in range(TOP_K)]

        for cp in idx_loads:
            cp.start()
        load(0, 0).start()
        for cp in idx_loads:
            cp.wait()

        @pl.loop(0, nchunk // 2)
        def _(i):
            for b in range(2):
                j = 2 * i + b

                @pl.when(j >= 1)
                def _():
                    for cp in scatters(j - 1, 1 - b):
                        cp.wait()

                @pl.when(j + 1 < nchunk)
                def _():
                    load(j + 1, 1 - b).start()

                load(j, b).wait()
                for cp in scatters(j, b):
                    cp.start()

        for cp in scatters(nchunk - 1, 1):
            cp.wait()

    return k(h2p, dest2d)


def _combine_rows(yb, dest2d):
    n_out = TOP_K * N_GRP
    per_w = n_out // SC_WORKERS
    nchunk = per_w // SC_CHUNK
    assert nchunk % 2 == 0

    @functools.partial(
        pl.kernel, mesh=_sc_mesh(), out_type=jax.ShapeDtypeStruct((n_out, HALF), U32),
        scratch_types=[pltpu.VMEM((nchunk, SC_CHUNK), I32)] + _row_buffers(),
        name="moe_combine")
    def k(tab_hbm, idx_hbm, out_hbm, idx_v, buf0, buf1, isem, g0, g1, s0, s1):
        wid = lax.axis_index("s") * SC_CORES + lax.axis_index("c")
        bufs, gsem, ssem = (buf0, buf1), (g0, g1), (s0, s1)
        idx_load = pltpu.make_async_copy(idx_hbm.at[pl.ds(wid * nchunk, nchunk)], idx_v, isem)

        def gather(j, b):
            return pltpu.make_async_copy(tab_hbm.at[idx_v.at[j]], bufs[b], gsem[b])

        def store(j, b):
            return pltpu.make_async_copy(bufs[b], out_hbm.at[pl.ds(wid * per_w + j * SC_CHUNK, SC_CHUNK)], ssem[b])

        idx_load.start()
        idx_load.wait()
        gather(0, 0).start()

        @pl.loop(0, nchunk // 2)
        def _(i):
            for b in range(2):
                j = 2 * i + b

                @pl.when(j >= 1)
                def _():
                    store(j - 1, 1 - b).wait()

                @pl.when(j + 1 < nchunk)
                def _():
                    gather(j + 1, 1 - b).start()

                gather(j, b).wait()
                store(j, b).start()

        store(nchunk - 1, 1).wait()

    return k(yb, dest2d)


def _lambda_kernel(p_ref, o_ref):
    p = p_ref[...]
    s1 = jnp.sum(p[0:1] * p[1:2], axis=-1, keepdims=True)
    s2 = jnp.sum(p[2:3] * p[3:4], axis=-1, keepdims=True)
    o_ref[...] = jnp.broadcast_to(jnp.exp(s1) - jnp.exp(s2) + LAMBDA_INIT, (1, LANES))


def kernel(x, c, w_ada, b_ada, norm1_g, w_in, lambda_q1, lambda_k1, lambda_q2, lambda_k2, diff_norm_g, w_alpha_up, b_alpha, gla_norm_g, w_branch_diff, w_branch_gla, w_out, norm2_g, w_router, b_router, w_gate_up, b_gate_up, w_down, b_down, final_norm_g):
    w_in0 = w_in[0]
    c_a, c_g = W_A, W_A + W_G
    w_a = w_in0[:, :c_a].astype(BF16)
    w_g = w_in0[:, c_a:c_g].astype(BF16)
    w_lr = jnp.pad(w_in0[:, c_g:c_g + GLA_RANK], ((0, 0), (0, LANES - GLA_RANK))).astype(BF16)
    w_gate = w_in0[:, c_g + GLA_RANK:].astype(BF16)
    w_up = jnp.pad(w_alpha_up[0], ((0, LANES - GLA_RANK), (0, 0)))
    lam_in = jnp.concatenate([lambda_q1, lambda_k1, lambda_q2, lambda_k2], axis=0)
    slopes = jnp.asarray(2.0 ** (-8.0 * np.arange(1, DIFF_HEADS + 1) / DIFF_HEADS), dtype=F32)

    mod = _modulation(c, w_ada[0], b_ada[0])
    mod4 = mod.reshape(N_MOD, BATCH, 1, D_MODEL)
    lam = pl.pallas_call(
        _lambda_kernel, out_shape=jax.ShapeDtypeStruct((1, LANES), F32), name="lambda")(lam_in)[0, :1]

    qkv_a, qkv_g, gates, glr = _in_proj(x, mod4, norm1_g, w_a, w_g, w_gate, w_lr)
    pos = jnp.arange(SEQ, dtype=I32)
    p_hi = (pos >> POS_LO_BITS).astype(F32)
    p_lo = (pos & (POS_LO - 1)).astype(F32)
    kaug = jnp.zeros((SEQ, LANES), F32).at[:, 0].set(p_hi).at[:, 1].set(p_hi).at[:, 2].set(p_lo).at[:, 3].set(
        p_lo).astype(BF16)
    o_a = _diff_attention(qkv_a, slopes, lam, kaug, diff_norm_g)
    o_b = _gla(qkv_g, glr, w_up, b_alpha, gla_norm_g)

    merge_args = (o_a.reshape(N_TOK, -1), o_b.reshape(N_TOK, -1), gates.reshape(N_TOK, W_GATE),
                  x.reshape(N_TOK, D_MODEL), mod4, norm2_g,
                  w_branch_diff[0].astype(BF16), w_branch_gla[0].astype(BF16), w_out[0].astype(BF16),
                  w_router[0].T, b_router[0].reshape(N_EXPERTS, 1))
    fin_g = final_norm_g.reshape(1, D_MODEL)

    routed = []
    for grp in range(MOE_GROUPS):
        x1, h2p, eidx, rank, wts, cnt = _merge_route(grp, *merge_args)
        dest8, be, nv, rows = _route_plan(cnt, eidx, rank)
        dest = dest8[:TOP_K].reshape(-1, SC_CHUNK)
        routed.append((x1, wts, dest, be, nv, rows, _dispatch_rows(h2p, dest)))
    gathered = []
    for x1, wts, dest, be, nv, rows, xb in routed:
        yb = _expert_ffn(be[0, :N_BLK], nv[0, :1], rows[0, :N_BLK], xb,
                         w_gate_up[0], b_gate_up[0], w_down[0], b_down[0])
        gathered.append(_combine_rows(yb, dest))
    out = None
    for grp, ((x1, wts, *_), yg) in enumerate(zip(routed, gathered)):
        out = _final(grp, x1, yg, wts, mod4, fin_g, out)
    return out.reshape(BATCH, SEQ, D_MODEL)
```

```python
import functools
import math

import jax
import jax.numpy as jnp
import numpy as np
from jax import lax
from jax.experimental import pallas as pl
from jax.experimental.pallas import tpu as pltpu
from jax.experimental.pallas import tpu_sc as plsc

F32 = jnp.float32
BF16 = jnp.bfloat16
U32 = jnp.uint32
I32 = jnp.int32

D_MODEL = 1024
BATCH = 16
SEQ = 2048
N_TOK = BATCH * SEQ
CHUNK = 64
DIFF_HEADS = 4
DIFF_DH = 64
GLA_HEADS = 4
GLA_DK = 64
GLA_DV = 128
GLA_RANK = 16
GLA_GATE_NORM = 16.0
N_EXPERTS = 32
TOP_K = 4
D_FF = D_MODEL
SWIGLU_LIMIT = 7.0
SWIGLU_ALPHA = 1.702
N_MOD = 6
EPS = 1e-6
LAMBDA_INIT = 0.8 - 0.6 * math.exp(-0.3 * 0)

LANES = 128
HALF = D_MODEL // 2
ROW_W = HALF
ROW_DT = U32

TM_IN = 1024
TQ = 512
ATTN_STRIPS = 4
ATTN_W = 2 * TQ // ATTN_STRIPS
VT_ROWS = 2 * DIFF_DH + 16
LOG2E = math.log2(math.e)
TM_MERGE = 512
BLK = 1024
BLK_STEP = 256
MOE_GROUPS = 1
N_GRP = N_TOK // MOE_GROUPS
N_BLK = (N_GRP * TOP_K) // BLK + N_EXPERTS
P_ROWS = N_BLK * BLK
TM_FIN = 1024
VMEM_LIMIT = 56 * 1024 * 1024
CHUNK_SHIFT = CHUNK.bit_length() - 1
POS_LO_BITS = 3
POS_LO = 1 << POS_LO_BITS


def _nt_dot(a, b):
    return lax.dot_general(a, b, (((1,), (1,)), ((), ())), preferred_element_type=F32)


def _tn_dot(a, b):
    return lax.dot_general(a, b, (((0,), (0,)), ((), ())), preferred_element_type=F32)


def _split3(x):
    hi = x.astype(BF16)
    r1 = x - hi.astype(F32)
    mid = r1.astype(BF16)
    lo = (r1 - mid.astype(F32)).astype(BF16)
    return hi, mid, lo


def _pack_halves(y):
    return pltpu.pack_elementwise([y[:, :HALF], y[:, HALF:]], packed_dtype=BF16)


def _unpack_halves(u):
    lo = pltpu.unpack_elementwise(u, index=0, packed_dtype=BF16, unpacked_dtype=F32)
    hi = pltpu.unpack_elementwise(u, index=1, packed_dtype=BF16, unpacked_dtype=F32)
    return lo, hi


def _mod_kernel(c_ref, w_ref, b_ref, o_ref):
    c = c_ref[...]
    s = c * jax.nn.sigmoid(c)
    o_ref[0] = jnp.dot(s.astype(BF16), w_ref[...].astype(BF16),
                       preferred_element_type=F32) + b_ref[...]


def _modulation(c, w_ada, b_ada):
    return pl.pallas_call(
        _mod_kernel,
        grid=(N_MOD,),
        in_specs=[
            pl.BlockSpec((BATCH, D_MODEL), lambda j: (0, 0)),
            pl.BlockSpec((D_MODEL, D_MODEL), lambda j: (0, j)),
            pl.BlockSpec((1, D_MODEL), lambda j: (0, j)),
        ],
        out_specs=pl.BlockSpec((1, BATCH, D_MODEL), lambda j: (j, 0, 0)),
        out_shape=jax.ShapeDtypeStruct((N_MOD, BATCH, D_MODEL), F32),
        compiler_params=pltpu.CompilerParams(dimension_semantics=("arbitrary",)),
        name="adaln_mod",
    )(c, w_ada, b_ada.reshape(1, N_MOD * D_MODEL))


W_A = 3 * DIFF_HEADS * 2 * DIFF_DH
W_G = 2 * GLA_HEADS * GLA_DK + 2 * GLA_HEADS * GLA_DV
W_GATE = 2 * D_MODEL


def _in_kernel(x_ref, sh_ref, sc_ref, g_ref, wa_ref, wg_ref, wgate_ref, wlr_ref,
               oa_ref, og_ref, ogate_ref, olr_ref):
    x = x_ref[...]
    inv = lax.rsqrt(jnp.mean(x * x, axis=-1, keepdims=True) + EPS)
    h = (x * inv * g_ref[...]) * (1.0 + sc_ref[...]) + sh_ref[...]
    hb = h.astype(BF16)
    oa_ref[...] = jnp.dot(hb, wa_ref[...], preferred_element_type=F32).astype(BF16)
    og_ref[...] = jnp.dot(hb, wg_ref[...], preferred_element_type=F32).astype(BF16)
    ogate_ref[...] = jnp.dot(hb, wgate_ref[...], preferred_element_type=F32).astype(BF16)
    olr_ref[...] = jnp.dot(hb, wlr_ref[...], preferred_element_type=F32)


def _in_proj(x, mod4, norm1_g, w_a, w_g, w_gate, w_lr):
    nrow = SEQ // TM_IN
    full = lambda shape: pl.BlockSpec(shape, lambda b, i: (0,) * len(shape), pipeline_mode=pl.Buffered(1))
    return pl.pallas_call(
        _in_kernel,
        grid=(BATCH, nrow),
        in_specs=[
            pl.BlockSpec((None, TM_IN, D_MODEL), lambda b, i: (b, i, 0)),
            pl.BlockSpec((None, None, 1, D_MODEL), lambda b, i: (0, b, 0, 0)),
            pl.BlockSpec((None, None, 1, D_MODEL), lambda b, i: (1, b, 0, 0)),
            full((1, D_MODEL)),
            full((D_MODEL, W_A)),
            full((D_MODEL, W_G)),
            full((D_MODEL, W_GATE)),
            full((D_MODEL, LANES)),
        ],
        out_specs=[
            pl.BlockSpec((None, TM_IN, W_A), lambda b, i: (b, i, 0)),
            pl.BlockSpec((None, TM_IN, W_G), lambda b, i: (b, i, 0)),
            pl.BlockSpec((None, TM_IN, W_GATE), lambda b, i: (b, i, 0)),
            pl.BlockSpec((None, TM_IN, LANES), lambda b, i: (b, i, 0)),
        ],
        out_shape=[
            jax.ShapeDtypeStruct((BATCH, SEQ, W_A), BF16),
            jax.ShapeDtypeStruct((BATCH, SEQ, W_G), BF16),
            jax.ShapeDtypeStruct((BATCH, SEQ, W_GATE), BF16),
            jax.ShapeDtypeStruct((BATCH, SEQ, LANES), F32),
        ],
        compiler_params=pltpu.CompilerParams(
            dimension_semantics=("arbitrary", "arbitrary"), vmem_limit_bytes=VMEM_LIMIT),
        name="in_proj",
    )(x, mod4, mod4, norm1_g, w_a, w_g, w_gate, w_lr)


def _attn_kernel(slope_ref, lam_ref, q_ref, k_ref, v_ref, kaug_ref, g_ref, o_ref,
                 sa_ref, sb_ref, lha_ref, lhb_ref, corr_ref, vt_ref, m_ref, acc_ref):
    c_alibi = slope_ref[pl.program_id(1)] * LOG2E
    lam = lam_ref[0]
    lane = lax.broadcasted_iota(I32, (TQ, 2 * DIFF_DH), 1)
    c_vec = jnp.full((TQ, 2 * DIFF_DH), c_alibi, F32)
    c_hi = c_vec.astype(BF16).astype(F32)
    c_lo = (c_vec - c_hi).astype(BF16).astype(F32)
    zero = jnp.zeros((TQ, 2 * DIFF_DH), BF16)
    aug = jnp.where(lane == 0, POS_LO * c_hi, jnp.where(lane == 1, POS_LO * c_lo,
                    jnp.where(lane == 2, c_hi, jnp.where(lane == 3, c_lo, 0.0)))).astype(BF16)

    def blk(j):
        return slice(j * TQ, (j + 1) * TQ)

    def stacked_queries(qi):
        q = (q_ref[blk(qi), :].astype(F32) * (DIFF_DH ** -0.5 * LOG2E)).astype(BF16)
        return jnp.concatenate([
            jnp.concatenate([jnp.where(lane < DIFF_DH, q, zero), aug], axis=1),
            jnp.concatenate([jnp.where(lane >= DIFF_DH, q, zero), aug], axis=1)], axis=0)

    lhs_refs = (lha_ref, lhb_ref)

    def strip(c):
        return slice(c * ATTN_W, (c + 1) * ATTN_W)

    def keys(j):
        return jnp.concatenate([k_ref[blk(j), :], kaug_ref[blk(j), :]], axis=1)

    def scores(kk, qi, c):
        return _nt_dot(kk, lhs_refs[qi % 2][strip(c), :])

    ones_rows = jnp.where(lax.broadcasted_iota(I32, (VT_ROWS - 2 * DIFF_DH, TQ), 0) == 0, 1.0, 0.0).astype(BF16)
    for j in range(SEQ // TQ):
        vt_ref[j, :2 * DIFF_DH, :] = v_ref[blk(j), :].astype(F32).T.astype(BF16)
        vt_ref[j, 2 * DIFF_DH:, :] = ones_rows

    def update(s, j, c, first):
        nk = s.shape[0]
        if first:
            m_new = jnp.max(s, axis=0, keepdims=True)
            p = jnp.exp2((s - m_new).astype(BF16))
            acc_ref[:, strip(c)] = jnp.dot(vt_ref[j, :, :nk], p, preferred_element_type=F32)
        else:
            m = m_ref[:, strip(c)]
            m_new = jnp.maximum(m, jnp.max(s, axis=0, keepdims=True))
            alpha = jnp.exp2(m - m_new)
            p = jnp.exp2((s - m_new).astype(BF16))
            acc_ref[:, strip(c)] = (alpha * acc_ref[:, strip(c)]
                                    + jnp.dot(vt_ref[j, :, :nk], p, preferred_element_type=F32))
        m_ref[:, strip(c)] = m_new

    def n_keys(qi, j, c):
        return TQ if j < qi else min(TQ, (c * ATTN_W) % TQ + ATTN_W)

    kr = lax.broadcasted_iota(I32, (TQ, TQ), 0)
    qc = lax.broadcasted_iota(I32, (TQ, TQ), 1)
    ahead = jnp.maximum(kr - qc, 0).astype(F32)
    corr_ref[...] = jnp.where((qc >> CHUNK_SHIFT) >= (kr >> CHUNK_SHIFT), (-2.0 * c_alibi) * ahead, -jnp.inf)

    pairs = [(qi, j) for qi in range(SEQ // TQ) for j in range(qi + 1)]
    bufs = (sa_ref, sb_ref)
    lhs_refs[0][...] = stacked_queries(0)
    kk = keys(0)
    for c in range(ATTN_STRIPS):
        nk = n_keys(0, 0, c)
        bufs[0][:nk, strip(c)] = scores(kk[:nk], 0, c)
    for t, (qi, j) in enumerate(pairs):
        nxt = pairs[t + 1] if t + 1 < len(pairs) else None
        if nxt is not None:
            if nxt[0] != qi:
                lhs_refs[nxt[0] % 2][...] = stacked_queries(nxt[0])
            kk = keys(nxt[1])
        for c in range(ATTN_STRIPS):
            if nxt is not None:
                nk = n_keys(*nxt, c)
                bufs[(t + 1) % 2][:nk, strip(c)] = scores(kk[:nk], nxt[0], c)
            nk = n_keys(qi, j, c)
            s = bufs[t % 2][:nk, strip(c)]
            if j == qi:
                lo = (c * ATTN_W) % TQ
                s = s + corr_ref[:nk, lo:lo + ATTN_W]
            update(s, j, c, first=(j == 0))
        if j == qi:
            ot = acc_ref[:2 * DIFF_DH, :] / acc_ref[2 * DIFF_DH:2 * DIFF_DH + 1, :]
            o = (ot[:, :TQ] - lam * ot[:, TQ:]).T
            inv = lax.rsqrt(jnp.mean(o * o, axis=-1, keepdims=True) + EPS)
            o_ref[blk(qi), :] = (o * inv * g_ref[...] * (1.0 - LAMBDA_INIT)).astype(BF16)


def _diff_attention(qkv_a, slopes, lam, kaug, diff_norm_g):
    return pl.pallas_call(
        _attn_kernel,
        grid=(BATCH, DIFF_HEADS),
        in_specs=[
            pl.BlockSpec(memory_space=pltpu.SMEM),
            pl.BlockSpec(memory_space=pltpu.SMEM),
            pl.BlockSpec((None, SEQ, LANES), lambda b, h: (b, 0, h)),
            pl.BlockSpec((None, SEQ, LANES), lambda b, h: (b, 0, DIFF_HEADS + h)),
            pl.BlockSpec((None, SEQ, LANES), lambda b, h: (b, 0, 2 * DIFF_HEADS + h)),
            pl.BlockSpec((SEQ, LANES), lambda b, h: (0, 0)),
            pl.BlockSpec((1, LANES), lambda b, h: (0, 0)),
        ],
        out_specs=pl.BlockSpec((None, SEQ, LANES), lambda b, h: (b, 0, h)),
        out_shape=jax.ShapeDtypeStruct((BATCH, SEQ, DIFF_HEADS * 2 * DIFF_DH), BF16),
        scratch_shapes=[
            pltpu.VMEM((TQ, 2 * TQ), F32), pltpu.VMEM((TQ, 2 * TQ), F32),
            pltpu.VMEM((2 * TQ, 4 * DIFF_DH), BF16), pltpu.VMEM((2 * TQ, 4 * DIFF_DH), BF16),
            pltpu.VMEM((TQ, TQ), F32),
            pltpu.VMEM((SEQ // TQ, VT_ROWS, TQ), BF16),
            pltpu.VMEM((1, 2 * TQ), F32),
            pltpu.VMEM((VT_ROWS, 2 * TQ), F32),
        ],
        compiler_params=pltpu.CompilerParams(
            dimension_semantics=("arbitrary", "arbitrary")),
        name="diff_attn",
    )(slopes, lam, qkv_a, qkv_a, qkv_a, kaug, diff_norm_g)


N_CHUNK = SEQ // CHUNK
PAIR = 2 * GLA_DK
PAIR_V = 2 * GLA_DV
CS_ROWS = 256
GLA_UNROLL = 32


def _gla_kernel(q_ref, k_ref, v_ref, r_ref, lr_ref, wup_ref, bup_ref, g_ref, o_ref,
                gcum_ref, state_ref):
    w_hi, w_mid, _ = _split3(wup_ref[...])
    rr = lax.broadcasted_iota(I32, (CS_ROWS, CS_ROWS), 0)
    cc = lax.broadcasted_iota(I32, (CS_ROWS, CS_ROWS), 1)
    tri = jnp.where(((rr >> CHUNK_SHIFT) == (cc >> CHUNK_SHIFT)) & (cc <= rr), 1.0, 0.0).astype(BF16)
    for blk in range(SEQ // CS_ROWS):
        rows = pl.ds(blk * CS_ROWS, CS_ROWS)
        a_hi, a_mid, _ = _split3(lr_ref[rows, :])
        z = (jnp.dot(a_hi, w_hi, preferred_element_type=F32)
             + jnp.dot(a_hi, w_mid, preferred_element_type=F32)
             + jnp.dot(a_mid, w_hi, preferred_element_type=F32)) + bup_ref[...]
        la = (jnp.minimum(z, 0.0) - jnp.log(1.0 + jnp.exp(-jnp.abs(z)))) * (1.0 / GLA_GATE_NORM)
        l_hi, l_mid, l_lo = _split3(la)
        gcum_ref[rows, :] = (jnp.dot(tri, l_hi, preferred_element_type=F32)
                             + jnp.dot(tri, l_mid, preferred_element_type=F32)
                             + jnp.dot(tri, l_lo, preferred_element_type=F32))

    state_ref[...] = jnp.zeros_like(state_ref)
    lane_k = lax.broadcasted_iota(I32, (1, PAIR), 1)
    row_v = lax.broadcasted_iota(I32, (PAIR_V, PAIR), 0)
    col_k = lax.broadcasted_iota(I32, (PAIR_V, PAIR), 1)
    same_head = (row_v >= GLA_DV) == (col_k >= GLA_DK)
    cr = lax.broadcasted_iota(I32, (CHUNK, CHUNK), 0)
    cs = lax.broadcasted_iota(I32, (CHUNK, CHUNK), 1)
    causal = cs <= cr
    scale = GLA_DK ** -0.5

    def chunk(n):
        rows = pl.ds(pl.multiple_of(n * CHUNK, CHUNK), CHUNK)
        gc = gcum_ref[rows, :]
        g_last = gcum_ref[pl.ds(n * CHUNK + CHUNK - 1, 1), :]
        qf = q_ref[rows, :].astype(F32) * scale
        kf = k_ref[rows, :].astype(F32)
        q_s = (qf * jnp.exp(gc)).astype(BF16)
        k_s = (kf * jnp.exp(-gc)).astype(BF16)
        k_d = (kf * jnp.exp(g_last - gc)).astype(BF16)
        decay = jnp.exp(g_last)
        for pr in range(GLA_HEADS // 2):
            kl = slice(pr * PAIR, (pr + 1) * PAIR)
            vl = slice(pr * PAIR_V, (pr + 1) * PAIR_V)
            qs_p, ks_p, kd_p = q_s[:, kl], k_s[:, kl], k_d[:, kl]
            v_p = v_ref[rows, vl]
            st = state_ref[pr]
            o_inter = _nt_dot(qs_p, st.astype(BF16))
            d_st = _tn_dot(v_p, kd_p)
            state_ref[pr] = st * decay[:, kl] + jnp.where(same_head, d_st, 0.0)
            for sub in range(2):
                hd = 2 * pr + sub
                in_head = (lane_k >= sub * GLA_DK) & (lane_k < (sub + 1) * GLA_DK)
                a = _nt_dot(jnp.where(in_head, qs_p, jnp.zeros_like(qs_p)), ks_p)
                a = jnp.where(causal, a, 0.0).astype(BF16)
                vs = slice(hd * GLA_DV, (hd + 1) * GLA_DV)
                o = (jnp.dot(a, v_ref[rows, vs], preferred_element_type=F32)
                     + o_inter[:, sub * GLA_DV:(sub + 1) * GLA_DV])
                inv = lax.rsqrt(jnp.mean(o * o, axis=-1, keepdims=True) + EPS)
                r = r_ref[rows, vs].astype(F32)
                o_ref[rows, vs] = (o * inv * g_ref[...] * (r * jax.nn.sigmoid(r))).astype(BF16)

    def chunk_group(t, _):
        for u in range(GLA_UNROLL):
            chunk(t * GLA_UNROLL + u)
        return 0

    lax.fori_loop(0, N_CHUNK // GLA_UNROLL, chunk_group, 0)


def _gla(qkv_g, glr, w_up, b_up, gla_norm_g):
    qk_w = GLA_HEADS * GLA_DK
    v_w = GLA_HEADS * GLA_DV
    return pl.pallas_call(
        _gla_kernel,
        grid=(BATCH,),
        in_specs=[
            pl.BlockSpec((None, SEQ, qk_w), lambda b: (b, 0, 0)),
            pl.BlockSpec((None, SEQ, qk_w), lambda b: (b, 0, 1)),
            pl.BlockSpec((None, SEQ, v_w), lambda b: (b, 0, 1)),
            pl.BlockSpec((None, SEQ, v_w), lambda b: (b, 0, 2)),
            pl.BlockSpec((None, SEQ, LANES), lambda b: (b, 0, 0)),
            pl.BlockSpec((LANES, qk_w), lambda b: (0, 0)),
            pl.BlockSpec((1, qk_w), lambda b: (0, 0)),
            pl.BlockSpec((1, GLA_DV), lambda b: (0, 0)),
        ],
        out_specs=pl.BlockSpec((None, SEQ, v_w), lambda b: (b, 0, 0)),
        out_shape=jax.ShapeDtypeStruct((BATCH, SEQ, v_w), BF16),
        scratch_shapes=[
            pltpu.VMEM((SEQ, qk_w), F32),
            pltpu.VMEM((GLA_HEADS // 2, PAIR_V, PAIR), F32),
        ],
        compiler_params=pltpu.CompilerParams(
            dimension_semantics=("arbitrary",), vmem_limit_bytes=VMEM_LIMIT),
        name="gla",
    )(qkv_g, qkv_g, qkv_g, qkv_g, glr, w_up, b_up, gla_norm_g)


def _merge_kernel(oa_ref, ob_ref, gate_ref, x_ref, gt1_ref, sh2_ref, sc2_ref, g2_ref,
                  wpa_ref, wpb_ref, wo_ref, wr_ref, br_ref,
                  x1_ref, h2p_ref, eidx_ref, rank_ref, wts_ref, cnt_ref,
                  upper_ref, carry_ref):
    i = pl.program_id(0)

    @pl.when(i == 0)
    def _():
        rr = lax.broadcasted_iota(I32, (TM_MERGE, TM_MERGE), 0)
        cc = lax.broadcasted_iota(I32, (TM_MERGE, TM_MERGE), 1)
        upper_ref[...] = jnp.where(rr < cc, 1.0, 0.0).astype(BF16)
        carry_ref[...] = jnp.zeros_like(carry_ref)

    ga = gate_ref[:, :D_MODEL].astype(F32)
    gb = gate_ref[:, D_MODEL:].astype(F32)
    merged = (jax.nn.sigmoid(ga) * jnp.dot(oa_ref[...], wpa_ref[...], preferred_element_type=F32)
              + jax.nn.sigmoid(gb) * jnp.dot(ob_ref[...], wpb_ref[...], preferred_element_type=F32))
    y = jnp.dot(merged.astype(BF16), wo_ref[...], preferred_element_type=F32)
    x1 = x_ref[...] + gt1_ref[...] * y
    x1_ref[...] = x1
    inv = lax.rsqrt(jnp.mean(x1 * x1, axis=-1, keepdims=True) + EPS)
    h2 = (x1 * inv * g2_ref[...]) * (1.0 + sc2_ref[...]) + sh2_ref[...]
    h2p_ref[...] = _pack_halves(h2)

    h_hi, h_mid, _ = _split3(h2)
    w_hi, w_mid, _ = _split3(wr_ref[...])
    logits = (_nt_dot(w_hi, h_hi) + _nt_dot(w_hi, h_mid) + _nt_dot(w_mid, h_hi)) + br_ref[...]

    eio = lax.broadcasted_iota(I32, (N_EXPERTS, TM_MERGE), 0)
    vals, idxs, sels = [], [], []
    cur = logits
    for _k in range(TOP_K):
        m = jnp.max(cur, axis=0, keepdims=True)
        idx = jnp.min(jnp.where(cur == m, eio, N_EXPERTS), axis=0, keepdims=True)
        sel = eio == idx
        vals.append(m)
        idxs.append(idx)
        sels.append(sel)
        cur = jnp.where(sel, -jnp.inf, cur)
    es = [jnp.exp(v - vals[0]) for v in vals]
    tot = es[0] + es[1] + es[2] + es[3]
    onehot = jnp.zeros((N_EXPERTS, TM_MERGE), F32)
    for sel in sels:
        onehot = onehot + jnp.where(sel, 1.0, 0.0)
    before = jnp.dot(onehot.astype(BF16), upper_ref[...], preferred_element_type=F32) + carry_ref[:, 0:1]
    ranks = [jnp.sum(jnp.where(sel, before, 0.0), axis=0, keepdims=True) for sel in sels]
    carry_ref[...] = carry_ref[...] + jnp.sum(onehot, axis=1, keepdims=True)
    cnt_ref[...] = carry_ref[...]

    zi = jnp.zeros((8 - TOP_K, TM_MERGE), I32)
    zf = jnp.zeros((8 - TOP_K, TM_MERGE), F32)
    eidx_ref[...] = jnp.concatenate(idxs + [zi], axis=0)
    rank_ref[...] = jnp.concatenate([r.astype(I32) for r in ranks] + [zi], axis=0)
    wts_ref[...] = jnp.concatenate([e / tot for e in es] + [zf], axis=0)


def _merge_route(grp, o_a, o_b, gates, x2d, mod4, norm2_g, w_pa, w_pb, w_o, w_rt, b_r):
    ntile = N_GRP // TM_MERGE
    first = grp * ntile
    per_b = SEQ // TM_MERGE
    full = lambda shape: pl.BlockSpec(shape, lambda i: (0,) * len(shape))
    row_in = lambda w: pl.BlockSpec((TM_MERGE, w), lambda i: (first + i, 0))
    row = lambda w: pl.BlockSpec((TM_MERGE, w), lambda i: (i, 0))
    modspec = lambda j: pl.BlockSpec((None, None, 1, D_MODEL), lambda i: (j, (first + i) // per_b, 0, 0))
    col = pl.BlockSpec((8, TM_MERGE), lambda i: (0, i))
    return pl.pallas_call(
        _merge_kernel,
        grid=(ntile,),
        in_specs=[
            row_in(DIFF_HEADS * 2 * DIFF_DH), row_in(GLA_HEADS * GLA_DV), row_in(W_GATE), row_in(D_MODEL),
            modspec(2), modspec(3), modspec(4),
            full((1, D_MODEL)),
            full((DIFF_HEADS * 2 * DIFF_DH, D_MODEL)), full((GLA_HEADS * GLA_DV, D_MODEL)),
            full((D_MODEL, D_MODEL)),
            full((N_EXPERTS, D_MODEL)), full((N_EXPERTS, 1)),
        ],
        out_specs=[
            row(D_MODEL), row(ROW_W), col, col, col,
            pl.BlockSpec((N_EXPERTS, LANES), lambda i: (0, 0)),
        ],
        out_shape=[
            jax.ShapeDtypeStruct((N_GRP, D_MODEL), F32),
            jax.ShapeDtypeStruct((N_GRP, ROW_W), ROW_DT),
            jax.ShapeDtypeStruct((8, N_GRP), I32),
            jax.ShapeDtypeStruct((8, N_GRP), I32),
            jax.ShapeDtypeStruct((8, N_GRP), F32),
            jax.ShapeDtypeStruct((N_EXPERTS, LANES), F32),
        ],
        scratch_shapes=[
            pltpu.VMEM((TM_MERGE, TM_MERGE), BF16),
            pltpu.VMEM((N_EXPERTS, LANES), F32),
        ],
        compiler_params=pltpu.CompilerParams(
            dimension_semantics=("arbitrary",), vmem_limit_bytes=VMEM_LIMIT),
        name="merge_route",
    )(o_a, o_b, gates, x2d, mod4, mod4, mod4, norm2_g, w_pa, w_pb, w_o, w_rt, b_r)


TP = 4096
NB_PAD = ((N_BLK + LANES - 1) // LANES) * LANES


def _plan_kernel(cnt_ref, eidx_ref, rank_ref, dest_ref, be_ref, nv_ref, rows_ref):
    cnt = cnt_ref[...]
    padded = jnp.floor((cnt + (BLK - 1.0)) * (1.0 / BLK)) * BLK
    er = lax.broadcasted_iota(I32, (N_EXPERTS, N_EXPERTS), 0)
    ec = lax.broadcasted_iota(I32, (N_EXPERTS, N_EXPERTS), 1)
    lower = jnp.where(ec < er, 1.0, 0.0).astype(BF16)
    p_hi, p_mid, p_lo = _split3(padded)
    starts = (jnp.dot(lower, p_hi, preferred_element_type=F32)
              + jnp.dot(lower, p_mid, preferred_element_type=F32)
              + jnp.dot(lower, p_lo, preferred_element_type=F32))
    ends = starts + padded
    blk_start = (lax.broadcasted_iota(I32, (1, NB_PAD), 1) * BLK).astype(F32)
    n_before = jnp.sum(jnp.where(ends[:, 0:1] <= blk_start, 1.0, 0.0), axis=0, keepdims=True)
    be_ref[...] = jnp.minimum(n_before, N_EXPERTS - 1.0).astype(I32)
    nv_ref[...] = (jnp.max(ends, axis=0, keepdims=True) * (1.0 / BLK)).astype(I32)
    owner = (starts[:, 0:1] <= blk_start) & (blk_start < ends[:, 0:1])
    filled = jnp.sum(jnp.where(owner, (starts + cnt)[:, 0:1] - blk_start, 0.0), axis=0, keepdims=True)
    rows_ref[...] = jnp.clip(filled, 0.0, float(BLK)).astype(I32)

    eio = lax.broadcasted_iota(I32, (N_EXPERTS, TP), 0)
    rows = []
    for k in range(TOP_K):
        onehot = eio == eidx_ref[k:k + 1, :]
        base = jnp.sum(jnp.where(onehot, starts[:, 0:1], 0.0), axis=0, keepdims=True)
        rows.append(base.astype(I32) + rank_ref[k:k + 1, :])
    dest_ref[...] = jnp.concatenate(rows + [jnp.zeros((8 - TOP_K, TP), I32)], axis=0)


def _route_plan(cnt, eidx, rank):
    return pl.pallas_call(
        _plan_kernel,
        grid=(N_GRP // TP,),
        in_specs=[
            pl.BlockSpec((N_EXPERTS, LANES), lambda i: (0, 0)),
            pl.BlockSpec((8, TP), lambda i: (0, i)),
            pl.BlockSpec((8, TP), lambda i: (0, i)),
        ],
        out_specs=[
            pl.BlockSpec((8, TP), lambda i: (0, i)),
            pl.BlockSpec((1, NB_PAD), lambda i: (0, 0)),
            pl.BlockSpec((1, LANES), lambda i: (0, 0)),
            pl.BlockSpec((1, NB_PAD), lambda i: (0, 0)),
        ],
        out_shape=[
            jax.ShapeDtypeStruct((8, N_GRP), I32),
            jax.ShapeDtypeStruct((1, NB_PAD), I32),
            jax.ShapeDtypeStruct((1, LANES), I32),
            jax.ShapeDtypeStruct((1, NB_PAD), I32),
        ],
        compiler_params=pltpu.CompilerParams(dimension_semantics=("arbitrary",)),
        name="route_plan",
    )(cnt, eidx, rank)


def _ffn_kernel(be_ref, nv_ref, first_ref, slot_ref, nxt_ref, rows_ref, x_ref, wgu_hbm, bgu_ref, wd_hbm, bd_ref,
                y_ref, wgu_buf, wd_buf, sem):
    i = pl.program_id(0)
    valid = i < nv_ref[0]
    filled = jnp.maximum(rows_ref[i], 1)
    s = slot_ref[i]

    def weight_copies(e, slot):
        return (pltpu.make_async_copy(wgu_hbm.at[e], wgu_buf.at[slot], sem.at[slot, 0]),
                pltpu.make_async_copy(wd_hbm.at[e], wd_buf.at[slot], sem.at[slot, 1]))

    @pl.when(i == 0)
    def _():
        for cp in weight_copies(be_ref[0], 0):
            cp.start()

    @pl.when(valid & (first_ref[i] == 1))
    def _():
        for cp in weight_copies(be_ref[i], s):
            cp.wait()

        @pl.when(nxt_ref[i] >= 0)
        def _():
            for cp in weight_copies(nxt_ref[i], 1 - s):
                cp.start()

    def expert_rows(n):
        lo, hi = _unpack_halves(x_ref[:n, :])
        x = jnp.concatenate([lo, hi], axis=1).astype(BF16)
        mm = lambda a, w: lax.dot_general(a, w, (((1,), (0,)), ((), ())), preferred_element_type=F32)
        gu = mm(x, wgu_buf[s]) + bgu_ref[...]
        gate = jnp.minimum(gu[:, :D_FF], SWIGLU_LIMIT)
        up = jnp.clip(gu[:, D_FF:], -SWIGLU_LIMIT, SWIGLU_LIMIT)
        act = (up + 1.0) * (gate * jax.nn.sigmoid(SWIGLU_ALPHA * gate))
        y = mm(act.astype(BF16), wd_buf[s]) + bd_ref[...]
        y_ref[:n, :] = _pack_halves(y)

    for n in range(BLK_STEP, BLK + 1, BLK_STEP):
        @pl.when(valid & (filled > n - BLK_STEP) & (filled <= n))
        def _(n=n):
            expert_rows(n)
            if n < BLK:
                y_ref[n:, :] = jnp.zeros((BLK - n, ROW_W), ROW_DT)

    @pl.when(jnp.logical_not(valid))
    def _():
        y_ref[...] = jnp.zeros_like(y_ref)


def _expert_ffn(blk_expert, n_valid, blk_rows, xb, w_gate_up, b_gate_up, w_down, b_down):
    idx = jnp.arange(N_BLK, dtype=I32)
    used = idx < n_valid[0]
    first = (used & ((idx == 0) | (blk_expert != jnp.roll(blk_expert, 1)))).astype(I32)
    slot = (jnp.cumsum(first) - 1) & 1
    later = used[None, :] & (idx[None, :] > idx[:, None]) & (blk_expert[None, :] != blk_expert[:, None])
    nxt_pos = jnp.min(jnp.where(later, idx[None, :], N_BLK), axis=1)
    nxt = jnp.where(nxt_pos < N_BLK, blk_expert[jnp.minimum(nxt_pos, N_BLK - 1)], -1).astype(I32)

    blockwise = lambda i, *_: (i, 0)
    per_expert = lambda i, be, *_: (be[i], 0, 0)
    grid_spec = pltpu.PrefetchScalarGridSpec(
        num_scalar_prefetch=6,
        grid=(N_BLK,),
        in_specs=[
            pl.BlockSpec((BLK, ROW_W), blockwise),
            pl.BlockSpec(memory_space=pl.ANY),
            pl.BlockSpec((None, 1, 2 * D_FF), per_expert),
            pl.BlockSpec(memory_space=pl.ANY),
            pl.BlockSpec((None, 1, D_MODEL), per_expert),
        ],
        out_specs=pl.BlockSpec((BLK, ROW_W), blockwise),
        scratch_shapes=[
            pltpu.VMEM((2, D_MODEL, 2 * D_FF), F32),
            pltpu.VMEM((2, D_FF, D_MODEL), F32),
            pltpu.SemaphoreType.DMA((2, 2)),
        ],
    )
    return pl.pallas_call(
        _ffn_kernel,
        grid_spec=grid_spec,
        out_shape=jax.ShapeDtypeStruct((P_ROWS, ROW_W), ROW_DT),
        compiler_params=pltpu.CompilerParams(
            dimension_semantics=("arbitrary",), vmem_limit_bytes=VMEM_LIMIT),
        name="expert_ffn",
    )(blk_expert, n_valid, first, slot.astype(I32), nxt, blk_rows, xb, w_gate_up,
      b_gate_up.reshape(N_EXPERTS, 1, 2 * D_FF), w_down, b_down.reshape(N_EXPERTS, 1, D_MODEL))


def _final_kernel(x1_ref, y0_ref, y1_ref, y2_ref, y3_ref, w_ref, gt2_ref, g_ref, *rest):
    o_ref = rest[-1]
    w = w_ref[...].T
    ylo = jnp.zeros((TM_FIN, HALF), F32)
    yhi = jnp.zeros((TM_FIN, HALF), F32)
    for k, y_ref in enumerate((y0_ref, y1_ref, y2_ref, y3_ref)):
        lo, hi = _unpack_halves(y_ref[...])
        wk = w[:, k:k + 1]
        ylo = ylo + wk * lo
        yhi = yhi + wk * hi
    y = jnp.concatenate([ylo, yhi], axis=1)
    x2 = x1_ref[...] + gt2_ref[...] * y
    inv = lax.rsqrt(jnp.mean(x2 * x2, axis=-1, keepdims=True) + EPS)
    o_ref[...] = x2 * inv * g_ref[...]


def _final(grp, x1, yg, w4, mod4, final_norm_g, out_so_far):
    per_b = SEQ // TM_FIN
    ntile = N_GRP // TM_FIN
    first = grp * ntile
    slot = lambda k: pl.BlockSpec((TM_FIN, ROW_W), lambda i: (k * ntile + i, 0))
    in_specs = [
        pl.BlockSpec((TM_FIN, D_MODEL), lambda i: (i, 0)),
        slot(0), slot(1), slot(2), slot(3),
        pl.BlockSpec((8, TM_FIN), lambda i: (0, i)),
        pl.BlockSpec((None, None, 1, D_MODEL), lambda i: (5, (first + i) // per_b, 0, 0)),
        pl.BlockSpec((1, D_MODEL), lambda i: (0, 0)),
    ]
    args = [x1, yg, yg, yg, yg, w4, mod4, final_norm_g]
    aliases = {}
    if out_so_far is not None:
        in_specs.append(pl.BlockSpec(memory_space=pl.ANY))
        args.append(out_so_far)
        aliases = {len(args) - 1: 0}
    return pl.pallas_call(
        _final_kernel,
        grid=(ntile,),
        in_specs=in_specs,
        out_specs=pl.BlockSpec((TM_FIN, D_MODEL), lambda i: (first + i, 0)),
        out_shape=jax.ShapeDtypeStruct((N_TOK, D_MODEL), F32),
        input_output_aliases=aliases,
        compiler_params=pltpu.CompilerParams(
            dimension_semantics=("arbitrary",), vmem_limit_bytes=VMEM_LIMIT),
        name="combine_final",
    )(*args)


SC_CORES = 2
SC_SUBCORES = 16
SC_WORKERS = SC_CORES * SC_SUBCORES
SC_CHUNK = 64


def _sc_mesh():
    return plsc.VectorSubcoreMesh(core_axis_name="c", subcore_axis_name="s")


def _row_buffers():
    return ([pltpu.VMEM((SC_CHUNK, HALF), U32)] * 2 + [pltpu.SemaphoreType.DMA] * 5)


def _dispatch_rows(h2p, dest2d):
    per_w = N_GRP // SC_WORKERS
    nchunk = per_w // SC_CHUNK
    rows_per_k = N_GRP // SC_CHUNK
    assert nchunk % 2 == 0

    @functools.partial(
        pl.kernel, mesh=_sc_mesh(), out_type=jax.ShapeDtypeStruct((P_ROWS, HALF), U32),
        scratch_types=[pltpu.VMEM((TOP_K * nchunk, SC_CHUNK), I32)] + _row_buffers(),
        name="moe_dispatch")
    def k(src_hbm, dest_hbm, out_hbm, idx_v, buf0, buf1, isem, l0, l1, s0, s1):
        wid = lax.axis_index("s") * SC_CORES + lax.axis_index("c")
        bufs, lsem, ssem = (buf0, buf1), (l0, l1), (s0, s1)
        idx_loads = [
            pltpu.make_async_copy(dest_hbm.at[pl.ds(kk * rows_per_k + wid * nchunk, nchunk)],
                                  idx_v.at[pl.ds(kk * nchunk, nchunk)], isem) for kk in range(TOP_K)]

        def load(j, b):
            return pltpu.make_async_copy(src_hbm.at[pl.ds(wid * per_w + j * SC_CHUNK, SC_CHUNK)], bufs[b], lsem[b])

        def scatters(j, b):
            return [pltpu.make_async_copy(bufs[b], out_hbm.at[idx_v.at[kk * nchunk + j]], ssem[b])
                    for kk in range(TOP_K)]

        for cp in idx_loads:
            cp.start()
        load(0, 0).start()
        for cp in idx_loads:
            cp.wait()

        @pl.loop(0, nchunk // 2)
        def _(i):
            for b in range(2):
                j = 2 * i + b

                @pl.when(j >= 1)
                def _():
                    for cp in scatters(j - 1, 1 - b):
                        cp.wait()

                @pl.when(j + 1 < nchunk)
                def _():
                    load(j + 1, 1 - b).start()

                load(j, b).wait()
                for cp in scatters(j, b):
                    cp.start()

        for cp in scatters(nchunk - 1, 1):
            cp.wait()

    return k(h2p, dest2d)


def _combine_rows(yb, dest2d):
    n_out = TOP_K * N_GRP
    per_w = n_out // SC_WORKERS
    nchunk = per_w // SC_CHUNK
    assert nchunk % 2 == 0

    @functools.partial(
        pl.kernel, mesh=_sc_mesh(), out_type=jax.ShapeDtypeStruct((n_out, HALF), U32),
        scratch_types=[pltpu.VMEM((nchunk, SC_CHUNK), I32)] + _row_buffers(),
        name="moe_combine")
    def k(tab_hbm, idx_hbm, out_hbm, idx_v, buf0, buf1, isem, g0, g1, s0, s1):
        wid = lax.axis_index("s") * SC_CORES + lax.axis_index("c")
        bufs, gsem, ssem = (buf0, buf1), (g0, g1), (s0, s1)
        idx_load = pltpu.make_async_copy(idx_hbm.at[pl.ds(wid * nchunk, nchunk)], idx_v, isem)

        def gather(j, b):
            return pltpu.make_async_copy(tab_hbm.at[idx_v.at[j]], bufs[b], gsem[b])

        def store(j, b):
            return pltpu.make_async_copy(bufs[b], out_hbm.at[pl.ds(wid * per_w + j * SC_CHUNK, SC_CHUNK)], ssem[b])

        idx_load.start()
        idx_load.wait()
        gather(0, 0).start()

        @pl.loop(0, nchunk // 2)
        def _(i):
            for b in range(2):
                j = 2 * i + b

                @pl.when(j >= 1)
                def _():
                    store(j - 1, 1 - b).wait()

                @pl.when(j + 1 < nchunk)
                def _():
                    gather(j + 1, 1 - b).start()

                gather(j, b).wait()
                store(j, b).start()

        store(nchunk - 1, 1).wait()

    return k(yb, dest2d)


def _lambda_kernel(p_ref, o_ref):
    p = p_ref[...]
    s1 = jnp.sum(p[0:1] * p[1:2], axis=-1, keepdims=True)
    s2 = jnp.sum(p[2:3] * p[3:4], axis=-1, keepdims=True)
    o_ref[...] = jnp.broadcast_to(jnp.exp(s1) - jnp.exp(s2) + LAMBDA_INIT, (1, LANES))


def kernel(x, c, w_ada, b_ada, norm1_g, w_in, lambda_q1, lambda_k1, lambda_q2, lambda_k2, diff_norm_g, w_alpha_up, b_alpha, gla_norm_g, w_branch_diff, w_branch_gla, w_out, norm2_g, w_router, b_router, w_gate_up, b_gate_up, w_down, b_down, final_norm_g):
    w_in0 = w_in[0]
    c_a, c_g = W_A, W_A + W_G
    w_a = w_in0[:, :c_a].astype(BF16)
    w_g = w_in0[:, c_a:c_g].astype(BF16)
    w_lr = jnp.pad(w_in0[:, c_g:c_g + GLA_RANK], ((0, 0), (0, LANES - GLA_RANK))).astype(BF16)
    w_gate = w_in0[:, c_g + GLA_RANK:].astype(BF16)
    w_up = jnp.pad(w_alpha_up[0], ((0, LANES - GLA_RANK), (0, 0)))
    lam_in = jnp.concatenate([lambda_q1, lambda_k1, lambda_q2, lambda_k2], axis=0)
    slopes = jnp.asarray(2.0 ** (-8.0 * np.arange(1, DIFF_HEADS + 1) / DIFF_HEADS), dtype=F32)

    mod = _modulation(c, w_ada[0], b_ada[0])
    mod4 = mod.reshape(N_MOD, BATCH, 1, D_MODEL)
    lam = pl.pallas_call(
        _lambda_kernel, out_shape=jax.ShapeDtypeStruct((1, LANES), F32), name="lambda")(lam_in)[0, :1]

    qkv_a, qkv_g, gates, glr = _in_proj(x, mod4, norm1_g, w_a, w_g, w_gate, w_lr)
    pos = jnp.arange(SEQ, dtype=I32)
    p_hi = (pos >> POS_LO_BITS).astype(F32)
    p_lo = (pos & (POS_LO - 1)).astype(F32)
    kaug = jnp.zeros((SEQ, LANES), F32).at[:, 0].set(p_hi).at[:, 1].set(p_hi).at[:, 2].set(p_lo).at[:, 3].set(
        p_lo).astype(BF16)
    o_a = _diff_attention(qkv_a, slopes, lam, kaug, diff_norm_g)
    o_b = _gla(qkv_g, glr, w_up, b_alpha, gla_norm_g)

    merge_args = (o_a.reshape(N_TOK, -1), o_b.reshape(N_TOK, -1), gates.reshape(N_TOK, W_GATE),
                  x.reshape(N_TOK, D_MODEL), mod4, norm2_g,
                  w_branch_diff[0].astype(BF16), w_branch_gla[0].astype(BF16), w_out[0].astype(BF16),
                  w_router[0].T, b_router[0].reshape(N_EXPERTS, 1))
    fin_g = final_norm_g.reshape(1, D_MODEL)

    routed = []
    for grp in range(MOE_GROUPS):
        x1, h2p, eidx, rank, wts, cnt = _merge_route(grp, *merge_args)
        dest8, be, nv, rows = _route_plan(cnt, eidx, rank)
        dest = dest8[:TOP_K].reshape(-1, SC_CHUNK)
        routed.append((x1, wts, dest, be, nv, rows, _dispatch_rows(h2p, dest)))
    gathered = []
    for x1, wts, dest, be, nv, rows, xb in routed:
        yb = _expert_ffn(be[0, :N_BLK], nv[0, :1], rows[0, :N_BLK], xb,
                         w_gate_up[0], b_gate_up[0], w_down[0], b_down[0])
        gathered.append(_combine_rows(yb, dest))
    out = None
    for grp, ((x1, wts, *_), yg) in enumerate(zip(routed, gathered)):
        out = _final(grp, x1, yg, wts, mod4, fin_g, out)
    return out.reshape(BATCH, SEQ, D_MODEL)
```

```python
import functools
import math

import jax
import jax.numpy as jnp
import numpy as np
from jax import lax
from jax.experimental import pallas as pl
from jax.experimental.pallas import tpu as pltpu
from jax.experimental.pallas import tpu_sc as plsc

F32 = jnp.float32
BF16 = jnp.bfloat16
U32 = jnp.uint32
I32 = jnp.int32

D_MODEL = 1024
BATCH = 16
SEQ = 2048
N_TOK = BATCH * SEQ
CHUNK = 64
DIFF_HEADS = 4
DIFF_DH = 64
GLA_HEADS = 4
GLA_DK = 64
GLA_DV = 128
GLA_RANK = 16
GLA_GATE_NORM = 16.0
N_EXPERTS = 32
TOP_K = 4
D_FF = D_MODEL
SWIGLU_LIMIT = 7.0
SWIGLU_ALPHA = 1.702
N_MOD = 6
EPS = 1e-6
LAMBDA_INIT = 0.8 - 0.6 * math.exp(-0.3 * 0)

LANES = 128
HALF = D_MODEL // 2
ROW_W = HALF
ROW_DT = U32

TM_IN = 1024
TQ = 512
ATTN_HEADS = 2
ATTN_STRIPS = 4
ATTN_W = 2 * TQ // ATTN_STRIPS
VT_ROWS = 2 * DIFF_DH + 16
LOG2E = math.log2(math.e)
TM_MERGE = 512
BLK = 1024
BLK_STEP = 256
MOE_GROUPS = 1
N_GRP = N_TOK // MOE_GROUPS
N_BLK = (N_GRP * TOP_K) // BLK + N_EXPERTS
P_ROWS = N_BLK * BLK
TM_FIN = 1024
VMEM_LIMIT = 56 * 1024 * 1024
CHUNK_SHIFT = CHUNK.bit_length() - 1
POS_LO_BITS = 3
POS_LO = 1 << POS_LO_BITS


def _nt_dot(a, b):
    return lax.dot_general(a, b, (((1,), (1,)), ((), ())), preferred_element_type=F32)


def _tn_dot(a, b):
    return lax.dot_general(a, b, (((0,), (0,)), ((), ())), preferred_element_type=F32)


def _split3(x):
    hi = x.astype(BF16)
    r1 = x - hi.astype(F32)
    mid = r1.astype(BF16)
    lo = (r1 - mid.astype(F32)).astype(BF16)
    return hi, mid, lo


def _pack_halves(y):
    return pltpu.pack_elementwise([y[:, :HALF], y[:, HALF:]], packed_dtype=BF16)


def _unpack_halves(u):
    lo = pltpu.unpack_elementwise(u, index=0, packed_dtype=BF16, unpacked_dtype=F32)
    hi = pltpu.unpack_elementwise(u, index=1, packed_dtype=BF16, unpacked_dtype=F32)
    return lo, hi


def _mod_kernel(c_ref, w_ref, b_ref, o_ref):
    c = c_ref[...]
    s = c * jax.nn.sigmoid(c)
    o_ref[0] = jnp.dot(s.astype(BF16), w_ref[...].astype(BF16),
                       preferred_element_type=F32) + b_ref[...]


def _modulation(c, w_ada, b_ada):
    return pl.pallas_call(
        _mod_kernel,
        grid=(N_MOD,),
        in_specs=[
            pl.BlockSpec((BATCH, D_MODEL), lambda j: (0, 0)),
            pl.BlockSpec((D_MODEL, D_MODEL), lambda j: (0, j)),
            pl.BlockSpec((1, D_MODEL), lambda j: (0, j)),
        ],
        out_specs=pl.BlockSpec((1, BATCH, D_MODEL), lambda j: (j, 0, 0)),
        out_shape=jax.ShapeDtypeStruct((N_MOD, BATCH, D_MODEL), F32),
        compiler_params=pltpu.CompilerParams(dimension_semantics=("arbitrary",)),
        name="adaln_mod",
    )(c, w_ada, b_ada.reshape(1, N_MOD * D_MODEL))


W_A = 3 * DIFF_HEADS * 2 * DIFF_DH
W_G = 2 * GLA_HEADS * GLA_DK + 2 * GLA_HEADS * GLA_DV
W_GATE = 2 * D_MODEL


def _in_kernel(x_ref, sh_ref, sc_ref, g_ref, wa_ref, wg_ref, wgate_ref, wlr_ref,
               oa_ref, og_ref, ogate_ref, olr_ref):
    x = x_ref[...]
    inv = lax.rsqrt(jnp.mean(x * x, axis=-1, keepdims=True) + EPS)
    h = (x * inv * g_ref[...]) * (1.0 + sc_ref[...]) + sh_ref[...]
    hb = h.astype(BF16)
    oa_ref[...] = jnp.dot(hb, wa_ref[...], preferred_element_type=F32).astype(BF16)
    og_ref[...] = jnp.dot(hb, wg_ref[...], preferred_element_type=F32).astype(BF16)
    ogate_ref[...] = jnp.dot(hb, wgate_ref[...], preferred_element_type=F32).astype(BF16)
    olr_ref[...] = jnp.dot(hb, wlr_ref[...], preferred_element_type=F32)


def _in_proj(x, mod4, norm1_g, w_a, w_g, w_gate, w_lr):
    nrow = SEQ // TM_IN
    full = lambda shape: pl.BlockSpec(shape, lambda b, i: (0,) * len(shape), pipeline_mode=pl.Buffered(1))
    return pl.pallas_call(
        _in_kernel,
        grid=(BATCH, nrow),
        in_specs=[
            pl.BlockSpec((None, TM_IN, D_MODEL), lambda b, i: (b, i, 0)),
            pl.BlockSpec((None, None, 1, D_MODEL), lambda b, i: (0, b, 0, 0)),
            pl.BlockSpec((None, None, 1, D_MODEL), lambda b, i: (1, b, 0, 0)),
            full((1, D_MODEL)),
            full((D_MODEL, W_A)),
            full((D_MODEL, W_G)),
            full((D_MODEL, W_GATE)),
            full((D_MODEL, LANES)),
        ],
        out_specs=[
            pl.BlockSpec((None, TM_IN, W_A), lambda b, i: (b, i, 0)),
            pl.BlockSpec((None, TM_IN, W_G), lambda b, i: (b, i, 0)),
            pl.BlockSpec((None, TM_IN, W_GATE), lambda b, i: (b, i, 0)),
            pl.BlockSpec((None, TM_IN, LANES), lambda b, i: (b, i, 0)),
        ],
        out_shape=[
            jax.ShapeDtypeStruct((BATCH, SEQ, W_A), BF16),
            jax.ShapeDtypeStruct((BATCH, SEQ, W_G), BF16),
            jax.ShapeDtypeStruct((BATCH, SEQ, W_GATE), BF16),
            jax.ShapeDtypeStruct((BATCH, SEQ, LANES), F32),
        ],
        compiler_params=pltpu.CompilerParams(
            dimension_semantics=("arbitrary", "arbitrary"), vmem_limit_bytes=VMEM_LIMIT),
        name="in_proj",
    )(x, mod4, mod4, norm1_g, w_a, w_g, w_gate, w_lr)


def _attn_kernel(slope_ref, lam_ref, q_ref, k_ref, v_ref, kaug_ref, g_ref, o_ref,
                 sa_ref, sb_ref, lha_ref, lhb_ref, corr_ref, vt_ref, m_ref, acc_ref):
    lam = lam_ref[0]
    lane = lax.broadcasted_iota(I32, (TQ, 2 * DIFF_DH), 1)
    zero = jnp.zeros((TQ, 2 * DIFF_DH), BF16)
    heads = range(ATTN_HEADS)
    c_alibi, aug = [], []
    for hh in heads:
        c_alibi.append(slope_ref[pl.program_id(1) * ATTN_HEADS + hh] * LOG2E)
        c_vec = jnp.full((TQ, 2 * DIFF_DH), c_alibi[hh], F32)
        c_hi = c_vec.astype(BF16).astype(F32)
        c_lo = (c_vec - c_hi).astype(BF16).astype(F32)
        aug.append(jnp.where(lane == 0, POS_LO * c_hi, jnp.where(lane == 1, POS_LO * c_lo,
                             jnp.where(lane == 2, c_hi, jnp.where(lane == 3, c_lo, 0.0)))).astype(BF16))

    def blk(j):
        return slice(j * TQ, (j + 1) * TQ)

    def cols(hh):
        return slice(hh * 2 * DIFF_DH, (hh + 1) * 2 * DIFF_DH)

    def stacked_queries(hh, qi):
        q = (q_ref[blk(qi), cols(hh)].astype(F32) * (DIFF_DH ** -0.5 * LOG2E)).astype(BF16)
        return jnp.concatenate([
            jnp.concatenate([jnp.where(lane < DIFF_DH, q, zero), aug[hh]], axis=1),
            jnp.concatenate([jnp.where(lane >= DIFF_DH, q, zero), aug[hh]], axis=1)], axis=0)

    lhs_refs = (lha_ref, lhb_ref)

    def strip(c):
        return slice(c * ATTN_W, (c + 1) * ATTN_W)

    def keys(hh, j):
        return jnp.concatenate([k_ref[blk(j), cols(hh)], kaug_ref[blk(j), :]], axis=1)

    def scores(hh, kk, qi, c):
        return _nt_dot(kk, lhs_refs[qi % 2][hh, strip(c), :])

    ones_rows = jnp.where(lax.broadcasted_iota(I32, (VT_ROWS - 2 * DIFF_DH, TQ), 0) == 0, 1.0, 0.0).astype(BF16)
    kr = lax.broadcasted_iota(I32, (TQ, TQ), 0)
    qc = lax.broadcasted_iota(I32, (TQ, TQ), 1)
    ahead = jnp.maximum(kr - qc, 0).astype(F32)
    for hh in heads:
        for j in range(SEQ // TQ):
            vt_ref[hh, j, :2 * DIFF_DH, :] = v_ref[blk(j), cols(hh)].astype(F32).T.astype(BF16)
            vt_ref[hh, j, 2 * DIFF_DH:, :] = ones_rows
        corr_ref[hh] = jnp.where((qc >> CHUNK_SHIFT) >= (kr >> CHUNK_SHIFT), (-2.0 * c_alibi[hh]) * ahead, -jnp.inf)

    def update(hh, s, j, c, first):
        nk = s.shape[0]
        if first:
            m_new = jnp.max(s, axis=0, keepdims=True)
            p = jnp.exp2((s - m_new).astype(BF16))
            acc_ref[hh, :, strip(c)] = jnp.dot(vt_ref[hh, j, :, :nk], p, preferred_element_type=F32)
        else:
            m = m_ref[hh, :, strip(c)]
            m_new = jnp.maximum(m, jnp.max(s, axis=0, keepdims=True))
            alpha = jnp.exp2(m - m_new)
            p = jnp.exp2((s - m_new).astype(BF16))
            acc_ref[hh, :, strip(c)] = (alpha * acc_ref[hh, :, strip(c)]
                                        + jnp.dot(vt_ref[hh, j, :, :nk], p, preferred_element_type=F32))
        m_ref[hh, :, strip(c)] = m_new

    def n_keys(qi, j, c):
        return TQ if j < qi else min(TQ, (c * ATTN_W) % TQ + ATTN_W)

    pairs = [(qi, j) for qi in range(SEQ // TQ) for j in range(qi + 1)]
    bufs = (sa_ref, sb_ref)
    kk = [None] * ATTN_HEADS
    for hh in heads:
        lhs_refs[0][hh] = stacked_queries(hh, 0)
        kk[hh] = keys(hh, 0)
    for c in range(ATTN_STRIPS):
        for hh in heads:
            nk = n_keys(0, 0, c)
            bufs[0][hh, :nk, strip(c)] = scores(hh, kk[hh][:nk], 0, c)
    for t, (qi, j) in enumerate(pairs):
        nxt = pairs[t + 1] if t + 1 < len(pairs) else None
        if nxt is not None:
            for hh in heads:
                if nxt[0] != qi:
                    lhs_refs[nxt[0] % 2][hh] = stacked_queries(hh, nxt[0])
                kk[hh] = keys(hh, nxt[1])
        for c in range(ATTN_STRIPS):
            for hh in heads:
                if nxt is not None:
                    nk = n_keys(*nxt, c)
                    bufs[(t + 1) % 2][hh, :nk, strip(c)] = scores(hh, kk[hh][:nk], nxt[0], c)
                nk = n_keys(qi, j, c)
                s = bufs[t % 2][hh, :nk, strip(c)]
                if j == qi:
                    lo = (c * ATTN_W) % TQ
                    s = s + corr_ref[hh, :nk, lo:lo + ATTN_W]
                update(hh, s, j, c, first=(j == 0))
        if j == qi:
            for hh in heads:
                ot = acc_ref[hh, :2 * DIFF_DH, :] / acc_ref[hh, 2 * DIFF_DH:2 * DIFF_DH + 1, :]
                o = (ot[:, :TQ] - lam * ot[:, TQ:]).T
                inv = lax.rsqrt(jnp.mean(o * o, axis=-1, keepdims=True) + EPS)
                o_ref[blk(qi), cols(hh)] = (o * inv * g_ref[...] * (1.0 - LAMBDA_INIT)).astype(BF16)


def _diff_attention(qkv_a, slopes, lam, kaug, diff_norm_g):
    groups = DIFF_HEADS // ATTN_HEADS
    width = ATTN_HEADS * 2 * DIFF_DH
    per_head = lambda shape, dt: pltpu.VMEM((ATTN_HEADS,) + shape, dt)
    return pl.pallas_call(
        _attn_kernel,
        grid=(BATCH, groups),
        in_specs=[
            pl.BlockSpec(memory_space=pltpu.SMEM),
            pl.BlockSpec(memory_space=pltpu.SMEM),
            pl.BlockSpec((None, SEQ, width), lambda b, h: (b, 0, h)),
            pl.BlockSpec((None, SEQ, width), lambda b, h: (b, 0, groups + h)),
            pl.BlockSpec((None, SEQ, width), lambda b, h: (b, 0, 2 * groups + h)),
            pl.BlockSpec((SEQ, LANES), lambda b, h: (0, 0)),
            pl.BlockSpec((1, LANES), lambda b, h: (0, 0)),
        ],
        out_specs=pl.BlockSpec((None, SEQ, width), lambda b, h: (b, 0, h)),
        out_shape=jax.ShapeDtypeStruct((BATCH, SEQ, DIFF_HEADS * 2 * DIFF_DH), BF16),
        scratch_shapes=[
            per_head((TQ, 2 * TQ), F32), per_head((TQ, 2 * TQ), F32),
            per_head((2 * TQ, 4 * DIFF_DH), BF16), per_head((2 * TQ, 4 * DIFF_DH), BF16),
            per_head((TQ, TQ), F32),
            per_head((SEQ // TQ, VT_ROWS, TQ), BF16),
            per_head((1, 2 * TQ), F32),
            per_head((VT_ROWS, 2 * TQ), F32),
        ],
        compiler_params=pltpu.CompilerParams(
            dimension_semantics=("arbitrary", "arbitrary"), vmem_limit_bytes=VMEM_LIMIT),
        name="diff_attn",
    )(slopes, lam, qkv_a, qkv_a, qkv_a, kaug, diff_norm_g)


N_CHUNK = SEQ // CHUNK
PAIR = 2 * GLA_DK
PAIR_V = 2 * GLA_DV
CS_ROWS = 256
GLA_UNROLL = 32


def _gla_kernel(q_ref, k_ref, v_ref, r_ref, lr_ref, wup_ref, bup_ref, g_ref, o_ref,
                gcum_ref, state_ref):
    w_hi, w_mid, _ = _split3(wup_ref[...])
    rr = lax.broadcasted_iota(I32, (CS_ROWS, CS_ROWS), 0)
    cc = lax.broadcasted_iota(I32, (CS_ROWS, CS_ROWS), 1)
    tri = jnp.where(((rr >> CHUNK_SHIFT) == (cc >> CHUNK_SHIFT)) & (cc <= rr), 1.0, 0.0).astype(BF16)
    for blk in range(SEQ // CS_ROWS):
        rows = pl.ds(blk * CS_ROWS, CS_ROWS)
        a_hi, a_mid, _ = _split3(lr_ref[rows, :])
        z = (jnp.dot(a_hi, w_hi, preferred_element_type=F32)
             + jnp.dot(a_hi, w_mid, preferred_element_type=F32)
             + jnp.dot(a_mid, w_hi, preferred_element_type=F32)) + bup_ref[...]
        la = (jnp.minimum(z, 0.0) - jnp.log(1.0 + jnp.exp(-jnp.abs(z)))) * (1.0 / GLA_GATE_NORM)
        l_hi, l_mid, l_lo = _split3(la)
        gcum_ref[rows, :] = (jnp.dot(tri, l_hi, preferred_element_type=F32)
                             + jnp.dot(tri, l_mid, preferred_element_type=F32)
                             + jnp.dot(tri, l_lo, preferred_element_type=F32))

    state_ref[...] = jnp.zeros_like(state_ref)
    lane_k = lax.broadcasted_iota(I32, (1, PAIR), 1)
    row_v = lax.broadcasted_iota(I32, (PAIR_V, PAIR), 0)
    col_k = lax.broadcasted_iota(I32, (PAIR_V, PAIR), 1)
    same_head = (row_v >= GLA_DV) == (col_k >= GLA_DK)
    cr = lax.broadcasted_iota(I32, (CHUNK, CHUNK), 0)
    cs = lax.broadcasted_iota(I32, (CHUNK, CHUNK), 1)
    causal = cs <= cr
    scale = GLA_DK ** -0.5

    def chunk(n):
        rows = pl.ds(pl.multiple_of(n * CHUNK, CHUNK), CHUNK)
        gc = gcum_ref[rows, :]
        g_last = gcum_ref[pl.ds(n * CHUNK + CHUNK - 1, 1), :]
        qf = q_ref[rows, :].astype(F32) * scale
        kf = k_ref[rows, :].astype(F32)
        q_s = (qf * jnp.exp(gc)).astype(BF16)
        k_s = (kf * jnp.exp(-gc)).astype(BF16)
        k_d = (kf * jnp.exp(g_last - gc)).astype(BF16)
        decay = jnp.exp(g_last)
        for pr in range(GLA_HEADS // 2):
            kl = slice(pr * PAIR, (pr + 1) * PAIR)
            vl = slice(pr * PAIR_V, (pr + 1) * PAIR_V)
            qs_p, ks_p, kd_p = q_s[:, kl], k_s[:, kl], k_d[:, kl]
            v_p = v_ref[rows, vl]
            st = state_ref[pr]
            o_inter = _nt_dot(qs_p, st.astype(BF16))
            d_st = _tn_dot(v_p, kd_p)
            state_ref[pr] = st * decay[:, kl] + jnp.where(same_head, d_st, 0.0)
            for sub in range(2):
                hd = 2 * pr + sub
                in_head = (lane_k >= sub * GLA_DK) & (lane_k < (sub + 1) * GLA_DK)
                a = _nt_dot(jnp.where(in_head, qs_p, jnp.zeros_like(qs_p)), ks_p)
                a = jnp.where(causal, a, 0.0).astype(BF16)
                vs = slice(hd * GLA_DV, (hd + 1) * GLA_DV)
                o = (jnp.dot(a, v_ref[rows, vs], preferred_element_type=F32)
                     + o_inter[:, sub * GLA_DV:(sub + 1) * GLA_DV])
                inv = lax.rsqrt(jnp.mean(o * o, axis=-1, keepdims=True) + EPS)
                r = r_ref[rows, vs].astype(F32)
                o_ref[rows, vs] = (o * inv * g_ref[...] * (r * jax.nn.sigmoid(r))).astype(BF16)

    def chunk_group(t, _):
        for u in range(GLA_UNROLL):
            chunk(t * GLA_UNROLL + u)
        return 0

    lax.fori_loop(0, N_CHUNK // GLA_UNROLL, chunk_group, 0)


def _gla(qkv_g, glr, w_up, b_up, gla_norm_g):
    qk_w = GLA_HEADS * GLA_DK
    v_w = GLA_HEADS * GLA_DV
    return pl.pallas_call(
        _gla_kernel,
        grid=(BATCH,),
        in_specs=[
            pl.BlockSpec((None, SEQ, qk_w), lambda b: (b, 0, 0)),
            pl.BlockSpec((None, SEQ, qk_w), lambda b: (b, 0, 1)),
            pl.BlockSpec((None, SEQ, v_w), lambda b: (b, 0, 1)),
            pl.BlockSpec((None, SEQ, v_w), lambda b: (b, 0, 2)),
            pl.BlockSpec((None, SEQ, LANES), lambda b: (b, 0, 0)),
            pl.BlockSpec((LANES, qk_w), lambda b: (0, 0)),
            pl.BlockSpec((1, qk_w), lambda b: (0, 0)),
            pl.BlockSpec((1, GLA_DV), lambda b: (0, 0)),
        ],
        out_specs=pl.BlockSpec((None, SEQ, v_w), lambda b: (b, 0, 0)),
        out_shape=jax.ShapeDtypeStruct((BATCH, SEQ, v_w), BF16),
        scratch_shapes=[
            pltpu.VMEM((SEQ, qk_w), F32),
            pltpu.VMEM((GLA_HEADS // 2, PAIR_V, PAIR), F32),
        ],
        compiler_params=pltpu.CompilerParams(
            dimension_semantics=("arbitrary",), vmem_limit_bytes=VMEM_LIMIT),
        name="gla",
    )(qkv_g, qkv_g, qkv_g, qkv_g, glr, w_up, b_up, gla_norm_g)


def _merge_kernel(oa_ref, ob_ref, gate_ref, x_ref, gt1_ref, sh2_ref, sc2_ref, g2_ref,
                  wpa_ref, wpb_ref, wo_ref, wr_ref, br_ref,
                  x1_ref, h2p_ref, eidx_ref, rank_ref, wts_ref, cnt_ref,
                  upper_ref, carry_ref):
    i = pl.program_id(0)

    @pl.when(i == 0)
    def _():
        rr = lax.broadcasted_iota(I32, (TM_MERGE, TM_MERGE), 0)
        cc = lax.broadcasted_iota(I32, (TM_MERGE, TM_MERGE), 1)
        upper_ref[...] = jnp.where(rr < cc, 1.0, 0.0).astype(BF16)
        carry_ref[...] = jnp.zeros_like(carry_ref)

    ga = gate_ref[:, :D_MODEL].astype(F32)
    gb = gate_ref[:, D_MODEL:].astype(F32)
    merged = (jax.nn.sigmoid(ga) * jnp.dot(oa_ref[...], wpa_ref[...], preferred_element_type=F32)
              + jax.nn.sigmoid(gb) * jnp.dot(ob_ref[...], wpb_ref[...], preferred_element_type=F32))
    y = jnp.dot(merged.astype(BF16), wo_ref[...], preferred_element_type=F32)
    x1 = x_ref[...] + gt1_ref[...] * y
    x1_ref[...] = x1
    inv = lax.rsqrt(jnp.mean(x1 * x1, axis=-1, keepdims=True) + EPS)
    h2 = (x1 * inv * g2_ref[...]) * (1.0 + sc2_ref[...]) + sh2_ref[...]
    h2p_ref[...] = _pack_halves(h2)

    h_hi, h_mid, _ = _split3(h2)
    w_hi, w_mid, _ = _split3(wr_ref[...])
    logits = (_nt_dot(w_hi, h_hi) + _nt_dot(w_hi, h_mid) + _nt_dot(w_mid, h_hi)) + br_ref[...]

    eio = lax.broadcasted_iota(I32, (N_EXPERTS, TM_MERGE), 0)
    vals, idxs, sels = [], [], []
    cur = logits
    for _k in range(TOP_K):
        m = jnp.max(cur, axis=0, keepdims=True)
        idx = jnp.min(jnp.where(cur == m, eio, N_EXPERTS), axis=0, keepdims=True)
        sel = eio == idx
        vals.append(m)
        idxs.append(idx)
        sels.append(sel)
        cur = jnp.where(sel, -jnp.inf, cur)
    es = [jnp.exp(v - vals[0]) for v in vals]
    tot = es[0] + es[1] + es[2] + es[3]
    onehot = jnp.zeros((N_EXPERTS, TM_MERGE), F32)
    for sel in sels:
        onehot = onehot + jnp.where(sel, 1.0, 0.0)
    before = jnp.dot(onehot.astype(BF16), upper_ref[...], preferred_element_type=F32) + carry_ref[:, 0:1]
    ranks = [jnp.sum(jnp.where(sel, before, 0.0), axis=0, keepdims=True) for sel in sels]
    carry_ref[...] = carry_ref[...] + jnp.sum(onehot, axis=1, keepdims=True)
    cnt_ref[...] = carry_ref[...]

    zi = jnp.zeros((8 - TOP_K, TM_MERGE), I32)
    zf = jnp.zeros((8 - TOP_K, TM_MERGE), F32)
    eidx_ref[...] = jnp.concatenate(idxs + [zi], axis=0)
    rank_ref[...] = jnp.concatenate([r.astype(I32) for r in ranks] + [zi], axis=0)
    wts_ref[...] = jnp.concatenate([e / tot for e in es] + [zf], axis=0)


def _merge_route(grp, o_a, o_b, gates, x2d, mod4, norm2_g, w_pa, w_pb, w_o, w_rt, b_r):
    ntile = N_GRP // TM_MERGE
    first = grp * ntile
    per_b = SEQ // TM_MERGE
    full = lambda shape: pl.BlockSpec(shape, lambda i: (0,) * len(shape))
    row_in = lambda w: pl.BlockSpec((TM_MERGE, w), lambda i: (first + i, 0))
    row = lambda w: pl.BlockSpec((TM_MERGE, w), lambda i: (i, 0))
    modspec = lambda j: pl.BlockSpec((None, None, 1, D_MODEL), lambda i: (j, (first + i) // per_b, 0, 0))
    col = pl.BlockSpec((8, TM_MERGE), lambda i: (0, i))
    return pl.pallas_call(
        _merge_kernel,
        grid=(ntile,),
        in_specs=[
            row_in(DIFF_HEADS * 2 * DIFF_DH), row_in(GLA_HEADS * GLA_DV), row_in(W_GATE), row_in(D_MODEL),
            modspec(2), modspec(3), modspec(4),
            full((1, D_MODEL)),
            full((DIFF_HEADS * 2 * DIFF_DH, D_MODEL)), full((GLA_HEADS * GLA_DV, D_MODEL)),
            full((D_MODEL, D_MODEL)),
            full((N_EXPERTS, D_MODEL)), full((N_EXPERTS, 1)),
        ],
        out_specs=[
            row(D_MODEL), row(ROW_W), col, col, col,
            pl.BlockSpec((N_EXPERTS, LANES), lambda i: (0, 0)),
        ],
        out_shape=[
            jax.ShapeDtypeStruct((N_GRP, D_MODEL), F32),
            jax.ShapeDtypeStruct((N_GRP, ROW_W), ROW_DT),
            jax.ShapeDtypeStruct((8, N_GRP), I32),
            jax.ShapeDtypeStruct((8, N_GRP), I32),
            jax.ShapeDtypeStruct((8, N_GRP), F32),
            jax.ShapeDtypeStruct((N_EXPERTS, LANES), F32),
        ],
        scratch_shapes=[
            pltpu.VMEM((TM_MERGE, TM_MERGE), BF16),
            pltpu.VMEM((N_EXPERTS, LANES), F32),
        ],
        compiler_params=pltpu.CompilerParams(
            dimension_semantics=("arbitrary",), vmem_limit_bytes=VMEM_LIMIT),
        name="merge_route",
    )(o_a, o_b, gates, x2d, mod4, mod4, mod4, norm2_g, w_pa, w_pb, w_o, w_rt, b_r)


TP = 4096
NB_PAD = ((N_BLK + LANES - 1) // LANES) * LANES


def _plan_kernel(cnt_ref, eidx_ref, rank_ref, dest_ref, be_ref, nv_ref, rows_ref):
    cnt = cnt_ref[...]
    padded = jnp.floor((cnt + (BLK - 1.0)) * (1.0 / BLK)) * BLK
    er = lax.broadcasted_iota(I32, (N_EXPERTS, N_EXPERTS), 0)
    ec = lax.broadcasted_iota(I32, (N_EXPERTS, N_EXPERTS), 1)
    lower = jnp.where(ec < er, 1.0, 0.0).astype(BF16)
    p_hi, p_mid, p_lo = _split3(padded)
    starts = (jnp.dot(lower, p_hi, preferred_element_type=F32)
              + jnp.dot(lower, p_mid, preferred_element_type=F32)
              + jnp.dot(lower, p_lo, preferred_element_type=F32))
    ends = starts + padded
    blk_start = (lax.broadcasted_iota(I32, (1, NB_PAD), 1) * BLK).astype(F32)
    n_before = jnp.sum(jnp.where(ends[:, 0:1] <= blk_start, 1.0, 0.0), axis=0, keepdims=True)
    be_ref[...] = jnp.minimum(n_before, N_EXPERTS - 1.0).astype(I32)
    nv_ref[...] = (jnp.max(ends, axis=0, keepdims=True) * (1.0 / BLK)).astype(I32)
    owner = (starts[:, 0:1] <= blk_start) & (blk_start < ends[:, 0:1])
    filled = jnp.sum(jnp.where(owner, (starts + cnt)[:, 0:1] - blk_start, 0.0), axis=0, keepdims=True)
    rows_ref[...] = jnp.clip(filled, 0.0, float(BLK)).astype(I32)

    eio = lax.broadcasted_iota(I32, (N_EXPERTS, TP), 0)
    rows = []
    for k in range(TOP_K):
        onehot = eio == eidx_ref[k:k + 1, :]
        base = jnp.sum(jnp.where(onehot, starts[:, 0:1], 0.0), axis=0, keepdims=True)
        rows.append(base.astype(I32) + rank_ref[k:k + 1, :])
    dest_ref[...] = jnp.concatenate(rows + [jnp.zeros((8 - TOP_K, TP), I32)], axis=0)


def _route_plan(cnt, eidx, rank):
    return pl.pallas_call(
        _plan_kernel,
        grid=(N_GRP // TP,),
        in_specs=[
            pl.BlockSpec((N_EXPERTS, LANES), lambda i: (0, 0)),
            pl.BlockSpec((8, TP), lambda i: (0, i)),
            pl.BlockSpec((8, TP), lambda i: (0, i)),
        ],
        out_specs=[
            pl.BlockSpec((8, TP), lambda i: (0, i)),
            pl.BlockSpec((1, NB_PAD), lambda i: (0, 0)),
            pl.BlockSpec((1, LANES), lambda i: (0, 0)),
            pl.BlockSpec((1, NB_PAD), lambda i: (0, 0)),
        ],
        out_shape=[
            jax.ShapeDtypeStruct((8, N_GRP), I32),
            jax.ShapeDtypeStruct((1, NB_PAD), I32),
            jax.ShapeDtypeStruct((1, LANES), I32),
            jax.ShapeDtypeStruct((1, NB_PAD), I32),
        ],
        compiler_params=pltpu.CompilerParams(dimension_semantics=("arbitrary",)),
        name="route_plan",
    )(cnt, eidx, rank)


def _ffn_kernel(be_ref, nv_ref, first_ref, slot_ref, nxt_ref, rows_ref, x_ref, wgu_hbm, bgu_ref, wd_hbm, bd_ref,
                y_ref, wgu_buf, wd_buf, sem):
    i = pl.program_id(0)
    valid = i < nv_ref[0]
    filled = jnp.maximum(rows_ref[i], 1)
    s = slot_ref[i]

    def weight_copies(e, slot):
        return (pltpu.make_async_copy(wgu_hbm.at[e], wgu_buf.at[slot], sem.at[slot, 0]),
                pltpu.make_async_copy(wd_hbm.at[e], wd_buf.at[slot], sem.at[slot, 1]))

    @pl.when(i == 0)
    def _():
        for cp in weight_copies(be_ref[0], 0):
            cp.start()

    @pl.when(valid & (first_ref[i] == 1))
    def _():
        for cp in weight_copies(be_ref[i], s):
            cp.wait()

        @pl.when(nxt_ref[i] >= 0)
        def _():
            for cp in weight_copies(nxt_ref[i], 1 - s):
                cp.start()

    def expert_rows(n):
        lo, hi = _unpack_halves(x_ref[:n, :])
        x = jnp.concatenate([lo, hi], axis=1).astype(BF16)
        mm = lambda a, w: lax.dot_general(a, w, (((1,), (0,)), ((), ())), preferred_element_type=F32)
        gu = mm(x, wgu_buf[s]) + bgu_ref[...]
        gate = jnp.minimum(gu[:, :D_FF], SWIGLU_LIMIT)
        up = jnp.clip(gu[:, D_FF:], -SWIGLU_LIMIT, SWIGLU_LIMIT)
        act = (up + 1.0) * (gate * jax.nn.sigmoid(SWIGLU_ALPHA * gate))
        y = mm(act.astype(BF16), wd_buf[s]) + bd_ref[...]
        y_ref[:n, :] = _pack_halves(y)

    for n in range(BLK_STEP, BLK + 1, BLK_STEP):
        @pl.when(valid & (filled > n - BLK_STEP) & (filled <= n))
        def _(n=n):
            expert_rows(n)
            if n < BLK:
                y_ref[n:, :] = jnp.zeros((BLK - n, ROW_W), ROW_DT)

    @pl.when(jnp.logical_not(valid))
    def _():
        y_ref[...] = jnp.zeros_like(y_ref)


def _expert_ffn(blk_expert, n_valid, blk_rows, xb, w_gate_up, b_gate_up, w_down, b_down):
    idx = jnp.arange(N_BLK, dtype=I32)
    used = idx < n_valid[0]
    first = (used & ((idx == 0) | (blk_expert != jnp.roll(blk_expert, 1)))).astype(I32)
    slot = (jnp.cumsum(first) - 1) & 1
    later = used[None, :] & (idx[None, :] > idx[:, None]) & (blk_expert[None, :] != blk_expert[:, None])
    nxt_pos = jnp.min(jnp.where(later, idx[None, :], N_BLK), axis=1)
    nxt = jnp.where(nxt_pos < N_BLK, blk_expert[jnp.minimum(nxt_pos, N_BLK - 1)], -1).astype(I32)

    blockwise = lambda i, *_: (i, 0)
    per_expert = lambda i, be, *_: (be[i], 0, 0)
    grid_spec = pltpu.PrefetchScalarGridSpec(
        num_scalar_prefetch=6,
        grid=(N_BLK,),
        in_specs=[
            pl.BlockSpec((BLK, ROW_W), blockwise),
            pl.BlockSpec(memory_space=pl.ANY),
            pl.BlockSpec((None, 1, 2 * D_FF), per_expert),
            pl.BlockSpec(memory_space=pl.ANY),
            pl.BlockSpec((None, 1, D_MODEL), per_expert),
        ],
        out_specs=pl.BlockSpec((BLK, ROW_W), blockwise),
        scratch_shapes=[
            pltpu.VMEM((2, D_MODEL, 2 * D_FF), F32),
            pltpu.VMEM((2, D_FF, D_MODEL), F32),
            pltpu.SemaphoreType.DMA((2, 2)),
        ],
    )
    return pl.pallas_call(
        _ffn_kernel,
        grid_spec=grid_spec,
        out_shape=jax.ShapeDtypeStruct((P_ROWS, ROW_W), ROW_DT),
        compiler_params=pltpu.CompilerParams(
            dimension_semantics=("arbitrary",), vmem_limit_bytes=VMEM_LIMIT),
        name="expert_ffn",
    )(blk_expert, n_valid, first, slot.astype(I32), nxt, blk_rows, xb, w_gate_up,
      b_gate_up.reshape(N_EXPERTS, 1, 2 * D_FF), w_down, b_down.reshape(N_EXPERTS, 1, D_MODEL))


def _final_kernel(x1_ref, y0_ref, y1_ref, y2_ref, y3_ref, w_ref, gt2_ref, g_ref, *rest):
    o_ref = rest[-1]
    w = w_ref[...].T
    ylo = jnp.zeros((TM_FIN, HALF), F32)
    yhi = jnp.zeros((TM_FIN, HALF), F32)
    for k, y_ref in enumerate((y0_ref, y1_ref, y2_ref, y3_ref)):
        lo, hi = _unpack_halves(y_ref[...])
        wk = w[:, k:k + 1]
        ylo = ylo + wk * lo
        yhi = yhi + wk * hi
    y = jnp.concatenate([ylo, yhi], axis=1)
    x2 = x1_ref[...] + gt2_ref[...] * y
    inv = lax.rsqrt(jnp.mean(x2 * x2, axis=-1, keepdims=True) + EPS)
    o_ref[...] = x2 * inv * g_ref[...]


def _final(grp, x1, yg, w4, mod4, final_norm_g, out_so_far):
    per_b = SEQ // TM_FIN
    ntile = N_GRP // TM_FIN
    first = grp * ntile
    slot = lambda k: pl.BlockSpec((TM_FIN, ROW_W), lambda i: (k * ntile + i, 0))
    in_specs = [
        pl.BlockSpec((TM_FIN, D_MODEL), lambda i: (i, 0)),
        slot(0), slot(1), slot(2), slot(3),
        pl.BlockSpec((8, TM_FIN), lambda i: (0, i)),
        pl.BlockSpec((None, None, 1, D_MODEL), lambda i: (5, (first + i) // per_b, 0, 0)),
        pl.BlockSpec((1, D_MODEL), lambda i: (0, 0)),
    ]
    args = [x1, yg, yg, yg, yg, w4, mod4, final_norm_g]
    aliases = {}
    if out_so_far is not None:
        in_specs.append(pl.BlockSpec(memory_space=pl.ANY))
        args.append(out_so_far)
        aliases = {len(args) - 1: 0}
    return pl.pallas_call(
        _final_kernel,
        grid=(ntile,),
        in_specs=in_specs,
        out_specs=pl.BlockSpec((TM_FIN, D_MODEL), lambda i: (first + i, 0)),
        out_shape=jax.ShapeDtypeStruct((N_TOK, D_MODEL), F32),
        input_output_aliases=aliases,
        compiler_params=pltpu.CompilerParams(
            dimension_semantics=("arbitrary",), vmem_limit_bytes=VMEM_LIMIT),
        name="combine_final",
    )(*args)


SC_CORES = 2
SC_SUBCORES = 16
SC_WORKERS = SC_CORES * SC_SUBCORES
SC_CHUNK = 64


def _sc_mesh():
    return plsc.VectorSubcoreMesh(core_axis_name="c", subcore_axis_name="s")


def _row_buffers():
    return ([pltpu.VMEM((SC_CHUNK, HALF), U32)] * 2 + [pltpu.SemaphoreType.DMA] * 5)


def _dispatch_rows(h2p, dest2d):
    per_w = N_GRP // SC_WORKERS
    nchunk = per_w // SC_CHUNK
    rows_per_k = N_GRP // SC_CHUNK
    assert nchunk % 2 == 0

    @functools.partial(
        pl.kernel, mesh=_sc_mesh(), out_type=jax.ShapeDtypeStruct((P_ROWS, HALF), U32),
        scratch_types=[pltpu.VMEM((TOP_K * nchunk, SC_CHUNK), I32)] + _row_buffers(),
        name="moe_dispatch")
    def k(src_hbm, dest_hbm, out_hbm, idx_v, buf0, buf1, isem, l0, l1, s0, s1):
        wid = lax.axis_index("s") * SC_CORES + lax.axis_index("c")
        bufs, lsem, ssem = (buf0, buf1), (l0, l1), (s0, s1)
        idx_loads = [
            pltpu.make_async_copy(dest_hbm.at[pl.ds(kk * rows_per_k + wid * nchunk, nchunk)],
                                  idx_v.at[pl.ds(kk * nchunk, nchunk)], isem) for kk in range(TOP_K)]

        def load(j, b):
            return pltpu.make_async_copy(src_hbm.at[pl.ds(wid * per_w + j * SC_CHUNK, SC_CHUNK)], bufs[b], lsem[b])

        def scatters(j, b):
            return [pltpu.make_async_copy(bufs[b], out_hbm.at[idx_v.at[kk * nchunk + j]], ssem[b])
                    for kk in range(TOP_K)]

        for cp in idx_loads:
            cp.start()
        load(0, 0).start()
        for cp in idx_loads:
            cp.wait()

        @pl.loop(0, nchunk // 2)
        def _(i):
            for b in range(2):
                j = 2 * i + b

                @pl.when(j >= 1)
                def _():
                    for cp in scatters(j - 1, 1 - b):
                        cp.wait()

                @pl.when(j + 1 < nchunk)
                def _():
                    load(j + 1, 1 - b).start()

                load(j, b).wait()
                for cp in scatters(j, b):
                    cp.start()

        for cp in scatters(nchunk - 1, 1):
            cp.wait()

    return k(h2p, dest2d)


def _combine_rows(yb, dest2d):
    n_out = TOP_K * N_GRP
    per_w = n_out // SC_WORKERS
    nchunk = per_w // SC_CHUNK
    assert nchunk % 2 == 0

    @functools.partial(
        pl.kernel, mesh=_sc_mesh(), out_type=jax.ShapeDtypeStruct((n_out, HALF), U32),
        scratch_types=[pltpu.VMEM((nchunk, SC_CHUNK), I32)] + _row_buffers(),
        name="moe_combine")
    def k(tab_hbm, idx_hbm, out_hbm, idx_v, buf0, buf1, isem, g0, g1, s0, s1):
        wid = lax.axis_index("s") * SC_CORES + lax.axis_index("c")
        bufs, gsem, ssem = (buf0, buf1), (g0, g1), (s0, s1)
        idx_load = pltpu.make_async_copy(idx_hbm.at[pl.ds(wid * nchunk, nchunk)], idx_v, isem)

        def gather(j, b):
            return pltpu.make_async_copy(tab_hbm.at[idx_v.at[j]], bufs[b], gsem[b])

        def store(j, b):
            return pltpu.make_async_copy(bufs[b], out_hbm.at[pl.ds(wid * per_w + j * SC_CHUNK, SC_CHUNK)], ssem[b])

        idx_load.start()
        idx_load.wait()
        gather(0, 0).start()

        @pl.loop(0, nchunk // 2)
        def _(i):
            for b in range(2):
                j = 2 * i + b

                @pl.when(j >= 1)
                def _():
                    store(j - 1, 1 - b).wait()

                @pl.when(j + 1 < nchunk)
                def _():
                    gather(j + 1, 1 - b).start()

                gather(j, b).wait()
                store(j, b).start()

        store(nchunk - 1, 1).wait()

    return k(yb, dest2d)


def _lambda_kernel(p_ref, o_ref):
    p = p_ref[...]
    s1 = jnp.sum(p[0:1] * p[1:2], axis=-1, keepdims=True)
    s2 = jnp.sum(p[2:3] * p[3:4], axis=-1, keepdims=True)
    o_ref[...] = jnp.broadcast_to(jnp.exp(s1) - jnp.exp(s2) + LAMBDA_INIT, (1, LANES))


def kernel(x, c, w_ada, b_ada, norm1_g, w_in, lambda_q1, lambda_k1, lambda_q2, lambda_k2, diff_norm_g, w_alpha_up, b_alpha, gla_norm_g, w_branch_diff, w_branch_gla, w_out, norm2_g, w_router, b_router, w_gate_up, b_gate_up, w_down, b_down, final_norm_g):
    w_in0 = w_in[0]
    c_a, c_g = W_A, W_A + W_G
    w_a = w_in0[:, :c_a].astype(BF16)
    w_g = w_in0[:, c_a:c_g].astype(BF16)
    w_lr = jnp.pad(w_in0[:, c_g:c_g + GLA_RANK], ((0, 0), (0, LANES - GLA_RANK))).astype(BF16)
    w_gate = w_in0[:, c_g + GLA_RANK:].astype(BF16)
    w_up = jnp.pad(w_alpha_up[0], ((0, LANES - GLA_RANK), (0, 0)))
    lam_in = jnp.concatenate([lambda_q1, lambda_k1, lambda_q2, lambda_k2], axis=0)
    slopes = jnp.asarray(2.0 ** (-8.0 * np.arange(1, DIFF_HEADS + 1) / DIFF_HEADS), dtype=F32)

    mod = _modulation(c, w_ada[0], b_ada[0])
    mod4 = mod.reshape(N_MOD, BATCH, 1, D_MODEL)
    lam = pl.pallas_call(
        _lambda_kernel, out_shape=jax.ShapeDtypeStruct((1, LANES), F32), name="lambda")(lam_in)[0, :1]

    qkv_a, qkv_g, gates, glr = _in_proj(x, mod4, norm1_g, w_a, w_g, w_gate, w_lr)
    pos = jnp.arange(SEQ, dtype=I32)
    p_hi = (pos >> POS_LO_BITS).astype(F32)
    p_lo = (pos & (POS_LO - 1)).astype(F32)
    kaug = jnp.zeros((SEQ, LANES), F32).at[:, 0].set(p_hi).at[:, 1].set(p_hi).at[:, 2].set(p_lo).at[:, 3].set(
        p_lo).astype(BF16)
    o_a = _diff_attention(qkv_a, slopes, lam, kaug, diff_norm_g)
    o_b = _gla(qkv_g, glr, w_up, b_alpha, gla_norm_g)

    merge_args = (o_a.reshape(N_TOK, -1), o_b.reshape(N_TOK, -1), gates.reshape(N_TOK, W_GATE),
                  x.reshape(N_TOK, D_MODEL), mod4, norm2_g,
                  w_branch_diff[0].astype(BF16), w_branch_gla[0].astype(BF16), w_out[0].astype(BF16),
                  w_router[0].T, b_router[0].reshape(N_EXPERTS, 1))
    fin_g = final_norm_g.reshape(1, D_MODEL)

    routed = []
    for grp in range(MOE_GROUPS):
        x1, h2p, eidx, rank, wts, cnt = _merge_route(grp, *merge_args)
        dest8, be, nv, rows = _route_plan(cnt, eidx, rank)
        dest = dest8[:TOP_K].reshape(-1, SC_CHUNK)
        routed.append((x1, wts, dest, be, nv, rows, _dispatch_rows(h2p, dest)))
    gathered = []
    for x1, wts, dest, be, nv, rows, xb in routed:
        yb = _expert_ffn(be[0, :N_BLK], nv[0, :1], rows[0, :N_BLK], xb,
                         w_gate_up[0], b_gate_up[0], w_down[0], b_down[0])
        gathered.append(_combine_rows(yb, dest))
    out = None
    for grp, ((x1, wts, *_), yg) in enumerate(zip(routed, gathered)):
        out = _final(grp, x1, yg, wts, mod4, fin_g, out)
    return out.reshape(BATCH, SEQ, D_MODEL)
```

```python
import functools
import math

import jax
import jax.numpy as jnp
import numpy as np
from jax import lax
from jax.experimental import pallas as pl
from jax.experimental.pallas import tpu as pltpu
from jax.experimental.pallas import tpu_sc as plsc

F32 = jnp.float32
BF16 = jnp.bfloat16
U32 = jnp.uint32
I32 = jnp.int32

D_MODEL = 1024
BATCH = 16
SEQ = 2048
N_TOK = BATCH * SEQ
CHUNK = 64
DIFF_HEADS = 4
DIFF_DH = 64
GLA_HEADS = 4
GLA_DK = 64
GLA_DV = 128
GLA_RANK = 16
GLA_GATE_NORM = 16.0
N_EXPERTS = 32
TOP_K = 4
D_FF = D_MODEL
SWIGLU_LIMIT = 7.0
SWIGLU_ALPHA = 1.702
N_MOD = 6
EPS = 1e-6
LAMBDA_INIT = 0.8 - 0.6 * math.exp(-0.3 * 0)

LANES = 128
HALF = D_MODEL // 2
ROW_W = HALF
ROW_DT = U32

TM_IN = 1024
TQ = 512
ATTN_HEADS = 2
ATTN_STRIPS = 4
ATTN_W = 2 * TQ // ATTN_STRIPS
VT_ROWS = 2 * DIFF_DH + 16
LOG2E = math.log2(math.e)
TM_MERGE = 1024
BLK = 1024
BLK_STEP = 256
MOE_GROUPS = 1
N_GRP = N_TOK // MOE_GROUPS
N_BLK = (N_GRP * TOP_K) // BLK + N_EXPERTS
P_ROWS = N_BLK * BLK
TM_FIN = 1024
VMEM_LIMIT = 56 * 1024 * 1024
CHUNK_SHIFT = CHUNK.bit_length() - 1
POS_LO_BITS = 3
POS_LO = 1 << POS_LO_BITS


def _nt_dot(a, b):
    return lax.dot_general(a, b, (((1,), (1,)), ((), ())), preferred_element_type=F32)


def _tn_dot(a, b):
    return lax.dot_general(a, b, (((0,), (0,)), ((), ())), preferred_element_type=F32)


def _split3(x):
    hi = x.astype(BF16)
    r1 = x - hi.astype(F32)
    mid = r1.astype(BF16)
    lo = (r1 - mid.astype(F32)).astype(BF16)
    return hi, mid, lo


def _pack_halves(y):
    return pltpu.pack_elementwise([y[:, :HALF], y[:, HALF:]], packed_dtype=BF16)


def _unpack_halves(u):
    lo = pltpu.unpack_elementwise(u, index=0, packed_dtype=BF16, unpacked_dtype=F32)
    hi = pltpu.unpack_elementwise(u, index=1, packed_dtype=BF16, unpacked_dtype=F32)
    return lo, hi


def _mod_kernel(c_ref, w_ref, b_ref, o_ref):
    c = c_ref[...]
    s = c * jax.nn.sigmoid(c)
    o_ref[0] = jnp.dot(s.astype(BF16), w_ref[...].astype(BF16),
                       preferred_element_type=F32) + b_ref[...]


def _modulation(c, w_ada, b_ada):
    return pl.pallas_call(
        _mod_kernel,
        grid=(N_MOD,),
        in_specs=[
            pl.BlockSpec((BATCH, D_MODEL), lambda j: (0, 0)),
            pl.BlockSpec((D_MODEL, D_MODEL), lambda j: (0, j)),
            pl.BlockSpec((1, D_MODEL), lambda j: (0, j)),
        ],
        out_specs=pl.BlockSpec((1, BATCH, D_MODEL), lambda j: (j, 0, 0)),
        out_shape=jax.ShapeDtypeStruct((N_MOD, BATCH, D_MODEL), F32),
        compiler_params=pltpu.CompilerParams(dimension_semantics=("arbitrary",)),
        name="adaln_mod",
    )(c, w_ada, b_ada.reshape(1, N_MOD * D_MODEL))


W_A = 3 * DIFF_HEADS * 2 * DIFF_DH
W_G = 2 * GLA_HEADS * GLA_DK + 2 * GLA_HEADS * GLA_DV
W_GATE = 2 * D_MODEL


def _in_kernel(x_ref, sh_ref, sc_ref, g_ref, wa_ref, wg_ref, wgate_ref, wlr_ref,
               oa_ref, og_ref, ogate_ref, olr_ref):
    x = x_ref[...]
    inv = lax.rsqrt(jnp.mean(x * x, axis=-1, keepdims=True) + EPS)
    h = (x * inv * g_ref[...]) * (1.0 + sc_ref[...]) + sh_ref[...]
    hb = h.astype(BF16)
    oa_ref[...] = jnp.dot(hb, wa_ref[...], preferred_element_type=F32).astype(BF16)
    og_ref[...] = jnp.dot(hb, wg_ref[...], preferred_element_type=F32).astype(BF16)
    ogate_ref[...] = jnp.dot(hb, wgate_ref[...], preferred_element_type=F32).astype(BF16)
    olr_ref[...] = jnp.dot(hb, wlr_ref[...], preferred_element_type=F32)


def _in_proj(x, mod4, norm1_g, w_a, w_g, w_gate, w_lr):
    nrow = SEQ // TM_IN
    full = lambda shape: pl.BlockSpec(shape, lambda b, i: (0,) * len(shape), pipeline_mode=pl.Buffered(1))
    return pl.pallas_call(
        _in_kernel,
        grid=(BATCH, nrow),
        in_specs=[
            pl.BlockSpec((None, TM_IN, D_MODEL), lambda b, i: (b, i, 0)),
            pl.BlockSpec((None, None, 1, D_MODEL), lambda b, i: (0, b, 0, 0)),
            pl.BlockSpec((None, None, 1, D_MODEL), lambda b, i: (1, b, 0, 0)),
            full((1, D_MODEL)),
            full((D_MODEL, W_A)),
            full((D_MODEL, W_G)),
            full((D_MODEL, W_GATE)),
            full((D_MODEL, LANES)),
        ],
        out_specs=[
            pl.BlockSpec((None, TM_IN, W_A), lambda b, i: (b, i, 0)),
            pl.BlockSpec((None, TM_IN, W_G), lambda b, i: (b, i, 0)),
            pl.BlockSpec((None, TM_IN, W_GATE), lambda b, i: (b, i, 0)),
            pl.BlockSpec((None, TM_IN, LANES), lambda b, i: (b, i, 0)),
        ],
        out_shape=[
            jax.ShapeDtypeStruct((BATCH, SEQ, W_A), BF16),
            jax.ShapeDtypeStruct((BATCH, SEQ, W_G), BF16),
            jax.ShapeDtypeStruct((BATCH, SEQ, W_GATE), BF16),
            jax.ShapeDtypeStruct((BATCH, SEQ, LANES), F32),
        ],
        compiler_params=pltpu.CompilerParams(
            dimension_semantics=("arbitrary", "arbitrary"), vmem_limit_bytes=VMEM_LIMIT),
        name="in_proj",
    )(x, mod4, mod4, norm1_g, w_a, w_g, w_gate, w_lr)


def _attn_kernel(slope_ref, lam_ref, q_ref, k_ref, v_ref, kaug_ref, g_ref, o_ref,
                 sa_ref, sb_ref, lha_ref, lhb_ref, corr_ref, vt_ref, m_ref, acc_ref):
    lam = lam_ref[0]
    lane = lax.broadcasted_iota(I32, (TQ, 2 * DIFF_DH), 1)
    zero = jnp.zeros((TQ, 2 * DIFF_DH), BF16)
    heads = range(ATTN_HEADS)
    c_alibi, aug = [], []
    for hh in heads:
        c_alibi.append(slope_ref[pl.program_id(1) * ATTN_HEADS + hh] * LOG2E)
        c_vec = jnp.full((TQ, 2 * DIFF_DH), c_alibi[hh], F32)
        c_hi = c_vec.astype(BF16).astype(F32)
        c_lo = (c_vec - c_hi).astype(BF16).astype(F32)
        aug.append(jnp.where(lane == 0, POS_LO * c_hi, jnp.where(lane == 1, POS_LO * c_lo,
                             jnp.where(lane == 2, c_hi, jnp.where(lane == 3, c_lo, 0.0)))).astype(BF16))

    def blk(j):
        return slice(j * TQ, (j + 1) * TQ)

    def cols(hh):
        return slice(hh * 2 * DIFF_DH, (hh + 1) * 2 * DIFF_DH)

    def stacked_queries(hh, qi):
        q = (q_ref[blk(qi), cols(hh)].astype(F32) * (DIFF_DH ** -0.5 * LOG2E)).astype(BF16)
        return jnp.concatenate([
            jnp.concatenate([jnp.where(lane < DIFF_DH, q, zero), aug[hh]], axis=1),
            jnp.concatenate([jnp.where(lane >= DIFF_DH, q, zero), aug[hh]], axis=1)], axis=0)

    lhs_refs = (lha_ref, lhb_ref)

    def strip(c):
        return slice(c * ATTN_W, (c + 1) * ATTN_W)

    def keys(hh, j):
        return jnp.concatenate([k_ref[blk(j), cols(hh)], kaug_ref[blk(j), :]], axis=1)

    def scores(hh, kk, qi, c):
        return _nt_dot(kk, lhs_refs[qi % 2][hh, strip(c), :])

    ones_rows = jnp.where(lax.broadcasted_iota(I32, (VT_ROWS - 2 * DIFF_DH, TQ), 0) == 0, 1.0, 0.0).astype(BF16)
    kr = lax.broadcasted_iota(I32, (TQ, TQ), 0)
    qc = lax.broadcasted_iota(I32, (TQ, TQ), 1)
    ahead = jnp.maximum(kr - qc, 0).astype(F32)
    for hh in heads:
        for j in range(SEQ // TQ):
            vt_ref[hh, j, :2 * DIFF_DH, :] = v_ref[blk(j), cols(hh)].astype(F32).T.astype(BF16)
            vt_ref[hh, j, 2 * DIFF_DH:, :] = ones_rows
        corr_ref[hh] = jnp.where((qc >> CHUNK_SHIFT) >= (kr >> CHUNK_SHIFT), (-2.0 * c_alibi[hh]) * ahead, -jnp.inf)

    def update(hh, s, j, c, first):
        nk = s.shape[0]
        if first:
            m_new = jnp.max(s, axis=0, keepdims=True)
            p = jnp.exp2((s - m_new).astype(BF16))
            acc_ref[hh, :, strip(c)] = jnp.dot(vt_ref[hh, j, :, :nk], p, preferred_element_type=F32)
        else:
            m = m_ref[hh, :, strip(c)]
            m_new = jnp.maximum(m, jnp.max(s, axis=0, keepdims=True))
            alpha = jnp.exp2(m - m_new)
            p = jnp.exp2((s - m_new).astype(BF16))
            acc_ref[hh, :, strip(c)] = (alpha * acc_ref[hh, :, strip(c)]
                                        + jnp.dot(vt_ref[hh, j, :, :nk], p, preferred_element_type=F32))
        m_ref[hh, :, strip(c)] = m_new

    def n_keys(qi, j, c):
        return TQ if j < qi else min(TQ, (c * ATTN_W) % TQ + ATTN_W)

    pairs = [(qi, j) for qi in range(SEQ // TQ) for j in range(qi + 1)]
    bufs = (sa_ref, sb_ref)
    kk = [None] * ATTN_HEADS
    for hh in heads:
        lhs_refs[0][hh] = stacked_queries(hh, 0)
        kk[hh] = keys(hh, 0)
    for c in range(ATTN_STRIPS):
        for hh in heads:
            nk = n_keys(0, 0, c)
            bufs[0][hh, :nk, strip(c)] = scores(hh, kk[hh][:nk], 0, c)
    for t, (qi, j) in enumerate(pairs):
        nxt = pairs[t + 1] if t + 1 < len(pairs) else None
        if nxt is not None:
            for hh in heads:
                if nxt[0] != qi:
                    lhs_refs[nxt[0] % 2][hh] = stacked_queries(hh, nxt[0])
                kk[hh] = keys(hh, nxt[1])
        for c in range(ATTN_STRIPS):
            for hh in heads:
                if nxt is not None:
                    nk = n_keys(*nxt, c)
                    bufs[(t + 1) % 2][hh, :nk, strip(c)] = scores(hh, kk[hh][:nk], nxt[0], c)
                nk = n_keys(qi, j, c)
                s = bufs[t % 2][hh, :nk, strip(c)]
                if j == qi:
                    lo = (c * ATTN_W) % TQ
                    s = s + corr_ref[hh, :nk, lo:lo + ATTN_W]
                update(hh, s, j, c, first=(j == 0))
        if j == qi:
            for hh in heads:
                ot = acc_ref[hh, :2 * DIFF_DH, :] / acc_ref[hh, 2 * DIFF_DH:2 * DIFF_DH + 1, :]
                o = (ot[:, :TQ] - lam * ot[:, TQ:]).T
                inv = lax.rsqrt(jnp.mean(o * o, axis=-1, keepdims=True) + EPS)
                o_ref[blk(qi), cols(hh)] = (o * inv * g_ref[...] * (1.0 - LAMBDA_INIT)).astype(BF16)


def _diff_attention(qkv_a, slopes, lam, kaug, diff_norm_g):
    groups = DIFF_HEADS // ATTN_HEADS
    width = ATTN_HEADS * 2 * DIFF_DH
    per_head = lambda shape, dt: pltpu.VMEM((ATTN_HEADS,) + shape, dt)
    return pl.pallas_call(
        _attn_kernel,
        grid=(BATCH, groups),
        in_specs=[
            pl.BlockSpec(memory_space=pltpu.SMEM),
            pl.BlockSpec(memory_space=pltpu.SMEM),
            pl.BlockSpec((None, SEQ, width), lambda b, h: (b, 0, h)),
            pl.BlockSpec((None, SEQ, width), lambda b, h: (b, 0, groups + h)),
            pl.BlockSpec((None, SEQ, width), lambda b, h: (b, 0, 2 * groups + h)),
            pl.BlockSpec((SEQ, LANES), lambda b, h: (0, 0)),
            pl.BlockSpec((1, LANES), lambda b, h: (0, 0)),
        ],
        out_specs=pl.BlockSpec((None, SEQ, width), lambda b, h: (b, 0, h)),
        out_shape=jax.ShapeDtypeStruct((BATCH, SEQ, DIFF_HEADS * 2 * DIFF_DH), BF16),
        scratch_shapes=[
            per_head((TQ, 2 * TQ), F32), per_head((TQ, 2 * TQ), F32),
            per_head((2 * TQ, 4 * DIFF_DH), BF16), per_head((2 * TQ, 4 * DIFF_DH), BF16),
            per_head((TQ, TQ), F32),
            per_head((SEQ // TQ, VT_ROWS, TQ), BF16),
            per_head((1, 2 * TQ), F32),
            per_head((VT_ROWS, 2 * TQ), F32),
        ],
        compiler_params=pltpu.CompilerParams(
            dimension_semantics=("arbitrary", "arbitrary"), vmem_limit_bytes=VMEM_LIMIT),
        name="diff_attn",
    )(slopes, lam, qkv_a, qkv_a, qkv_a, kaug, diff_norm_g)


N_CHUNK = SEQ // CHUNK
PAIR = 2 * GLA_DK
PAIR_V = 2 * GLA_DV
CS_ROWS = 256
GLA_UNROLL = 32


def _gla_kernel(q_ref, k_ref, v_ref, r_ref, lr_ref, wup_ref, bup_ref, g_ref, o_ref,
                gcum_ref, state_ref):
    w_hi, w_mid, _ = _split3(wup_ref[...])
    rr = lax.broadcasted_iota(I32, (CS_ROWS, CS_ROWS), 0)
    cc = lax.broadcasted_iota(I32, (CS_ROWS, CS_ROWS), 1)
    tri = jnp.where(((rr >> CHUNK_SHIFT) == (cc >> CHUNK_SHIFT)) & (cc <= rr), 1.0, 0.0).astype(BF16)
    for blk in range(SEQ // CS_ROWS):
        rows = pl.ds(blk * CS_ROWS, CS_ROWS)
        a_hi, a_mid, _ = _split3(lr_ref[rows, :])
        z = (jnp.dot(a_hi, w_hi, preferred_element_type=F32)
             + jnp.dot(a_hi, w_mid, preferred_element_type=F32)
             + jnp.dot(a_mid, w_hi, preferred_element_type=F32)) + bup_ref[...]
        la = (jnp.minimum(z, 0.0) - jnp.log(1.0 + jnp.exp(-jnp.abs(z)))) * (1.0 / GLA_GATE_NORM)
        l_hi, l_mid, l_lo = _split3(la)
        gcum_ref[rows, :] = (jnp.dot(tri, l_hi, preferred_element_type=F32)
                             + jnp.dot(tri, l_mid, preferred_element_type=F32)
                             + jnp.dot(tri, l_lo, preferred_element_type=F32))

    state_ref[...] = jnp.zeros_like(state_ref)
    lane_k = lax.broadcasted_iota(I32, (1, PAIR), 1)
    row_v = lax.broadcasted_iota(I32, (PAIR_V, PAIR), 0)
    col_k = lax.broadcasted_iota(I32, (PAIR_V, PAIR), 1)
    same_head = (row_v >= GLA_DV) == (col_k >= GLA_DK)
    cr = lax.broadcasted_iota(I32, (CHUNK, CHUNK), 0)
    cs = lax.broadcasted_iota(I32, (CHUNK, CHUNK), 1)
    causal = cs <= cr
    scale = GLA_DK ** -0.5

    def chunk(n):
        rows = pl.ds(pl.multiple_of(n * CHUNK, CHUNK), CHUNK)
        gc = gcum_ref[rows, :]
        g_last = gcum_ref[pl.ds(n * CHUNK + CHUNK - 1, 1), :]
        qf = q_ref[rows, :].astype(F32) * scale
        kf = k_ref[rows, :].astype(F32)
        q_s = (qf * jnp.exp(gc)).astype(BF16)
        k_s = (kf * jnp.exp(-gc)).astype(BF16)
        k_d = (kf * jnp.exp(g_last - gc)).astype(BF16)
        decay = jnp.exp(g_last)
        for pr in range(GLA_HEADS // 2):
            kl = slice(pr * PAIR, (pr + 1) * PAIR)
            vl = slice(pr * PAIR_V, (pr + 1) * PAIR_V)
            qs_p, ks_p, kd_p = q_s[:, kl], k_s[:, kl], k_d[:, kl]
            v_p = v_ref[rows, vl]
            st = state_ref[pr]
            o_inter = _nt_dot(qs_p, st.astype(BF16))
            d_st = _tn_dot(v_p, kd_p)
            state_ref[pr] = st * decay[:, kl] + jnp.where(same_head, d_st, 0.0)
            for sub in range(2):
                hd = 2 * pr + sub
                in_head = (lane_k >= sub * GLA_DK) & (lane_k < (sub + 1) * GLA_DK)
                a = _nt_dot(jnp.where(in_head, qs_p, jnp.zeros_like(qs_p)), ks_p)
                a = jnp.where(causal, a, 0.0).astype(BF16)
                vs = slice(hd * GLA_DV, (hd + 1) * GLA_DV)
                o = (jnp.dot(a, v_ref[rows, vs], preferred_element_type=F32)
                     + o_inter[:, sub * GLA_DV:(sub + 1) * GLA_DV])
                inv = lax.rsqrt(jnp.mean(o * o, axis=-1, keepdims=True) + EPS)
                r = r_ref[rows, vs].astype(F32)
                o_ref[rows, vs] = (o * inv * g_ref[...] * (r * jax.nn.sigmoid(r))).astype(BF16)

    def chunk_group(t, _):
        for u in range(GLA_UNROLL):
            chunk(t * GLA_UNROLL + u)
        return 0

    lax.fori_loop(0, N_CHUNK // GLA_UNROLL, chunk_group, 0)


def _gla(qkv_g, glr, w_up, b_up, gla_norm_g):
    qk_w = GLA_HEADS * GLA_DK
    v_w = GLA_HEADS * GLA_DV
    return pl.pallas_call(
        _gla_kernel,
        grid=(BATCH,),
        in_specs=[
            pl.BlockSpec((None, SEQ, qk_w), lambda b: (b, 0, 0)),
            pl.BlockSpec((None, SEQ, qk_w), lambda b: (b, 0, 1)),
            pl.BlockSpec((None, SEQ, v_w), lambda b: (b, 0, 1)),
            pl.BlockSpec((None, SEQ, v_w), lambda b: (b, 0, 2)),
            pl.BlockSpec((None, SEQ, LANES), lambda b: (b, 0, 0)),
            pl.BlockSpec((LANES, qk_w), lambda b: (0, 0)),
            pl.BlockSpec((1, qk_w), lambda b: (0, 0)),
            pl.BlockSpec((1, GLA_DV), lambda b: (0, 0)),
        ],
        out_specs=pl.BlockSpec((None, SEQ, v_w), lambda b: (b, 0, 0)),
        out_shape=jax.ShapeDtypeStruct((BATCH, SEQ, v_w), BF16),
        scratch_shapes=[
            pltpu.VMEM((SEQ, qk_w), F32),
            pltpu.VMEM((GLA_HEADS // 2, PAIR_V, PAIR), F32),
        ],
        compiler_params=pltpu.CompilerParams(
            dimension_semantics=("arbitrary",), vmem_limit_bytes=VMEM_LIMIT),
        name="gla",
    )(qkv_g, qkv_g, qkv_g, qkv_g, glr, w_up, b_up, gla_norm_g)


def _merge_kernel(oa_ref, ob_ref, gate_ref, x_ref, gt1_ref, sh2_ref, sc2_ref, g2_ref,
                  wpa_ref, wpb_ref, wo_ref, wr_ref, br_ref,
                  x1_ref, h2p_ref, eidx_ref, rank_ref, wts_ref, cnt_ref,
                  upper_ref, carry_ref):
    i = pl.program_id(0)

    @pl.when(i == 0)
    def _():
        rr = lax.broadcasted_iota(I32, (TM_MERGE, TM_MERGE), 0)
        cc = lax.broadcasted_iota(I32, (TM_MERGE, TM_MERGE), 1)
        upper_ref[...] = jnp.where(rr < cc, 1.0, 0.0).astype(BF16)
        carry_ref[...] = jnp.zeros_like(carry_ref)

    ga = gate_ref[:, :D_MODEL].astype(F32)
    gb = gate_ref[:, D_MODEL:].astype(F32)
    merged = (jax.nn.sigmoid(ga) * jnp.dot(oa_ref[...], wpa_ref[...], preferred_element_type=F32)
              + jax.nn.sigmoid(gb) * jnp.dot(ob_ref[...], wpb_ref[...], preferred_element_type=F32))
    y = jnp.dot(merged.astype(BF16), wo_ref[...], preferred_element_type=F32)
    x1 = x_ref[...] + gt1_ref[...] * y
    x1_ref[...] = x1
    inv = lax.rsqrt(jnp.mean(x1 * x1, axis=-1, keepdims=True) + EPS)
    h2 = (x1 * inv * g2_ref[...]) * (1.0 + sc2_ref[...]) + sh2_ref[...]
    h2p_ref[...] = _pack_halves(h2)

    h_hi, h_mid, _ = _split3(h2)
    w_hi, w_mid, _ = _split3(wr_ref[...])
    logits = (_nt_dot(w_hi, h_hi) + _nt_dot(w_hi, h_mid) + _nt_dot(w_mid, h_hi)) + br_ref[...]

    eio = lax.broadcasted_iota(I32, (N_EXPERTS, TM_MERGE), 0)
    vals, idxs, sels = [], [], []
    cur = logits
    for _k in range(TOP_K):
        m = jnp.max(cur, axis=0, keepdims=True)
        idx = jnp.min(jnp.where(cur == m, eio, N_EXPERTS), axis=0, keepdims=True)
        sel = eio == idx
        vals.append(m)
        idxs.append(idx)
        sels.append(sel)
        cur = jnp.where(sel, -jnp.inf, cur)
    es = [jnp.exp(v - vals[0]) for v in vals]
    tot = es[0] + es[1] + es[2] + es[3]
    onehot = jnp.zeros((N_EXPERTS, TM_MERGE), F32)
    for sel in sels:
        onehot = onehot + jnp.where(sel, 1.0, 0.0)
    before = jnp.dot(onehot.astype(BF16), upper_ref[...], preferred_element_type=F32) + carry_ref[:, 0:1]
    ranks = [jnp.sum(jnp.where(sel, before, 0.0), axis=0, keepdims=True) for sel in sels]
    carry_ref[...] = carry_ref[...] + jnp.sum(onehot, axis=1, keepdims=True)
    cnt_ref[...] = carry_ref[...]

    zi = jnp.zeros((8 - TOP_K, TM_MERGE), I32)
    zf = jnp.zeros((8 - TOP_K, TM_MERGE), F32)
    eidx_ref[...] = jnp.concatenate(idxs + [zi], axis=0)
    rank_ref[...] = jnp.concatenate([r.astype(I32) for r in ranks] + [zi], axis=0)
    wts_ref[...] = jnp.concatenate([e / tot for e in es] + [zf], axis=0)


def _merge_route(grp, o_a, o_b, gates, x2d, mod4, norm2_g, w_pa, w_pb, w_o, w_rt, b_r):
    ntile = N_GRP // TM_MERGE
    first = grp * ntile
    per_b = SEQ // TM_MERGE
    full = lambda shape: pl.BlockSpec(shape, lambda i: (0,) * len(shape), pipeline_mode=pl.Buffered(1))
    row_in = lambda w: pl.BlockSpec((TM_MERGE, w), lambda i: (first + i, 0))
    row = lambda w: pl.BlockSpec((TM_MERGE, w), lambda i: (i, 0))
    modspec = lambda j: pl.BlockSpec((None, None, 1, D_MODEL), lambda i: (j, (first + i) // per_b, 0, 0))
    col = pl.BlockSpec((8, TM_MERGE), lambda i: (0, i))
    return pl.pallas_call(
        _merge_kernel,
        grid=(ntile,),
        in_specs=[
            row_in(DIFF_HEADS * 2 * DIFF_DH), row_in(GLA_HEADS * GLA_DV), row_in(W_GATE), row_in(D_MODEL),
            modspec(2), modspec(3), modspec(4),
            full((1, D_MODEL)),
            full((DIFF_HEADS * 2 * DIFF_DH, D_MODEL)), full((GLA_HEADS * GLA_DV, D_MODEL)),
            full((D_MODEL, D_MODEL)),
            full((N_EXPERTS, D_MODEL)), full((N_EXPERTS, 1)),
        ],
        out_specs=[
            row(D_MODEL), row(ROW_W), col, col, col,
            pl.BlockSpec((N_EXPERTS, LANES), lambda i: (0, 0)),
        ],
        out_shape=[
            jax.ShapeDtypeStruct((N_GRP, D_MODEL), F32),
            jax.ShapeDtypeStruct((N_GRP, ROW_W), ROW_DT),
            jax.ShapeDtypeStruct((8, N_GRP), I32),
            jax.ShapeDtypeStruct((8, N_GRP), I32),
            jax.ShapeDtypeStruct((8, N_GRP), F32),
            jax.ShapeDtypeStruct((N_EXPERTS, LANES), F32),
        ],
        scratch_shapes=[
            pltpu.VMEM((TM_MERGE, TM_MERGE), BF16),
            pltpu.VMEM((N_EXPERTS, LANES), F32),
        ],
        compiler_params=pltpu.CompilerParams(
            dimension_semantics=("arbitrary",), vmem_limit_bytes=VMEM_LIMIT),
        name="merge_route",
    )(o_a, o_b, gates, x2d, mod4, mod4, mod4, norm2_g, w_pa, w_pb, w_o, w_rt, b_r)


TP = 4096
NB_PAD = ((N_BLK + LANES - 1) // LANES) * LANES


def _plan_kernel(cnt_ref, eidx_ref, rank_ref, dest_ref, be_ref, nv_ref, rows_ref):
    cnt = cnt_ref[...]
    padded = jnp.floor((cnt + (BLK - 1.0)) * (1.0 / BLK)) * BLK
    er = lax.broadcasted_iota(I32, (N_EXPERTS, N_EXPERTS), 0)
    ec = lax.broadcasted_iota(I32, (N_EXPERTS, N_EXPERTS), 1)
    lower = jnp.where(ec < er, 1.0, 0.0).astype(BF16)
    p_hi, p_mid, p_lo = _split3(padded)
    starts = (jnp.dot(lower, p_hi, preferred_element_type=F32)
              + jnp.dot(lower, p_mid, preferred_element_type=F32)
              + jnp.dot(lower, p_lo, preferred_element_type=F32))
    ends = starts + padded
    blk_start = (lax.broadcasted_iota(I32, (1, NB_PAD), 1) * BLK).astype(F32)
    n_before = jnp.sum(jnp.where(ends[:, 0:1] <= blk_start, 1.0, 0.0), axis=0, keepdims=True)
    be_ref[...] = jnp.minimum(n_before, N_EXPERTS - 1.0).astype(I32)
    nv_ref[...] = (jnp.max(ends, axis=0, keepdims=True) * (1.0 / BLK)).astype(I32)
    owner = (starts[:, 0:1] <= blk_start) & (blk_start < ends[:, 0:1])
    filled = jnp.sum(jnp.where(owner, (starts + cnt)[:, 0:1] - blk_start, 0.0), axis=0, keepdims=True)
    rows_ref[...] = jnp.clip(filled, 0.0, float(BLK)).astype(I32)

    eio = lax.broadcasted_iota(I32, (N_EXPERTS, TP), 0)
    rows = []
    for k in range(TOP_K):
        onehot = eio == eidx_ref[k:k + 1, :]
        base = jnp.sum(jnp.where(onehot, starts[:, 0:1], 0.0), axis=0, keepdims=True)
        rows.append(base.astype(I32) + rank_ref[k:k + 1, :])
    dest_ref[...] = jnp.concatenate(rows + [jnp.zeros((8 - TOP_K, TP), I32)], axis=0)


def _route_plan(cnt, eidx, rank):
    return pl.pallas_call(
        _plan_kernel,
        grid=(N_GRP // TP,),
        in_specs=[
            pl.BlockSpec((N_EXPERTS, LANES), lambda i: (0, 0)),
            pl.BlockSpec((8, TP), lambda i: (0, i)),
            pl.BlockSpec((8, TP), lambda i: (0, i)),
        ],
        out_specs=[
            pl.BlockSpec((8, TP), lambda i: (0, i)),
            pl.BlockSpec((1, NB_PAD), lambda i: (0, 0)),
            pl.BlockSpec((1, LANES), lambda i: (0, 0)),
            pl.BlockSpec((1, NB_PAD), lambda i: (0, 0)),
        ],
        out_shape=[
            jax.ShapeDtypeStruct((8, N_GRP), I32),
            jax.ShapeDtypeStruct((1, NB_PAD), I32),
            jax.ShapeDtypeStruct((1, LANES), I32),
            jax.ShapeDtypeStruct((1, NB_PAD), I32),
        ],
        compiler_params=pltpu.CompilerParams(dimension_semantics=("arbitrary",)),
        name="route_plan",
    )(cnt, eidx, rank)


def _ffn_kernel(be_ref, nv_ref, first_ref, slot_ref, nxt_ref, rows_ref, x_ref, wgu_hbm, bgu_ref, wd_hbm, bd_ref,
                y_ref, wgu_buf, wd_buf, sem):
    i = pl.program_id(0)
    valid = i < nv_ref[0]
    filled = jnp.maximum(rows_ref[i], 1)
    s = slot_ref[i]

    def weight_copies(e, slot):
        return (pltpu.make_async_copy(wgu_hbm.at[e], wgu_buf.at[slot], sem.at[slot, 0]),
                pltpu.make_async_copy(wd_hbm.at[e], wd_buf.at[slot], sem.at[slot, 1]))

    @pl.when(i == 0)
    def _():
        for cp in weight_copies(be_ref[0], 0):
            cp.start()

    @pl.when(valid & (first_ref[i] == 1))
    def _():
        for cp in weight_copies(be_ref[i], s):
            cp.wait()

        @pl.when(nxt_ref[i] >= 0)
        def _():
            for cp in weight_copies(nxt_ref[i], 1 - s):
                cp.start()

    def expert_rows(n):
        lo, hi = _unpack_halves(x_ref[:n, :])
        x = jnp.concatenate([lo, hi], axis=1).astype(BF16)
        mm = lambda a, w: lax.dot_general(a, w, (((1,), (0,)), ((), ())), preferred_element_type=F32)
        gu = mm(x, wgu_buf[s]) + bgu_ref[...]
        gate = jnp.minimum(gu[:, :D_FF], SWIGLU_LIMIT)
        up = jnp.clip(gu[:, D_FF:], -SWIGLU_LIMIT, SWIGLU_LIMIT)
        act = (up + 1.0) * (gate * jax.nn.sigmoid(SWIGLU_ALPHA * gate))
        y = mm(act.astype(BF16), wd_buf[s]) + bd_ref[...]
        y_ref[:n, :] = _pack_halves(y)

    for n in range(BLK_STEP, BLK + 1, BLK_STEP):
        @pl.when(valid & (filled > n - BLK_STEP) & (filled <= n))
        def _(n=n):
            expert_rows(n)
            if n < BLK:
                y_ref[n:, :] = jnp.zeros((BLK - n, ROW_W), ROW_DT)

    @pl.when(jnp.logical_not(valid))
    def _():
        y_ref[...] = jnp.zeros_like(y_ref)


def _expert_ffn(blk_expert, n_valid, blk_rows, xb, w_gate_up, b_gate_up, w_down, b_down):
    idx = jnp.arange(N_BLK, dtype=I32)
    used = idx < n_valid[0]
    first = (used & ((idx == 0) | (blk_expert != jnp.roll(blk_expert, 1)))).astype(I32)
    slot = (jnp.cumsum(first) - 1) & 1
    later = used[None, :] & (idx[None, :] > idx[:, None]) & (blk_expert[None, :] != blk_expert[:, None])
    nxt_pos = jnp.min(jnp.where(later, idx[None, :], N_BLK), axis=1)
    nxt = jnp.where(nxt_pos < N_BLK, blk_expert[jnp.minimum(nxt_pos, N_BLK - 1)], -1).astype(I32)

    blockwise = lambda i, *_: (i, 0)
    per_expert = lambda i, be, *_: (be[i], 0, 0)
    grid_spec = pltpu.PrefetchScalarGridSpec(
        num_scalar_prefetch=6,
        grid=(N_BLK,),
        in_specs=[
            pl.BlockSpec((BLK, ROW_W), blockwise),
            pl.BlockSpec(memory_space=pl.ANY),
            pl.BlockSpec((None, 1, 2 * D_FF), per_expert),
            pl.BlockSpec(memory_space=pl.ANY),
            pl.BlockSpec((None, 1, D_MODEL), per_expert),
        ],
        out_specs=pl.BlockSpec((BLK, ROW_W), blockwise),
        scratch_shapes=[
            pltpu.VMEM((2, D_MODEL, 2 * D_FF), F32),
            pltpu.VMEM((2, D_FF, D_MODEL), F32),
            pltpu.SemaphoreType.DMA((2, 2)),
        ],
    )
    return pl.pallas_call(
        _ffn_kernel,
        grid_spec=grid_spec,
        out_shape=jax.ShapeDtypeStruct((P_ROWS, ROW_W), ROW_DT),
        compiler_params=pltpu.CompilerParams(
            dimension_semantics=("arbitrary",), vmem_limit_bytes=VMEM_LIMIT),
        name="expert_ffn",
    )(blk_expert, n_valid, first, slot.astype(I32), nxt, blk_rows, xb, w_gate_up,
      b_gate_up.reshape(N_EXPERTS, 1, 2 * D_FF), w_down, b_down.reshape(N_EXPERTS, 1, D_MODEL))


def _final_kernel(x1_ref, y0_ref, y1_ref, y2_ref, y3_ref, w_ref, gt2_ref, g_ref, *rest):
    o_ref = rest[-1]
    w = w_ref[...].T
    ylo = jnp.zeros((TM_FIN, HALF), F32)
    yhi = jnp.zeros((TM_FIN, HALF), F32)
    for k, y_ref in enumerate((y0_ref, y1_ref, y2_ref, y3_ref)):
        lo, hi = _unpack_halves(y_ref[...])
        wk = w[:, k:k + 1]
        ylo = ylo + wk * lo
        yhi = yhi + wk * hi
    y = jnp.concatenate([ylo, yhi], axis=1)
    x2 = x1_ref[...] + gt2_ref[...] * y
    inv = lax.rsqrt(jnp.mean(x2 * x2, axis=-1, keepdims=True) + EPS)
    o_ref[...] = x2 * inv * g_ref[...]


def _final(grp, x1, yg, w4, mod4, final_norm_g, out_so_far):
    per_b = SEQ // TM_FIN
    ntile = N_GRP // TM_FIN
    first = grp * ntile
    slot = lambda k: pl.BlockSpec((TM_FIN, ROW_W), lambda i: (k * ntile + i, 0))
    in_specs = [
        pl.BlockSpec((TM_FIN, D_MODEL), lambda i: (i, 0)),
        slot(0), slot(1), slot(2), slot(3),
        pl.BlockSpec((8, TM_FIN), lambda i: (0, i)),
        pl.BlockSpec((None, None, 1, D_MODEL), lambda i: (5, (first + i) // per_b, 0, 0)),
        pl.BlockSpec((1, D_MODEL), lambda i: (0, 0)),
    ]
    args = [x1, yg, yg, yg, yg, w4, mod4, final_norm_g]
    aliases = {}
    if out_so_far is not None:
        in_specs.append(pl.BlockSpec(memory_space=pl.ANY))
        args.append(out_so_far)
        aliases = {len(args) - 1: 0}
    return pl.pallas_call(
        _final_kernel,
        grid=(ntile,),
        in_specs=in_specs,
        out_specs=pl.BlockSpec((TM_FIN, D_MODEL), lambda i: (first + i, 0)),
        out_shape=jax.ShapeDtypeStruct((N_TOK, D_MODEL), F32),
        input_output_aliases=aliases,
        compiler_params=pltpu.CompilerParams(
            dimension_semantics=("arbitrary",), vmem_limit_bytes=VMEM_LIMIT),
        name="combine_final",
    )(*args)


SC_CORES = 2
SC_SUBCORES = 16
SC_WORKERS = SC_CORES * SC_SUBCORES
SC_CHUNK = 64


def _sc_mesh():
    return plsc.VectorSubcoreMesh(core_axis_name="c", subcore_axis_name="s")


def _row_buffers():
    return ([pltpu.VMEM((SC_CHUNK, HALF), U32)] * 2 + [pltpu.SemaphoreType.DMA] * 5)


def _dispatch_rows(h2p, dest2d):
    per_w = N_GRP // SC_WORKERS
    nchunk = per_w // SC_CHUNK
    rows_per_k = N_GRP // SC_CHUNK
    assert nchunk % 2 == 0

    @functools.partial(
        pl.kernel, mesh=_sc_mesh(), out_type=jax.ShapeDtypeStruct((P_ROWS, HALF), U32),
        scratch_types=[pltpu.VMEM((TOP_K * nchunk, SC_CHUNK), I32)] + _row_buffers(),
        name="moe_dispatch")
    def k(src_hbm, dest_hbm, out_hbm, idx_v, buf0, buf1, isem, l0, l1, s0, s1):
        wid = lax.axis_index("s") * SC_CORES + lax.axis_index("c")
        bufs, lsem, ssem = (buf0, buf1), (l0, l1), (s0, s1)
        idx_loads = [
            pltpu.make_async_copy(dest_hbm.at[pl.ds(kk * rows_per_k + wid * nchunk, nchunk)],
                                  idx_v.at[pl.ds(kk * nchunk, nchunk)], isem) for kk in range(TOP_K)]

        def load(j, b):
            return pltpu.make_async_copy(src_hbm.at[pl.ds(wid * per_w + j * SC_CHUNK, SC_CHUNK)], bufs[b], lsem[b])

        def scatters(j, b):
            return [pltpu.make_async_copy(bufs[b], out_hbm.at[idx_v.at[kk * nchunk + j]], ssem[b])
                    for kk in range(TOP_K)]

        for cp in idx_loads:
            cp.start()
        load(0, 0).start()
        for cp in idx_loads:
            cp.wait()

        @pl.loop(0, nchunk // 2)
        def _(i):
            for b in range(2):
                j = 2 * i + b

                @pl.when(j >= 1)
                def _():
                    for cp in scatters(j - 1, 1 - b):
                        cp.wait()

                @pl.when(j + 1 < nchunk)
                def _():
                    load(j + 1, 1 - b).start()

                load(j, b).wait()
                for cp in scatters(j, b):
                    cp.start()

        for cp in scatters(nchunk - 1, 1):
            cp.wait()

    return k(h2p, dest2d)


def _combine_rows(yb, dest2d):
    n_out = TOP_K * N_GRP
    per_w = n_out // SC_WORKERS
    nchunk = per_w // SC_CHUNK
    assert nchunk % 2 == 0

    @functools.partial(
        pl.kernel, mesh=_sc_mesh(), out_type=jax.ShapeDtypeStruct((n_out, HALF), U32),
        scratch_types=[pltpu.VMEM((nchunk, SC_CHUNK), I32)] + _row_buffers(),
        name="moe_combine")
    def k(tab_hbm, idx_hbm, out_hbm, idx_v, buf0, buf1, isem, g0, g1, s0, s1):
        wid = lax.axis_index("s") * SC_CORES + lax.axis_index("c")
        bufs, gsem, ssem = (buf0, buf1), (g0, g1), (s0, s1)
        idx_load = pltpu.make_async_copy(idx_hbm.at[pl.ds(wid * nchunk, nchunk)], idx_v, isem)

        def gather(j, b):
            return pltpu.make_async_copy(tab_hbm.at[idx_v.at[j]], bufs[b], gsem[b])

        def store(j, b):
            return pltpu.make_async_copy(bufs[b], out_hbm.at[pl.ds(wid * per_w + j * SC_CHUNK, SC_CHUNK)], ssem[b])

        idx_load.start()
        idx_load.wait()
        gather(0, 0).start()

        @pl.loop(0, nchunk // 2)
        def _(i):
            for b in range(2):
                j = 2 * i + b

                @pl.when(j >= 1)
                def _():
                    store(j - 1, 1 - b).wait()

                @pl.when(j + 1 < nchunk)
                def _():
                    gather(j + 1, 1 - b).start()

                gather(j, b).wait()
                store(j, b).start()

        store(nchunk - 1, 1).wait()

    return k(yb, dest2d)


def _lambda_kernel(p_ref, o_ref):
    p = p_ref[...]
    s1 = jnp.sum(p[0:1] * p[1:2], axis=-1, keepdims=True)
    s2 = jnp.sum(p[2:3] * p[3:4], axis=-1, keepdims=True)
    o_ref[...] = jnp.broadcast_to(jnp.exp(s1) - jnp.exp(s2) + LAMBDA_INIT, (1, LANES))


def kernel(x, c, w_ada, b_ada, norm1_g, w_in, lambda_q1, lambda_k1, lambda_q2, lambda_k2, diff_norm_g, w_alpha_up, b_alpha, gla_norm_g, w_branch_diff, w_branch_gla, w_out, norm2_g, w_router, b_router, w_gate_up, b_gate_up, w_down, b_down, final_norm_g):
    w_in0 = w_in[0]
    c_a, c_g = W_A, W_A + W_G
    w_a = w_in0[:, :c_a].astype(BF16)
    w_g = w_in0[:, c_a:c_g].astype(BF16)
    w_lr = jnp.pad(w_in0[:, c_g:c_g + GLA_RANK], ((0, 0), (0, LANES - GLA_RANK))).astype(BF16)
    w_gate = w_in0[:, c_g + GLA_RANK:].astype(BF16)
    w_up = jnp.pad(w_alpha_up[0], ((0, LANES - GLA_RANK), (0, 0)))
    lam_in = jnp.concatenate([lambda_q1, lambda_k1, lambda_q2, lambda_k2], axis=0)
    slopes = jnp.asarray(2.0 ** (-8.0 * np.arange(1, DIFF_HEADS + 1) / DIFF_HEADS), dtype=F32)

    mod = _modulation(c, w_ada[0], b_ada[0])
    mod4 = mod.reshape(N_MOD, BATCH, 1, D_MODEL)
    lam = pl.pallas_call(
        _lambda_kernel, out_shape=jax.ShapeDtypeStruct((1, LANES), F32), name="lambda")(lam_in)[0, :1]

    qkv_a, qkv_g, gates, glr = _in_proj(x, mod4, norm1_g, w_a, w_g, w_gate, w_lr)
    pos = jnp.arange(SEQ, dtype=I32)
    p_hi = (pos >> POS_LO_BITS).astype(F32)
    p_lo = (pos & (POS_LO - 1)).astype(F32)
    kaug = jnp.zeros((SEQ, LANES), F32).at[:, 0].set(p_hi).at[:, 1].set(p_hi).at[:, 2].set(p_lo).at[:, 3].set(
        p_lo).astype(BF16)
    o_a = _diff_attention(qkv_a, slopes, lam, kaug, diff_norm_g)
    o_b = _gla(qkv_g, glr, w_up, b_alpha, gla_norm_g)

    merge_args = (o_a.reshape(N_TOK, -1), o_b.reshape(N_TOK, -1), gates.reshape(N_TOK, W_GATE),
                  x.reshape(N_TOK, D_MODEL), mod4, norm2_g,
                  w_branch_diff[0].astype(BF16), w_branch_gla[0].astype(BF16), w_out[0].astype(BF16),
                  w_router[0].T, b_router[0].reshape(N_EXPERTS, 1))
    fin_g = final_norm_g.reshape(1, D_MODEL)

    routed = []
    for grp in range(MOE_GROUPS):
        x1, h2p, eidx, rank, wts, cnt = _merge_route(grp, *merge_args)
        dest8, be, nv, rows = _route_plan(cnt, eidx, rank)
        dest = dest8[:TOP_K].reshape(-1, SC_CHUNK)
        routed.append((x1, wts, dest, be, nv, rows, _dispatch_rows(h2p, dest)))
    gathered = []
    for x1, wts, dest, be, nv, rows, xb in routed:
        yb = _expert_ffn(be[0, :N_BLK], nv[0, :1], rows[0, :N_BLK], xb,
                         w_gate_up[0], b_gate_up[0], w_down[0], b_down[0])
        gathered.append(_combine_rows(yb, dest))
    out = None
    for grp, ((x1, wts, *_), yg) in enumerate(zip(routed, gathered)):
        out = _final(grp, x1, yg, wts, mod4, fin_g, out)
    return out.reshape(BATCH, SEQ, D_MODEL)
```

```python
import functools
import math

import jax
import jax.numpy as jnp
import numpy as np
from jax import lax
from jax.experimental import pallas as pl
from jax.experimental.pallas import tpu as pltpu
from jax.experimental.pallas import tpu_sc as plsc

F32 = jnp.float32
BF16 = jnp.bfloat16
U32 = jnp.uint32
I32 = jnp.int32

D_MODEL = 1024
BATCH = 16
SEQ = 2048
N_TOK = BATCH * SEQ
CHUNK = 64
DIFF_HEADS = 4
DIFF_DH = 64
GLA_HEADS = 4
GLA_DK = 64
GLA_DV = 128
GLA_RANK = 16
GLA_GATE_NORM = 16.0
N_EXPERTS = 32
TOP_K = 4
D_FF = D_MODEL
SWIGLU_LIMIT = 7.0
SWIGLU_ALPHA = 1.702
N_MOD = 6
EPS = 1e-6
LAMBDA_INIT = 0.8 - 0.6 * math.exp(-0.3 * 0)

LANES = 128
HALF = D_MODEL // 2
ROW_W = HALF
ROW_DT = U32

TM_IN = 1024
TQ = 512
ATTN_HEADS = 2
ATTN_STRIPS = 4
ATTN_W = 2 * TQ // ATTN_STRIPS
VT_ROWS = 2 * DIFF_DH + 16
LOG2E = math.log2(math.e)
TM_MERGE = 512
BLK = 1024
BLK_STEP = 256
MOE_GROUPS = 1
N_GRP = N_TOK // MOE_GROUPS
N_BLK = (N_GRP * TOP_K) // BLK + N_EXPERTS
P_ROWS = N_BLK * BLK
TM_FIN = 1024
VMEM_LIMIT = 56 * 1024 * 1024
CHUNK_SHIFT = CHUNK.bit_length() - 1
POS_LO_BITS = 3
POS_LO = 1 << POS_LO_BITS


def _nt_dot(a, b):
    return lax.dot_general(a, b, (((1,), (1,)), ((), ())), preferred_element_type=F32)


def _tn_dot(a, b):
    return lax.dot_general(a, b, (((0,), (0,)), ((), ())), preferred_element_type=F32)


def _split3(x):
    hi = x.astype(BF16)
    r1 = x - hi.astype(F32)
    mid = r1.astype(BF16)
    lo = (r1 - mid.astype(F32)).astype(BF16)
    return hi, mid, lo


def _pack_halves(y):
    return pltpu.pack_elementwise([y[:, :HALF], y[:, HALF:]], packed_dtype=BF16)


def _unpack_halves(u):
    lo = pltpu.unpack_elementwise(u, index=0, packed_dtype=BF16, unpacked_dtype=F32)
    hi = pltpu.unpack_elementwise(u, index=1, packed_dtype=BF16, unpacked_dtype=F32)
    return lo, hi


def _mod_kernel(c_ref, w_ref, b_ref, o_ref):
    c = c_ref[...]
    s = c * jax.nn.sigmoid(c)
    o_ref[0] = jnp.dot(s.astype(BF16), w_ref[...].astype(BF16),
                       preferred_element_type=F32) + b_ref[...]


def _modulation(c, w_ada, b_ada):
    return pl.pallas_call(
        _mod_kernel,
        grid=(N_MOD,),
        in_specs=[
            pl.BlockSpec((BATCH, D_MODEL), lambda j: (0, 0)),
            pl.BlockSpec((D_MODEL, D_MODEL), lambda j: (0, j)),
            pl.BlockSpec((1, D_MODEL), lambda j: (0, j)),
        ],
        out_specs=pl.BlockSpec((1, BATCH, D_MODEL), lambda j: (j, 0, 0)),
        out_shape=jax.ShapeDtypeStruct((N_MOD, BATCH, D_MODEL), F32),
        compiler_params=pltpu.CompilerParams(dimension_semantics=("arbitrary",)),
        name="adaln_mod",
    )(c, w_ada, b_ada.reshape(1, N_MOD * D_MODEL))


W_A = 3 * DIFF_HEADS * 2 * DIFF_DH
W_G = 2 * GLA_HEADS * GLA_DK + 2 * GLA_HEADS * GLA_DV
W_GATE = 2 * D_MODEL
W_Z = GLA_HEADS * GLA_DK


def _fold_kernel(wlr_ref, wup_ref, o_ref):
    a_hi, a_mid, _ = _split3(wlr_ref[...])
    b_hi, b_mid, _ = _split3(wup_ref[...])
    o_ref[...] = (jnp.dot(a_hi, b_hi, preferred_element_type=F32)
                  + jnp.dot(a_hi, b_mid, preferred_element_type=F32)
                  + jnp.dot(a_mid, b_hi, preferred_element_type=F32))


def _in_kernel(x_ref, sh_ref, sc_ref, g_ref, wa_ref, wg_ref, wgate_ref, wlr_ref,
               oa_ref, og_ref, ogate_ref, olr_ref):
    x = x_ref[...]
    inv = lax.rsqrt(jnp.mean(x * x, axis=-1, keepdims=True) + EPS)
    h = (x * inv * g_ref[...]) * (1.0 + sc_ref[...]) + sh_ref[...]
    hb = h.astype(BF16)
    oa_ref[...] = jnp.dot(hb, wa_ref[...], preferred_element_type=F32).astype(BF16)
    og_ref[...] = jnp.dot(hb, wg_ref[...], preferred_element_type=F32).astype(BF16)
    ogate_ref[...] = jnp.dot(hb, wgate_ref[...], preferred_element_type=F32).astype(BF16)
    olr_ref[...] = jnp.dot(hb, wlr_ref[...], preferred_element_type=F32)


def _in_proj(x, mod4, norm1_g, w_a, w_g, w_gate, w_lr):
    nrow = SEQ // TM_IN
    full = lambda shape: pl.BlockSpec(shape, lambda b, i: (0,) * len(shape), pipeline_mode=pl.Buffered(1))
    return pl.pallas_call(
        _in_kernel,
        grid=(BATCH, nrow),
        in_specs=[
            pl.BlockSpec((None, TM_IN, D_MODEL), lambda b, i: (b, i, 0)),
            pl.BlockSpec((None, None, 1, D_MODEL), lambda b, i: (0, b, 0, 0)),
            pl.BlockSpec((None, None, 1, D_MODEL), lambda b, i: (1, b, 0, 0)),
            full((1, D_MODEL)),
            full((D_MODEL, W_A)),
            full((D_MODEL, W_G)),
            full((D_MODEL, W_GATE)),
            full((D_MODEL, W_Z)),
        ],
        out_specs=[
            pl.BlockSpec((None, TM_IN, W_A), lambda b, i: (b, i, 0)),
            pl.BlockSpec((None, TM_IN, W_G), lambda b, i: (b, i, 0)),
            pl.BlockSpec((None, TM_IN, W_GATE), lambda b, i: (b, i, 0)),
            pl.BlockSpec((None, TM_IN, W_Z), lambda b, i: (b, i, 0)),
        ],
        out_shape=[
            jax.ShapeDtypeStruct((BATCH, SEQ, W_A), BF16),
            jax.ShapeDtypeStruct((BATCH, SEQ, W_G), BF16),
            jax.ShapeDtypeStruct((BATCH, SEQ, W_GATE), BF16),
            jax.ShapeDtypeStruct((BATCH, SEQ, W_Z), F32),
        ],
        compiler_params=pltpu.CompilerParams(
            dimension_semantics=("arbitrary", "arbitrary"), vmem_limit_bytes=VMEM_LIMIT),
        name="in_proj",
    )(x, mod4, mod4, norm1_g, w_a, w_g, w_gate, w_lr)


def _attn_kernel(slope_ref, lam_ref, q_ref, k_ref, v_ref, kaug_ref, g_ref, o_ref,
                 sa_ref, sb_ref, lha_ref, lhb_ref, corr_ref, vt_ref, m_ref, acc_ref):
    lam = lam_ref[0]
    lane = lax.broadcasted_iota(I32, (TQ, 2 * DIFF_DH), 1)
    zero = jnp.zeros((TQ, 2 * DIFF_DH), BF16)
    heads = range(ATTN_HEADS)
    c_alibi, aug = [], []
    for hh in heads:
        c_alibi.append(slope_ref[pl.program_id(1) * ATTN_HEADS + hh] * LOG2E)
        c_vec = jnp.full((TQ, 2 * DIFF_DH), c_alibi[hh], F32)
        c_hi = c_vec.astype(BF16).astype(F32)
        c_lo = (c_vec - c_hi).astype(BF16).astype(F32)
        aug.append(jnp.where(lane == 0, POS_LO * c_hi, jnp.where(lane == 1, POS_LO * c_lo,
                             jnp.where(lane == 2, c_hi, jnp.where(lane == 3, c_lo, 0.0)))).astype(BF16))

    def blk(j):
        return slice(j * TQ, (j + 1) * TQ)

    def cols(hh):
        return slice(hh * 2 * DIFF_DH, (hh + 1) * 2 * DIFF_DH)

    def stacked_queries(hh, qi):
        q = (q_ref[blk(qi), cols(hh)].astype(F32) * (DIFF_DH ** -0.5 * LOG2E)).astype(BF16)
        return jnp.concatenate([
            jnp.concatenate([jnp.where(lane < DIFF_DH, q, zero), aug[hh]], axis=1),
            jnp.concatenate([jnp.where(lane >= DIFF_DH, q, zero), aug[hh]], axis=1)], axis=0)

    lhs_refs = (lha_ref, lhb_ref)

    def strip(c):
        return slice(c * ATTN_W, (c + 1) * ATTN_W)

    def keys(hh, j):
        return jnp.concatenate([k_ref[blk(j), cols(hh)], kaug_ref[blk(j), :]], axis=1)

    def scores(hh, kk, qi, c):
        return _nt_dot(kk, lhs_refs[qi % 2][hh, strip(c), :])

    ones_rows = jnp.where(lax.broadcasted_iota(I32, (VT_ROWS - 2 * DIFF_DH, TQ), 0) == 0, 1.0, 0.0).astype(BF16)
    kr = lax.broadcasted_iota(I32, (TQ, TQ), 0)
    qc = lax.broadcasted_iota(I32, (TQ, TQ), 1)
    ahead = jnp.maximum(kr - qc, 0).astype(F32)
    for hh in heads:
        for j in range(SEQ // TQ):
            vt_ref[hh, j, :2 * DIFF_DH, :] = v_ref[blk(j), cols(hh)].astype(F32).T.astype(BF16)
            vt_ref[hh, j, 2 * DIFF_DH:, :] = ones_rows
        corr_ref[hh] = jnp.where((qc >> CHUNK_SHIFT) >= (kr >> CHUNK_SHIFT), (-2.0 * c_alibi[hh]) * ahead, -jnp.inf)

    def update(hh, s, j, c, first):
        nk = s.shape[0]
        if first:
            m_new = jnp.max(s, axis=0, keepdims=True)
            p = jnp.exp2((s - m_new).astype(BF16))
            acc_ref[hh, :, strip(c)] = jnp.dot(vt_ref[hh, j, :, :nk], p, preferred_element_type=F32)
        else:
            m = m_ref[hh, :, strip(c)]
            m_new = jnp.maximum(m, jnp.max(s, axis=0, keepdims=True))
            alpha = jnp.exp2(m - m_new)
            p = jnp.exp2((s - m_new).astype(BF16))
            acc_ref[hh, :, strip(c)] = (alpha * acc_ref[hh, :, strip(c)]
                                        + jnp.dot(vt_ref[hh, j, :, :nk], p, preferred_element_type=F32))
        m_ref[hh, :, strip(c)] = m_new

    def n_keys(qi, j, c):
        return TQ if j < qi else min(TQ, (c * ATTN_W) % TQ + ATTN_W)

    pairs = [(qi, j) for qi in range(SEQ // TQ) for j in range(qi + 1)]
    bufs = (sa_ref, sb_ref)
    kk = [None] * ATTN_HEADS
    for hh in heads:
        lhs_refs[0][hh] = stacked_queries(hh, 0)
        kk[hh] = keys(hh, 0)
    for c in range(ATTN_STRIPS):
        for hh in heads:
            nk = n_keys(0, 0, c)
            bufs[0][hh, :nk, strip(c)] = scores(hh, kk[hh][:nk], 0, c)
    for t, (qi, j) in enumerate(pairs):
        nxt = pairs[t + 1] if t + 1 < len(pairs) else None
        if nxt is not None:
            for hh in heads:
                if nxt[0] != qi:
                    lhs_refs[nxt[0] % 2][hh] = stacked_queries(hh, nxt[0])
                kk[hh] = keys(hh, nxt[1])
        for c in range(ATTN_STRIPS):
            for hh in heads:
                if nxt is not None:
                    nk = n_keys(*nxt, c)
                    bufs[(t + 1) % 2][hh, :nk, strip(c)] = scores(hh, kk[hh][:nk], nxt[0], c)
                nk = n_keys(qi, j, c)
                s = bufs[t % 2][hh, :nk, strip(c)]
                if j == qi:
                    lo = (c * ATTN_W) % TQ
                    s = s + corr_ref[hh, :nk, lo:lo + ATTN_W]
                update(hh, s, j, c, first=(j == 0))
        if j == qi:
            for hh in heads:
                ot = acc_ref[hh, :2 * DIFF_DH, :] / acc_ref[hh, 2 * DIFF_DH:2 * DIFF_DH + 1, :]
                o = (ot[:, :TQ] - lam * ot[:, TQ:]).T
                inv = lax.rsqrt(jnp.mean(o * o, axis=-1, keepdims=True) + EPS)
                o_ref[blk(qi), cols(hh)] = (o * inv * g_ref[...] * (1.0 - LAMBDA_INIT)).astype(BF16)


def _diff_attention(qkv_a, slopes, lam, kaug, diff_norm_g):
    groups = DIFF_HEADS // ATTN_HEADS
    width = ATTN_HEADS * 2 * DIFF_DH
    per_head = lambda shape, dt: pltpu.VMEM((ATTN_HEADS,) + shape, dt)
    return pl.pallas_call(
        _attn_kernel,
        grid=(BATCH, groups),
        in_specs=[
            pl.BlockSpec(memory_space=pltpu.SMEM),
            pl.BlockSpec(memory_space=pltpu.SMEM),
            pl.BlockSpec((None, SEQ, width), lambda b, h: (b, 0, h)),
            pl.BlockSpec((None, SEQ, width), lambda b, h: (b, 0, groups + h)),
            pl.BlockSpec((None, SEQ, width), lambda b, h: (b, 0, 2 * groups + h)),
            pl.BlockSpec((SEQ, LANES), lambda b, h: (0, 0)),
            pl.BlockSpec((1, LANES), lambda b, h: (0, 0)),
        ],
        out_specs=pl.BlockSpec((None, SEQ, width), lambda b, h: (b, 0, h)),
        out_shape=jax.ShapeDtypeStruct((BATCH, SEQ, DIFF_HEADS * 2 * DIFF_DH), BF16),
        scratch_shapes=[
            per_head((TQ, 2 * TQ), F32), per_head((TQ, 2 * TQ), F32),
            per_head((2 * TQ, 4 * DIFF_DH), BF16), per_head((2 * TQ, 4 * DIFF_DH), BF16),
            per_head((TQ, TQ), F32),
            per_head((SEQ // TQ, VT_ROWS, TQ), BF16),
            per_head((1, 2 * TQ), F32),
            per_head((VT_ROWS, 2 * TQ), F32),
        ],
        compiler_params=pltpu.CompilerParams(
            dimension_semantics=("arbitrary", "arbitrary"), vmem_limit_bytes=VMEM_LIMIT),
        name="diff_attn",
    )(slopes, lam, qkv_a, qkv_a, qkv_a, kaug, diff_norm_g)


N_CHUNK = SEQ // CHUNK
PAIR = 2 * GLA_DK
PAIR_V = 2 * GLA_DV
CS_ROWS = 256
GLA_UNROLL = 32


def _gla_kernel(q_ref, k_ref, v_ref, r_ref, z_ref, bup_ref, g_ref, o_ref,
                gcum_ref, state_ref):
    rr = lax.broadcasted_iota(I32, (CS_ROWS, CS_ROWS), 0)
    cc = lax.broadcasted_iota(I32, (CS_ROWS, CS_ROWS), 1)
    tri = jnp.where(((rr >> CHUNK_SHIFT) == (cc >> CHUNK_SHIFT)) & (cc <= rr), 1.0, 0.0).astype(BF16)
    for blk in range(SEQ // CS_ROWS):
        rows = pl.ds(blk * CS_ROWS, CS_ROWS)
        z = z_ref[rows, :] + bup_ref[...]
        la = (jnp.minimum(z, 0.0) - jnp.log(1.0 + jnp.exp(-jnp.abs(z)))) * (1.0 / GLA_GATE_NORM)
        l_hi, l_mid, l_lo = _split3(la)
        gcum_ref[rows, :] = (jnp.dot(tri, l_hi, preferred_element_type=F32)
                             + jnp.dot(tri, l_mid, preferred_element_type=F32)
                             + jnp.dot(tri, l_lo, preferred_element_type=F32))

    state_ref[...] = jnp.zeros_like(state_ref)
    lane_k = lax.broadcasted_iota(I32, (1, PAIR), 1)
    row_v = lax.broadcasted_iota(I32, (PAIR_V, PAIR), 0)
    col_k = lax.broadcasted_iota(I32, (PAIR_V, PAIR), 1)
    same_head = (row_v >= GLA_DV) == (col_k >= GLA_DK)
    cr = lax.broadcasted_iota(I32, (CHUNK, CHUNK), 0)
    cs = lax.broadcasted_iota(I32, (CHUNK, CHUNK), 1)
    causal = cs <= cr
    scale = GLA_DK ** -0.5

    def chunk(n):
        rows = pl.ds(pl.multiple_of(n * CHUNK, CHUNK), CHUNK)
        gc = gcum_ref[rows, :]
        g_last = gcum_ref[pl.ds(n * CHUNK + CHUNK - 1, 1), :]
        qf = q_ref[rows, :].astype(F32) * scale
        kf = k_ref[rows, :].astype(F32)
        q_s = (qf * jnp.exp(gc)).astype(BF16)
        k_s = (kf * jnp.exp(-gc)).astype(BF16)
        k_d = (kf * jnp.exp(g_last - gc)).astype(BF16)
        decay = jnp.exp(g_last)
        for pr in range(GLA_HEADS // 2):
            kl = slice(pr * PAIR, (pr + 1) * PAIR)
            vl = slice(pr * PAIR_V, (pr + 1) * PAIR_V)
            qs_p, ks_p, kd_p = q_s[:, kl], k_s[:, kl], k_d[:, kl]
            v_p = v_ref[rows, vl]
            st = state_ref[pr]
            o_inter = _nt_dot(qs_p, st.astype(BF16))
            d_st = _tn_dot(v_p, kd_p)
            state_ref[pr] = st * decay[:, kl] + jnp.where(same_head, d_st, 0.0)
            for sub in range(2):
                hd = 2 * pr + sub
                in_head = (lane_k >= sub * GLA_DK) & (lane_k < (sub + 1) * GLA_DK)
                a = _nt_dot(jnp.where(in_head, qs_p, jnp.zeros_like(qs_p)), ks_p)
                a = jnp.where(causal, a, 0.0).astype(BF16)
                vs = slice(hd * GLA_DV, (hd + 1) * GLA_DV)
                o = (jnp.dot(a, v_ref[rows, vs], preferred_element_type=F32)
                     + o_inter[:, sub * GLA_DV:(sub + 1) * GLA_DV])
                inv = lax.rsqrt(jnp.mean(o * o, axis=-1, keepdims=True) + EPS)
                r = r_ref[rows, vs].astype(F32)
                o_ref[rows, vs] = (o * inv * g_ref[...] * (r * jax.nn.sigmoid(r))).astype(BF16)

    def chunk_group(t, _):
        for u in range(GLA_UNROLL):
            chunk(t * GLA_UNROLL + u)
        return 0

    lax.fori_loop(0, N_CHUNK // GLA_UNROLL, chunk_group, 0)


def _gla(qkv_g, gate_z, b_up, gla_norm_g):
    qk_w = GLA_HEADS * GLA_DK
    v_w = GLA_HEADS * GLA_DV
    return pl.pallas_call(
        _gla_kernel,
        grid=(BATCH,),
        in_specs=[
            pl.BlockSpec((None, SEQ, qk_w), lambda b: (b, 0, 0)),
            pl.BlockSpec((None, SEQ, qk_w), lambda b: (b, 0, 1)),
            pl.BlockSpec((None, SEQ, v_w), lambda b: (b, 0, 1)),
            pl.BlockSpec((None, SEQ, v_w), lambda b: (b, 0, 2)),
            pl.BlockSpec((None, SEQ, qk_w), lambda b: (b, 0, 0)),
            pl.BlockSpec((1, qk_w), lambda b: (0, 0)),
            pl.BlockSpec((1, GLA_DV), lambda b: (0, 0)),
        ],
        out_specs=pl.BlockSpec((None, SEQ, v_w), lambda b: (b, 0, 0)),
        out_shape=jax.ShapeDtypeStruct((BATCH, SEQ, v_w), BF16),
        scratch_shapes=[
            pltpu.VMEM((SEQ, qk_w), F32),
            pltpu.VMEM((GLA_HEADS // 2, PAIR_V, PAIR), F32),
        ],
        compiler_params=pltpu.CompilerParams(
            dimension_semantics=("arbitrary",), vmem_limit_bytes=VMEM_LIMIT),
        name="gla",
    )(qkv_g, qkv_g, qkv_g, qkv_g, gate_z, b_up, gla_norm_g)


def _merge_kernel(oa_ref, ob_ref, gate_ref, x_ref, gt1_ref, sh2_ref, sc2_ref, g2_ref,
                  wpa_ref, wpb_ref, wo_ref, wr_ref, br_ref,
                  x1_ref, h2p_ref, eidx_ref, rank_ref, wts_ref, cnt_ref,
                  upper_ref, carry_ref):
    i = pl.program_id(0)

    @pl.when(i == 0)
    def _():
        rr = lax.broadcasted_iota(I32, (TM_MERGE, TM_MERGE), 0)
        cc = lax.broadcasted_iota(I32, (TM_MERGE, TM_MERGE), 1)
        upper_ref[...] = jnp.where(rr < cc, 1.0, 0.0).astype(BF16)
        carry_ref[...] = jnp.zeros_like(carry_ref)

    ga = gate_ref[:, :D_MODEL].astype(F32)
    gb = gate_ref[:, D_MODEL:].astype(F32)
    merged = (jax.nn.sigmoid(ga) * jnp.dot(oa_ref[...], wpa_ref[...], preferred_element_type=F32)
              + jax.nn.sigmoid(gb) * jnp.dot(ob_ref[...], wpb_ref[...], preferred_element_type=F32))
    y = jnp.dot(merged.astype(BF16), wo_ref[...], preferred_element_type=F32)
    x1 = x_ref[...] + gt1_ref[...] * y
    x1_ref[...] = x1
    inv = lax.rsqrt(jnp.mean(x1 * x1, axis=-1, keepdims=True) + EPS)
    h2 = (x1 * inv * g2_ref[...]) * (1.0 + sc2_ref[...]) + sh2_ref[...]
    h2p_ref[...] = _pack_halves(h2)

    h_hi, h_mid, _ = _split3(h2)
    w_hi, w_mid, _ = _split3(wr_ref[...])
    logits = (_nt_dot(w_hi, h_hi) + _nt_dot(w_hi, h_mid) + _nt_dot(w_mid, h_hi)) + br_ref[...]

    eio = lax.broadcasted_iota(I32, (N_EXPERTS, TM_MERGE), 0)
    vals, idxs, sels = [], [], []
    cur = logits
    for _k in range(TOP_K):
        m = jnp.max(cur, axis=0, keepdims=True)
        idx = jnp.min(jnp.where(cur == m, eio, N_EXPERTS), axis=0, keepdims=True)
        sel = eio == idx
        vals.append(m)
        idxs.append(idx)
        sels.append(sel)
        cur = jnp.where(sel, -jnp.inf, cur)
    es = [jnp.exp(v - vals[0]) for v in vals]
    tot = es[0] + es[1] + es[2] + es[3]
    onehot = jnp.zeros((N_EXPERTS, TM_MERGE), F32)
    for sel in sels:
        onehot = onehot + jnp.where(sel, 1.0, 0.0)
    before = jnp.dot(onehot.astype(BF16), upper_ref[...], preferred_element_type=F32) + carry_ref[:, 0:1]
    ranks = [jnp.sum(jnp.where(sel, before, 0.0), axis=0, keepdims=True) for sel in sels]
    carry_ref[...] = carry_ref[...] + jnp.sum(onehot, axis=1, keepdims=True)
    cnt_ref[...] = carry_ref[...]

    zi = jnp.zeros((8 - TOP_K, TM_MERGE), I32)
    zf = jnp.zeros((8 - TOP_K, TM_MERGE), F32)
    eidx_ref[...] = jnp.concatenate(idxs + [zi], axis=0)
    rank_ref[...] = jnp.concatenate([r.astype(I32) for r in ranks] + [zi], axis=0)
    wts_ref[...] = jnp.concatenate([e / tot for e in es] + [zf], axis=0)


def _merge_route(grp, o_a, o_b, gates, x2d, mod4, norm2_g, w_pa, w_pb, w_o, w_rt, b_r):
    ntile = N_GRP // TM_MERGE
    first = grp * ntile
    per_b = SEQ // TM_MERGE
    full = lambda shape: pl.BlockSpec(shape, lambda i: (0,) * len(shape))
    row_in = lambda w: pl.BlockSpec((TM_MERGE, w), lambda i: (first + i, 0))
    row = lambda w: pl.BlockSpec((TM_MERGE, w), lambda i: (i, 0))
    modspec = lambda j: pl.BlockSpec((None, None, 1, D_MODEL), lambda i: (j, (first + i) // per_b, 0, 0))
    col = pl.BlockSpec((8, TM_MERGE), lambda i: (0, i))
    return pl.pallas_call(
        _merge_kernel,
        grid=(ntile,),
        in_specs=[
            row_in(DIFF_HEADS * 2 * DIFF_DH), row_in(GLA_HEADS * GLA_DV), row_in(W_GATE), row_in(D_MODEL),
            modspec(2), modspec(3), modspec(4),
            full((1, D_MODEL)),
            full((DIFF_HEADS * 2 * DIFF_DH, D_MODEL)), full((GLA_HEADS * GLA_DV, D_MODEL)),
            full((D_MODEL, D_MODEL)),
            full((N_EXPERTS, D_MODEL)), full((N_EXPERTS, 1)),
        ],
        out_specs=[
            row(D_MODEL), row(ROW_W), col, col, col,
            pl.BlockSpec((N_EXPERTS, LANES), lambda i: (0, 0)),
        ],
        out_shape=[
            jax.ShapeDtypeStruct((N_GRP, D_MODEL), F32),
            jax.ShapeDtypeStruct((N_GRP, ROW_W), ROW_DT),
            jax.ShapeDtypeStruct((8, N_GRP), I32),
            jax.ShapeDtypeStruct((8, N_GRP), I32),
            jax.ShapeDtypeStruct((8, N_GRP), F32),
            jax.ShapeDtypeStruct((N_EXPERTS, LANES), F32),
        ],
        scratch_shapes=[
            pltpu.VMEM((TM_MERGE, TM_MERGE), BF16),
            pltpu.VMEM((N_EXPERTS, LANES), F32),
        ],
        compiler_params=pltpu.CompilerParams(
            dimension_semantics=("arbitrary",), vmem_limit_bytes=VMEM_LIMIT),
        name="merge_route",
    )(o_a, o_b, gates, x2d, mod4, mod4, mod4, norm2_g, w_pa, w_pb, w_o, w_rt, b_r)


TP = 4096
NB_PAD = ((N_BLK + LANES - 1) // LANES) * LANES


def _plan_kernel(cnt_ref, eidx_ref, rank_ref, dest_ref, be_ref, nv_ref, rows_ref):
    cnt = cnt_ref[...]
    padded = jnp.floor((cnt + (BLK - 1.0)) * (1.0 / BLK)) * BLK
    er = lax.broadcasted_iota(I32, (N_EXPERTS, N_EXPERTS), 0)
    ec = lax.broadcasted_iota(I32, (N_EXPERTS, N_EXPERTS), 1)
    lower = jnp.where(ec < er, 1.0, 0.0).astype(BF16)
    p_hi, p_mid, p_lo = _split3(padded)
    starts = (jnp.dot(lower, p_hi, preferred_element_type=F32)
              + jnp.dot(lower, p_mid, preferred_element_type=F32)
              + jnp.dot(lower, p_lo, preferred_element_type=F32))
    ends = starts + padded
    blk_start = (lax.broadcasted_iota(I32, (1, NB_PAD), 1) * BLK).astype(F32)
    n_before = jnp.sum(jnp.where(ends[:, 0:1] <= blk_start, 1.0, 0.0), axis=0, keepdims=True)
    be_ref[...] = jnp.minimum(n_before, N_EXPERTS - 1.0).astype(I32)
    nv_ref[...] = (jnp.max(ends, axis=0, keepdims=True) * (1.0 / BLK)).astype(I32)
    owner = (starts[:, 0:1] <= blk_start) & (blk_start < ends[:, 0:1])
    filled = jnp.sum(jnp.where(owner, (starts + cnt)[:, 0:1] - blk_start, 0.0), axis=0, keepdims=True)
    rows_ref[...] = jnp.clip(filled, 0.0, float(BLK)).astype(I32)

    eio = lax.broadcasted_iota(I32, (N_EXPERTS, TP), 0)
    rows = []
    for k in range(TOP_K):
        onehot = eio == eidx_ref[k:k + 1, :]
        base = jnp.sum(jnp.where(onehot, starts[:, 0:1], 0.0), axis=0, keepdims=True)
        rows.append(base.astype(I32) + rank_ref[k:k + 1, :])
    dest_ref[...] = jnp.concatenate(rows + [jnp.zeros((8 - TOP_K, TP), I32)], axis=0)


def _route_plan(cnt, eidx, rank):
    return pl.pallas_call(
        _plan_kernel,
        grid=(N_GRP // TP,),
        in_specs=[
            pl.BlockSpec((N_EXPERTS, LANES), lambda i: (0, 0)),
            pl.BlockSpec((8, TP), lambda i: (0, i)),
            pl.BlockSpec((8, TP), lambda i: (0, i)),
        ],
        out_specs=[
            pl.BlockSpec((8, TP), lambda i: (0, i)),
            pl.BlockSpec((1, NB_PAD), lambda i: (0, 0)),
            pl.BlockSpec((1, LANES), lambda i: (0, 0)),
            pl.BlockSpec((1, NB_PAD), lambda i: (0, 0)),
        ],
        out_shape=[
            jax.ShapeDtypeStruct((8, N_GRP), I32),
            jax.ShapeDtypeStruct((1, NB_PAD), I32),
            jax.ShapeDtypeStruct((1, LANES), I32),
            jax.ShapeDtypeStruct((1, NB_PAD), I32),
        ],
        compiler_params=pltpu.CompilerParams(dimension_semantics=("arbitrary",)),
        name="route_plan",
    )(cnt, eidx, rank)


def _ffn_kernel(be_ref, nv_ref, first_ref, slot_ref, nxt_ref, rows_ref, x_ref, wgu_hbm, bgu_ref, wd_hbm, bd_ref,
                y_ref, wgu_buf, wd_buf, sem):
    i = pl.program_id(0)
    valid = i < nv_ref[0]
    filled = jnp.maximum(rows_ref[i], 1)
    s = slot_ref[i]

    def weight_copies(e, slot):
        return (pltpu.make_async_copy(wgu_hbm.at[e], wgu_buf.at[slot], sem.at[slot, 0]),
                pltpu.make_async_copy(wd_hbm.at[e], wd_buf.at[slot], sem.at[slot, 1]))

    @pl.when(i == 0)
    def _():
        for cp in weight_copies(be_ref[0], 0):
            cp.start()

    @pl.when(valid & (first_ref[i] == 1))
    def _():
        for cp in weight_copies(be_ref[i], s):
            cp.wait()

        @pl.when(nxt_ref[i] >= 0)
        def _():
            for cp in weight_copies(nxt_ref[i], 1 - s):
                cp.start()

    def expert_rows(n):
        lo, hi = _unpack_halves(x_ref[:n, :])
        x = jnp.concatenate([lo, hi], axis=1).astype(BF16)
        mm = lambda a, w: lax.dot_general(a, w, (((1,), (0,)), ((), ())), preferred_element_type=F32)
        gu = mm(x, wgu_buf[s]) + bgu_ref[...]
        gate = jnp.minimum(gu[:, :D_FF], SWIGLU_LIMIT)
        up = jnp.clip(gu[:, D_FF:], -SWIGLU_LIMIT, SWIGLU_LIMIT)
        act = (up + 1.0) * (gate * jax.nn.sigmoid(SWIGLU_ALPHA * gate))
        y = mm(act.astype(BF16), wd_buf[s]) + bd_ref[...]
        y_ref[:n, :] = _pack_halves(y)

    for n in range(BLK_STEP, BLK + 1, BLK_STEP):
        @pl.when(valid & (filled > n - BLK_STEP) & (filled <= n))
        def _(n=n):
            expert_rows(n)
            if n < BLK:
                y_ref[n:, :] = jnp.zeros((BLK - n, ROW_W), ROW_DT)

    @pl.when(jnp.logical_not(valid))
    def _():
        y_ref[...] = jnp.zeros_like(y_ref)


def _expert_ffn(blk_expert, n_valid, blk_rows, xb, w_gate_up, b_gate_up, w_down, b_down):
    idx = jnp.arange(N_BLK, dtype=I32)
    used = idx < n_valid[0]
    first = (used & ((idx == 0) | (blk_expert != jnp.roll(blk_expert, 1)))).astype(I32)
    slot = (jnp.cumsum(first) - 1) & 1
    later = used[None, :] & (idx[None, :] > idx[:, None]) & (blk_expert[None, :] != blk_expert[:, None])
    nxt_pos = jnp.min(jnp.where(later, idx[None, :], N_BLK), axis=1)
    nxt = jnp.where(nxt_pos < N_BLK, blk_expert[jnp.minimum(nxt_pos, N_BLK - 1)], -1).astype(I32)

    blockwise = lambda i, *_: (i, 0)
    per_expert = lambda i, be, *_: (be[i], 0, 0)
    grid_spec = pltpu.PrefetchScalarGridSpec(
        num_scalar_prefetch=6,
        grid=(N_BLK,),
        in_specs=[
            pl.BlockSpec((BLK, ROW_W), blockwise),
            pl.BlockSpec(memory_space=pl.ANY),
            pl.BlockSpec((None, 1, 2 * D_FF), per_expert),
            pl.BlockSpec(memory_space=pl.ANY),
            pl.BlockSpec((None, 1, D_MODEL), per_expert),
        ],
        out_specs=pl.BlockSpec((BLK, ROW_W), blockwise),
        scratch_shapes=[
            pltpu.VMEM((2, D_MODEL, 2 * D_FF), F32),
            pltpu.VMEM((2, D_FF, D_MODEL), F32),
            pltpu.SemaphoreType.DMA((2, 2)),
        ],
    )
    return pl.pallas_call(
        _ffn_kernel,
        grid_spec=grid_spec,
        out_shape=jax.ShapeDtypeStruct((P_ROWS, ROW_W), ROW_DT),
        compiler_params=pltpu.CompilerParams(
            dimension_semantics=("arbitrary",), vmem_limit_bytes=VMEM_LIMIT),
        name="expert_ffn",
    )(blk_expert, n_valid, first, slot.astype(I32), nxt, blk_rows, xb, w_gate_up,
      b_gate_up.reshape(N_EXPERTS, 1, 2 * D_FF), w_down, b_down.reshape(N_EXPERTS, 1, D_MODEL))


def _final_kernel(x1_ref, y0_ref, y1_ref, y2_ref, y3_ref, w_ref, gt2_ref, g_ref, *rest):
    o_ref = rest[-1]
    w = w_ref[...].T
    ylo = jnp.zeros((TM_FIN, HALF), F32)
    yhi = jnp.zeros((TM_FIN, HALF), F32)
    for k, y_ref in enumerate((y0_ref, y1_ref, y2_ref, y3_ref)):
        lo, hi = _unpack_halves(y_ref[...])
        wk = w[:, k:k + 1]
        ylo = ylo + wk * lo
        yhi = yhi + wk * hi
    y = jnp.concatenate([ylo, yhi], axis=1)
    x2 = x1_ref[...] + gt2_ref[...] * y
    inv = lax.rsqrt(jnp.mean(x2 * x2, axis=-1, keepdims=True) + EPS)
    o_ref[...] = x2 * inv * g_ref[...]


def _final(grp, x1, yg, w4, mod4, final_norm_g, out_so_far):
    per_b = SEQ // TM_FIN
    ntile = N_GRP // TM_FIN
    first = grp * ntile
    slot = lambda k: pl.BlockSpec((TM_FIN, ROW_W), lambda i: (k * ntile + i, 0))
    in_specs = [
        pl.BlockSpec((TM_FIN, D_MODEL), lambda i: (i, 0)),
        slot(0), slot(1), slot(2), slot(3),
        pl.BlockSpec((8, TM_FIN), lambda i: (0, i)),
        pl.BlockSpec((None, None, 1, D_MODEL), lambda i: (5, (first + i) // per_b, 0, 0)),
        pl.BlockSpec((1, D_MODEL), lambda i: (0, 0)),
    ]
    args = [x1, yg, yg, yg, yg, w4, mod4, final_norm_g]
    aliases = {}
    if out_so_far is not None:
        in_specs.append(pl.BlockSpec(memory_space=pl.ANY))
        args.append(out_so_far)
        aliases = {len(args) - 1: 0}
    return pl.pallas_call(
        _final_kernel,
        grid=(ntile,),
        in_specs=in_specs,
        out_specs=pl.BlockSpec((TM_FIN, D_MODEL), lambda i: (first + i, 0)),
        out_shape=jax.ShapeDtypeStruct((N_TOK, D_MODEL), F32),
        input_output_aliases=aliases,
        compiler_params=pltpu.CompilerParams(
            dimension_semantics=("arbitrary",), vmem_limit_bytes=VMEM_LIMIT),
        name="combine_final",
    )(*args)


SC_CORES = 2
SC_SUBCORES = 16
SC_WORKERS = SC_CORES * SC_SUBCORES
SC_CHUNK = 64


def _sc_mesh():
    return plsc.VectorSubcoreMesh(core_axis_name="c", subcore_axis_name="s")


def _row_buffers():
    return ([pltpu.VMEM((SC_CHUNK, HALF), U32)] * 2 + [pltpu.SemaphoreType.DMA] * 5)


def _dispatch_rows(h2p, dest2d):
    per_w = N_GRP // SC_WORKERS
    nchunk = per_w // SC_CHUNK
    rows_per_k = N_GRP // SC_CHUNK
    assert nchunk % 2 == 0

    @functools.partial(
        pl.kernel, mesh=_sc_mesh(), out_type=jax.ShapeDtypeStruct((P_ROWS, HALF), U32),
        scratch_types=[pltpu.VMEM((TOP_K * nchunk, SC_CHUNK), I32)] + _row_buffers(),
        name="moe_dispatch")
    def k(src_hbm, dest_hbm, out_hbm, idx_v, buf0, buf1, isem, l0, l1, s0, s1):
        wid = lax.axis_index("s") * SC_CORES + lax.axis_index("c")
        bufs, lsem, ssem = (buf0, buf1), (l0, l1), (s0, s1)
        idx_loads = [
            pltpu.make_async_copy(dest_hbm.at[pl.ds(kk * rows_per_k + wid * nchunk, nchunk)],
                                  idx_v.at[pl.ds(kk * nchunk, nchunk)], isem) for kk in range(TOP_K)]

        def load(j, b):
            return pltpu.make_async_copy(src_hbm.at[pl.ds(wid * per_w + j * SC_CHUNK, SC_CHUNK)], bufs[b], lsem[b])

        def scatters(j, b):
            return [pltpu.make_async_copy(bufs[b], out_hbm.at[idx_v.at[kk * nchunk + j]], ssem[b])
                    for kk in range(TOP_K)]

        for cp in idx_loads:
            cp.start()
        load(0, 0).start()
        for cp in idx_loads:
            cp.wait()

        @pl.loop(0, nchunk // 2)
        def _(i):
            for b in range(2):
                j = 2 * i + b

                @pl.when(j >= 1)
                def _():
                    for cp in scatters(j - 1, 1 - b):
                        cp.wait()

                @pl.when(j + 1 < nchunk)
                def _():
                    load(j + 1, 1 - b).start()

                load(j, b).wait()
                for cp in scatters(j, b):
                    cp.start()

        for cp in scatters(nchunk - 1, 1):
            cp.wait()

    return k(h2p, dest2d)


def _combine_rows(yb, dest2d):
    n_out = TOP_K * N_GRP
    per_w = n_out // SC_WORKERS
    nchunk = per_w // SC_CHUNK
    assert nchunk % 2 == 0

    @functools.partial(
        pl.kernel, mesh=_sc_mesh(), out_type=jax.ShapeDtypeStruct((n_out, HALF), U32),
        scratch_types=[pltpu.VMEM((nchunk, SC_CHUNK), I32)] + _row_buffers(),
        name="moe_combine")
    def k(tab_hbm, idx_hbm, out_hbm, idx_v, buf0, buf1, isem, g0, g1, s0, s1):
        wid = lax.axis_index("s") * SC_CORES + lax.axis_index("c")
        bufs, gsem, ssem = (buf0, buf1), (g0, g1), (s0, s1)
        idx_load = pltpu.make_async_copy(idx_hbm.at[pl.ds(wid * nchunk, nchunk)], idx_v, isem)

        def gather(j, b):
            return pltpu.make_async_copy(tab_hbm.at[idx_v.at[j]], bufs[b], gsem[b])

        def store(j, b):
            return pltpu.make_async_copy(bufs[b], out_hbm.at[pl.ds(wid * per_w + j * SC_CHUNK, SC_CHUNK)], ssem[b])

        idx_load.start()
        idx_load.wait()
        gather(0, 0).start()

        @pl.loop(0, nchunk // 2)
        def _(i):
            for b in range(2):
                j = 2 * i + b

                @pl.when(j >= 1)
                def _():
                    store(j - 1, 1 - b).wait()

                @pl.when(j + 1 < nchunk)
                def _():
                    gather(j + 1, 1 - b).start()

                gather(j, b).wait()
                store(j, b).start()

        store(nchunk - 1, 1).wait()

    return k(yb, dest2d)


def _lambda_kernel(p_ref, o_ref):
    p = p_ref[...]
    s1 = jnp.sum(p[0:1] * p[1:2], axis=-1, keepdims=True)
    s2 = jnp.sum(p[2:3] * p[3:4], axis=-1, keepdims=True)
    o_ref[...] = jnp.broadcast_to(jnp.exp(s1) - jnp.exp(s2) + LAMBDA_INIT, (1, LANES))


def kernel(x, c, w_ada, b_ada, norm1_g, w_in, lambda_q1, lambda_k1, lambda_q2, lambda_k2, diff_norm_g, w_alpha_up, b_alpha, gla_norm_g, w_branch_diff, w_branch_gla, w_out, norm2_g, w_router, b_router, w_gate_up, b_gate_up, w_down, b_down, final_norm_g):
    w_in0 = w_in[0]
    c_a, c_g = W_A, W_A + W_G
    w_a = w_in0[:, :c_a].astype(BF16)
    w_g = w_in0[:, c_a:c_g].astype(BF16)
    w_lr = jnp.pad(w_in0[:, c_g:c_g + GLA_RANK], ((0, 0), (0, LANES - GLA_RANK)))
    w_gate = w_in0[:, c_g + GLA_RANK:].astype(BF16)
    w_up = jnp.pad(w_alpha_up[0], ((0, LANES - GLA_RANK), (0, 0)))
    lam_in = jnp.concatenate([lambda_q1, lambda_k1, lambda_q2, lambda_k2], axis=0)
    slopes = jnp.asarray(2.0 ** (-8.0 * np.arange(1, DIFF_HEADS + 1) / DIFF_HEADS), dtype=F32)

    mod = _modulation(c, w_ada[0], b_ada[0])
    mod4 = mod.reshape(N_MOD, BATCH, 1, D_MODEL)
    lam = pl.pallas_call(
        _lambda_kernel, out_shape=jax.ShapeDtypeStruct((1, LANES), F32), name="lambda")(lam_in)[0, :1]

    w_z = pl.pallas_call(
        _fold_kernel, out_shape=jax.ShapeDtypeStruct((D_MODEL, W_Z), F32), name="fold_gate")(w_lr, w_up)
    qkv_a, qkv_g, gates, gate_z = _in_proj(x, mod4, norm1_g, w_a, w_g, w_gate, w_z.astype(BF16))
    pos = jnp.arange(SEQ, dtype=I32)
    p_hi = (pos >> POS_LO_BITS).astype(F32)
    p_lo = (pos & (POS_LO - 1)).astype(F32)
    kaug = jnp.zeros((SEQ, LANES), F32).at[:, 0].set(p_hi).at[:, 1].set(p_hi).at[:, 2].set(p_lo).at[:, 3].set(
        p_lo).astype(BF16)
    o_a = _diff_attention(qkv_a, slopes, lam, kaug, diff_norm_g)
    o_b = _gla(qkv_g, gate_z, b_alpha, gla_norm_g)

    merge_args = (o_a.reshape(N_TOK, -1), o_b.reshape(N_TOK, -1), gates.reshape(N_TOK, W_GATE),
                  x.reshape(N_TOK, D_MODEL), mod4, norm2_g,
                  w_branch_diff[0].astype(BF16), w_branch_gla[0].astype(BF16), w_out[0].astype(BF16),
                  w_router[0].T, b_router[0].reshape(N_EXPERTS, 1))
    fin_g = final_norm_g.reshape(1, D_MODEL)

    routed = []
    for grp in range(MOE_GROUPS):
        x1, h2p, eidx, rank, wts, cnt = _merge_route(grp, *merge_args)
        dest8, be, nv, rows = _route_plan(cnt, eidx, rank)
        dest = dest8[:TOP_K].reshape(-1, SC_CHUNK)
        routed.append((x1, wts, dest, be, nv, rows, _dispatch_rows(h2p, dest)))
    gathered = []
    for x1, wts, dest, be, nv, rows, xb in routed:
        yb = _expert_ffn(be[0, :N_BLK], nv[0, :1], rows[0, :N_BLK], xb,
                         w_gate_up[0], b_gate_up[0], w_down[0], b_down[0])
        gathered.append(_combine_rows(yb, dest))
    out = None
    for grp, ((x1, wts, *_), yg) in enumerate(zip(routed, gathered)):
        out = _final(grp, x1, yg, wts, mod4, fin_g, out)
    return out.reshape(BATCH, SEQ, D_MODEL)
```

```python
import functools
import math

import jax
import jax.numpy as jnp
import numpy as np
from jax import lax
from jax.experimental import pallas as pl
from jax.experimental.pallas import tpu as pltpu
from jax.experimental.pallas import tpu_sc as plsc

F32 = jnp.float32
BF16 = jnp.bfloat16
U32 = jnp.uint32
I32 = jnp.int32

D_MODEL = 1024
BATCH = 16
SEQ = 2048
N_TOK = BATCH * SEQ
CHUNK = 64
DIFF_HEADS = 4
DIFF_DH = 64
GLA_HEADS = 4
GLA_DK = 64
GLA_DV = 128
GLA_RANK = 16
GLA_GATE_NORM = 16.0
N_EXPERTS = 32
TOP_K = 4
D_FF = D_MODEL
SWIGLU_LIMIT = 7.0
SWIGLU_ALPHA = 1.702
N_MOD = 6
EPS = 1e-6
LAMBDA_INIT = 0.8 - 0.6 * math.exp(-0.3 * 0)

LANES = 128
HALF = D_MODEL // 2
ROW_W = HALF
ROW_DT = U32

TM_IN = 1024
TQ = 512
ATTN_HEADS = 2
ATTN_STRIPS = 4
ATTN_W = 2 * TQ // ATTN_STRIPS
VT_ROWS = 2 * DIFF_DH + 16
LOG2E = math.log2(math.e)
TM_MERGE = 512
BLK = 1024
BLK_STEP = 256
FF_PARTS = 2
MOE_GROUPS = 1
N_GRP = N_TOK // MOE_GROUPS
N_BLK = (N_GRP * TOP_K) // BLK + N_EXPERTS
P_ROWS = N_BLK * BLK
TM_FIN = 1024
VMEM_LIMIT = 56 * 1024 * 1024
CHUNK_SHIFT = CHUNK.bit_length() - 1
POS_LO_BITS = 3
POS_LO = 1 << POS_LO_BITS


def _nt_dot(a, b):
    return lax.dot_general(a, b, (((1,), (1,)), ((), ())), preferred_element_type=F32)


def _tn_dot(a, b):
    return lax.dot_general(a, b, (((0,), (0,)), ((), ())), preferred_element_type=F32)


def _split3(x):
    hi = x.astype(BF16)
    r1 = x - hi.astype(F32)
    mid = r1.astype(BF16)
    lo = (r1 - mid.astype(F32)).astype(BF16)
    return hi, mid, lo


def _pack_halves(y):
    return pltpu.pack_elementwise([y[:, :HALF], y[:, HALF:]], packed_dtype=BF16)


def _unpack_halves(u):
    lo = pltpu.unpack_elementwise(u, index=0, packed_dtype=BF16, unpacked_dtype=F32)
    hi = pltpu.unpack_elementwise(u, index=1, packed_dtype=BF16, unpacked_dtype=F32)
    return lo, hi


def _mod_kernel(c_ref, w_ref, b_ref, o_ref):
    c = c_ref[...]
    s = c * jax.nn.sigmoid(c)
    o_ref[0] = jnp.dot(s.astype(BF16), w_ref[...].astype(BF16),
                       preferred_element_type=F32) + b_ref[...]


def _modulation(c, w_ada, b_ada):
    return pl.pallas_call(
        _mod_kernel,
        grid=(N_MOD,),
        in_specs=[
            pl.BlockSpec((BATCH, D_MODEL), lambda j: (0, 0)),
            pl.BlockSpec((D_MODEL, D_MODEL), lambda j: (0, j)),
            pl.BlockSpec((1, D_MODEL), lambda j: (0, j)),
        ],
        out_specs=pl.BlockSpec((1, BATCH, D_MODEL), lambda j: (j, 0, 0)),
        out_shape=jax.ShapeDtypeStruct((N_MOD, BATCH, D_MODEL), F32),
        compiler_params=pltpu.CompilerParams(dimension_semantics=("arbitrary",)),
        name="adaln_mod",
    )(c, w_ada, b_ada.reshape(1, N_MOD * D_MODEL))


W_A = 3 * DIFF_HEADS * 2 * DIFF_DH
W_G = 2 * GLA_HEADS * GLA_DK + 2 * GLA_HEADS * GLA_DV
W_GATE = 2 * D_MODEL


def _in_kernel(x_ref, sh_ref, sc_ref, g_ref, wa_ref, wg_ref, wgate_ref, wlr_ref,
               oa_ref, og_ref, ogate_ref, olr_ref):
    x = x_ref[...]
    inv = lax.rsqrt(jnp.mean(x * x, axis=-1, keepdims=True) + EPS)
    h = (x * inv * g_ref[...]) * (1.0 + sc_ref[...]) + sh_ref[...]
    hb = h.astype(BF16)
    oa_ref[...] = jnp.dot(hb, wa_ref[...], preferred_element_type=F32).astype(BF16)
    og_ref[...] = jnp.dot(hb, wg_ref[...], preferred_element_type=F32).astype(BF16)
    ogate_ref[...] = jnp.dot(hb, wgate_ref[...], preferred_element_type=F32).astype(BF16)
    olr_ref[...] = jnp.dot(hb, wlr_ref[...], preferred_element_type=F32)


def _in_proj(x, mod4, norm1_g, w_a, w_g, w_gate, w_lr):
    nrow = SEQ // TM_IN
    full = lambda shape: pl.BlockSpec(shape, lambda b, i: (0,) * len(shape), pipeline_mode=pl.Buffered(1))
    return pl.pallas_call(
        _in_kernel,
        grid=(BATCH, nrow),
        in_specs=[
            pl.BlockSpec((None, TM_IN, D_MODEL), lambda b, i: (b, i, 0)),
            pl.BlockSpec((None, None, 1, D_MODEL), lambda b, i: (0, b, 0, 0)),
            pl.BlockSpec((None, None, 1, D_MODEL), lambda b, i: (1, b, 0, 0)),
            full((1, D_MODEL)),
            full((D_MODEL, W_A)),
            full((D_MODEL, W_G)),
            full((D_MODEL, W_GATE)),
            full((D_MODEL, LANES)),
        ],
        out_specs=[
            pl.BlockSpec((None, TM_IN, W_A), lambda b, i: (b, i, 0)),
            pl.BlockSpec((None, TM_IN, W_G), lambda b, i: (b, i, 0)),
            pl.BlockSpec((None, TM_IN, W_GATE), lambda b, i: (b, i, 0)),
            pl.BlockSpec((None, TM_IN, LANES), lambda b, i: (b, i, 0)),
        ],
        out_shape=[
            jax.ShapeDtypeStruct((BATCH, SEQ, W_A), BF16),
            jax.ShapeDtypeStruct((BATCH, SEQ, W_G), BF16),
            jax.ShapeDtypeStruct((BATCH, SEQ, W_GATE), BF16),
            jax.ShapeDtypeStruct((BATCH, SEQ, LANES), F32),
        ],
        compiler_params=pltpu.CompilerParams(
            dimension_semantics=("arbitrary", "arbitrary"), vmem_limit_bytes=VMEM_LIMIT),
        name="in_proj",
    )(x, mod4, mod4, norm1_g, w_a, w_g, w_gate, w_lr)


def _attn_kernel(slope_ref, lam_ref, q_ref, k_ref, v_ref, kaug_ref, g_ref, o_ref,
                 sa_ref, sb_ref, lha_ref, lhb_ref, corr_ref, vt_ref, m_ref, acc_ref):
    lam = lam_ref[0]
    lane = lax.broadcasted_iota(I32, (TQ, 2 * DIFF_DH), 1)
    zero = jnp.zeros((TQ, 2 * DIFF_DH), BF16)
    heads = range(ATTN_HEADS)
    c_alibi, aug = [], []
    for hh in heads:
        c_alibi.append(slope_ref[pl.program_id(1) * ATTN_HEADS + hh] * LOG2E)
        c_vec = jnp.full((TQ, 2 * DIFF_DH), c_alibi[hh], F32)
        c_hi = c_vec.astype(BF16).astype(F32)
        c_lo = (c_vec - c_hi).astype(BF16).astype(F32)
        aug.append(jnp.where(lane == 0, POS_LO * c_hi, jnp.where(lane == 1, POS_LO * c_lo,
                             jnp.where(lane == 2, c_hi, jnp.where(lane == 3, c_lo, 0.0)))).astype(BF16))

    def blk(j):
        return slice(j * TQ, (j + 1) * TQ)

    def cols(hh):
        return slice(hh * 2 * DIFF_DH, (hh + 1) * 2 * DIFF_DH)

    def stacked_queries(hh, qi):
        q = (q_ref[blk(qi), cols(hh)].astype(F32) * (DIFF_DH ** -0.5 * LOG2E)).astype(BF16)
        return jnp.concatenate([
            jnp.concatenate([jnp.where(lane < DIFF_DH, q, zero), aug[hh]], axis=1),
            jnp.concatenate([jnp.where(lane >= DIFF_DH, q, zero), aug[hh]], axis=1)], axis=0)

    lhs_refs = (lha_ref, lhb_ref)

    def strip(c):
        return slice(c * ATTN_W, (c + 1) * ATTN_W)

    def keys(hh, j):
        return jnp.concatenate([k_ref[blk(j), cols(hh)], kaug_ref[blk(j), :]], axis=1)

    def scores(hh, kk, qi, c):
        return _nt_dot(kk, lhs_refs[qi % 2][hh, strip(c), :])

    ones_rows = jnp.where(lax.broadcasted_iota(I32, (VT_ROWS - 2 * DIFF_DH, TQ), 0) == 0, 1.0, 0.0).astype(BF16)
    kr = lax.broadcasted_iota(I32, (TQ, TQ), 0)
    qc = lax.broadcasted_iota(I32, (TQ, TQ), 1)
    ahead = jnp.maximum(kr - qc, 0).astype(F32)
    for hh in heads:
        for j in range(SEQ // TQ):
            vt_ref[hh, j, :2 * DIFF_DH, :] = v_ref[blk(j), cols(hh)].astype(F32).T.astype(BF16)
            vt_ref[hh, j, 2 * DIFF_DH:, :] = ones_rows
        corr_ref[hh] = jnp.where((qc >> CHUNK_SHIFT) >= (kr >> CHUNK_SHIFT), (-2.0 * c_alibi[hh]) * ahead, -jnp.inf)

    def update(hh, s, j, c, first):
        nk = s.shape[0]
        if first:
            m_new = jnp.max(s, axis=0, keepdims=True)
            p = jnp.exp2((s - m_new).astype(BF16))
            acc_ref[hh, :, strip(c)] = jnp.dot(vt_ref[hh, j, :, :nk], p, preferred_element_type=F32)
        else:
            m = m_ref[hh, :, strip(c)]
            m_new = jnp.maximum(m, jnp.max(s, axis=0, keepdims=True))
            alpha = jnp.exp2(m - m_new)
            p = jnp.exp2((s - m_new).astype(BF16))
            acc_ref[hh, :, strip(c)] = (alpha * acc_ref[hh, :, strip(c)]
                                        + jnp.dot(vt_ref[hh, j, :, :nk], p, preferred_element_type=F32))
        m_ref[hh, :, strip(c)] = m_new

    def n_keys(qi, j, c):
        return TQ if j < qi else min(TQ, (c * ATTN_W) % TQ + ATTN_W)

    pairs = [(qi, j) for qi in range(SEQ // TQ) for j in range(qi + 1)]
    bufs = (sa_ref, sb_ref)
    kk = [None] * ATTN_HEADS
    for hh in heads:
        lhs_refs[0][hh] = stacked_queries(hh, 0)
        kk[hh] = keys(hh, 0)
    for c in range(ATTN_STRIPS):
        for hh in heads:
            nk = n_keys(0, 0, c)
            bufs[0][hh, :nk, strip(c)] = scores(hh, kk[hh][:nk], 0, c)
    for t, (qi, j) in enumerate(pairs):
        nxt = pairs[t + 1] if t + 1 < len(pairs) else None
        if nxt is not None:
            for hh in heads:
                if nxt[0] != qi:
                    lhs_refs[nxt[0] % 2][hh] = stacked_queries(hh, nxt[0])
                kk[hh] = keys(hh, nxt[1])
        for c in range(ATTN_STRIPS):
            for hh in heads:
                if nxt is not None:
                    nk = n_keys(*nxt, c)
                    bufs[(t + 1) % 2][hh, :nk, strip(c)] = scores(hh, kk[hh][:nk], nxt[0], c)
                nk = n_keys(qi, j, c)
                s = bufs[t % 2][hh, :nk, strip(c)]
                if j == qi:
                    lo = (c * ATTN_W) % TQ
                    s = s + corr_ref[hh, :nk, lo:lo + ATTN_W]
                update(hh, s, j, c, first=(j == 0))
        if j == qi:
            for hh in heads:
                ot = acc_ref[hh, :2 * DIFF_DH, :] / acc_ref[hh, 2 * DIFF_DH:2 * DIFF_DH + 1, :]
                o = (ot[:, :TQ] - lam * ot[:, TQ:]).T
                inv = lax.rsqrt(jnp.mean(o * o, axis=-1, keepdims=True) + EPS)
                o_ref[blk(qi), cols(hh)] = (o * inv * g_ref[...] * (1.0 - LAMBDA_INIT)).astype(BF16)


def _diff_attention(qkv_a, slopes, lam, kaug, diff_norm_g):
    groups = DIFF_HEADS // ATTN_HEADS
    width = ATTN_HEADS * 2 * DIFF_DH
    per_head = lambda shape, dt: pltpu.VMEM((ATTN_HEADS,) + shape, dt)
    return pl.pallas_call(
        _attn_kernel,
        grid=(BATCH, groups),
        in_specs=[
            pl.BlockSpec(memory_space=pltpu.SMEM),
            pl.BlockSpec(memory_space=pltpu.SMEM),
            pl.BlockSpec((None, SEQ, width), lambda b, h: (b, 0, h)),
            pl.BlockSpec((None, SEQ, width), lambda b, h: (b, 0, groups + h)),
            pl.BlockSpec((None, SEQ, width), lambda b, h: (b, 0, 2 * groups + h)),
            pl.BlockSpec((SEQ, LANES), lambda b, h: (0, 0)),
            pl.BlockSpec((1, LANES), lambda b, h: (0, 0)),
        ],
        out_specs=pl.BlockSpec((None, SEQ, width), lambda b, h: (b, 0, h)),
        out_shape=jax.ShapeDtypeStruct((BATCH, SEQ, DIFF_HEADS * 2 * DIFF_DH), BF16),
        scratch_shapes=[
            per_head((TQ, 2 * TQ), F32), per_head((TQ, 2 * TQ), F32),
            per_head((2 * TQ, 4 * DIFF_DH), BF16), per_head((2 * TQ, 4 * DIFF_DH), BF16),
            per_head((TQ, TQ), F32),
            per_head((SEQ // TQ, VT_ROWS, TQ), BF16),
            per_head((1, 2 * TQ), F32),
            per_head((VT_ROWS, 2 * TQ), F32),
        ],
        compiler_params=pltpu.CompilerParams(
            dimension_semantics=("arbitrary", "arbitrary"), vmem_limit_bytes=VMEM_LIMIT),
        name="diff_attn",
    )(slopes, lam, qkv_a, qkv_a, qkv_a, kaug, diff_norm_g)


N_CHUNK = SEQ // CHUNK
PAIR = 2 * GLA_DK
PAIR_V = 2 * GLA_DV
CS_ROWS = 256
GLA_UNROLL = 32


def _gla_kernel(q_ref, k_ref, v_ref, r_ref, lr_ref, wup_ref, bup_ref, g_ref, o_ref,
                gcum_ref, state_ref):
    w_hi, w_mid, _ = _split3(wup_ref[...])
    rr = lax.broadcasted_iota(I32, (CS_ROWS, CS_ROWS), 0)
    cc = lax.broadcasted_iota(I32, (CS_ROWS, CS_ROWS), 1)
    tri = jnp.where(((rr >> CHUNK_SHIFT) == (cc >> CHUNK_SHIFT)) & (cc <= rr), 1.0, 0.0).astype(BF16)
    for blk in range(SEQ // CS_ROWS):
        rows = pl.ds(blk * CS_ROWS, CS_ROWS)
        a_hi, a_mid, _ = _split3(lr_ref[rows, :])
        z = (jnp.dot(a_hi, w_hi, preferred_element_type=F32)
             + jnp.dot(a_hi, w_mid, preferred_element_type=F32)
             + jnp.dot(a_mid, w_hi, preferred_element_type=F32)) + bup_ref[...]
        la = (jnp.minimum(z, 0.0) - jnp.log(1.0 + jnp.exp(-jnp.abs(z)))) * (1.0 / GLA_GATE_NORM)
        l_hi, l_mid, l_lo = _split3(la)
        gcum_ref[rows, :] = (jnp.dot(tri, l_hi, preferred_element_type=F32)
                             + jnp.dot(tri, l_mid, preferred_element_type=F32)
                             + jnp.dot(tri, l_lo, preferred_element_type=F32))

    state_ref[...] = jnp.zeros_like(state_ref)
    lane_k = lax.broadcasted_iota(I32, (1, PAIR), 1)
    row_v = lax.broadcasted_iota(I32, (PAIR_V, PAIR), 0)
    col_k = lax.broadcasted_iota(I32, (PAIR_V, PAIR), 1)
    same_head = (row_v >= GLA_DV) == (col_k >= GLA_DK)
    cr = lax.broadcasted_iota(I32, (CHUNK, CHUNK), 0)
    cs = lax.broadcasted_iota(I32, (CHUNK, CHUNK), 1)
    causal = cs <= cr
    scale = GLA_DK ** -0.5

    def chunk(n):
        rows = pl.ds(pl.multiple_of(n * CHUNK, CHUNK), CHUNK)
        gc = gcum_ref[rows, :]
        g_last = gcum_ref[pl.ds(n * CHUNK + CHUNK - 1, 1), :]
        qf = q_ref[rows, :].astype(F32) * scale
        kf = k_ref[rows, :].astype(F32)
        q_s = (qf * jnp.exp(gc)).astype(BF16)
        k_s = (kf * jnp.exp(-gc)).astype(BF16)
        k_d = (kf * jnp.exp(g_last - gc)).astype(BF16)
        decay = jnp.exp(g_last)
        for pr in range(GLA_HEADS // 2):
            kl = slice(pr * PAIR, (pr + 1) * PAIR)
            vl = slice(pr * PAIR_V, (pr + 1) * PAIR_V)
            qs_p, ks_p, kd_p = q_s[:, kl], k_s[:, kl], k_d[:, kl]
            v_p = v_ref[rows, vl]
            st = state_ref[pr]
            o_inter = _nt_dot(qs_p, st.astype(BF16))
            d_st = _tn_dot(v_p, kd_p)
            state_ref[pr] = st * decay[:, kl] + jnp.where(same_head, d_st, 0.0)
            for sub in range(2):
                hd = 2 * pr + sub
                in_head = (lane_k >= sub * GLA_DK) & (lane_k < (sub + 1) * GLA_DK)
                a = _nt_dot(jnp.where(in_head, qs_p, jnp.zeros_like(qs_p)), ks_p)
                a = jnp.where(causal, a, 0.0).astype(BF16)
                vs = slice(hd * GLA_DV, (hd + 1) * GLA_DV)
                o = (jnp.dot(a, v_ref[rows, vs], preferred_element_type=F32)
                     + o_inter[:, sub * GLA_DV:(sub + 1) * GLA_DV])
                inv = lax.rsqrt(jnp.mean(o * o, axis=-1, keepdims=True) + EPS)
                r = r_ref[rows, vs].astype(F32)
                o_ref[rows, vs] = (o * inv * g_ref[...] * (r * jax.nn.sigmoid(r))).astype(BF16)

    def chunk_group(t, _):
        for u in range(GLA_UNROLL):
            chunk(t * GLA_UNROLL + u)
        return 0

    lax.fori_loop(0, N_CHUNK // GLA_UNROLL, chunk_group, 0)


def _gla(qkv_g, glr, w_up, b_up, gla_norm_g):
    qk_w = GLA_HEADS * GLA_DK
    v_w = GLA_HEADS * GLA_DV
    return pl.pallas_call(
        _gla_kernel,
        grid=(BATCH,),
        in_specs=[
            pl.BlockSpec((None, SEQ, qk_w), lambda b: (b, 0, 0)),
            pl.BlockSpec((None, SEQ, qk_w), lambda b: (b, 0, 1)),
            pl.BlockSpec((None, SEQ, v_w), lambda b: (b, 0, 1)),
            pl.BlockSpec((None, SEQ, v_w), lambda b: (b, 0, 2)),
            pl.BlockSpec((None, SEQ, LANES), lambda b: (b, 0, 0)),
            pl.BlockSpec((LANES, qk_w), lambda b: (0, 0)),
            pl.BlockSpec((1, qk_w), lambda b: (0, 0)),
            pl.BlockSpec((1, GLA_DV), lambda b: (0, 0)),
        ],
        out_specs=pl.BlockSpec((None, SEQ, v_w), lambda b: (b, 0, 0)),
        out_shape=jax.ShapeDtypeStruct((BATCH, SEQ, v_w), BF16),
        scratch_shapes=[
            pltpu.VMEM((SEQ, qk_w), F32),
            pltpu.VMEM((GLA_HEADS // 2, PAIR_V, PAIR), F32),
        ],
        compiler_params=pltpu.CompilerParams(
            dimension_semantics=("arbitrary",), vmem_limit_bytes=VMEM_LIMIT),
        name="gla",
    )(qkv_g, qkv_g, qkv_g, qkv_g, glr, w_up, b_up, gla_norm_g)


def _merge_kernel(oa_ref, ob_ref, gate_ref, x_ref, gt1_ref, sh2_ref, sc2_ref, g2_ref,
                  wpa_ref, wpb_ref, wo_ref, wr_ref, br_ref,
                  x1_ref, h2p_ref, eidx_ref, rank_ref, wts_ref, cnt_ref,
                  upper_ref, carry_ref):
    i = pl.program_id(0)

    @pl.when(i == 0)
    def _():
        rr = lax.broadcasted_iota(I32, (TM_MERGE, TM_MERGE), 0)
        cc = lax.broadcasted_iota(I32, (TM_MERGE, TM_MERGE), 1)
        upper_ref[...] = jnp.where(rr < cc, 1.0, 0.0).astype(BF16)
        carry_ref[...] = jnp.zeros_like(carry_ref)

    ga = gate_ref[:, :D_MODEL].astype(F32)
    gb = gate_ref[:, D_MODEL:].astype(F32)
    merged = (jax.nn.sigmoid(ga) * jnp.dot(oa_ref[...], wpa_ref[...], preferred_element_type=F32)
              + jax.nn.sigmoid(gb) * jnp.dot(ob_ref[...], wpb_ref[...], preferred_element_type=F32))
    y = jnp.dot(merged.astype(BF16), wo_ref[...], preferred_element_type=F32)
    x1 = x_ref[...] + gt1_ref[...] * y
    x1_ref[...] = x1
    inv = lax.rsqrt(jnp.mean(x1 * x1, axis=-1, keepdims=True) + EPS)
    h2 = (x1 * inv * g2_ref[...]) * (1.0 + sc2_ref[...]) + sh2_ref[...]
    h2p_ref[...] = _pack_halves(h2)

    h_hi, h_mid, _ = _split3(h2)
    w_hi, w_mid, _ = _split3(wr_ref[...])
    logits = (_nt_dot(w_hi, h_hi) + _nt_dot(w_hi, h_mid) + _nt_dot(w_mid, h_hi)) + br_ref[...]

    eio = lax.broadcasted_iota(I32, (N_EXPERTS, TM_MERGE), 0)
    vals, idxs, sels = [], [], []
    cur = logits
    for _k in range(TOP_K):
        m = jnp.max(cur, axis=0, keepdims=True)
        idx = jnp.min(jnp.where(cur == m, eio, N_EXPERTS), axis=0, keepdims=True)
        sel = eio == idx
        vals.append(m)
        idxs.append(idx)
        sels.append(sel)
        cur = jnp.where(sel, -jnp.inf, cur)
    es = [jnp.exp(v - vals[0]) for v in vals]
    tot = es[0] + es[1] + es[2] + es[3]
    onehot = jnp.zeros((N_EXPERTS, TM_MERGE), F32)
    for sel in sels:
        onehot = onehot + jnp.where(sel, 1.0, 0.0)
    before = jnp.dot(onehot.astype(BF16), upper_ref[...], preferred_element_type=F32) + carry_ref[:, 0:1]
    ranks = [jnp.sum(jnp.where(sel, before, 0.0), axis=0, keepdims=True) for sel in sels]
    carry_ref[...] = carry_ref[...] + jnp.sum(onehot, axis=1, keepdims=True)
    cnt_ref[...] = carry_ref[...]

    zi = jnp.zeros((8 - TOP_K, TM_MERGE), I32)
    zf = jnp.zeros((8 - TOP_K, TM_MERGE), F32)
    eidx_ref[...] = jnp.concatenate(idxs + [zi], axis=0)
    rank_ref[...] = jnp.concatenate([r.astype(I32) for r in ranks] + [zi], axis=0)
    wts_ref[...] = jnp.concatenate([e / tot for e in es] + [zf], axis=0)


def _merge_route(grp, o_a, o_b, gates, x2d, mod4, norm2_g, w_pa, w_pb, w_o, w_rt, b_r):
    ntile = N_GRP // TM_MERGE
    first = grp * ntile
    per_b = SEQ // TM_MERGE
    full = lambda shape: pl.BlockSpec(shape, lambda i: (0,) * len(shape))
    row_in = lambda w: pl.BlockSpec((TM_MERGE, w), lambda i: (first + i, 0))
    row = lambda w: pl.BlockSpec((TM_MERGE, w), lambda i: (i, 0))
    modspec = lambda j: pl.BlockSpec((None, None, 1, D_MODEL), lambda i: (j, (first + i) // per_b, 0, 0))
    col = pl.BlockSpec((8, TM_MERGE), lambda i: (0, i))
    return pl.pallas_call(
        _merge_kernel,
        grid=(ntile,),
        in_specs=[
            row_in(DIFF_HEADS * 2 * DIFF_DH), row_in(GLA_HEADS * GLA_DV), row_in(W_GATE), row_in(D_MODEL),
            modspec(2), modspec(3), modspec(4),
            full((1, D_MODEL)),
            full((DIFF_HEADS * 2 * DIFF_DH, D_MODEL)), full((GLA_HEADS * GLA_DV, D_MODEL)),
            full((D_MODEL, D_MODEL)),
            full((N_EXPERTS, D_MODEL)), full((N_EXPERTS, 1)),
        ],
        out_specs=[
            row(D_MODEL), row(ROW_W), col, col, col,
            pl.BlockSpec((N_EXPERTS, LANES), lambda i: (0, 0)),
        ],
        out_shape=[
            jax.ShapeDtypeStruct((N_GRP, D_MODEL), F32),
            jax.ShapeDtypeStruct((N_GRP, ROW_W), ROW_DT),
            jax.ShapeDtypeStruct((8, N_GRP), I32),
            jax.ShapeDtypeStruct((8, N_GRP), I32),
            jax.ShapeDtypeStruct((8, N_GRP), F32),
            jax.ShapeDtypeStruct((N_EXPERTS, LANES), F32),
        ],
        scratch_shapes=[
            pltpu.VMEM((TM_MERGE, TM_MERGE), BF16),
            pltpu.VMEM((N_EXPERTS, LANES), F32),
        ],
        compiler_params=pltpu.CompilerParams(
            dimension_semantics=("arbitrary",), vmem_limit_bytes=VMEM_LIMIT),
        name="merge_route",
    )(o_a, o_b, gates, x2d, mod4, mod4, mod4, norm2_g, w_pa, w_pb, w_o, w_rt, b_r)


TP = 4096
NB_PAD = ((N_BLK + LANES - 1) // LANES) * LANES


def _plan_kernel(cnt_ref, eidx_ref, rank_ref, dest_ref, be_ref, nv_ref, rows_ref):
    cnt = cnt_ref[...]
    padded = jnp.floor((cnt + (BLK - 1.0)) * (1.0 / BLK)) * BLK
    er = lax.broadcasted_iota(I32, (N_EXPERTS, N_EXPERTS), 0)
    ec = lax.broadcasted_iota(I32, (N_EXPERTS, N_EXPERTS), 1)
    lower = jnp.where(ec < er, 1.0, 0.0).astype(BF16)
    p_hi, p_mid, p_lo = _split3(padded)
    starts = (jnp.dot(lower, p_hi, preferred_element_type=F32)
              + jnp.dot(lower, p_mid, preferred_element_type=F32)
              + jnp.dot(lower, p_lo, preferred_element_type=F32))
    ends = starts + padded
    blk_start = (lax.broadcasted_iota(I32, (1, NB_PAD), 1) * BLK).astype(F32)
    n_before = jnp.sum(jnp.where(ends[:, 0:1] <= blk_start, 1.0, 0.0), axis=0, keepdims=True)
    be_ref[...] = jnp.minimum(n_before, N_EXPERTS - 1.0).astype(I32)
    nv_ref[...] = (jnp.max(ends, axis=0, keepdims=True) * (1.0 / BLK)).astype(I32)
    owner = (starts[:, 0:1] <= blk_start) & (blk_start < ends[:, 0:1])
    filled = jnp.sum(jnp.where(owner, (starts + cnt)[:, 0:1] - blk_start, 0.0), axis=0, keepdims=True)
    rows_ref[...] = jnp.clip(filled, 0.0, float(BLK)).astype(I32)

    eio = lax.broadcasted_iota(I32, (N_EXPERTS, TP), 0)
    rows = []
    for k in range(TOP_K):
        onehot = eio == eidx_ref[k:k + 1, :]
        base = jnp.sum(jnp.where(onehot, starts[:, 0:1], 0.0), axis=0, keepdims=True)
        rows.append(base.astype(I32) + rank_ref[k:k + 1, :])
    dest_ref[...] = jnp.concatenate(rows + [jnp.zeros((8 - TOP_K, TP), I32)], axis=0)


def _route_plan(cnt, eidx, rank):
    return pl.pallas_call(
        _plan_kernel,
        grid=(N_GRP // TP,),
        in_specs=[
            pl.BlockSpec((N_EXPERTS, LANES), lambda i: (0, 0)),
            pl.BlockSpec((8, TP), lambda i: (0, i)),
            pl.BlockSpec((8, TP), lambda i: (0, i)),
        ],
        out_specs=[
            pl.BlockSpec((8, TP), lambda i: (0, i)),
            pl.BlockSpec((1, NB_PAD), lambda i: (0, 0)),
            pl.BlockSpec((1, LANES), lambda i: (0, 0)),
            pl.BlockSpec((1, NB_PAD), lambda i: (0, 0)),
        ],
        out_shape=[
            jax.ShapeDtypeStruct((8, N_GRP), I32),
            jax.ShapeDtypeStruct((1, NB_PAD), I32),
            jax.ShapeDtypeStruct((1, LANES), I32),
            jax.ShapeDtypeStruct((1, NB_PAD), I32),
        ],
        compiler_params=pltpu.CompilerParams(dimension_semantics=("arbitrary",)),
        name="route_plan",
    )(cnt, eidx, rank)


def _ffn_kernel(be_ref, nv_ref, first_ref, slot_ref, nxt_ref, rows_ref, x_ref, wgu_hbm, bgu_ref, wd_hbm, bd_ref,
                y_ref, wgu_buf, wd_buf, sem):
    i = pl.program_id(0)
    valid = i < nv_ref[0]
    filled = jnp.maximum(rows_ref[i], 1)
    s = slot_ref[i]

    def weight_copies(e, slot):
        return (pltpu.make_async_copy(wgu_hbm.at[e], wgu_buf.at[slot], sem.at[slot, 0]),
                pltpu.make_async_copy(wd_hbm.at[e], wd_buf.at[slot], sem.at[slot, 1]))

    @pl.when(i == 0)
    def _():
        for cp in weight_copies(be_ref[0], 0):
            cp.start()

    @pl.when(valid & (first_ref[i] == 1))
    def _():
        for cp in weight_copies(be_ref[i], s):
            cp.wait()

        @pl.when(nxt_ref[i] >= 0)
        def _():
            for cp in weight_copies(nxt_ref[i], 1 - s):
                cp.start()

    def expert_rows(n):
        lo, hi = _unpack_halves(x_ref[:n, :])
        x = jnp.concatenate([lo, hi], axis=1).astype(BF16)
        mm = lambda a, w: lax.dot_general(a, w, (((1,), (0,)), ((), ())), preferred_element_type=F32)
        y = bd_ref[...]
        for hf in range(FF_PARTS):
            lo = hf * (D_FF // FF_PARTS)
            hid = slice(lo, lo + D_FF // FF_PARTS)
            uph = slice(D_FF + lo, D_FF + lo + D_FF // FF_PARTS)
            gate = jnp.minimum(mm(x, wgu_buf[s, :, hid]) + bgu_ref[:, hid], SWIGLU_LIMIT)
            up = jnp.clip(mm(x, wgu_buf[s, :, uph]) + bgu_ref[:, uph], -SWIGLU_LIMIT, SWIGLU_LIMIT)
            act = (up + 1.0) * (gate * jax.nn.sigmoid(SWIGLU_ALPHA * gate))
            y = y + mm(act.astype(BF16), wd_buf[s, hid, :])
        y_ref[:n, :] = _pack_halves(y)

    for n in range(BLK_STEP, BLK + 1, BLK_STEP):
        @pl.when(valid & (filled > n - BLK_STEP) & (filled <= n))
        def _(n=n):
            expert_rows(n)
            if n < BLK:
                y_ref[n:, :] = jnp.zeros((BLK - n, ROW_W), ROW_DT)

    @pl.when(jnp.logical_not(valid))
    def _():
        y_ref[...] = jnp.zeros_like(y_ref)


def _expert_ffn(blk_expert, n_valid, blk_rows, xb, w_gate_up, b_gate_up, w_down, b_down):
    idx = jnp.arange(N_BLK, dtype=I32)
    used = idx < n_valid[0]
    first = (used & ((idx == 0) | (blk_expert != jnp.roll(blk_expert, 1)))).astype(I32)
    slot = (jnp.cumsum(first) - 1) & 1
    later = used[None, :] & (idx[None, :] > idx[:, None]) & (blk_expert[None, :] != blk_expert[:, None])
    nxt_pos = jnp.min(jnp.where(later, idx[None, :], N_BLK), axis=1)
    nxt = jnp.where(nxt_pos < N_BLK, blk_expert[jnp.minimum(nxt_pos, N_BLK - 1)], -1).astype(I32)

    blockwise = lambda i, *_: (i, 0)
    per_expert = lambda i, be, *_: (be[i], 0, 0)
    grid_spec = pltpu.PrefetchScalarGridSpec(
        num_scalar_prefetch=6,
        grid=(N_BLK,),
        in_specs=[
            pl.BlockSpec((BLK, ROW_W), blockwise),
            pl.BlockSpec(memory_space=pl.ANY),
            pl.BlockSpec((None, 1, 2 * D_FF), per_expert),
            pl.BlockSpec(memory_space=pl.ANY),
            pl.BlockSpec((None, 1, D_MODEL), per_expert),
        ],
        out_specs=pl.BlockSpec((BLK, ROW_W), blockwise),
        scratch_shapes=[
            pltpu.VMEM((2, D_MODEL, 2 * D_FF), F32),
            pltpu.VMEM((2, D_FF, D_MODEL), F32),
            pltpu.SemaphoreType.DMA((2, 2)),
        ],
    )
    return pl.pallas_call(
        _ffn_kernel,
        grid_spec=grid_spec,
        out_shape=jax.ShapeDtypeStruct((P_ROWS, ROW_W), ROW_DT),
        compiler_params=pltpu.CompilerParams(
            dimension_semantics=("arbitrary",), vmem_limit_bytes=VMEM_LIMIT),
        name="expert_ffn",
    )(blk_expert, n_valid, first, slot.astype(I32), nxt, blk_rows, xb, w_gate_up,
      b_gate_up.reshape(N_EXPERTS, 1, 2 * D_FF), w_down, b_down.reshape(N_EXPERTS, 1, D_MODEL))


def _final_kernel(x1_ref, y0_ref, y1_ref, y2_ref, y3_ref, w_ref, gt2_ref, g_ref, *rest):
    o_ref = rest[-1]
    w = w_ref[...].T
    ylo = jnp.zeros((TM_FIN, HALF), F32)
    yhi = jnp.zeros((TM_FIN, HALF), F32)
    for k, y_ref in enumerate((y0_ref, y1_ref, y2_ref, y3_ref)):
        lo, hi = _unpack_halves(y_ref[...])
        wk = w[:, k:k + 1]
        ylo = ylo + wk * lo
        yhi = yhi + wk * hi
    y = jnp.concatenate([ylo, yhi], axis=1)
    x2 = x1_ref[...] + gt2_ref[...] * y
    inv = lax.rsqrt(jnp.mean(x2 * x2, axis=-1, keepdims=True) + EPS)
    o_ref[...] = x2 * inv * g_ref[...]


def _final(grp, x1, yg, w4, mod4, final_norm_g, out_so_far):
    per_b = SEQ // TM_FIN
    ntile = N_GRP // TM_FIN
    first = grp * ntile
    slot = lambda k: pl.BlockSpec((TM_FIN, ROW_W), lambda i: (k * ntile + i, 0))
    in_specs = [
        pl.BlockSpec((TM_FIN, D_MODEL), lambda i: (i, 0)),
        slot(0), slot(1), slot(2), slot(3),
        pl.BlockSpec((8, TM_FIN), lambda i: (0, i)),
        pl.BlockSpec((None, None, 1, D_MODEL), lambda i: (5, (first + i) // per_b, 0, 0)),
        pl.BlockSpec((1, D_MODEL), lambda i: (0, 0)),
    ]
    args = [x1, yg, yg, yg, yg, w4, mod4, final_norm_g]
    aliases = {}
    if out_so_far is not None:
        in_specs.append(pl.BlockSpec(memory_space=pl.ANY))
        args.append(out_so_far)
        aliases = {len(args) - 1: 0}
    return pl.pallas_call(
        _final_kernel,
        grid=(ntile,),
        in_specs=in_specs,
        out_specs=pl.BlockSpec((TM_FIN, D_MODEL), lambda i: (first + i, 0)),
        out_shape=jax.ShapeDtypeStruct((N_TOK, D_MODEL), F32),
        input_output_aliases=aliases,
        compiler_params=pltpu.CompilerParams(
            dimension_semantics=("arbitrary",), vmem_limit_bytes=VMEM_LIMIT),
        name="combine_final",
    )(*args)


SC_CORES = 2
SC_SUBCORES = 16
SC_WORKERS = SC_CORES * SC_SUBCORES
SC_CHUNK = 64


def _sc_mesh():
    return plsc.VectorSubcoreMesh(core_axis_name="c", subcore_axis_name="s")


def _row_buffers():
    return ([pltpu.VMEM((SC_CHUNK, HALF), U32)] * 2 + [pltpu.SemaphoreType.DMA] * 5)


def _dispatch_rows(h2p, dest2d):
    per_w = N_GRP // SC_WORKERS
    nchunk = per_w // SC_CHUNK
    rows_per_k = N_GRP // SC_CHUNK
    assert nchunk % 2 == 0

    @functools.partial(
        pl.kernel, mesh=_sc_mesh(), out_type=jax.ShapeDtypeStruct((P_ROWS, HALF), U32),
        scratch_types=[pltpu.VMEM((TOP_K * nchunk, SC_CHUNK), I32)] + _row_buffers(),
        name="moe_dispatch")
    def k(src_hbm, dest_hbm, out_hbm, idx_v, buf0, buf1, isem, l0, l1, s0, s1):
        wid = lax.axis_index("s") * SC_CORES + lax.axis_index("c")
        bufs, lsem, ssem = (buf0, buf1), (l0, l1), (s0, s1)
        idx_loads = [
            pltpu.make_async_copy(dest_hbm.at[pl.ds(kk * rows_per_k + wid * nchunk, nchunk)],
                                  idx_v.at[pl.ds(kk * nchunk, nchunk)], isem) for kk in range(TOP_K)]

        def load(j, b):
            return pltpu.make_async_copy(src_hbm.at[pl.ds(wid * per_w + j * SC_CHUNK, SC_CHUNK)], bufs[b], lsem[b])

        def scatters(j, b):
            return [pltpu.make_async_copy(bufs[b], out_hbm.at[idx_v.at[kk * nchunk + j]], ssem[b])
                    for kk in range(TOP_K)]

        for cp in idx_loads:
            cp.start()
        load(0, 0).start()
        for cp in idx_loads:
            cp.wait()

        @pl.loop(0, nchunk // 2)
        def _(i):
            for b in range(2):
                j = 2 * i + b

                @pl.when(j >= 1)
                def _():
                    for cp in scatters(j - 1, 1 - b):
                        cp.wait()

                @pl.when(j + 1 < nchunk)
                def _():
                    load(j + 1, 1 - b).start()

                load(j, b).wait()
                for cp in scatters(j, b):
                    cp.start()

        for cp in scatters(nchunk - 1, 1):
            cp.wait()

    return k(h2p, dest2d)


def _combine_rows(yb, dest2d):
    n_out = TOP_K * N_GRP
    per_w = n_out // SC_WORKERS
    nchunk = per_w // SC_CHUNK
    assert nchunk % 2 == 0

    @functools.partial(
        pl.kernel, mesh=_sc_mesh(), out_type=jax.ShapeDtypeStruct((n_out, HALF), U32),
        scratch_types=[pltpu.VMEM((nchunk, SC_CHUNK), I32)] + _row_buffers(),
        name="moe_combine")
    def k(tab_hbm, idx_hbm, out_hbm, idx_v, buf0, buf1, isem, g0, g1, s0, s1):
        wid = lax.axis_index("s") * SC_CORES + lax.axis_index("c")
        bufs, gsem, ssem = (buf0, buf1), (g0, g1), (s0, s1)
        idx_load = pltpu.make_async_copy(idx_hbm.at[pl.ds(wid * nchunk, nchunk)], idx_v, isem)

        def gather(j, b):
            return pltpu.make_async_copy(tab_hbm.at[idx_v.at[j]], bufs[b], gsem[b])

        def store(j, b):
            return pltpu.make_async_copy(bufs[b], out_hbm.at[pl.ds(wid * per_w + j * SC_CHUNK, SC_CHUNK)], ssem[b])

        idx_load.start()
        idx_load.wait()
        gather(0, 0).start()

        @pl.loop(0, nchunk // 2)
        def _(i):
            for b in range(2):
                j = 2 * i + b

                @pl.when(j >= 1)
                def _():
                    store(j - 1, 1 - b).wait()

                @pl.when(j + 1 < nchunk)
                def _():
                    gather(j + 1, 1 - b).start()

                gather(j, b).wait()
                store(j, b).start()

        store(nchunk - 1, 1).wait()

    return k(yb, dest2d)


def _lambda_kernel(p_ref, o_ref):
    p = p_ref[...]
    s1 = jnp.sum(p[0:1] * p[1:2], axis=-1, keepdims=True)
    s2 = jnp.sum(p[2:3] * p[3:4], axis=-1, keepdims=True)
    o_ref[...] = jnp.broadcast_to(jnp.exp(s1) - jnp.exp(s2) + LAMBDA_INIT, (1, LANES))


def kernel(x, c, w_ada, b_ada, norm1_g, w_in, lambda_q1, lambda_k1, lambda_q2, lambda_k2, diff_norm_g, w_alpha_up, b_alpha, gla_norm_g, w_branch_diff, w_branch_gla, w_out, norm2_g, w_router, b_router, w_gate_up, b_gate_up, w_down, b_down, final_norm_g):
    w_in0 = w_in[0]
    c_a, c_g = W_A, W_A + W_G
    w_a = w_in0[:, :c_a].astype(BF16)
    w_g = w_in0[:, c_a:c_g].astype(BF16)
    w_lr = jnp.pad(w_in0[:, c_g:c_g + GLA_RANK], ((0, 0), (0, LANES - GLA_RANK))).astype(BF16)
    w_gate = w_in0[:, c_g + GLA_RANK:].astype(BF16)
    w_up = jnp.pad(w_alpha_up[0], ((0, LANES - GLA_RANK), (0, 0)))
    lam_in = jnp.concatenate([lambda_q1, lambda_k1, lambda_q2, lambda_k2], axis=0)
    slopes = jnp.asarray(2.0 ** (-8.0 * np.arange(1, DIFF_HEADS + 1) / DIFF_HEADS), dtype=F32)

    mod = _modulation(c, w_ada[0], b_ada[0])
    mod4 = mod.reshape(N_MOD, BATCH, 1, D_MODEL)
    lam = pl.pallas_call(
        _lambda_kernel, out_shape=jax.ShapeDtypeStruct((1, LANES), F32), name="lambda")(lam_in)[0, :1]

    qkv_a, qkv_g, gates, glr = _in_proj(x, mod4, norm1_g, w_a, w_g, w_gate, w_lr)
    pos = jnp.arange(SEQ, dtype=I32)
    p_hi = (pos >> POS_LO_BITS).astype(F32)
    p_lo = (pos & (POS_LO - 1)).astype(F32)
    kaug = jnp.zeros((SEQ, LANES), F32).at[:, 0].set(p_hi).at[:, 1].set(p_hi).at[:, 2].set(p_lo).at[:, 3].set(
        p_lo).astype(BF16)
    o_a = _diff_attention(qkv_a, slopes, lam, kaug, diff_norm_g)
    o_b = _gla(qkv_g, glr, w_up, b_alpha, gla_norm_g)

    merge_args = (o_a.reshape(N_TOK, -1), o_b.reshape(N_TOK, -1), gates.reshape(N_TOK, W_GATE),
                  x.reshape(N_TOK, D_MODEL), mod4, norm2_g,
                  w_branch_diff[0].astype(BF16), w_branch_gla[0].astype(BF16), w_out[0].astype(BF16),
                  w_router[0].T, b_router[0].reshape(N_EXPERTS, 1))
    fin_g = final_norm_g.reshape(1, D_MODEL)

    routed = []
    for grp in range(MOE_GROUPS):
        x1, h2p, eidx, rank, wts, cnt = _merge_route(grp, *merge_args)
        dest8, be, nv, rows = _route_plan(cnt, eidx, rank)
        dest = dest8[:TOP_K].reshape(-1, SC_CHUNK)
        routed.append((x1, wts, dest, be, nv, rows, _dispatch_rows(h2p, dest)))
    gathered = []
    for x1, wts, dest, be, nv, rows, xb in routed:
        yb = _expert_ffn(be[0, :N_BLK], nv[0, :1], rows[0, :N_BLK], xb,
                         w_gate_up[0], b_gate_up[0], w_down[0], b_down[0])
        gathered.append(_combine_rows(yb, dest))
    out = None
    for grp, ((x1, wts, *_), yg) in enumerate(zip(routed, gathered)):
        out = _final(grp, x1, yg, wts, mod4, fin_g, out)
    return out.reshape(BATCH, SEQ, D_MODEL)
```
